```python
import jax, jax.numpy as jnp
from jax import lax
import numpy as np

D_MODEL = 1024
BATCH = 4
SEQ = 4096
DEPTH = 2

GRID_W = 64
CTX_LEN = 256
N_MIXERS = 2
N_POOL_LAYERS = (DEPTH + N_MIXERS - 1) // N_MIXERS
N_ATTN_LAYERS = DEPTH // N_MIXERS
POOL_WINDOWS = (2, 4, 8, 16)
N_POOL_GROUPS = len(POOL_WINDOWS)
POOL_GROUP_DIM = D_MODEL // N_POOL_GROUPS
HEAD_DIM = 64
N_HEADS = D_MODEL // HEAD_DIM
N_KV_HEADS = 2
GQA_GROUP = N_HEADS // N_KV_HEADS
Q_DIM = N_HEADS * HEAD_DIM
KV_DIM = N_KV_HEADS * HEAD_DIM
WINDOW = 128
BLOCK = 128
ROPE_BASE = 10000.0
AXIS_ROT = HEAD_DIM // 2
D_FF = 4 * D_MODEL
N_MOD = 6
EPS = 1e-6
NEG = -1e30

kernel_name = 'hybrid_pool_swa_dit_block'


def rmsnorm(x, g):
    xf = x.astype(jnp.float32)
    y = xf * lax.rsqrt(jnp.mean(xf * xf, axis=-1, keepdims=True) + EPS)
    return (y * g.astype(jnp.float32)).astype(x.dtype)


def modulate(h, shift, scale):
    return h * (1 + scale) + shift


def pool_minus_self(u):
    B, L, G, C = u.shape
    uf = u.astype(jnp.float32)
    cs = jnp.concatenate([jnp.zeros((B, 1, G, C), jnp.float32), lax.cumsum(uf, axis=1)], axis=1)
    t = jnp.arange(L)
    outs = []
    for g, w in enumerate(POOL_WINDOWS):
        lo = jnp.clip(t - w // 2, 0, L)
        hi = jnp.clip(t + w // 2, 0, L)
        cnt = (hi - lo).astype(jnp.float32)
        csg = cs[:, :, g]
        s = jnp.take(csg, hi, axis=1) - jnp.take(csg, lo, axis=1)
        outs.append(s / cnt[None, :, None])
    pooled = jnp.stack(outs, axis=2)
    return (pooled - uf).astype(u.dtype)


def pool_mixer(h, w_in, w_grp, scale, w_out):
    B, L, _ = h.shape
    u = (h @ w_in).reshape(B, L, N_POOL_GROUPS, POOL_GROUP_DIM)
    d = pool_minus_self(u)
    y = jnp.einsum('blgc,gce->blge', d, w_grp).reshape(B, L, D_MODEL) * scale
    return y @ w_out


def axial_angles(L):
    rows = L // GRID_W
    row = jnp.repeat(jnp.arange(rows), GRID_W).astype(jnp.float32)
    col = jnp.tile(jnp.arange(GRID_W), rows).astype(jnp.float32)
    inv = ROPE_BASE ** (-jnp.arange(0, AXIS_ROT, 2, dtype=jnp.float32) / AXIS_ROT)
    ang_r = row[:, None] * inv[None]
    ang_c = col[:, None] * inv[None]
    return (jnp.cos(ang_r), jnp.sin(ang_r), jnp.cos(ang_c), jnp.sin(ang_c))


def rotate_axis(u, cos, sin):
    r = u.shape[-1] // 2
    u1, u2 = u[..., :r], u[..., r:]
    cos = cos[:, None, :].astype(u.dtype)
    sin = sin[:, None, :].astype(u.dtype)
    return jnp.concatenate([u1 * cos - u2 * sin, u1 * sin + u2 * cos], axis=-1)


def rope_2d(t, ang):
    cos_r, sin_r, cos_c, sin_c = ang
    return jnp.concatenate([rotate_axis(t[..., :AXIS_ROT], cos_r, sin_r),
                            rotate_axis(t[..., AXIS_ROT:], cos_c, sin_c)], axis=-1)


def window_attention(h, hc, w_qkv, sink, w_o, ang, with_ctx_out):
    B, L, _ = h.shape
    ctx_len = hc.shape[1]
    nb = L // BLOCK
    qscale = HEAD_DIM ** -0.5
    qkv = h @ w_qkv
    q = qkv[..., :Q_DIM].reshape(B, L, N_HEADS, HEAD_DIM)
    k = qkv[..., Q_DIM:Q_DIM + KV_DIM].reshape(B, L, N_KV_HEADS, HEAD_DIM)
    v = qkv[..., Q_DIM + KV_DIM:].reshape(B, L, N_KV_HEADS, HEAD_DIM)
    q = rope_2d(q, ang) * qscale
    k = rope_2d(k, ang)
    kv_c = hc @ w_qkv[:, Q_DIM:]
    kc = kv_c[..., :KV_DIM].reshape(B, ctx_len, N_KV_HEADS, HEAD_DIM)
    vc = kv_c[..., KV_DIM:].reshape(B, ctx_len, N_KV_HEADS, HEAD_DIM)

    qb = q.reshape(B, nb, BLOCK, N_KV_HEADS, GQA_GROUP, HEAD_DIM)

    def band(t):
        tp = jnp.pad(t, ((0, 0), (BLOCK, BLOCK), (0, 0), (0, 0)))
        tp = tp.reshape(B, nb + 2, BLOCK, N_KV_HEADS, HEAD_DIM)
        return jnp.concatenate([tp[:, :-2], tp[:, 1:-1], tp[:, 2:]], axis=2)

    kw, vw = band(k), band(v)
    s_win = jnp.einsum('bnqhgd,bnkhd->bnhgqk', qb, kw).astype(jnp.float32)
    qpos = jnp.arange(L).reshape(nb, BLOCK)
    kpos = (jnp.arange(nb) * BLOCK - BLOCK)[:, None] + jnp.arange(3 * BLOCK)[None, :]
    valid = ((kpos >= 0) & (kpos < L))[:, None, :] & (jnp.abs(qpos[:, :, None] - kpos[:, None, :]) <= WINDOW)
    s_win = jnp.where(valid[None, :, None, None], s_win, NEG)
    s_ctx = jnp.einsum('bnqhgd,bchd->bnhgqc', qb, kc).astype(jnp.float32)
    sink_f = sink.astype(jnp.float32).reshape(1, 1, N_KV_HEADS, GQA_GROUP, 1, 1)
    s_sink = jnp.broadcast_to(sink_f, s_win.shape[:-1] + (1,))
    p = jax.nn.softmax(jnp.concatenate([s_win, s_ctx, s_sink], axis=-1), axis=-1)
    p_win = p[..., :3 * BLOCK].astype(v.dtype)
    p_ctx = p[..., 3 * BLOCK:3 * BLOCK + ctx_len].astype(v.dtype)
    o = (jnp.einsum('bnhgqk,bnkhd->bnqhgd', p_win, vw)
         + jnp.einsum('bnhgqc,bchd->bnqhgd', p_ctx, vc))
    y = o.reshape(B, L, Q_DIM) @ w_o

    yc = None
    if with_ctx_out:
        qc = (hc @ w_qkv[:, :Q_DIM]).reshape(B, ctx_len, N_KV_HEADS, GQA_GROUP, HEAD_DIM) * qscale
        sc = jnp.einsum('bqhgd,bchd->bhgqc', qc, kc).astype(jnp.float32)
        sc_sink = jnp.broadcast_to(sink.astype(jnp.float32).reshape(1, N_KV_HEADS, GQA_GROUP, 1, 1),
                                   sc.shape[:-1] + (1,))
        pc = jax.nn.softmax(jnp.concatenate([sc, sc_sink], axis=-1), axis=-1)
        oc = jnp.einsum('bhgqc,bchd->bqhgd', pc[..., :ctx_len].astype(vc.dtype), vc)
        yc = oc.reshape(B, ctx_len, Q_DIM) @ w_o
    return y, yc


def sq_relu_mlp(h, w1, w2):
    return jnp.square(jax.nn.relu(h @ w1)) @ w2


def setup_inputs(seed: int = 0) -> dict:
    key = jax.random.key(seed)
    ks = jax.random.split(key, 20)
    nrm = jax.random.normal
    f32 = jnp.float32
    return {
        'x': nrm(ks[0], (BATCH, SEQ, D_MODEL), f32),
        'c': nrm(ks[1], (BATCH, D_MODEL), f32),
        'ctx': nrm(ks[2], (BATCH, CTX_LEN, D_MODEL), f32),
        'c_ctx': nrm(ks[3], (D_MODEL,), f32),
        'ada_w': nrm(ks[4], (DEPTH, D_MODEL, N_MOD * D_MODEL), f32) * (0.5 * D_MODEL ** -0.5),
        'ada_b': nrm(ks[5], (DEPTH, N_MOD * D_MODEL), f32) * 0.01,
        'norm_mix_g': 1.0 + 0.05 * nrm(ks[6], (DEPTH, D_MODEL), f32),
        'norm_mlp_g': 1.0 + 0.05 * nrm(ks[7], (DEPTH, D_MODEL), f32),
        'pool_w_in': nrm(ks[8], (N_POOL_LAYERS, D_MODEL, D_MODEL), f32) * D_MODEL ** -0.5,
        'pool_w_grp': nrm(ks[9], (N_POOL_LAYERS, N_POOL_GROUPS, POOL_GROUP_DIM, POOL_GROUP_DIM), f32) * POOL_GROUP_DIM ** -0.5,
        'pool_scale': 1.0 + 0.1 * nrm(ks[10], (N_POOL_LAYERS, D_MODEL), f32),
        'pool_w_out': nrm(ks[11], (N_POOL_LAYERS, D_MODEL, D_MODEL), f32) * D_MODEL ** -0.5,
        'attn_w_qkv': nrm(ks[12], (N_ATTN_LAYERS, D_MODEL, Q_DIM + 2 * KV_DIM), f32) * D_MODEL ** -0.5,
        'attn_sink': 0.5 * nrm(ks[13], (N_ATTN_LAYERS, N_HEADS), f32),
        'attn_w_o': nrm(ks[14], (N_ATTN_LAYERS, Q_DIM, D_MODEL), f32) * Q_DIM ** -0.5,
        'mlp_w1': nrm(ks[15], (DEPTH, D_MODEL, D_FF), f32) * D_MODEL ** -0.5,
        'mlp_w2': nrm(ks[16], (DEPTH, D_FF, D_MODEL), f32) * D_FF ** -0.5,
        'final_g': 1.0 + 0.05 * nrm(ks[17], (D_MODEL,), f32),
    }


def reference(x, c, ctx, c_ctx, ada_w, ada_b, norm_mix_g, norm_mlp_g, pool_w_in, pool_w_grp,
              pool_scale, pool_w_out, attn_w_qkv, attn_sink, attn_w_o, mlp_w1, mlp_w2, final_g):
    L = x.shape[1]
    ang = axial_angles(L)
    silu_c = jax.nn.silu(c)
    silu_cc = jax.nn.silu(c_ctx)[None]
    for i in range(DEPTH):
        last = i == DEPTH - 1
        kind = i % N_MIXERS
        j = i // N_MIXERS
        mod = (silu_c @ ada_w[i] + ada_b[i])[:, None, :]
        mod_c = (silu_cc @ ada_w[i] + ada_b[i])[:, None, :]
        sh1, sc1, g1, sh2, sc2, g2 = jnp.split(mod, N_MOD, axis=-1)
        csh1, csc1, cg1, csh2, csc2, cg2 = jnp.split(mod_c, N_MOD, axis=-1)

        h = modulate(rmsnorm(x, norm_mix_g[i]), sh1, sc1)
        if kind == 0:
            y = pool_mixer(h, pool_w_in[j], pool_w_grp[j], pool_scale[j], pool_w_out[j])
            if not last:
                hc = modulate(rmsnorm(ctx, norm_mix_g[i]), csh1, csc1)
                yc = pool_mixer(hc, pool_w_in[j], pool_w_grp[j], pool_scale[j], pool_w_out[j])
        else:
            hc = modulate(rmsnorm(ctx, norm_mix_g[i]), csh1, csc1)
            y, yc = window_attention(h, hc, attn_w_qkv[j], attn_sink[j], attn_w_o[j], ang, not last)

        x = x + g1 * y
        h = modulate(rmsnorm(x, norm_mlp_g[i]), sh2, sc2)
        x = x + g2 * sq_relu_mlp(h, mlp_w1[i], mlp_w2[i])

        if not last:
            ctx = ctx + cg1 * yc
            hc = modulate(rmsnorm(ctx, norm_mlp_g[i]), csh2, csc2)
            ctx = ctx + cg2 * sq_relu_mlp(hc, mlp_w1[i], mlp_w2[i])
    return rmsnorm(x, final_g)
```

```python
import functools

import jax
import jax.numpy as jnp
from jax import lax
from jax.experimental import pallas as pl
from jax.experimental.pallas import tpu as pltpu

D_MODEL = 1024
DEPTH = 2
GRID_W = 64
POOL_WINDOWS = (2, 4, 8, 16)
POOL_GROUP_DIM = D_MODEL // len(POOL_WINDOWS)
POOL_HALO = 8
HEAD_DIM = 64
N_HEADS = D_MODEL // HEAD_DIM
N_KV_HEADS = 2
GQA_GROUP = N_HEADS // N_KV_HEADS
Q_DIM = N_HEADS * HEAD_DIM
KV_DIM = N_KV_HEADS * HEAD_DIM
WINDOW = 128
BLOCK = 128
ROPE_BASE = 10000.0
AXIS_ROT = HEAD_DIM // 2
D_FF = 4 * D_MODEL
N_MOD = 6
EPS = 1e-6
NEG = -1e30
MOD_ROWS = 8
LANES = 128
VMEM_LIMIT = 56 * 1024 * 1024

BF16 = jnp.bfloat16
F32 = jnp.float32


def _const_spec(shape):
    nd = len(shape)
    return pl.BlockSpec(shape, lambda *_: (0,) * nd, pipeline_mode=pl.Buffered(1))


def _dot(a, b):
    return jnp.dot(a, b, preferred_element_type=F32)


def _rms_mod(x, g, shift, scale):
    y = x * lax.rsqrt(jnp.mean(x * x, axis=-1, keepdims=True) + EPS)
    return (y * g) * (1.0 + scale) + shift


def _mlp_residual(x, mod_ref, g_ref, w1_ref, w2_ref):
    sh2 = mod_ref[:, 3 * D_MODEL:4 * D_MODEL]
    sc2 = mod_ref[:, 4 * D_MODEL:5 * D_MODEL]
    g2 = mod_ref[:, 5 * D_MODEL:6 * D_MODEL]
    h = _rms_mod(x, g_ref[...], sh2, sc2).astype(BF16)
    a = jnp.maximum(_dot(h, w1_ref[...]), 0.0)
    a = (a * a).astype(BF16)
    return x + g2 * _dot(a, w2_ref[...])


def _mod_kernel(c_ref, w_ref, b_ref, o_ref):
    c = c_ref[...]
    s = (c * jax.nn.sigmoid(c)).astype(BF16)
    r = _dot(s, w_ref[...].astype(BF16)) + b_ref[...]
    for row in range(MOD_ROWS):
        o_ref[row] = r[row:row + 1, :]


def _modulation(c8, ada_w, ada_b):
    tn = 1536
    n = N_MOD * D_MODEL
    return pl.pallas_call(
        _mod_kernel,
        grid=(DEPTH, n // tn),
        in_specs=[
            pl.BlockSpec((MOD_ROWS, D_MODEL), lambda i, j: (0, 0)),
            pl.BlockSpec((None, D_MODEL, tn), lambda i, j: (i, 0, j)),
            pl.BlockSpec((None, 1, tn), lambda i, j: (i, 0, j)),
        ],
        out_specs=pl.BlockSpec((None, MOD_ROWS, 1, tn), lambda i, j: (i, 0, 0, j)),
        out_shape=jax.ShapeDtypeStruct((DEPTH, MOD_ROWS, 1, n), F32),
        compiler_params=pltpu.CompilerParams(
            dimension_semantics=("arbitrary", "arbitrary"), vmem_limit_bytes=VMEM_LIMIT),
        name="modulation",
    )(c8, ada_w, ada_b.reshape(DEPTH, 1, n))


def _layer0_kernel(seq_len, tm, x_ref, xp_ref, xn_ref, mod_ref, gmix_ref, gmlp_ref, pscale_ref,
                   win_ref, wgrp_ref, wout_ref, w1_ref, w2_ref, o_ref, u_ref, d_ref):
    i = pl.program_id(1)
    sh1 = mod_ref[:, 0:D_MODEL]
    sc1 = mod_ref[:, D_MODEL:2 * D_MODEL]
    g1 = mod_ref[:, 2 * D_MODEL:3 * D_MODEL]
    gmix = gmix_ref[...]
    w_in = win_ref[...]

    def proj(xblk):
        return _dot(_rms_mod(xblk, gmix, sh1, sc1).astype(BF16), w_in)

    x = x_ref[...]
    u_ref[POOL_HALO:POOL_HALO + tm, :] = proj(x)
    up = proj(xp_ref[...])
    un = proj(xn_ref[...])
    u_ref[0:POOL_HALO, :] = jnp.where(i > 0, up, 0.0)
    u_ref[POOL_HALO + tm:POOL_HALO + tm + POOL_HALO, :] = jnp.where(
        (i + 1) * tm < seq_len, un, 0.0)

    t = i * tm + lax.broadcasted_iota(jnp.int32, (tm, 1), 0)
    for g, w in enumerate(POOL_WINDOWS):
        c0, c1 = g * POOL_GROUP_DIM, (g + 1) * POOL_GROUP_DIM
        acc = u_ref[POOL_HALO - w // 2:POOL_HALO - w // 2 + tm, c0:c1]
        for s in range(-w // 2 + 1, w // 2):
            acc = acc + u_ref[POOL_HALO + s:POOL_HALO + s + tm, c0:c1]
        lo = jnp.clip(t - w // 2, 0, seq_len)
        hi = jnp.clip(t + w // 2, 0, seq_len)
        inv_cnt = 1.0 / (hi - lo).astype(F32)
        d = acc * inv_cnt - u_ref[POOL_HALO:POOL_HALO + tm, c0:c1]
        d_ref[:, c0:c1] = _dot(d.astype(BF16), wgrp_ref[g])

    y = _dot((d_ref[...] * pscale_ref[...]).astype(BF16), wout_ref[...])
    x1 = x + g1 * y
    o_ref[...] = _mlp_residual(x1, mod_ref, gmlp_ref, w1_ref, w2_ref)


def _layer0(x, mod, mod_row, layer, tm, gmix, gmlp, pscale, w_in, w_grp, w_out, w1, w2):
    nb, seq_len, _ = x.shape
    rows8 = seq_len // POOL_HALO
    tpb = tm // POOL_HALO

    def mod_map(b, i):
        return (layer, b if mod_row is None else mod_row, 0, 0)

    return pl.pallas_call(
        functools.partial(_layer0_kernel, seq_len, tm),
        grid=(nb, seq_len // tm),
        in_specs=[
            pl.BlockSpec((None, tm, D_MODEL), lambda b, i: (b, i, 0)),
            pl.BlockSpec((None, POOL_HALO, D_MODEL), lambda b, i: (b, jnp.maximum(i * tpb - 1, 0), 0)),
            pl.BlockSpec((None, POOL_HALO, D_MODEL),
                         lambda b, i: (b, jnp.minimum((i + 1) * tpb, rows8 - 1), 0)),
            pl.BlockSpec((None, None, 1, N_MOD * D_MODEL), mod_map),
            _const_spec((1, D_MODEL)),
            _const_spec((1, D_MODEL)),
            _const_spec((1, D_MODEL)),
            _const_spec((D_MODEL, D_MODEL)),
            _const_spec((len(POOL_WINDOWS), POOL_GROUP_DIM, POOL_GROUP_DIM)),
            _const_spec((D_MODEL, D_MODEL)),
            _const_spec((D_MODEL, D_FF)),
            _const_spec((D_FF, D_MODEL)),
        ],
        out_specs=pl.BlockSpec((None, tm, D_MODEL), lambda b, i: (b, i, 0)),
        out_shape=jax.ShapeDtypeStruct(x.shape, F32),
        scratch_shapes=[pltpu.VMEM((tm + 2 * POOL_HALO, D_MODEL), F32),
                        pltpu.VMEM((tm, D_MODEL), F32)],
        compiler_params=pltpu.CompilerParams(
            dimension_semantics=("arbitrary", "arbitrary"), vmem_limit_bytes=VMEM_LIMIT),
        name="layer0",
    )(x, x, x, mod, gmix, gmlp, pscale, w_in, w_grp, w_out, w1, w2)


def _rope_block(t, cos, sin_signed, first_half):
    partner = jnp.where(first_half, pltpu.roll(t, LANES - AXIS_ROT // 2, 1),
                        pltpu.roll(t, AXIS_ROT // 2, 1))
    return t * cos + partner * sin_signed


def _qkv_kernel(x_ref, mod_ref, gmix_ref, w_ref, cos_ref, sin_ref, q_ref, kv_ref):
    sh1 = mod_ref[:, 0:D_MODEL]
    sc1 = mod_ref[:, D_MODEL:2 * D_MODEL]
    h = _rms_mod(x_ref[...], gmix_ref[...], sh1, sc1).astype(BF16)
    qkv = _dot(h, w_ref[...])
    cos = cos_ref[...]
    sin = sin_ref[...]
    lane = lax.broadcasted_iota(jnp.int32, cos.shape, 1)
    first_half = (lane % AXIS_ROT) < (AXIS_ROT // 2)
    qscale = HEAD_DIM ** -0.5
    for cb in range(Q_DIM // LANES):
        blk = _rope_block(qkv[:, cb * LANES:(cb + 1) * LANES], cos, sin, first_half)
        q_ref[:, cb * LANES:(cb + 1) * LANES] = (blk * qscale).astype(BF16)
    k = _rope_block(qkv[:, Q_DIM:Q_DIM + KV_DIM], cos, sin, first_half)
    kv_ref[:, 0:KV_DIM] = k.astype(BF16)
    kv_ref[:, KV_DIM:2 * KV_DIM] = qkv[:, Q_DIM + KV_DIM:].astype(BF16)


def _qkv(x, mod, layer, tm, gmix, w_qkv, cos, sin):
    nb, seq_len, _ = x.shape
    return pl.pallas_call(
        _qkv_kernel,
        grid=(nb, seq_len // tm),
        in_specs=[
            pl.BlockSpec((None, tm, D_MODEL), lambda b, i: (b, i, 0)),
            pl.BlockSpec((None, None, 1, N_MOD * D_MODEL), lambda b, i: (layer, b, 0, 0)),
            _const_spec((1, D_MODEL)),
            _const_spec((D_MODEL, Q_DIM + 2 * KV_DIM)),
            pl.BlockSpec((tm, LANES), lambda b, i: (i, 0)),
            pl.BlockSpec((tm, LANES), lambda b, i: (i, 0)),
        ],
        out_specs=[pl.BlockSpec((None, tm, Q_DIM), lambda b, i: (b, i, 0)),
                   pl.BlockSpec((None, tm, 2 * KV_DIM), lambda b, i: (b, i, 0))],
        out_shape=[jax.ShapeDtypeStruct((nb, seq_len, Q_DIM), BF16),
                   jax.ShapeDtypeStruct((nb, seq_len, 2 * KV_DIM), BF16)],
        compiler_params=pltpu.CompilerParams(
            dimension_semantics=("arbitrary", "arbitrary"), vmem_limit_bytes=VMEM_LIMIT),
        name="qkv_rope",
    )(x, mod, gmix, w_qkv, cos, sin)


def _ctx_kv_kernel(x_ref, mod_ref, gmix_ref, w_ref, kv_ref):
    sh1 = mod_ref[:, 0:D_MODEL]
    sc1 = mod_ref[:, D_MODEL:2 * D_MODEL]
    h = _rms_mod(x_ref[...], gmix_ref[...], sh1, sc1).astype(BF16)
    kv_ref[...] = _dot(h, w_ref[...]).astype(BF16)


def _ctx_kv(ctx, mod, mod_row, layer, gmix, w_kv):
    nb, ctx_len, _ = ctx.shape
    return pl.pallas_call(
        _ctx_kv_kernel,
        grid=(nb,),
        in_specs=[
            pl.BlockSpec((None, ctx_len, D_MODEL), lambda b: (b, 0, 0)),
            pl.BlockSpec((None, None, 1, N_MOD * D_MODEL), lambda b: (layer, mod_row, 0, 0)),
            _const_spec((1, D_MODEL)),
            _const_spec((D_MODEL, 2 * KV_DIM)),
        ],
        out_specs=pl.BlockSpec((None, ctx_len, 2 * KV_DIM), lambda b: (b, 0, 0)),
        out_shape=jax.ShapeDtypeStruct((nb, ctx_len, 2 * KV_DIM), BF16),
        compiler_params=pltpu.CompilerParams(
            dimension_semantics=("arbitrary",), vmem_limit_bytes=VMEM_LIMIT),
        name="ctx_kv",
    )(ctx, mod, gmix, w_kv)


def _dot_nt(a, b):
    return lax.dot_general(a, b, (((1,), (1,)), ((), ())), preferred_element_type=F32)


def _attn_kernel(seq_len, tq, sink_ref, q_ref, kvp_ref, kvm_ref, kvn_ref, kvc_ref, o_ref, kv_ext):
    i = pl.program_id(1)
    kv_ext[0:BLOCK, :] = kvp_ref[...]
    kv_ext[BLOCK:BLOCK + tq, :] = kvm_ref[...]
    kv_ext[BLOCK + tq:2 * BLOCK + tq, :] = kvn_ref[...]
    kvc = kvc_ref[...]
    nwin = 3 * BLOCK

    def body(j, carry):
        q0 = pl.multiple_of(j * BLOCK, BLOCK)
        qt = q_ref[pl.ds(q0, BLOCK), :]
        kvw = kv_ext[pl.ds(q0, nwin), :]
        qpos = i * tq + j * BLOCK + lax.broadcasted_iota(jnp.int32, (BLOCK, nwin), 0)
        kpos = i * tq + (j - 1) * BLOCK + lax.broadcasted_iota(jnp.int32, (BLOCK, nwin), 1)
        valid = (kpos >= 0) & (kpos < seq_len) & (jnp.abs(qpos - kpos) <= WINDOW)
        for kvh in range(N_KV_HEADS):
            kw = kvw[:, kvh * HEAD_DIM:(kvh + 1) * HEAD_DIM]
            vw = kvw[:, KV_DIM + kvh * HEAD_DIM:KV_DIM + (kvh + 1) * HEAD_DIM]
            kc = kvc[:, kvh * HEAD_DIM:(kvh + 1) * HEAD_DIM]
            vc = kvc[:, KV_DIM + kvh * HEAD_DIM:KV_DIM + (kvh + 1) * HEAD_DIM]
            heads = [kvh * GQA_GROUP + g for g in range(GQA_GROUP)]
            qs = jnp.concatenate([qt[:, h * HEAD_DIM:(h + 1) * HEAD_DIM] for h in heads], axis=0)
            s_w = _dot_nt(qs, kw)
            s_c = _dot_nt(qs, kc)
            e_w, e_c, inv_den = [], [], []
            for g, h in enumerate(heads):
                sw = jnp.where(valid, s_w[g * BLOCK:(g + 1) * BLOCK, :], NEG)
                sc = s_c[g * BLOCK:(g + 1) * BLOCK, :]
                sk = sink_ref[h]
                m = jnp.maximum(jnp.max(sw, axis=1, keepdims=True), jnp.max(sc, axis=1, keepdims=True))
                m = jnp.maximum(m, sk)
                ew = jnp.exp(sw - m)
                ec = jnp.exp(sc - m)
                den = (jnp.sum(ew, axis=1, keepdims=True) + jnp.sum(ec, axis=1, keepdims=True)
                       + jnp.exp(sk - m))
                e_w.append(ew.astype(BF16))
                e_c.append(ec.astype(BF16))
                inv_den.append(1.0 / den)
            o = _dot(jnp.concatenate(e_w, axis=0), vw) + _dot(jnp.concatenate(e_c, axis=0), vc)
            for g, h in enumerate(heads):
                og = o[g * BLOCK:(g + 1) * BLOCK, :] * inv_den[g]
                o_ref[pl.ds(q0, BLOCK), h * HEAD_DIM:(h + 1) * HEAD_DIM] = og.astype(BF16)
        return carry

    lax.fori_loop(0, tq // BLOCK, body, 0)


def _attention(q, kv, kvc, sink, tq):
    nb, seq_len, _ = q.shape
    ctx_len = kvc.shape[1]
    nblk = seq_len // BLOCK
    qpb = tq // BLOCK
    return pl.pallas_call(
        functools.partial(_attn_kernel, seq_len, tq),
        grid=(nb, seq_len // tq),
        in_specs=[
            pl.BlockSpec(memory_space=pltpu.SMEM),
            pl.BlockSpec((None, tq, Q_DIM), lambda b, i: (b, i, 0)),
            pl.BlockSpec((None, BLOCK, 2 * KV_DIM), lambda b, i: (b, jnp.maximum(i * qpb - 1, 0), 0)),
            pl.BlockSpec((None, tq, 2 * KV_DIM), lambda b, i: (b, i, 0)),
            pl.BlockSpec((None, BLOCK, 2 * KV_DIM),
                         lambda b, i: (b, jnp.minimum((i + 1) * qpb, nblk - 1), 0)),
            pl.BlockSpec((None, ctx_len, 2 * KV_DIM), lambda b, i: (b, 0, 0)),
        ],
        out_specs=pl.BlockSpec((None, tq, Q_DIM), lambda b, i: (b, i, 0)),
        out_shape=jax.ShapeDtypeStruct((nb, seq_len, Q_DIM), BF16),
        scratch_shapes=[pltpu.VMEM((tq + 2 * BLOCK, 2 * KV_DIM), BF16)],
        compiler_params=pltpu.CompilerParams(
            dimension_semantics=("arbitrary", "arbitrary"), vmem_limit_bytes=VMEM_LIMIT),
        name="attention",
    )(sink, q, kv, kv, kv, kvc)


def _post_kernel(x_ref, o_ref_in, mod_ref, gmlp_ref, gfin_ref, wo_ref, w1_ref, w2_ref, out_ref):
    g1 = mod_ref[:, 2 * D_MODEL:3 * D_MODEL]
    x1 = x_ref[...] + g1 * _dot(o_ref_in[...], wo_ref[...])
    x2 = _mlp_residual(x1, mod_ref, gmlp_ref, w1_ref, w2_ref)
    y = x2 * lax.rsqrt(jnp.mean(x2 * x2, axis=-1, keepdims=True) + EPS)
    out_ref[...] = y * gfin_ref[...]


def _post(x, o, mod, layer, tm, gmlp, gfin, w_o, w1, w2):
    nb, seq_len, _ = x.shape
    return pl.pallas_call(
        _post_kernel,
        grid=(nb, seq_len // tm),
        in_specs=[
            pl.BlockSpec((None, tm, D_MODEL), lambda b, i: (b, i, 0)),
            pl.BlockSpec((None, tm, Q_DIM), lambda b, i: (b, i, 0)),
            pl.BlockSpec((None, None, 1, N_MOD * D_MODEL), lambda b, i: (layer, b, 0, 0)),
            _const_spec((1, D_MODEL)),
            _const_spec((1, D_MODEL)),
            _const_spec((Q_DIM, D_MODEL)),
            _const_spec((D_MODEL, D_FF)),
            _const_spec((D_FF, D_MODEL)),
        ],
        out_specs=pl.BlockSpec((None, tm, D_MODEL), lambda b, i: (b, i, 0)),
        out_shape=jax.ShapeDtypeStruct(x.shape, F32),
        compiler_params=pltpu.CompilerParams(
            dimension_semantics=("arbitrary", "arbitrary"), vmem_limit_bytes=VMEM_LIMIT),
        name="post_attn",
    )(x, o, mod, gmlp, gfin, w_o, w1, w2)


def _rope_tables(seq_len):
    rows = seq_len // GRID_W
    row = jnp.repeat(jnp.arange(rows), GRID_W).astype(F32)
    col = jnp.tile(jnp.arange(GRID_W), rows).astype(F32)
    inv = ROPE_BASE ** (-jnp.arange(0, AXIS_ROT, 2, dtype=F32) / AXIS_ROT)
    ang_r = row[:, None] * inv[None]
    ang_c = col[:, None] * inv[None]
    cos = jnp.concatenate([jnp.cos(ang_r)] * 2 + [jnp.cos(ang_c)] * 2, axis=-1)
    sin = jnp.concatenate([-jnp.sin(ang_r), jnp.sin(ang_r), -jnp.sin(ang_c), jnp.sin(ang_c)], axis=-1)
    reps = LANES // HEAD_DIM
    return jnp.tile(cos, (1, reps)), jnp.tile(sin, (1, reps))


def kernel(x, c, ctx, c_ctx, ada_w, ada_b, norm_mix_g, norm_mlp_g, pool_w_in, pool_w_grp,
           pool_scale, pool_w_out, attn_w_qkv, attn_sink, attn_w_o, mlp_w1, mlp_w2, final_g):
    nb, seq_len, _ = x.shape
    ctx_len = ctx.shape[1]
    ctx_row = nb
    c8 = jnp.concatenate(
        [c, c_ctx[None], jnp.zeros((MOD_ROWS - nb - 1, D_MODEL), F32)], axis=0)
    mod = _modulation(c8, ada_w, ada_b)

    gmix = norm_mix_g.reshape(DEPTH, 1, D_MODEL)
    gmlp = norm_mlp_g.reshape(DEPTH, 1, D_MODEL)
    w1 = mlp_w1.astype(BF16)
    w2 = mlp_w2.astype(BF16)

    l0 = (gmix[0], gmlp[0], pool_scale[0].reshape(1, D_MODEL), pool_w_in[0].astype(BF16),
          pool_w_grp[0].astype(BF16), pool_w_out[0].astype(BF16), w1[0], w2[0])
    x1 = _layer0(x, mod, None, 0, 512, *l0)
    ctx1 = _layer0(ctx, mod, ctx_row, 0, ctx_len, *l0)

    w_qkv = attn_w_qkv[0].astype(BF16)
    cos, sin = _rope_tables(seq_len)
    q, kv = _qkv(x1, mod, 1, 512, gmix[1], w_qkv, cos, sin)
    kvc = _ctx_kv(ctx1, mod, ctx_row, 1, gmix[1], w_qkv[:, Q_DIM:])
    o = _attention(q, kv, kvc, attn_sink[0], 512)
    return _post(x1, o, mod, 1, 512, gmlp[1], final_g.reshape(1, D_MODEL),
                 attn_w_o[0].astype(BF16), w1[1], w2[1])
```

```python
import functools

import jax
import jax.numpy as jnp
from jax import lax
from jax.experimental import pallas as pl
from jax.experimental.pallas import tpu as pltpu

D_MODEL = 1024
DEPTH = 2
GRID_W = 64
POOL_WINDOWS = (2, 4, 8, 16)
POOL_GROUP_DIM = D_MODEL // len(POOL_WINDOWS)
POOL_HALO = 8
HEAD_DIM = 64
N_HEADS = D_MODEL // HEAD_DIM
N_KV_HEADS = 2
GQA_GROUP = N_HEADS // N_KV_HEADS
Q_DIM = N_HEADS * HEAD_DIM
KV_DIM = N_KV_HEADS * HEAD_DIM
WINDOW = 128
BLOCK = 128
ROPE_BASE = 10000.0
AXIS_ROT = HEAD_DIM // 2
D_FF = 4 * D_MODEL
N_MOD = 6
EPS = 1e-6
NEG = -1e30
MOD_ROWS = 8
LANES = 128
LOG2E = 1.4426950408889634
KV_W = 3 * LANES
SOFTMAX_ROWS = 32
VMEM_LIMIT = 56 * 1024 * 1024

BF16 = jnp.bfloat16
F32 = jnp.float32


def _const_spec(shape):
    nd = len(shape)
    return pl.BlockSpec(shape, lambda *_: (0,) * nd, pipeline_mode=pl.Buffered(1))


def _dot(a, b):
    return jnp.dot(a, b, preferred_element_type=F32)


def _rms_mod(x, g, shift, scale):
    y = x * lax.rsqrt(jnp.mean(x * x, axis=-1, keepdims=True) + EPS)
    return (y * g) * (1.0 + scale) + shift


def _mlp_residual(x, mod_ref, g_ref, w1_ref, w2_ref):
    sh2 = mod_ref[:, 3 * D_MODEL:4 * D_MODEL]
    sc2 = mod_ref[:, 4 * D_MODEL:5 * D_MODEL]
    g2 = mod_ref[:, 5 * D_MODEL:6 * D_MODEL]
    h = _rms_mod(x, g_ref[...], sh2, sc2).astype(BF16)
    a = jnp.maximum(_dot(h, w1_ref[...]), 0.0)
    a = (a * a).astype(BF16)
    return x + g2 * _dot(a, w2_ref[...])


def _mod_kernel(c_ref, w_ref, b_ref, o_ref):
    c = c_ref[...]
    s = (c * jax.nn.sigmoid(c)).astype(BF16)
    r = _dot(s, w_ref[...].astype(BF16)) + b_ref[...]
    for row in range(MOD_ROWS):
        o_ref[row] = r[row:row + 1, :]


def _modulation(c8, ada_w, ada_b):
    tn = 1536
    n = N_MOD * D_MODEL
    return pl.pallas_call(
        _mod_kernel,
        grid=(DEPTH, n // tn),
        in_specs=[
            pl.BlockSpec((MOD_ROWS, D_MODEL), lambda i, j: (0, 0)),
            pl.BlockSpec((None, D_MODEL, tn), lambda i, j: (i, 0, j)),
            pl.BlockSpec((None, 1, tn), lambda i, j: (i, 0, j)),
        ],
        out_specs=pl.BlockSpec((None, MOD_ROWS, 1, tn), lambda i, j: (i, 0, 0, j)),
        out_shape=jax.ShapeDtypeStruct((DEPTH, MOD_ROWS, 1, n), F32),
        compiler_params=pltpu.CompilerParams(
            dimension_semantics=("arbitrary", "arbitrary"), vmem_limit_bytes=VMEM_LIMIT),
        name="modulation",
    )(c8, ada_w, ada_b.reshape(DEPTH, 1, n))


def _layer0_kernel(seq_len, tm, x_ref, xp_ref, xn_ref, mod_ref, gmix_ref, gmlp_ref, pscale_ref,
                   win_ref, wgrp_ref, wout_ref, w1_ref, w2_ref, o_ref, u_ref, d_ref):
    i = pl.program_id(1)
    sh1 = mod_ref[:, 0:D_MODEL]
    sc1 = mod_ref[:, D_MODEL:2 * D_MODEL]
    g1 = mod_ref[:, 2 * D_MODEL:3 * D_MODEL]
    gmix = gmix_ref[...]
    w_in = win_ref[...]

    def proj(xblk):
        return _dot(_rms_mod(xblk, gmix, sh1, sc1).astype(BF16), w_in)

    x = x_ref[...]
    u_ref[POOL_HALO:POOL_HALO + tm, :] = proj(x)
    up = proj(xp_ref[...])
    un = proj(xn_ref[...])
    u_ref[0:POOL_HALO, :] = jnp.where(i > 0, up, 0.0)
    u_ref[POOL_HALO + tm:POOL_HALO + tm + POOL_HALO, :] = jnp.where(
        (i + 1) * tm < seq_len, un, 0.0)

    t = i * tm + lax.broadcasted_iota(jnp.int32, (tm, 1), 0)
    for g, w in enumerate(POOL_WINDOWS):
        c0, c1 = g * POOL_GROUP_DIM, (g + 1) * POOL_GROUP_DIM
        acc = u_ref[POOL_HALO - w // 2:POOL_HALO - w // 2 + tm, c0:c1]
        for s in range(-w // 2 + 1, w // 2):
            acc = acc + u_ref[POOL_HALO + s:POOL_HALO + s + tm, c0:c1]
        lo = jnp.clip(t - w // 2, 0, seq_len)
        hi = jnp.clip(t + w // 2, 0, seq_len)
        inv_cnt = 1.0 / (hi - lo).astype(F32)
        d = acc * inv_cnt - u_ref[POOL_HALO:POOL_HALO + tm, c0:c1]
        d_ref[:, c0:c1] = _dot(d.astype(BF16), wgrp_ref[g])

    y = _dot((d_ref[...] * pscale_ref[...]).astype(BF16), wout_ref[...])
    x1 = x + g1 * y
    o_ref[...] = _mlp_residual(x1, mod_ref, gmlp_ref, w1_ref, w2_ref)


def _layer0(x, mod, mod_row, layer, tm, gmix, gmlp, pscale, w_in, w_grp, w_out, w1, w2):
    nb, seq_len, _ = x.shape
    rows8 = seq_len // POOL_HALO
    tpb = tm // POOL_HALO

    def mod_map(b, i):
        return (layer, b if mod_row is None else mod_row, 0, 0)

    return pl.pallas_call(
        functools.partial(_layer0_kernel, seq_len, tm),
        grid=(nb, seq_len // tm),
        in_specs=[
            pl.BlockSpec((None, tm, D_MODEL), lambda b, i: (b, i, 0)),
            pl.BlockSpec((None, POOL_HALO, D_MODEL), lambda b, i: (b, jnp.maximum(i * tpb - 1, 0), 0)),
            pl.BlockSpec((None, POOL_HALO, D_MODEL),
                         lambda b, i: (b, jnp.minimum((i + 1) * tpb, rows8 - 1), 0)),
            pl.BlockSpec((None, None, 1, N_MOD * D_MODEL), mod_map),
            _const_spec((1, D_MODEL)),
            _const_spec((1, D_MODEL)),
            _const_spec((1, D_MODEL)),
            _const_spec((D_MODEL, D_MODEL)),
            _const_spec((len(POOL_WINDOWS), POOL_GROUP_DIM, POOL_GROUP_DIM)),
            _const_spec((D_MODEL, D_MODEL)),
            _const_spec((D_MODEL, D_FF)),
            _const_spec((D_FF, D_MODEL)),
        ],
        out_specs=pl.BlockSpec((None, tm, D_MODEL), lambda b, i: (b, i, 0)),
        out_shape=jax.ShapeDtypeStruct(x.shape, F32),
        scratch_shapes=[pltpu.VMEM((tm + 2 * POOL_HALO, D_MODEL), F32),
                        pltpu.VMEM((tm, D_MODEL), F32)],
        compiler_params=pltpu.CompilerParams(
            dimension_semantics=("arbitrary", "arbitrary"), vmem_limit_bytes=VMEM_LIMIT),
        name="layer0",
    )(x, x, x, mod, gmix, gmlp, pscale, w_in, w_grp, w_out, w1, w2)


def _rope_block(t, cos, sin_signed, first_half):
    partner = jnp.where(first_half, pltpu.roll(t, LANES - AXIS_ROT // 2, 1),
                        pltpu.roll(t, AXIS_ROT // 2, 1))
    return t * cos + partner * sin_signed


def _store_kv(kv_ref, k, v):
    low = lax.broadcasted_iota(jnp.int32, v.shape, 1) < HEAD_DIM
    kv_ref[:, 0:KV_DIM] = k.astype(BF16)
    kv_ref[:, LANES:2 * LANES] = jnp.where(low, v, 1.0).astype(BF16)
    kv_ref[:, 2 * LANES:3 * LANES] = jnp.where(low, pltpu.roll(v, HEAD_DIM, 1), 1.0).astype(BF16)


def _qkv_kernel(x_ref, mod_ref, gmix_ref, w_ref, cos_ref, sin_ref, q_ref, kv_ref):
    sh1 = mod_ref[:, 0:D_MODEL]
    sc1 = mod_ref[:, D_MODEL:2 * D_MODEL]
    h = _rms_mod(x_ref[...], gmix_ref[...], sh1, sc1).astype(BF16)
    qkv = _dot(h, w_ref[...])
    cos = cos_ref[...]
    sin = sin_ref[...]
    lane = lax.broadcasted_iota(jnp.int32, cos.shape, 1)
    first_half = (lane % AXIS_ROT) < (AXIS_ROT // 2)
    qscale = HEAD_DIM ** -0.5 * LOG2E
    for cb in range(Q_DIM // LANES):
        blk = _rope_block(qkv[:, cb * LANES:(cb + 1) * LANES], cos, sin, first_half)
        q_ref[:, cb * LANES:(cb + 1) * LANES] = (blk * qscale).astype(BF16)
    k = _rope_block(qkv[:, Q_DIM:Q_DIM + KV_DIM], cos, sin, first_half)
    _store_kv(kv_ref, k, qkv[:, Q_DIM + KV_DIM:])


def _qkv(x, mod, layer, tm, gmix, w_qkv, cos, sin):
    nb, seq_len, _ = x.shape
    return pl.pallas_call(
        _qkv_kernel,
        grid=(nb, seq_len // tm),
        in_specs=[
            pl.BlockSpec((None, tm, D_MODEL), lambda b, i: (b, i, 0)),
            pl.BlockSpec((None, None, 1, N_MOD * D_MODEL), lambda b, i: (layer, b, 0, 0)),
            _const_spec((1, D_MODEL)),
            _const_spec((D_MODEL, Q_DIM + 2 * KV_DIM)),
            pl.BlockSpec((tm, LANES), lambda b, i: (i, 0)),
            pl.BlockSpec((tm, LANES), lambda b, i: (i, 0)),
        ],
        out_specs=[pl.BlockSpec((None, tm, Q_DIM), lambda b, i: (b, i, 0)),
                   pl.BlockSpec((None, tm, KV_W), lambda b, i: (b, i, 0))],
        out_shape=[jax.ShapeDtypeStruct((nb, seq_len, Q_DIM), BF16),
                   jax.ShapeDtypeStruct((nb, seq_len, KV_W), BF16)],
        compiler_params=pltpu.CompilerParams(
            dimension_semantics=("arbitrary", "arbitrary"), vmem_limit_bytes=VMEM_LIMIT),
        name="qkv_rope",
    )(x, mod, gmix, w_qkv, cos, sin)


def _ctx_kv_kernel(x_ref, mod_ref, gmix_ref, w_ref, kv_ref):
    sh1 = mod_ref[:, 0:D_MODEL]
    sc1 = mod_ref[:, D_MODEL:2 * D_MODEL]
    h = _rms_mod(x_ref[...], gmix_ref[...], sh1, sc1).astype(BF16)
    kv = _dot(h, w_ref[...])
    _store_kv(kv_ref, kv[:, 0:KV_DIM], kv[:, KV_DIM:])


def _ctx_kv(ctx, mod, mod_row, layer, gmix, w_kv):
    nb, ctx_len, _ = ctx.shape
    return pl.pallas_call(
        _ctx_kv_kernel,
        grid=(nb,),
        in_specs=[
            pl.BlockSpec((None, ctx_len, D_MODEL), lambda b: (b, 0, 0)),
            pl.BlockSpec((None, None, 1, N_MOD * D_MODEL), lambda b: (layer, mod_row, 0, 0)),
            _const_spec((1, D_MODEL)),
            _const_spec((D_MODEL, 2 * KV_DIM)),
        ],
        out_specs=pl.BlockSpec((None, ctx_len, KV_W), lambda b: (b, 0, 0)),
        out_shape=jax.ShapeDtypeStruct((nb, ctx_len, KV_W), BF16),
        compiler_params=pltpu.CompilerParams(
            dimension_semantics=("arbitrary",), vmem_limit_bytes=VMEM_LIMIT),
        name="ctx_kv",
    )(ctx, mod, gmix, w_kv)


def _dot_nt(a, b):
    return lax.dot_general(a, b, (((1,), (1,)), ((), ())), preferred_element_type=F32)


def _attn_kernel(seq_len, tq, ctx_len, sink_ref, q_ref, kvp_ref, kvm_ref, kvn_ref, kvc_ref, o_ref,
                 kv_ext, s_ref, e_ref, es_ref):
    i = pl.program_id(1)
    kv_ext[0:BLOCK, :] = kvp_ref[...]
    kv_ext[BLOCK:BLOCK + tq, :] = kvm_ref[...]
    kv_ext[BLOCK + tq:2 * BLOCK + tq, :] = kvn_ref[...]
    nwin = 3 * BLOCK
    nkey = nwin + ctx_len
    row = lax.broadcasted_iota(jnp.int32, (BLOCK, BLOCK), 0)
    col = lax.broadcasted_iota(jnp.int32, (BLOCK, BLOCK), 1)
    low = lax.broadcasted_iota(jnp.int32, (BLOCK, LANES), 1) < HEAD_DIM

    def body(j, carry):
        q0 = pl.multiple_of(j * BLOCK, BLOCK)
        qt = q_ref[pl.ds(q0, BLOCK), :]
        mask_l = (col >= row) & (i * tq + (j - 1) * BLOCK >= 0)
        mask_r = (col <= row) & (i * tq + (j + 1) * BLOCK < seq_len)

        for kvh in range(N_KV_HEADS):
            heads = [kvh * GQA_GROUP + g for g in range(GQA_GROUP)]
            qs = jnp.concatenate([qt[:, h * HEAD_DIM:(h + 1) * HEAD_DIM] for h in heads], axis=0)
            kw = kv_ext[pl.ds(q0, nwin), kvh * HEAD_DIM:(kvh + 1) * HEAD_DIM]
            s_ref[kvh, :, 0:nwin] = _dot_nt(qs, kw)
            s_ref[kvh, :, nwin:nkey] = _dot_nt(qs, kvc_ref[:, kvh * HEAD_DIM:(kvh + 1) * HEAD_DIM])

        for kvh in range(N_KV_HEADS):
            for g in range(GQA_GROUP):
                sk = sink_ref[kvh * GQA_GROUP + g] * LOG2E
                for t0 in range(0, BLOCK, SOFTMAX_ROWS):
                    r0 = g * BLOCK + t0
                    rows = slice(r0, r0 + SOFTMAX_ROWS)
                    trows = slice(t0, t0 + SOFTMAX_ROWS)
                    pieces = [s_ref[kvh, rows, p * LANES:(p + 1) * LANES] for p in range(nkey // LANES)]
                    pieces[0] = jnp.where(mask_l[trows], pieces[0], NEG)
                    pieces[2] = jnp.where(mask_r[trows], pieces[2], NEG)
                    m = pieces[0]
                    for p in pieces[1:]:
                        m = jnp.maximum(m, p)
                    m = jnp.maximum(jnp.max(m, axis=1, keepdims=True), sk)
                    for p, piece in enumerate(pieces):
                        e_ref[kvh, rows, p * LANES:(p + 1) * LANES] = jnp.exp2(piece - m).astype(BF16)
                    es_ref[kvh, rows, :] = jnp.broadcast_to(jnp.exp2(sk - m), (SOFTMAX_ROWS, LANES))
            vw = kv_ext[pl.ds(q0, nwin), (1 + kvh) * LANES:(2 + kvh) * LANES]
            vc = kvc_ref[:, (1 + kvh) * LANES:(2 + kvh) * LANES]
            o = _dot(e_ref[kvh, :, 0:nwin], vw) + _dot(e_ref[kvh, :, nwin:nkey], vc)
            inv = pltpu.roll(1.0 / (o + es_ref[kvh]), HEAD_DIM, 1)
            on = o * inv
            for pair in range(GQA_GROUP // 2):
                even = on[(2 * pair) * BLOCK:(2 * pair + 1) * BLOCK]
                odd = on[(2 * pair + 1) * BLOCK:(2 * pair + 2) * BLOCK]
                tile = jnp.where(low, even, pltpu.roll(odd, HEAD_DIM, 1))
                c0 = (kvh * GQA_GROUP + 2 * pair) * HEAD_DIM
                o_ref[pl.ds(q0, BLOCK), c0:c0 + LANES] = tile.astype(BF16)
        return carry

    lax.fori_loop(0, tq // BLOCK, body, 0)


def _attention(q, kv, kvc, sink, tq):
    nb, seq_len, _ = q.shape
    ctx_len = kvc.shape[1]
    nblk = seq_len // BLOCK
    qpb = tq // BLOCK
    rows = GQA_GROUP * BLOCK
    nkey = 3 * BLOCK + ctx_len
    return pl.pallas_call(
        functools.partial(_attn_kernel, seq_len, tq, ctx_len),
        grid=(nb, seq_len // tq),
        in_specs=[
            pl.BlockSpec(memory_space=pltpu.SMEM),
            pl.BlockSpec((None, tq, Q_DIM), lambda b, i: (b, i, 0)),
            pl.BlockSpec((None, BLOCK, KV_W), lambda b, i: (b, jnp.maximum(i * qpb - 1, 0), 0)),
            pl.BlockSpec((None, tq, KV_W), lambda b, i: (b, i, 0)),
            pl.BlockSpec((None, BLOCK, KV_W),
                         lambda b, i: (b, jnp.minimum((i + 1) * qpb, nblk - 1), 0)),
            pl.BlockSpec((None, ctx_len, KV_W), lambda b, i: (b, 0, 0)),
        ],
        out_specs=pl.BlockSpec((None, tq, Q_DIM), lambda b, i: (b, i, 0)),
        out_shape=jax.ShapeDtypeStruct((nb, seq_len, Q_DIM), BF16),
        scratch_shapes=[pltpu.VMEM((tq + 2 * BLOCK, KV_W), BF16),
                        pltpu.VMEM((N_KV_HEADS, rows, nkey), F32),
                        pltpu.VMEM((N_KV_HEADS, rows, nkey), BF16),
                        pltpu.VMEM((N_KV_HEADS, rows, LANES), F32)],
        compiler_params=pltpu.CompilerParams(
            dimension_semantics=("arbitrary", "arbitrary"), vmem_limit_bytes=VMEM_LIMIT),
        name="attention",
    )(sink, q, kv, kv, kv, kvc)


def _post_kernel(x_ref, o_ref_in, mod_ref, gmlp_ref, gfin_ref, wo_ref, w1_ref, w2_ref, out_ref):
    g1 = mod_ref[:, 2 * D_MODEL:3 * D_MODEL]
    x1 = x_ref[...] + g1 * _dot(o_ref_in[...], wo_ref[...])
    x2 = _mlp_residual(x1, mod_ref, gmlp_ref, w1_ref, w2_ref)
    y = x2 * lax.rsqrt(jnp.mean(x2 * x2, axis=-1, keepdims=True) + EPS)
    out_ref[...] = y * gfin_ref[...]


def _post(x, o, mod, layer, tm, gmlp, gfin, w_o, w1, w2):
    nb, seq_len, _ = x.shape
    return pl.pallas_call(
        _post_kernel,
        grid=(nb, seq_len // tm),
        in_specs=[
            pl.BlockSpec((None, tm, D_MODEL), lambda b, i: (b, i, 0)),
            pl.BlockSpec((None, tm, Q_DIM), lambda b, i: (b, i, 0)),
            pl.BlockSpec((None, None, 1, N_MOD * D_MODEL), lambda b, i: (layer, b, 0, 0)),
            _const_spec((1, D_MODEL)),
            _const_spec((1, D_MODEL)),
            _const_spec((Q_DIM, D_MODEL)),
            _const_spec((D_MODEL, D_FF)),
            _const_spec((D_FF, D_MODEL)),
        ],
        out_specs=pl.BlockSpec((None, tm, D_MODEL), lambda b, i: (b, i, 0)),
        out_shape=jax.ShapeDtypeStruct(x.shape, F32),
        compiler_params=pltpu.CompilerParams(
            dimension_semantics=("arbitrary", "arbitrary"), vmem_limit_bytes=VMEM_LIMIT),
        name="post_attn",
    )(x, o, mod, gmlp, gfin, w_o, w1, w2)


def _rope_tables(seq_len):
    rows = seq_len // GRID_W
    row = jnp.repeat(jnp.arange(rows), GRID_W).astype(F32)
    col = jnp.tile(jnp.arange(GRID_W), rows).astype(F32)
    inv = ROPE_BASE ** (-jnp.arange(0, AXIS_ROT, 2, dtype=F32) / AXIS_ROT)
    ang_r = row[:, None] * inv[None]
    ang_c = col[:, None] * inv[None]
    cos = jnp.concatenate([jnp.cos(ang_r)] * 2 + [jnp.cos(ang_c)] * 2, axis=-1)
    sin = jnp.concatenate([-jnp.sin(ang_r), jnp.sin(ang_r), -jnp.sin(ang_c), jnp.sin(ang_c)], axis=-1)
    reps = LANES // HEAD_DIM
    return jnp.tile(cos, (1, reps)), jnp.tile(sin, (1, reps))


def kernel(x, c, ctx, c_ctx, ada_w, ada_b, norm_mix_g, norm_mlp_g, pool_w_in, pool_w_grp,
           pool_scale, pool_w_out, attn_w_qkv, attn_sink, attn_w_o, mlp_w1, mlp_w2, final_g):
    nb, seq_len, _ = x.shape
    ctx_len = ctx.shape[1]
    ctx_row = nb
    c8 = jnp.concatenate(
        [c, c_ctx[None], jnp.zeros((MOD_ROWS - nb - 1, D_MODEL), F32)], axis=0)
    mod = _modulation(c8, ada_w, ada_b)

    gmix = norm_mix_g.reshape(DEPTH, 1, D_MODEL)
    gmlp = norm_mlp_g.reshape(DEPTH, 1, D_MODEL)
    w1 = mlp_w1.astype(BF16)
    w2 = mlp_w2.astype(BF16)

    l0 = (gmix[0], gmlp[0], pool_scale[0].reshape(1, D_MODEL), pool_w_in[0].astype(BF16),
          pool_w_grp[0].astype(BF16), pool_w_out[0].astype(BF16), w1[0], w2[0])
    x1 = _layer0(x, mod, None, 0, 512, *l0)
    ctx1 = _layer0(ctx, mod, ctx_row, 0, ctx_len, *l0)

    w_qkv = attn_w_qkv[0].astype(BF16)
    cos, sin = _rope_tables(seq_len)
    q, kv = _qkv(x1, mod, 1, 512, gmix[1], w_qkv, cos, sin)
    kvc = _ctx_kv(ctx1, mod, ctx_row, 1, gmix[1], w_qkv[:, Q_DIM:])
    o = _attention(q, kv, kvc, attn_sink[0], 512)
    return _post(x1, o, mod, 1, 512, gmlp[1], final_g.reshape(1, D_MODEL),
                 attn_w_o[0].astype(BF16), w1[1], w2[1])
```

```python
import functools

import jax
import jax.numpy as jnp
from jax import lax
from jax.experimental import pallas as pl
from jax.experimental.pallas import tpu as pltpu

D_MODEL = 1024
DEPTH = 2
GRID_W = 64
POOL_WINDOWS = (2, 4, 8, 16)
POOL_GROUP_DIM = D_MODEL // len(POOL_WINDOWS)
POOL_HALO = 8
HEAD_DIM = 64
N_HEADS = D_MODEL // HEAD_DIM
N_KV_HEADS = 2
GQA_GROUP = N_HEADS // N_KV_HEADS
Q_DIM = N_HEADS * HEAD_DIM
KV_DIM = N_KV_HEADS * HEAD_DIM
WINDOW = 128
BLOCK = 128
ROPE_BASE = 10000.0
AXIS_ROT = HEAD_DIM // 2
D_FF = 4 * D_MODEL
N_MOD = 6
EPS = 1e-6
NEG = -1e30
MOD_ROWS = 8
LANES = 128
LOG2E = 1.4426950408889634
KV_W = 3 * LANES
SOFTMAX_ROWS = 32
VMEM_LIMIT = 56 * 1024 * 1024

BF16 = jnp.bfloat16
F32 = jnp.float32


def _const_spec(shape):
    nd = len(shape)
    return pl.BlockSpec(shape, lambda *_: (0,) * nd, pipeline_mode=pl.Buffered(1))


def _dot(a, b):
    return jnp.dot(a, b, preferred_element_type=F32)


def _rms_mod(x, g, shift, scale):
    y = x * lax.rsqrt(jnp.mean(x * x, axis=-1, keepdims=True) + EPS)
    return (y * g) * (1.0 + scale) + shift


def _mlp_residual(x, mod_ref, g_ref, w1_ref, w2_ref):
    sh2 = mod_ref[:, 3 * D_MODEL:4 * D_MODEL]
    sc2 = mod_ref[:, 4 * D_MODEL:5 * D_MODEL]
    g2 = mod_ref[:, 5 * D_MODEL:6 * D_MODEL]
    h = _rms_mod(x, g_ref[...], sh2, sc2).astype(BF16)
    a = jnp.maximum(_dot(h, w1_ref[...]), 0.0)
    a = (a * a).astype(BF16)
    return x + g2 * _dot(a, w2_ref[...])


def _mod_kernel(c_ref, w_ref, b_ref, o_ref):
    c = c_ref[...]
    s = (c * jax.nn.sigmoid(c)).astype(BF16)
    r = _dot(s, w_ref[...].astype(BF16)) + b_ref[...]
    for row in range(MOD_ROWS):
        o_ref[row] = r[row:row + 1, :]


def _modulation(c8, ada_w, ada_b):
    tn = 1536
    n = N_MOD * D_MODEL
    return pl.pallas_call(
        _mod_kernel,
        grid=(DEPTH, n // tn),
        in_specs=[
            pl.BlockSpec((MOD_ROWS, D_MODEL), lambda i, j: (0, 0)),
            pl.BlockSpec((None, D_MODEL, tn), lambda i, j: (i, 0, j)),
            pl.BlockSpec((None, 1, tn), lambda i, j: (i, 0, j)),
        ],
        out_specs=pl.BlockSpec((None, MOD_ROWS, 1, tn), lambda i, j: (i, 0, 0, j)),
        out_shape=jax.ShapeDtypeStruct((DEPTH, MOD_ROWS, 1, n), F32),
        compiler_params=pltpu.CompilerParams(
            dimension_semantics=("arbitrary", "arbitrary"), vmem_limit_bytes=VMEM_LIMIT),
        name="modulation",
    )(c8, ada_w, ada_b.reshape(DEPTH, 1, n))


def _layer0_kernel(seq_len, tm, n_tiles, skew, x_ref, xp_ref, xn_ref, moda_ref, modb_ref, gmix_ref,
                   gmlp_ref, pscale_ref, win_ref, wgrp_ref, wout_ref, w1_ref, w2_ref, o_ref,
                   u_ref, d_ref, x1_ref, h2_ref):
    step = pl.program_id(0)
    i = jnp.minimum(step, n_tiles - 1) % (seq_len // tm)
    g2 = modb_ref[:, 5 * D_MODEL:6 * D_MODEL]
    half = D_FF // 2

    def up(h2, c0, c1):
        a = jnp.maximum(_dot(h2, w1_ref[:, c0:c1]), 0.0)
        return (a * a).astype(BF16)

    if skew:
        @pl.when(step == 0)
        def _():
            x1_ref[...] = jnp.zeros(x1_ref.shape, F32)
            h2_ref[...] = jnp.zeros(h2_ref.shape, BF16)

        h2_prev = h2_ref[...]
        a_lo = up(h2_prev, 0, half)

    sh1 = moda_ref[:, 0:D_MODEL]
    sc1 = moda_ref[:, D_MODEL:2 * D_MODEL]
    g1 = moda_ref[:, 2 * D_MODEL:3 * D_MODEL]
    sh2 = moda_ref[:, 3 * D_MODEL:4 * D_MODEL]
    sc2 = moda_ref[:, 4 * D_MODEL:5 * D_MODEL]
    gmix = gmix_ref[...]
    w_in = win_ref[...]

    x = x_ref[...]
    xe = jnp.concatenate([xp_ref[...], x, xn_ref[...]], axis=0)
    u = _dot(_rms_mod(xe, gmix, sh1, sc1).astype(BF16), w_in)
    u_ref[0:POOL_HALO, :] = jnp.where(i > 0, u[0:POOL_HALO], 0.0)
    u_ref[POOL_HALO:POOL_HALO + tm, :] = u[POOL_HALO:POOL_HALO + tm]
    u_ref[POOL_HALO + tm:POOL_HALO + tm + POOL_HALO, :] = jnp.where(
        (i + 1) * tm < seq_len, u[POOL_HALO + tm:], 0.0)
    if skew:
        a_hi = up(h2_prev, half, D_FF)

    t = i * tm + lax.broadcasted_iota(jnp.int32, (tm, 1), 0)
    for g, w in enumerate(POOL_WINDOWS):
        c0, c1 = g * POOL_GROUP_DIM, (g + 1) * POOL_GROUP_DIM
        acc = u_ref[POOL_HALO - w // 2:POOL_HALO - w // 2 + tm, c0:c1]
        for s in range(-w // 2 + 1, w // 2):
            acc = acc + u_ref[POOL_HALO + s:POOL_HALO + s + tm, c0:c1]
        lo = jnp.clip(t - w // 2, 0, seq_len)
        hi = jnp.clip(t + w // 2, 0, seq_len)
        inv_cnt = 1.0 / (hi - lo).astype(F32)
        d = acc * inv_cnt - u_ref[POOL_HALO:POOL_HALO + tm, c0:c1]
        d_ref[:, c0:c1] = _dot(d.astype(BF16), wgrp_ref[g])

    y = _dot((d_ref[...] * pscale_ref[...]).astype(BF16), wout_ref[...])
    if skew:
        down = _dot(jnp.concatenate([a_lo, a_hi], axis=1), w2_ref[...])
    x1 = x + g1 * y
    h2 = _rms_mod(x1, gmlp_ref[...], sh2, sc2).astype(BF16)
    if skew:
        o_ref[...] = x1_ref[...] + g2 * down
        x1_ref[...] = x1
        h2_ref[...] = h2
    else:
        o_ref[...] = x1 + g2 * _dot(up(h2, 0, D_FF), w2_ref[...])


def _layer0(x, mod, mod_row, layer, tm, skew, gmix, gmlp, pscale, w_in, w_grp, w_out, w1, w2):
    nb, seq_len, _ = x.shape
    rows8 = seq_len // POOL_HALO
    tpb = tm // POOL_HALO
    nt = seq_len // tm
    n_tiles = nb * nt

    def tile_a(t):
        ta = jnp.minimum(t, n_tiles - 1)
        return ta // nt, ta % nt

    def tile_b(t):
        tb = jnp.maximum(t - 1, 0) if skew else t
        return tb // nt, tb % nt

    def x_map(t):
        b, i = tile_a(t)
        return (b, i, 0)

    def prev_map(t):
        b, i = tile_a(t)
        return (b, jnp.maximum(i * tpb - 1, 0), 0)

    def next_map(t):
        b, i = tile_a(t)
        return (b, jnp.minimum((i + 1) * tpb, rows8 - 1), 0)

    def moda_map(t):
        return (layer, tile_a(t)[0] if mod_row is None else mod_row, 0, 0)

    def modb_map(t):
        return (layer, tile_b(t)[0] if mod_row is None else mod_row, 0, 0)

    def out_map(t):
        b, i = tile_b(t)
        return (b, i, 0)

    return pl.pallas_call(
        functools.partial(_layer0_kernel, seq_len, tm, n_tiles, skew),
        grid=(n_tiles + (1 if skew else 0),),
        in_specs=[
            pl.BlockSpec((None, tm, D_MODEL), x_map),
            pl.BlockSpec((None, POOL_HALO, D_MODEL), prev_map),
            pl.BlockSpec((None, POOL_HALO, D_MODEL), next_map),
            pl.BlockSpec((None, None, 1, N_MOD * D_MODEL), moda_map),
            pl.BlockSpec((None, None, 1, N_MOD * D_MODEL), modb_map),
            _const_spec((1, D_MODEL)),
            _const_spec((1, D_MODEL)),
            _const_spec((1, D_MODEL)),
            _const_spec((D_MODEL, D_MODEL)),
            _const_spec((len(POOL_WINDOWS), POOL_GROUP_DIM, POOL_GROUP_DIM)),
            _const_spec((D_MODEL, D_MODEL)),
            _const_spec((D_MODEL, D_FF)),
            _const_spec((D_FF, D_MODEL)),
        ],
        out_specs=pl.BlockSpec((None, tm, D_MODEL), out_map),
        out_shape=jax.ShapeDtypeStruct(x.shape, F32),
        scratch_shapes=[pltpu.VMEM((tm + 2 * POOL_HALO, D_MODEL), F32),
                        pltpu.VMEM((tm, D_MODEL), F32),
                        pltpu.VMEM((tm, D_MODEL), F32),
                        pltpu.VMEM((tm, D_MODEL), BF16)],
        compiler_params=pltpu.CompilerParams(
            dimension_semantics=("arbitrary",), vmem_limit_bytes=VMEM_LIMIT),
        name="layer0",
    )(x, x, x, mod, mod, gmix, gmlp, pscale, w_in, w_grp, w_out, w1, w2)


def _rope_block(t, cos, sin_signed, first_half):
    partner = jnp.where(first_half, pltpu.roll(t, LANES - AXIS_ROT // 2, 1),
                        pltpu.roll(t, AXIS_ROT // 2, 1))
    return t * cos + partner * sin_signed


def _store_kv(kv_ref, k, v):
    low = lax.broadcasted_iota(jnp.int32, v.shape, 1) < HEAD_DIM
    kv_ref[:, 0:KV_DIM] = k.astype(BF16)
    kv_ref[:, LANES:2 * LANES] = jnp.where(low, v, 1.0).astype(BF16)
    kv_ref[:, 2 * LANES:3 * LANES] = jnp.where(low, pltpu.roll(v, HEAD_DIM, 1), 1.0).astype(BF16)


def _qkv_kernel(x_ref, mod_ref, gmix_ref, w_ref, cos_ref, sin_ref, q_ref, kv_ref):
    sh1 = mod_ref[:, 0:D_MODEL]
    sc1 = mod_ref[:, D_MODEL:2 * D_MODEL]
    h = _rms_mod(x_ref[...], gmix_ref[...], sh1, sc1).astype(BF16)
    qkv = _dot(h, w_ref[...])
    cos = cos_ref[...]
    sin = sin_ref[...]
    lane = lax.broadcasted_iota(jnp.int32, cos.shape, 1)
    first_half = (lane % AXIS_ROT) < (AXIS_ROT // 2)
    qscale = HEAD_DIM ** -0.5 * LOG2E
    for cb in range(Q_DIM // LANES):
        blk = _rope_block(qkv[:, cb * LANES:(cb + 1) * LANES], cos, sin, first_half)
        q_ref[:, cb * LANES:(cb + 1) * LANES] = (blk * qscale).astype(BF16)
    k = _rope_block(qkv[:, Q_DIM:Q_DIM + KV_DIM], cos, sin, first_half)
    _store_kv(kv_ref, k, qkv[:, Q_DIM + KV_DIM:])


def _qkv(x, mod, layer, tm, gmix, w_qkv, cos, sin):
    nb, seq_len, _ = x.shape
    return pl.pallas_call(
        _qkv_kernel,
        grid=(nb, seq_len // tm),
        in_specs=[
            pl.BlockSpec((None, tm, D_MODEL), lambda b, i: (b, i, 0)),
            pl.BlockSpec((None, None, 1, N_MOD * D_MODEL), lambda b, i: (layer, b, 0, 0)),
            _const_spec((1, D_MODEL)),
            _const_spec((D_MODEL, Q_DIM + 2 * KV_DIM)),
            pl.BlockSpec((tm, LANES), lambda b, i: (i, 0)),
            pl.BlockSpec((tm, LANES), lambda b, i: (i, 0)),
        ],
        out_specs=[pl.BlockSpec((None, tm, Q_DIM), lambda b, i: (b, i, 0)),
                   pl.BlockSpec((None, tm, KV_W), lambda b, i: (b, i, 0))],
        out_shape=[jax.ShapeDtypeStruct((nb, seq_len, Q_DIM), BF16),
                   jax.ShapeDtypeStruct((nb, seq_len, KV_W), BF16)],
        compiler_params=pltpu.CompilerParams(
            dimension_semantics=("arbitrary", "arbitrary"), vmem_limit_bytes=VMEM_LIMIT),
        name="qkv_rope",
    )(x, mod, gmix, w_qkv, cos, sin)


def _ctx_kv_kernel(x_ref, mod_ref, gmix_ref, w_ref, kv_ref):
    sh1 = mod_ref[:, 0:D_MODEL]
    sc1 = mod_ref[:, D_MODEL:2 * D_MODEL]
    h = _rms_mod(x_ref[...], gmix_ref[...], sh1, sc1).astype(BF16)
    kv = _dot(h, w_ref[...])
    _store_kv(kv_ref, kv[:, 0:KV_DIM], kv[:, KV_DIM:])


def _ctx_kv(ctx, mod, mod_row, layer, gmix, w_kv):
    nb, ctx_len, _ = ctx.shape
    return pl.pallas_call(
        _ctx_kv_kernel,
        grid=(nb,),
        in_specs=[
            pl.BlockSpec((None, ctx_len, D_MODEL), lambda b: (b, 0, 0)),
            pl.BlockSpec((None, None, 1, N_MOD * D_MODEL), lambda b: (layer, mod_row, 0, 0)),
            _const_spec((1, D_MODEL)),
            _const_spec((D_MODEL, 2 * KV_DIM)),
        ],
        out_specs=pl.BlockSpec((None, ctx_len, KV_W), lambda b: (b, 0, 0)),
        out_shape=jax.ShapeDtypeStruct((nb, ctx_len, KV_W), BF16),
        compiler_params=pltpu.CompilerParams(
            dimension_semantics=("arbitrary",), vmem_limit_bytes=VMEM_LIMIT),
        name="ctx_kv",
    )(ctx, mod, gmix, w_kv)


def _dot_nt(a, b):
    return lax.dot_general(a, b, (((1,), (1,)), ((), ())), preferred_element_type=F32)


def _attn_kernel(seq_len, tq, ctx_len, sink_ref, q_ref, kvp_ref, kvm_ref, kvn_ref, kvc_ref, o_ref,
                 kv_ext, s_ref, e_ref, es_ref):
    i = pl.program_id(1)
    kv_ext[0:BLOCK, :] = kvp_ref[...]
    kv_ext[BLOCK:BLOCK + tq, :] = kvm_ref[...]
    kv_ext[BLOCK + tq:2 * BLOCK + tq, :] = kvn_ref[...]
    nblk = tq // BLOCK
    nwin = 3 * BLOCK
    nkey = nwin + ctx_len
    row = lax.broadcasted_iota(jnp.int32, (BLOCK, BLOCK), 0)
    col = lax.broadcasted_iota(jnp.int32, (BLOCK, BLOCK), 1)
    low = lax.broadcasted_iota(jnp.int32, (BLOCK, LANES), 1) < HEAD_DIM

    def scores(j, kvh):
        q0 = pl.multiple_of(j * BLOCK, BLOCK)
        qt = q_ref[pl.ds(q0, BLOCK), kvh * GQA_GROUP * HEAD_DIM:(kvh + 1) * GQA_GROUP * HEAD_DIM]
        qs = jnp.concatenate([qt[:, g * HEAD_DIM:(g + 1) * HEAD_DIM] for g in range(GQA_GROUP)], axis=0)
        kw = kv_ext[pl.ds(q0, nwin), kvh * HEAD_DIM:(kvh + 1) * HEAD_DIM]
        s_ref[kvh, :, 0:nwin] = _dot_nt(qs, kw)
        s_ref[kvh, :, nwin:nkey] = _dot_nt(qs, kvc_ref[:, kvh * HEAD_DIM:(kvh + 1) * HEAD_DIM])

    def softmax(j, kvh):
        mask_l = (col >= row) & (i * tq + (j - 1) * BLOCK >= 0)
        mask_r = (col <= row) & (i * tq + (j + 1) * BLOCK < seq_len)
        for g in range(GQA_GROUP):
            sk = sink_ref[kvh * GQA_GROUP + g] * LOG2E
            for t0 in range(0, BLOCK, SOFTMAX_ROWS):
                r0 = g * BLOCK + t0
                rows = slice(r0, r0 + SOFTMAX_ROWS)
                trows = slice(t0, t0 + SOFTMAX_ROWS)
                pieces = [s_ref[kvh, rows, p * LANES:(p + 1) * LANES] for p in range(nkey // LANES)]
                pieces[0] = jnp.where(mask_l[trows], pieces[0], NEG)
                pieces[2] = jnp.where(mask_r[trows], pieces[2], NEG)
                m = pieces[0]
                for p in pieces[1:]:
                    m = jnp.maximum(m, p)
                m = jnp.maximum(jnp.max(m, axis=1, keepdims=True), sk)
                for p, piece in enumerate(pieces):
                    e_ref[kvh, rows, p * LANES:(p + 1) * LANES] = jnp.exp2(piece - m).astype(BF16)
                es_ref[kvh, rows, :] = jnp.broadcast_to(jnp.exp2(sk - m), (SOFTMAX_ROWS, LANES))

    def weighted_values(j, kvh):
        q0 = pl.multiple_of(j * BLOCK, BLOCK)
        vw = kv_ext[pl.ds(q0, nwin), (1 + kvh) * LANES:(2 + kvh) * LANES]
        vc = kvc_ref[:, (1 + kvh) * LANES:(2 + kvh) * LANES]
        o = _dot(e_ref[kvh, :, 0:nwin], vw) + _dot(e_ref[kvh, :, nwin:nkey], vc)
        inv = pltpu.roll(1.0 / (o + es_ref[kvh]), HEAD_DIM, 1)
        on = o * inv
        for pair in range(GQA_GROUP // 2):
            even = on[(2 * pair) * BLOCK:(2 * pair + 1) * BLOCK]
            odd = on[(2 * pair + 1) * BLOCK:(2 * pair + 2) * BLOCK]
            tile = jnp.where(low, even, pltpu.roll(odd, HEAD_DIM, 1))
            c0 = (kvh * GQA_GROUP + 2 * pair) * HEAD_DIM
            o_ref[pl.ds(q0, BLOCK), c0:c0 + LANES] = tile.astype(BF16)

    def body(j, carry):
        scores(j, 0)
        scores(j, 1)
        softmax(j, 0)
        weighted_values(j, 0)
        softmax(j, 1)
        weighted_values(j, 1)
        return carry

    lax.fori_loop(0, nblk, body, 0)


def _attention(q, kv, kvc, sink, tq):
    nb, seq_len, _ = q.shape
    ctx_len = kvc.shape[1]
    nblk = seq_len // BLOCK
    qpb = tq // BLOCK
    rows = GQA_GROUP * BLOCK
    nkey = 3 * BLOCK + ctx_len
    return pl.pallas_call(
        functools.partial(_attn_kernel, seq_len, tq, ctx_len),
        grid=(nb, seq_len // tq),
        in_specs=[
            pl.BlockSpec(memory_space=pltpu.SMEM),
            pl.BlockSpec((None, tq, Q_DIM), lambda b, i: (b, i, 0)),
            pl.BlockSpec((None, BLOCK, KV_W), lambda b, i: (b, jnp.maximum(i * qpb - 1, 0), 0)),
            pl.BlockSpec((None, tq, KV_W), lambda b, i: (b, i, 0)),
            pl.BlockSpec((None, BLOCK, KV_W),
                         lambda b, i: (b, jnp.minimum((i + 1) * qpb, nblk - 1), 0)),
            pl.BlockSpec((None, ctx_len, KV_W), lambda b, i: (b, 0, 0)),
        ],
        out_specs=pl.BlockSpec((None, tq, Q_DIM), lambda b, i: (b, i, 0)),
        out_shape=jax.ShapeDtypeStruct((nb, seq_len, Q_DIM), BF16),
        scratch_shapes=[pltpu.VMEM((tq + 2 * BLOCK, KV_W), BF16),
                        pltpu.VMEM((N_KV_HEADS, rows, nkey), F32),
                        pltpu.VMEM((N_KV_HEADS, rows, nkey), BF16),
                        pltpu.VMEM((N_KV_HEADS, rows, LANES), F32)],
        compiler_params=pltpu.CompilerParams(
            dimension_semantics=("arbitrary", "arbitrary"), vmem_limit_bytes=VMEM_LIMIT),
        name="attention",
    )(sink, q, kv, kv, kv, kvc)


def _post_kernel(x_ref, o_ref_in, mod_ref, gmlp_ref, gfin_ref, wo_ref, w1_ref, w2_ref, out_ref):
    g1 = mod_ref[:, 2 * D_MODEL:3 * D_MODEL]
    x1 = x_ref[...] + g1 * _dot(o_ref_in[...], wo_ref[...])
    x2 = _mlp_residual(x1, mod_ref, gmlp_ref, w1_ref, w2_ref)
    y = x2 * lax.rsqrt(jnp.mean(x2 * x2, axis=-1, keepdims=True) + EPS)
    out_ref[...] = y * gfin_ref[...]


def _post(x, o, mod, layer, tm, gmlp, gfin, w_o, w1, w2):
    nb, seq_len, _ = x.shape
    return pl.pallas_call(
        _post_kernel,
        grid=(nb, seq_len // tm),
        in_specs=[
            pl.BlockSpec((None, tm, D_MODEL), lambda b, i: (b, i, 0)),
            pl.BlockSpec((None, tm, Q_DIM), lambda b, i: (b, i, 0)),
            pl.BlockSpec((None, None, 1, N_MOD * D_MODEL), lambda b, i: (layer, b, 0, 0)),
            _const_spec((1, D_MODEL)),
            _const_spec((1, D_MODEL)),
            _const_spec((Q_DIM, D_MODEL)),
            _const_spec((D_MODEL, D_FF)),
            _const_spec((D_FF, D_MODEL)),
        ],
        out_specs=pl.BlockSpec((None, tm, D_MODEL), lambda b, i: (b, i, 0)),
        out_shape=jax.ShapeDtypeStruct(x.shape, F32),
        compiler_params=pltpu.CompilerParams(
            dimension_semantics=("arbitrary", "arbitrary"), vmem_limit_bytes=VMEM_LIMIT),
        name="post_attn",
    )(x, o, mod, gmlp, gfin, w_o, w1, w2)


def _rope_tables(seq_len):
    rows = seq_len // GRID_W
    row = jnp.repeat(jnp.arange(rows), GRID_W).astype(F32)
    col = jnp.tile(jnp.arange(GRID_W), rows).astype(F32)
    inv = ROPE_BASE ** (-jnp.arange(0, AXIS_ROT, 2, dtype=F32) / AXIS_ROT)
    ang_r = row[:, None] * inv[None]
    ang_c = col[:, None] * inv[None]
    cos = jnp.concatenate([jnp.cos(ang_r)] * 2 + [jnp.cos(ang_c)] * 2, axis=-1)
    sin = jnp.concatenate([-jnp.sin(ang_r), jnp.sin(ang_r), -jnp.sin(ang_c), jnp.sin(ang_c)], axis=-1)
    reps = LANES // HEAD_DIM
    return jnp.tile(cos, (1, reps)), jnp.tile(sin, (1, reps))


def kernel(x, c, ctx, c_ctx, ada_w, ada_b, norm_mix_g, norm_mlp_g, pool_w_in, pool_w_grp,
           pool_scale, pool_w_out, attn_w_qkv, attn_sink, attn_w_o, mlp_w1, mlp_w2, final_g):
    nb, seq_len, _ = x.shape
    ctx_len = ctx.shape[1]
    ctx_row = nb
    c8 = jnp.concatenate(
        [c, c_ctx[None], jnp.zeros((MOD_ROWS - nb - 1, D_MODEL), F32)], axis=0)
    mod = _modulation(c8, ada_w, ada_b)

    gmix = norm_mix_g.reshape(DEPTH, 1, D_MODEL)
    gmlp = norm_mlp_g.reshape(DEPTH, 1, D_MODEL)
    w1 = mlp_w1.astype(BF16)
    w2 = mlp_w2.astype(BF16)

    l0 = (gmix[0], gmlp[0], pool_scale[0].reshape(1, D_MODEL), pool_w_in[0].astype(BF16),
          pool_w_grp[0].astype(BF16), pool_w_out[0].astype(BF16), w1[0], w2[0])
    x1 = _layer0(x, mod, None, 0, 512, True, *l0)
    ctx1 = _layer0(ctx, mod, ctx_row, 0, ctx_len, False, *l0)

    w_qkv = attn_w_qkv[0].astype(BF16)
    cos, sin = _rope_tables(seq_len)
    q, kv = _qkv(x1, mod, 1, 512, gmix[1], w_qkv, cos, sin)
    kvc = _ctx_kv(ctx1, mod, ctx_row, 1, gmix[1], w_qkv[:, Q_DIM:])
    o = _attention(q, kv, kvc, attn_sink[0], 512)
    return _post(x1, o, mod, 1, 512, gmlp[1], final_g.reshape(1, D_MODEL),
                 attn_w_o[0].astype(BF16), w1[1], w2[1])
```

```python
import functools

import numpy as np
import jax
import jax.numpy as jnp
from jax import lax
from jax.experimental import pallas as pl
from jax.experimental.pallas import tpu as pltpu

D_MODEL = 1024
DEPTH = 2
GRID_W = 64
POOL_WINDOWS = (2, 4, 8, 16)
POOL_GROUP_DIM = D_MODEL // len(POOL_WINDOWS)
POOL_HALO = 8
HEAD_DIM = 64
N_HEADS = D_MODEL // HEAD_DIM
N_KV_HEADS = 2
GQA_GROUP = N_HEADS // N_KV_HEADS
Q_DIM = N_HEADS * HEAD_DIM
KV_DIM = N_KV_HEADS * HEAD_DIM
WINDOW = 128
BLOCK = 128
ROPE_BASE = 10000.0
AXIS_ROT = HEAD_DIM // 2
ROT_HALF = AXIS_ROT // 2
D_FF = 4 * D_MODEL
N_MOD = 6
EPS = 1e-6
NEG = -1e30
MOD_ROWS = 8
LANES = 128
LOG2E = 1.4426950408889634
KV_W = 4 * LANES
SOFTMAX_ROWS = 32
VMEM_LIMIT = 56 * 1024 * 1024

BF16 = jnp.bfloat16
F32 = jnp.float32


def _const_spec(shape):
    nd = len(shape)
    return pl.BlockSpec(shape, lambda *_: (0,) * nd, pipeline_mode=pl.Buffered(1))


def _dot(a, b):
    return jnp.dot(a, b, preferred_element_type=F32)


def _rms_mod(x, g, shift, scale):
    y = x * lax.rsqrt(jnp.mean(x * x, axis=-1, keepdims=True) + EPS)
    return (y * g) * (1.0 + scale) + shift


def _mlp_residual(x, mod_ref, g_ref, w1_ref, w2_ref):
    sh2 = mod_ref[:, 3 * D_MODEL:4 * D_MODEL]
    sc2 = mod_ref[:, 4 * D_MODEL:5 * D_MODEL]
    g2 = mod_ref[:, 5 * D_MODEL:6 * D_MODEL]
    h = _rms_mod(x, g_ref[...], sh2, sc2).astype(BF16)
    a = jnp.maximum(_dot(h, w1_ref[...]), 0.0)
    a = (a * a).astype(BF16)
    return x + g2 * _dot(a, w2_ref[...])


def _skewed_steps(step, n_tiles, body):
    pl.when(step == 0)(lambda: body(True, False))
    pl.when((step > 0) & (step < n_tiles))(lambda: body(True, True))
    pl.when(step == n_tiles)(lambda: body(False, True))


def _skew_maps(n_tiles, tiles_per_row):
    def first(t):
        ta = jnp.minimum(t, n_tiles - 1)
        return ta // tiles_per_row, ta % tiles_per_row

    def second(t):
        tb = jnp.maximum(t - 1, 0)
        return tb // tiles_per_row, tb % tiles_per_row

    return first, second


def _mod_kernel(c_ref, w_ref, b_ref, o_ref):
    c = c_ref[...]
    s = (c * jax.nn.sigmoid(c)).astype(BF16)
    r = _dot(s, w_ref[...].astype(BF16)) + b_ref[...]
    for row in range(MOD_ROWS):
        o_ref[row] = r[row:row + 1, :]


def _modulation(c8, ada_w, ada_b):
    tn = 1536
    n = N_MOD * D_MODEL
    return pl.pallas_call(
        _mod_kernel,
        grid=(DEPTH, n // tn),
        in_specs=[
            pl.BlockSpec((MOD_ROWS, D_MODEL), lambda i, j: (0, 0)),
            pl.BlockSpec((None, D_MODEL, tn), lambda i, j: (i, 0, j)),
            pl.BlockSpec((None, 1, tn), lambda i, j: (i, 0, j)),
        ],
        out_specs=pl.BlockSpec((None, MOD_ROWS, 1, tn), lambda i, j: (i, 0, 0, j)),
        out_shape=jax.ShapeDtypeStruct((DEPTH, MOD_ROWS, 1, n), F32),
        compiler_params=pltpu.CompilerParams(
            dimension_semantics=("arbitrary", "arbitrary"), vmem_limit_bytes=VMEM_LIMIT),
        name="modulation",
    )(c8, ada_w, ada_b.reshape(DEPTH, 1, n))


def _layer0_kernel(seq_len, tm, n_tiles, skew, x_ref, xp_ref, xn_ref, moda_ref, modb_ref, gmix_ref,
                   gmlp_ref, pscale_ref, win_ref, wgrp_ref, wout_ref, w1_ref, w2_ref, o_ref,
                   u_ref, d_ref, x1_ref, h2_ref):
    step = pl.program_id(0)
    i = jnp.minimum(step, n_tiles - 1) % (seq_len // tm)
    half = D_FF // 2

    def mlp_up(h2, c0, c1):
        a = jnp.maximum(_dot(h2, w1_ref[:, c0:c1]), 0.0)
        return (a * a).astype(BF16)

    def body(do_mixer, do_prev_mlp):
        if do_prev_mlp:
            g2 = modb_ref[:, 5 * D_MODEL:6 * D_MODEL]
            h2_prev = h2_ref[...]
            a_lo = mlp_up(h2_prev, 0, half)

        if do_mixer:
            sh1 = moda_ref[:, 0:D_MODEL]
            sc1 = moda_ref[:, D_MODEL:2 * D_MODEL]
            x = x_ref[...]
            xe = jnp.concatenate([xp_ref[...], x, xn_ref[...]], axis=0)
            u = _dot(_rms_mod(xe, gmix_ref[...], sh1, sc1).astype(BF16), win_ref[...])
            u_ref[0:POOL_HALO, :] = jnp.where(i > 0, u[0:POOL_HALO], 0.0)
            u_ref[POOL_HALO:POOL_HALO + tm, :] = u[POOL_HALO:POOL_HALO + tm]
            u_ref[POOL_HALO + tm:POOL_HALO + tm + POOL_HALO, :] = jnp.where(
                (i + 1) * tm < seq_len, u[POOL_HALO + tm:], 0.0)

        if do_prev_mlp:
            a_hi = mlp_up(h2_prev, half, D_FF)

        if do_mixer:
            t = i * tm + lax.broadcasted_iota(jnp.int32, (tm, 1), 0)
            for g, w in enumerate(POOL_WINDOWS):
                c0, c1 = g * POOL_GROUP_DIM, (g + 1) * POOL_GROUP_DIM
                acc = u_ref[POOL_HALO - w // 2:POOL_HALO - w // 2 + tm, c0:c1]
                for s in range(-w // 2 + 1, w // 2):
                    acc = acc + u_ref[POOL_HALO + s:POOL_HALO + s + tm, c0:c1]
                lo = jnp.clip(t - w // 2, 0, seq_len)
                hi = jnp.clip(t + w // 2, 0, seq_len)
                inv_cnt = 1.0 / (hi - lo).astype(F32)
                d = acc * inv_cnt - u_ref[POOL_HALO:POOL_HALO + tm, c0:c1]
                d_ref[:, c0:c1] = _dot(d.astype(BF16), wgrp_ref[g])
            y = _dot((d_ref[...] * pscale_ref[...]).astype(BF16), wout_ref[...])

        if do_prev_mlp:
            down = _dot(jnp.concatenate([a_lo, a_hi], axis=1), w2_ref[...])

        if do_mixer:
            g1 = moda_ref[:, 2 * D_MODEL:3 * D_MODEL]
            sh2 = moda_ref[:, 3 * D_MODEL:4 * D_MODEL]
            sc2 = moda_ref[:, 4 * D_MODEL:5 * D_MODEL]
            x1 = x + g1 * y
            h2 = _rms_mod(x1, gmlp_ref[...], sh2, sc2).astype(BF16)

        if do_prev_mlp:
            o_ref[...] = x1_ref[...] + g2 * down
        if do_mixer and skew:
            x1_ref[...] = x1
            h2_ref[...] = h2
        if do_mixer and not skew:
            g2_now = moda_ref[:, 5 * D_MODEL:6 * D_MODEL]
            o_ref[...] = x1 + g2_now * _dot(mlp_up(h2, 0, D_FF), w2_ref[...])

    if skew:
        _skewed_steps(step, n_tiles, body)
    else:
        body(True, False)


def _layer0(x, mod, mod_row, layer, tm, skew, gmix, gmlp, pscale, w_in, w_grp, w_out, w1, w2):
    nb, seq_len, _ = x.shape
    rows8 = seq_len // POOL_HALO
    tpb = tm // POOL_HALO
    nt = seq_len // tm
    n_tiles = nb * nt
    tile_a, tile_b = _skew_maps(n_tiles, nt)
    if not skew:
        tile_b = tile_a

    def x_map(t):
        b, i = tile_a(t)
        return (b, i, 0)

    def prev_map(t):
        b, i = tile_a(t)
        return (b, jnp.maximum(i * tpb - 1, 0), 0)

    def next_map(t):
        b, i = tile_a(t)
        return (b, jnp.minimum((i + 1) * tpb, rows8 - 1), 0)

    def moda_map(t):
        return (layer, tile_a(t)[0] if mod_row is None else mod_row, 0, 0)

    def modb_map(t):
        return (layer, tile_b(t)[0] if mod_row is None else mod_row, 0, 0)

    def out_map(t):
        b, i = tile_b(t)
        return (b, i, 0)

    return pl.pallas_call(
        functools.partial(_layer0_kernel, seq_len, tm, n_tiles, skew),
        grid=(n_tiles + (1 if skew else 0),),
        in_specs=[
            pl.BlockSpec((None, tm, D_MODEL), x_map),
            pl.BlockSpec((None, POOL_HALO, D_MODEL), prev_map),
            pl.BlockSpec((None, POOL_HALO, D_MODEL), next_map),
            pl.BlockSpec((None, None, 1, N_MOD * D_MODEL), moda_map),
            pl.BlockSpec((None, None, 1, N_MOD * D_MODEL), modb_map),
            _const_spec((1, D_MODEL)),
            _const_spec((1, D_MODEL)),
            _const_spec((1, D_MODEL)),
            _const_spec((D_MODEL, D_MODEL)),
            _const_spec((len(POOL_WINDOWS), POOL_GROUP_DIM, POOL_GROUP_DIM)),
            _const_spec((D_MODEL, D_MODEL)),
            _const_spec((D_MODEL, D_FF)),
            _const_spec((D_FF, D_MODEL)),
        ],
        out_specs=pl.BlockSpec((None, tm, D_MODEL), out_map),
        out_shape=jax.ShapeDtypeStruct(x.shape, F32),
        scratch_shapes=[pltpu.VMEM((tm + 2 * POOL_HALO, D_MODEL), F32),
                        pltpu.VMEM((tm, D_MODEL), F32),
                        pltpu.VMEM((tm, D_MODEL), F32),
                        pltpu.VMEM((tm, D_MODEL), BF16)],
        compiler_params=pltpu.CompilerParams(
            dimension_semantics=("arbitrary",), vmem_limit_bytes=VMEM_LIMIT),
        name="layer0",
    )(x, x, x, mod, mod, gmix, gmlp, pscale, w_in, w_grp, w_out, w1, w2)


def _pair_layout_columns(n_pairs):
    cols = []
    for p in range(n_pairs):
        for lane in range(LANES):
            part = lane // HEAD_DIM
            head = 2 * p + (lane % HEAD_DIM) // AXIS_ROT
            axis, f = divmod(lane % AXIS_ROT, ROT_HALF)
            cols.append(head * HEAD_DIM + axis * AXIS_ROT + part * ROT_HALF + f)
    return np.asarray(cols, np.int32)


def _qkv_columns():
    return np.concatenate([_pair_layout_columns(N_HEADS // 2),
                           Q_DIM + _pair_layout_columns(N_KV_HEADS // 2),
                           Q_DIM + KV_DIM + np.arange(KV_DIM, dtype=np.int32)])


def _store_kv(kv_ref, k, v):
    lane = lax.broadcasted_iota(jnp.int32, k.shape, 1)
    head_a = (lane % HEAD_DIM) < AXIS_ROT
    kv_ref[:, 0:LANES] = jnp.where(head_a, k, pltpu.roll(k, AXIS_ROT, 1)).astype(BF16)
    kv_ref[:, LANES:2 * LANES] = jnp.where(head_a, pltpu.roll(k, LANES - AXIS_ROT, 1), k).astype(BF16)
    low = lane < HEAD_DIM
    kv_ref[:, 2 * LANES:3 * LANES] = jnp.where(low, v, 1.0).astype(BF16)
    kv_ref[:, 3 * LANES:4 * LANES] = jnp.where(low, pltpu.roll(v, HEAD_DIM, 1), 1.0).astype(BF16)


def _qkv_kernel(n_tiles, x_ref, mod_ref, gmix_ref, w_ref, cos_ref, sin_ref, q_ref, kv_ref, h_ref):
    def body(do_norm, do_project):
        if do_project:
            qkv = _dot(h_ref[...], w_ref[...])
        if do_norm:
            sh1 = mod_ref[:, 0:D_MODEL]
            sc1 = mod_ref[:, D_MODEL:2 * D_MODEL]
            h = _rms_mod(x_ref[...], gmix_ref[...], sh1, sc1).astype(BF16)
        if do_project:
            cos = cos_ref[...]
            sin = sin_ref[...]
            qscale = HEAD_DIM ** -0.5 * LOG2E

            def rope(t):
                return t * cos + pltpu.roll(t, HEAD_DIM, 1) * sin

            for cb in range(Q_DIM // LANES):
                blk = rope(qkv[:, cb * LANES:(cb + 1) * LANES])
                q_ref[:, cb * LANES:(cb + 1) * LANES] = (blk * qscale).astype(BF16)
            _store_kv(kv_ref, rope(qkv[:, Q_DIM:Q_DIM + KV_DIM]), qkv[:, Q_DIM + KV_DIM:])
        if do_norm:
            h_ref[...] = h

    _skewed_steps(pl.program_id(0), n_tiles, body)


def _qkv(x, mod, layer, tm, gmix, w_qkv, cos, sin):
    nb, seq_len, _ = x.shape
    nt = seq_len // tm
    n_tiles = nb * nt
    tile_a, tile_b = _skew_maps(n_tiles, nt)

    def out_map(t):
        b, i = tile_b(t)
        return (b, i, 0)

    return pl.pallas_call(
        functools.partial(_qkv_kernel, n_tiles),
        grid=(n_tiles + 1,),
        in_specs=[
            pl.BlockSpec((None, tm, D_MODEL), lambda t: (*tile_a(t), 0)),
            pl.BlockSpec((None, None, 1, N_MOD * D_MODEL), lambda t: (layer, tile_a(t)[0], 0, 0)),
            _const_spec((1, D_MODEL)),
            _const_spec((D_MODEL, Q_DIM + 2 * KV_DIM)),
            pl.BlockSpec((tm, LANES), lambda t: (tile_b(t)[1], 0)),
            pl.BlockSpec((tm, LANES), lambda t: (tile_b(t)[1], 0)),
        ],
        out_specs=[pl.BlockSpec((None, tm, Q_DIM), out_map),
                   pl.BlockSpec((None, tm, KV_W), out_map)],
        out_shape=[jax.ShapeDtypeStruct((nb, seq_len, Q_DIM), BF16),
                   jax.ShapeDtypeStruct((nb, seq_len, KV_W), BF16)],
        scratch_shapes=[pltpu.VMEM((tm, D_MODEL), BF16)],
        compiler_params=pltpu.CompilerParams(
            dimension_semantics=("arbitrary",), vmem_limit_bytes=VMEM_LIMIT),
        name="qkv_rope",
    )(x, mod, gmix, w_qkv, cos, sin)


def _ctx_kv_kernel(x_ref, mod_ref, gmix_ref, w_ref, kv_ref):
    sh1 = mod_ref[:, 0:D_MODEL]
    sc1 = mod_ref[:, D_MODEL:2 * D_MODEL]
    h = _rms_mod(x_ref[...], gmix_ref[...], sh1, sc1).astype(BF16)
    kv = _dot(h, w_ref[...])
    _store_kv(kv_ref, kv[:, 0:KV_DIM], kv[:, KV_DIM:])


def _ctx_kv(ctx, mod, mod_row, layer, gmix, w_kv):
    nb, ctx_len, _ = ctx.shape
    return pl.pallas_call(
        _ctx_kv_kernel,
        grid=(nb,),
        in_specs=[
            pl.BlockSpec((None, ctx_len, D_MODEL), lambda b: (b, 0, 0)),
            pl.BlockSpec((None, None, 1, N_MOD * D_MODEL), lambda b: (layer, mod_row, 0, 0)),
            _const_spec((1, D_MODEL)),
            _const_spec((D_MODEL, 2 * KV_DIM)),
        ],
        out_specs=pl.BlockSpec((None, ctx_len, KV_W), lambda b: (b, 0, 0)),
        out_shape=jax.ShapeDtypeStruct((nb, ctx_len, KV_W), BF16),
        compiler_params=pltpu.CompilerParams(
            dimension_semantics=("arbitrary",), vmem_limit_bytes=VMEM_LIMIT),
        name="ctx_kv",
    )(ctx, mod, gmix, w_kv)


def _dot_nt(a, b):
    return lax.dot_general(a, b, (((1,), (1,)), ((), ())), preferred_element_type=F32)


def _attn_kernel(seq_len, tq, ctx_len, sink_ref, q_ref, kvp_ref, kvm_ref, kvn_ref, kvc_ref, o_ref,
                 kv_ext, *bufs):
    s_refs, e_refs, m_refs = bufs[0:2], bufs[2:4], bufs[4:6]
    i = pl.program_id(1)
    kv_ext[0:BLOCK, :] = kvp_ref[...]
    kv_ext[BLOCK:BLOCK + tq, :] = kvm_ref[...]
    kv_ext[BLOCK + tq:2 * BLOCK + tq, :] = kvn_ref[...]
    nblk = tq // BLOCK
    nwin = 3 * BLOCK
    nkey = nwin + ctx_len
    row = lax.broadcasted_iota(jnp.int32, (BLOCK, BLOCK), 0)
    col = lax.broadcasted_iota(jnp.int32, (BLOCK, BLOCK), 1)
    lane = lax.broadcasted_iota(jnp.int32, (BLOCK, LANES), 1)
    low = lane < HEAD_DIM
    head_a = (lane % HEAD_DIM) < AXIS_ROT

    chunks = [(g, t0) for g in range(GQA_GROUP) for t0 in range(0, BLOCK, SOFTMAX_ROWS)]

    def scores(j, kvh):
        q0 = pl.multiple_of(j * BLOCK, BLOCK)
        per_head = []
        for g in range(GQA_GROUP):
            h = kvh * GQA_GROUP + g
            pair = q_ref[pl.ds(q0, BLOCK), (h // 2) * LANES:(h // 2 + 1) * LANES]
            mine = head_a if h % 2 == 0 else jnp.logical_not(head_a)
            per_head.append(jnp.where(mine, pair, jnp.zeros_like(pair)))
        qs = jnp.concatenate(per_head, axis=0)
        kw = kv_ext[pl.ds(q0, nwin), kvh * LANES:(kvh + 1) * LANES]
        s_refs[kvh][:, 0:nwin] = _dot_nt(qs, kw)
        s_refs[kvh][:, nwin:nkey] = _dot_nt(qs, kvc_ref[:, kvh * LANES:(kvh + 1) * LANES])

    def softmax(j, kvh):
        mask_l = (col >= row) & (i * tq + (j - 1) * BLOCK >= 0)
        mask_r = (col <= row) & (i * tq + (j + 1) * BLOCK < seq_len)

        def masked_pieces(rows, trows):
            pieces = [s_refs[kvh][rows, p * LANES:(p + 1) * LANES] for p in range(nkey // LANES)]
            pieces[0] = jnp.where(mask_l[trows], pieces[0], NEG)
            pieces[2] = jnp.where(mask_r[trows], pieces[2], NEG)
            return pieces

        for g, t0 in chunks:
            rows = slice(g * BLOCK + t0, g * BLOCK + t0 + SOFTMAX_ROWS)
            pieces = masked_pieces(rows, slice(t0, t0 + SOFTMAX_ROWS))
            m = pieces[0]
            for p in pieces[1:]:
                m = jnp.maximum(m, p)
            m = jnp.maximum(jnp.max(m, axis=1, keepdims=True), sink_ref[kvh * GQA_GROUP + g] * LOG2E)
            m_refs[kvh][rows, :] = jnp.broadcast_to(m, (SOFTMAX_ROWS, LANES))
        for g, t0 in chunks:
            rows = slice(g * BLOCK + t0, g * BLOCK + t0 + SOFTMAX_ROWS)
            pieces = masked_pieces(rows, slice(t0, t0 + SOFTMAX_ROWS))
            m = m_refs[kvh][rows, :]
            for p, piece in enumerate(pieces):
                e_refs[kvh][rows, p * LANES:(p + 1) * LANES] = jnp.exp2((piece - m).astype(BF16))

    def weighted_values(j, kvh):
        q0 = pl.multiple_of(j * BLOCK, BLOCK)
        vcols = slice((N_KV_HEADS + kvh) * LANES, (N_KV_HEADS + kvh + 1) * LANES)
        o = (_dot(e_refs[kvh][:, 0:nwin], kv_ext[pl.ds(q0, nwin), vcols])
             + _dot(e_refs[kvh][:, nwin:nkey], kvc_ref[:, vcols]))
        on = []
        for g in range(GQA_GROUP):
            rows = slice(g * BLOCK, (g + 1) * BLOCK)
            sink_term = jnp.exp2(sink_ref[kvh * GQA_GROUP + g] * LOG2E - m_refs[kvh][rows, :])
            on.append(o[rows] * pltpu.roll(1.0 / (o[rows] + sink_term), HEAD_DIM, 1))
        for pair in range(GQA_GROUP // 2):
            tile = jnp.where(low, on[2 * pair], pltpu.roll(on[2 * pair + 1], HEAD_DIM, 1))
            c0 = (kvh * GQA_GROUP + 2 * pair) * HEAD_DIM
            o_ref[pl.ds(q0, BLOCK), c0:c0 + LANES] = tile.astype(BF16)

    def body(j, carry):
        scores(j, 0)
        scores(j, 1)
        softmax(j, 0)
        weighted_values(j, 0)
        softmax(j, 1)
        weighted_values(j, 1)
        return carry

    lax.fori_loop(0, nblk, body, 0)


def _attention(q, kv, kvc, sink, tq):
    nb, seq_len, _ = q.shape
    ctx_len = kvc.shape[1]
    nblk = seq_len // BLOCK
    qpb = tq // BLOCK
    rows = GQA_GROUP * BLOCK
    nkey = 3 * BLOCK + ctx_len
    return pl.pallas_call(
        functools.partial(_attn_kernel, seq_len, tq, ctx_len),
        grid=(nb, seq_len // tq),
        in_specs=[
            pl.BlockSpec(memory_space=pltpu.SMEM),
            pl.BlockSpec((None, tq, Q_DIM), lambda b, i: (b, i, 0)),
            pl.BlockSpec((None, BLOCK, KV_W), lambda b, i: (b, jnp.maximum(i * qpb - 1, 0), 0)),
            pl.BlockSpec((None, tq, KV_W), lambda b, i: (b, i, 0)),
            pl.BlockSpec((None, BLOCK, KV_W),
                         lambda b, i: (b, jnp.minimum((i + 1) * qpb, nblk - 1), 0)),
            pl.BlockSpec((None, ctx_len, KV_W), lambda b, i: (b, 0, 0)),
        ],
        out_specs=pl.BlockSpec((None, tq, Q_DIM), lambda b, i: (b, i, 0)),
        out_shape=jax.ShapeDtypeStruct((nb, seq_len, Q_DIM), BF16),
        scratch_shapes=([pltpu.VMEM((tq + 2 * BLOCK, KV_W), BF16)]
                        + [pltpu.VMEM((rows, nkey), F32)] * N_KV_HEADS
                        + [pltpu.VMEM((rows, nkey), BF16)] * N_KV_HEADS
                        + [pltpu.VMEM((rows, LANES), F32)] * N_KV_HEADS),
        compiler_params=pltpu.CompilerParams(
            dimension_semantics=("arbitrary", "arbitrary"), vmem_limit_bytes=VMEM_LIMIT),
        name="attention",
    )(sink, q, kv, kv, kv, kvc)


def _post_kernel(x_ref, o_ref_in, mod_ref, gmlp_ref, gfin_ref, wo_ref, w1_ref, w2_ref, out_ref):
    g1 = mod_ref[:, 2 * D_MODEL:3 * D_MODEL]
    x1 = x_ref[...] + g1 * _dot(o_ref_in[...], wo_ref[...])
    x2 = _mlp_residual(x1, mod_ref, gmlp_ref, w1_ref, w2_ref)
    y = x2 * lax.rsqrt(jnp.mean(x2 * x2, axis=-1, keepdims=True) + EPS)
    out_ref[...] = y * gfin_ref[...]


def _post(x, o, mod, layer, tm, gmlp, gfin, w_o, w1, w2):
    nb, seq_len, _ = x.shape
    return pl.pallas_call(
        _post_kernel,
        grid=(nb, seq_len // tm),
        in_specs=[
            pl.BlockSpec((None, tm, D_MODEL), lambda b, i: (b, i, 0)),
            pl.BlockSpec((None, tm, Q_DIM), lambda b, i: (b, i, 0)),
            pl.BlockSpec((None, None, 1, N_MOD * D_MODEL), lambda b, i: (layer, b, 0, 0)),
            _const_spec((1, D_MODEL)),
            _const_spec((1, D_MODEL)),
            _const_spec((Q_DIM, D_MODEL)),
            _const_spec((D_MODEL, D_FF)),
            _const_spec((D_FF, D_MODEL)),
        ],
        out_specs=pl.BlockSpec((None, tm, D_MODEL), lambda b, i: (b, i, 0)),
        out_shape=jax.ShapeDtypeStruct(x.shape, F32),
        compiler_params=pltpu.CompilerParams(
            dimension_semantics=("arbitrary", "arbitrary"), vmem_limit_bytes=VMEM_LIMIT),
        name="post_attn",
    )(x, o, mod, gmlp, gfin, w_o, w1, w2)


def _rope_tables(seq_len):
    rows = seq_len // GRID_W
    row = jnp.repeat(jnp.arange(rows), GRID_W).astype(F32)
    col = jnp.tile(jnp.arange(GRID_W), rows).astype(F32)
    inv = ROPE_BASE ** (-jnp.arange(0, AXIS_ROT, 2, dtype=F32) / AXIS_ROT)
    ang = jnp.concatenate([row[:, None] * inv[None], col[:, None] * inv[None]], axis=-1)
    cos = jnp.tile(jnp.cos(ang), (1, LANES // AXIS_ROT))
    sin = jnp.sin(ang)
    return cos, jnp.concatenate([-sin, -sin, sin, sin], axis=-1)


def kernel(x, c, ctx, c_ctx, ada_w, ada_b, norm_mix_g, norm_mlp_g, pool_w_in, pool_w_grp,
           pool_scale, pool_w_out, attn_w_qkv, attn_sink, attn_w_o, mlp_w1, mlp_w2, final_g):
    nb, seq_len, _ = x.shape
    ctx_len = ctx.shape[1]
    ctx_row = nb
    c8 = jnp.concatenate(
        [c, c_ctx[None], jnp.zeros((MOD_ROWS - nb - 1, D_MODEL), F32)], axis=0)
    mod = _modulation(c8, ada_w, ada_b)

    gmix = norm_mix_g.reshape(DEPTH, 1, D_MODEL)
    gmlp = norm_mlp_g.reshape(DEPTH, 1, D_MODEL)
    w1 = mlp_w1.astype(BF16)
    w2 = mlp_w2.astype(BF16)

    l0 = (gmix[0], gmlp[0], pool_scale[0].reshape(1, D_MODEL), pool_w_in[0].astype(BF16),
          pool_w_grp[0].astype(BF16), pool_w_out[0].astype(BF16), w1[0], w2[0])
    x1 = _layer0(x, mod, None, 0, 512, True, *l0)
    ctx1 = _layer0(ctx, mod, ctx_row, 0, ctx_len, False, *l0)

    w_qkv = attn_w_qkv[0][:, _qkv_columns()].astype(BF16)
    cos, sin = _rope_tables(seq_len)
    q, kv = _qkv(x1, mod, 1, 512, gmix[1], w_qkv, cos, sin)
    kvc = _ctx_kv(ctx1, mod, ctx_row, 1, gmix[1], w_qkv[:, Q_DIM:])
    o = _attention(q, kv, kvc, attn_sink[0], 512)
    return _post(x1, o, mod, 1, 512, gmlp[1], final_g.reshape(1, D_MODEL),
                 attn_w_o[0].astype(BF16), w1[1], w2[1])
```

```python
import functools

import jax
import jax.numpy as jnp
from jax import lax
from jax.experimental import pallas as pl
from jax.experimental.pallas import tpu as pltpu

D_MODEL = 1024
DEPTH = 2
GRID_W = 64
POOL_WINDOWS = (2, 4, 8, 16)
POOL_GROUP_DIM = D_MODEL // len(POOL_WINDOWS)
POOL_HALO = 8
HEAD_DIM = 64
N_HEADS = D_MODEL // HEAD_DIM
N_KV_HEADS = 2
GQA_GROUP = N_HEADS // N_KV_HEADS
Q_DIM = N_HEADS * HEAD_DIM
KV_DIM = N_KV_HEADS * HEAD_DIM
WINDOW = 128
BLOCK = 128
ROPE_BASE = 10000.0
AXIS_ROT = HEAD_DIM // 2
ROT_HALF = AXIS_ROT // 2
D_FF = 4 * D_MODEL
N_MOD = 6
EPS = 1e-6
NEG = -1e30
MOD_ROWS = 8
LANES = 128
LOG2E = 1.4426950408889634
KV_W = 4 * LANES
SOFTMAX_ROWS = 32
VMEM_LIMIT = 56 * 1024 * 1024

BF16 = jnp.bfloat16
F32 = jnp.float32


def _const_spec(shape):
    nd = len(shape)
    return pl.BlockSpec(shape, lambda *_: (0,) * nd, pipeline_mode=pl.Buffered(1))


def _dot(a, b):
    return jnp.dot(a, b, preferred_element_type=F32)


def _rms_mod(x, g, shift, scale):
    y = x * lax.rsqrt(jnp.mean(x * x, axis=-1, keepdims=True) + EPS)
    return y * (g * (1.0 + scale)) + shift


def _mlp_residual(x, mod_ref, g_ref, w1_ref, w2_ref):
    sh2 = mod_ref[:, 3 * D_MODEL:4 * D_MODEL]
    sc2 = mod_ref[:, 4 * D_MODEL:5 * D_MODEL]
    g2 = mod_ref[:, 5 * D_MODEL:6 * D_MODEL]
    h = _rms_mod(x, g_ref[...], sh2, sc2).astype(BF16)
    a = jnp.maximum(_dot(h, w1_ref[...]), 0.0)
    a = (a * a).astype(BF16)
    return x + g2 * _dot(a, w2_ref[...])


def _skewed_steps(step, n_tiles, body):
    pl.when(step == 0)(lambda: body(True, False))
    pl.when((step > 0) & (step < n_tiles))(lambda: body(True, True))
    pl.when(step == n_tiles)(lambda: body(False, True))


def _skew_maps(n_tiles, tiles_per_row):
    def first(t):
        ta = jnp.minimum(t, n_tiles - 1)
        return ta // tiles_per_row, ta % tiles_per_row

    def second(t):
        tb = jnp.maximum(t - 1, 0)
        return tb // tiles_per_row, tb % tiles_per_row

    return first, second


def _mod_kernel(c_ref, w_ref, b_ref, o_ref):
    c = c_ref[...]
    s = (c * jax.nn.sigmoid(c)).astype(BF16)
    r = _dot(s, w_ref[...].astype(BF16)) + b_ref[...]
    for row in range(MOD_ROWS):
        o_ref[row] = r[row:row + 1, :]


def _modulation(c8, ada_w, ada_b):
    tn = 1536
    n = N_MOD * D_MODEL
    return pl.pallas_call(
        _mod_kernel,
        grid=(DEPTH, n // tn),
        in_specs=[
            pl.BlockSpec((MOD_ROWS, D_MODEL), lambda i, j: (0, 0)),
            pl.BlockSpec((None, D_MODEL, tn), lambda i, j: (i, 0, j)),
            pl.BlockSpec((None, 1, tn), lambda i, j: (i, 0, j)),
        ],
        out_specs=pl.BlockSpec((None, MOD_ROWS, 1, tn), lambda i, j: (i, 0, 0, j)),
        out_shape=jax.ShapeDtypeStruct((DEPTH, MOD_ROWS, 1, n), F32),
        compiler_params=pltpu.CompilerParams(
            dimension_semantics=("arbitrary", "arbitrary"), vmem_limit_bytes=VMEM_LIMIT),
        name="modulation",
    )(c8, ada_w, ada_b.reshape(DEPTH, 1, n))


def _to_pair_layout(v):
    lane = lax.broadcasted_iota(jnp.int32, v.shape, 1)
    for width in (ROT_HALF, AXIS_ROT):
        hi = (lane // (2 * width)) % 2
        lo = (lane // width) % 2
        v = jnp.where(hi == lo, v, jnp.where(hi == 0, pltpu.roll(v, LANES - width, 1),
                                             pltpu.roll(v, width, 1)))
    return v


def _prepare_next_weights(prep_in, prep_out):
    wqkv_in, wqkv_out = prep_in[0], prep_out[0]
    for cb in range((Q_DIM + KV_DIM) // LANES):
        cols = slice(cb * LANES, (cb + 1) * LANES)
        wqkv_out[:, cols] = _to_pair_layout(wqkv_in[:, cols]).astype(BF16)
    wqkv_out[:, Q_DIM + KV_DIM:] = wqkv_in[:, Q_DIM + KV_DIM:].astype(BF16)
    for src, dst in zip(prep_in[1:], prep_out[1:]):
        dst[...] = src[...].astype(BF16)


def _layer0_kernel(seq_len, tm, n_tiles, skew, n_prep, *refs):
    (x_ref, xp_ref, xn_ref, moda_ref, modb_ref, gmix_ref, gmlp_ref, pscale_ref, win_ref, wgrp_ref,
     wout_ref, w1_ref, w2_ref) = refs[:13]
    prep_in = refs[13:13 + n_prep]
    o_ref = refs[13 + n_prep]
    prep_out = refs[14 + n_prep:14 + 2 * n_prep]
    u_ref, d_ref, x1_ref, h2_ref = refs[14 + 2 * n_prep:]
    step = pl.program_id(0)
    i = jnp.minimum(step, n_tiles - 1) % (seq_len // tm)
    half = D_FF // 2

    def mlp_up(h2, c0, c1):
        a = jnp.maximum(_dot(h2, w1_ref[:, c0:c1]), 0.0)
        return (a * a).astype(BF16)

    def body(do_mixer, do_prev_mlp):
        if do_prev_mlp:
            g2 = modb_ref[:, 5 * D_MODEL:6 * D_MODEL]
            h2_prev = h2_ref[...]
            a_lo = mlp_up(h2_prev, 0, half)

        if do_mixer:
            sh1 = moda_ref[:, 0:D_MODEL]
            sc1 = moda_ref[:, D_MODEL:2 * D_MODEL]
            x = x_ref[...]
            xe = jnp.concatenate([xp_ref[...], x, xn_ref[...]], axis=0)
            u = _dot(_rms_mod(xe, gmix_ref[...], sh1, sc1).astype(BF16), win_ref[...])
            u_ref[0:POOL_HALO, :] = jnp.where(i > 0, u[0:POOL_HALO], 0.0)
            u_ref[POOL_HALO:POOL_HALO + tm, :] = u[POOL_HALO:POOL_HALO + tm]
            u_ref[POOL_HALO + tm:POOL_HALO + tm + POOL_HALO, :] = jnp.where(
                (i + 1) * tm < seq_len, u[POOL_HALO + tm:], 0.0)

        if do_prev_mlp:
            a_hi = mlp_up(h2_prev, half, D_FF)

        if do_mixer:
            t = i * tm + lax.broadcasted_iota(jnp.int32, (tm, 1), 0)
            for g, w in enumerate(POOL_WINDOWS):
                c0, c1 = g * POOL_GROUP_DIM, (g + 1) * POOL_GROUP_DIM
                acc = u_ref[POOL_HALO - w // 2:POOL_HALO - w // 2 + tm, c0:c1]
                for s in range(-w // 2 + 1, w // 2):
                    acc = acc + u_ref[POOL_HALO + s:POOL_HALO + s + tm, c0:c1]
                lo = jnp.clip(t - w // 2, 0, seq_len)
                hi = jnp.clip(t + w // 2, 0, seq_len)
                inv_cnt = 1.0 / (hi - lo).astype(F32)
                d = acc * inv_cnt - u_ref[POOL_HALO:POOL_HALO + tm, c0:c1]
                d_ref[:, c0:c1] = _dot(d.astype(BF16), wgrp_ref[g])
            y = _dot((d_ref[...] * pscale_ref[...]).astype(BF16), wout_ref[...])

        if do_prev_mlp:
            down = _dot(jnp.concatenate([a_lo, a_hi], axis=1), w2_ref[...])

        if do_mixer:
            g1 = moda_ref[:, 2 * D_MODEL:3 * D_MODEL]
            sh2 = moda_ref[:, 3 * D_MODEL:4 * D_MODEL]
            sc2 = moda_ref[:, 4 * D_MODEL:5 * D_MODEL]
            x1 = x + g1 * y
            h2 = _rms_mod(x1, gmlp_ref[...], sh2, sc2).astype(BF16)

        if do_prev_mlp:
            o_ref[...] = x1_ref[...] + g2 * down
        if do_mixer and n_prep:
            _prepare_next_weights(prep_in, prep_out)
        if do_mixer and skew:
            x1_ref[...] = x1
            h2_ref[...] = h2
        if do_mixer and not skew:
            g2_now = moda_ref[:, 5 * D_MODEL:6 * D_MODEL]
            o_ref[...] = x1 + g2_now * _dot(mlp_up(h2, 0, D_FF), w2_ref[...])

    if skew:
        _skewed_steps(step, n_tiles, body)
    else:
        body(True, False)


def _layer0(x, mod, mod_row, layer, tm, skew, next_weights, gmix, gmlp, pscale, w_in, w_grp, w_out, w1, w2):
    nb, seq_len, _ = x.shape
    rows8 = seq_len // POOL_HALO
    tpb = tm // POOL_HALO
    nt = seq_len // tm
    n_tiles = nb * nt
    tile_a, tile_b = _skew_maps(n_tiles, nt)
    if not skew:
        tile_b = tile_a

    def x_map(t):
        b, i = tile_a(t)
        return (b, i, 0)

    def prev_map(t):
        b, i = tile_a(t)
        return (b, jnp.maximum(i * tpb - 1, 0), 0)

    def next_map(t):
        b, i = tile_a(t)
        return (b, jnp.minimum((i + 1) * tpb, rows8 - 1), 0)

    def moda_map(t):
        return (layer, tile_a(t)[0] if mod_row is None else mod_row, 0, 0)

    def modb_map(t):
        return (layer, tile_b(t)[0] if mod_row is None else mod_row, 0, 0)

    def out_map(t):
        b, i = tile_b(t)
        return (b, i, 0)

    prep_in_specs, prep_out_specs = [], []
    for w, idx in next_weights:
        rows = w.shape[1] // n_tiles
        prep_in_specs.append(pl.BlockSpec(
            (None, rows, w.shape[2]), lambda t, idx=idx: (idx, jnp.minimum(t, n_tiles - 1), 0)))
        prep_out_specs.append(pl.BlockSpec(
            (rows, w.shape[2]), lambda t: (jnp.minimum(t, n_tiles - 1), 0)))

    outs = pl.pallas_call(
        functools.partial(_layer0_kernel, seq_len, tm, n_tiles, skew, len(next_weights)),
        grid=(n_tiles + (1 if skew else 0),),
        in_specs=[
            pl.BlockSpec((None, tm, D_MODEL), x_map),
            pl.BlockSpec((None, POOL_HALO, D_MODEL), prev_map),
            pl.BlockSpec((None, POOL_HALO, D_MODEL), next_map),
            pl.BlockSpec((None, None, 1, N_MOD * D_MODEL), moda_map),
            pl.BlockSpec((None, None, 1, N_MOD * D_MODEL), modb_map),
            _const_spec((1, D_MODEL)),
            _const_spec((1, D_MODEL)),
            _const_spec((1, D_MODEL)),
            _const_spec((D_MODEL, D_MODEL)),
            _const_spec((len(POOL_WINDOWS), POOL_GROUP_DIM, POOL_GROUP_DIM)),
            _const_spec((D_MODEL, D_MODEL)),
            _const_spec((D_MODEL, D_FF)),
            _const_spec((D_FF, D_MODEL)),
        ] + prep_in_specs,
        out_specs=[pl.BlockSpec((None, tm, D_MODEL), out_map)] + prep_out_specs,
        out_shape=[jax.ShapeDtypeStruct(x.shape, F32)]
        + [jax.ShapeDtypeStruct(w.shape[1:], BF16) for w, _ in next_weights],
        scratch_shapes=[pltpu.VMEM((tm + 2 * POOL_HALO, D_MODEL), F32),
                        pltpu.VMEM((tm, D_MODEL), F32),
                        pltpu.VMEM((tm, D_MODEL), F32),
                        pltpu.VMEM((tm, D_MODEL), BF16)],
        compiler_params=pltpu.CompilerParams(
            dimension_semantics=("arbitrary",), vmem_limit_bytes=VMEM_LIMIT),
        name="layer0",
    )(x, x, x, mod, mod, gmix, gmlp, pscale, w_in, w_grp, w_out, w1, w2, *[w for w, _ in next_weights])
    return outs[0], tuple(outs[1:])


def _store_kv(kv_ref, k, v):
    lane = lax.broadcasted_iota(jnp.int32, k.shape, 1)
    head_a = (lane % HEAD_DIM) < AXIS_ROT
    kv_ref[:, 0:LANES] = jnp.where(head_a, k, pltpu.roll(k, AXIS_ROT, 1)).astype(BF16)
    kv_ref[:, LANES:2 * LANES] = jnp.where(head_a, pltpu.roll(k, LANES - AXIS_ROT, 1), k).astype(BF16)
    low = lane < HEAD_DIM
    kv_ref[:, 2 * LANES:3 * LANES] = jnp.where(low, v, 1.0).astype(BF16)
    kv_ref[:, 3 * LANES:4 * LANES] = jnp.where(low, pltpu.roll(v, HEAD_DIM, 1), 1.0).astype(BF16)


def _qkv_kernel(n_tiles, x_ref, mod_ref, gmix_ref, w_ref, cos_ref, sin_ref, q_ref, kv_ref, h_ref):
    def body(do_norm, do_project):
        if do_project:
            qkv = _dot(h_ref[...], w_ref[...])
        if do_norm:
            sh1 = mod_ref[:, 0:D_MODEL]
            sc1 = mod_ref[:, D_MODEL:2 * D_MODEL]
            h = _rms_mod(x_ref[...], gmix_ref[...], sh1, sc1).astype(BF16)
        if do_project:
            cos = cos_ref[...]
            sin = sin_ref[...]
            qscale = HEAD_DIM ** -0.5 * LOG2E
            cos_q = cos * qscale
            sin_q = sin * qscale

            def rope(t, c, s):
                return t * c + pltpu.roll(t, HEAD_DIM, 1) * s

            for cb in range(Q_DIM // LANES):
                cols = slice(cb * LANES, (cb + 1) * LANES)
                q_ref[:, cols] = rope(qkv[:, cols], cos_q, sin_q).astype(BF16)
            _store_kv(kv_ref, rope(qkv[:, Q_DIM:Q_DIM + KV_DIM], cos, sin), qkv[:, Q_DIM + KV_DIM:])
        if do_norm:
            h_ref[...] = h

    _skewed_steps(pl.program_id(0), n_tiles, body)


def _qkv(x, mod, layer, tm, gmix, w_qkv, cos, sin):
    nb, seq_len, _ = x.shape
    nt = seq_len // tm
    n_tiles = nb * nt
    tile_a, tile_b = _skew_maps(n_tiles, nt)

    def out_map(t):
        b, i = tile_b(t)
        return (b, i, 0)

    return pl.pallas_call(
        functools.partial(_qkv_kernel, n_tiles),
        grid=(n_tiles + 1,),
        in_specs=[
            pl.BlockSpec((None, tm, D_MODEL), lambda t: (*tile_a(t), 0)),
            pl.BlockSpec((None, None, 1, N_MOD * D_MODEL), lambda t: (layer, tile_a(t)[0], 0, 0)),
            _const_spec((1, D_MODEL)),
            _const_spec((D_MODEL, Q_DIM + 2 * KV_DIM)),
            pl.BlockSpec((tm, LANES), lambda t: (tile_b(t)[1], 0)),
            pl.BlockSpec((tm, LANES), lambda t: (tile_b(t)[1], 0)),
        ],
        out_specs=[pl.BlockSpec((None, tm, Q_DIM), out_map),
                   pl.BlockSpec((None, tm, KV_W), out_map)],
        out_shape=[jax.ShapeDtypeStruct((nb, seq_len, Q_DIM), BF16),
                   jax.ShapeDtypeStruct((nb, seq_len, KV_W), BF16)],
        scratch_shapes=[pltpu.VMEM((tm, D_MODEL), BF16)],
        compiler_params=pltpu.CompilerParams(
            dimension_semantics=("arbitrary",), vmem_limit_bytes=VMEM_LIMIT),
        name="qkv_rope",
    )(x, mod, gmix, w_qkv, cos, sin)


def _ctx_kv_kernel(x_ref, mod_ref, gmix_ref, w_ref, kv_ref):
    sh1 = mod_ref[:, 0:D_MODEL]
    sc1 = mod_ref[:, D_MODEL:2 * D_MODEL]
    h = _rms_mod(x_ref[...], gmix_ref[...], sh1, sc1).astype(BF16)
    kv = _dot(h, w_ref[...])
    _store_kv(kv_ref, kv[:, 0:KV_DIM], kv[:, KV_DIM:])


def _ctx_kv(ctx, mod, mod_row, layer, gmix, w_kv):
    nb, ctx_len, _ = ctx.shape
    return pl.pallas_call(
        _ctx_kv_kernel,
        grid=(nb,),
        in_specs=[
            pl.BlockSpec((None, ctx_len, D_MODEL), lambda b: (b, 0, 0)),
            pl.BlockSpec((None, None, 1, N_MOD * D_MODEL), lambda b: (layer, mod_row, 0, 0)),
            _const_spec((1, D_MODEL)),
            _const_spec((D_MODEL, 2 * KV_DIM)),
        ],
        out_specs=pl.BlockSpec((None, ctx_len, KV_W), lambda b: (b, 0, 0)),
        out_shape=jax.ShapeDtypeStruct((nb, ctx_len, KV_W), BF16),
        compiler_params=pltpu.CompilerParams(
            dimension_semantics=("arbitrary",), vmem_limit_bytes=VMEM_LIMIT),
        name="ctx_kv",
    )(ctx, mod, gmix, w_kv)


def _dot_nt(a, b):
    return lax.dot_general(a, b, (((1,), (1,)), ((), ())), preferred_element_type=F32)


def _attn_kernel(seq_len, tq, ctx_len, sink_ref, q_ref, kvp_ref, kvm_ref, kvn_ref, kvc_ref, o_ref,
                 kv_ext, *bufs):
    s_refs, e_refs, m_refs = bufs[0:2], bufs[2:4], bufs[4:6]
    i = pl.program_id(1)
    kv_ext[0:BLOCK, :] = kvp_ref[...]
    kv_ext[BLOCK:BLOCK + tq, :] = kvm_ref[...]
    kv_ext[BLOCK + tq:2 * BLOCK + tq, :] = kvn_ref[...]
    nblk = tq // BLOCK
    nwin = 3 * BLOCK
    nkey = nwin + ctx_len
    row = lax.broadcasted_iota(jnp.int32, (BLOCK, BLOCK), 0)
    col = lax.broadcasted_iota(jnp.int32, (BLOCK, BLOCK), 1)
    lane = lax.broadcasted_iota(jnp.int32, (BLOCK, LANES), 1)
    low = lane < HEAD_DIM
    head_a = (lane % HEAD_DIM) < AXIS_ROT

    chunks = [(g, t0) for g in range(GQA_GROUP) for t0 in range(0, BLOCK, SOFTMAX_ROWS)]

    def scores(j, kvh):
        q0 = pl.multiple_of(j * BLOCK, BLOCK)
        per_head = []
        for g in range(GQA_GROUP):
            h = kvh * GQA_GROUP + g
            pair = q_ref[pl.ds(q0, BLOCK), (h // 2) * LANES:(h // 2 + 1) * LANES]
            mine = head_a if h % 2 == 0 else jnp.logical_not(head_a)
            per_head.append(jnp.where(mine, pair, jnp.zeros_like(pair)))
        qs = jnp.concatenate(per_head, axis=0)
        kw = kv_ext[pl.ds(q0, nwin), kvh * LANES:(kvh + 1) * LANES]
        s_refs[kvh][:, 0:nwin] = _dot_nt(qs, kw)
        s_refs[kvh][:, nwin:nkey] = _dot_nt(qs, kvc_ref[:, kvh * LANES:(kvh + 1) * LANES])

    def softmax(j, kvh):
        mask_l = (col >= row) & (i * tq + (j - 1) * BLOCK >= 0)
        mask_r = (col <= row) & (i * tq + (j + 1) * BLOCK < seq_len)

        def masked_pieces(rows, trows):
            pieces = [s_refs[kvh][rows, p * LANES:(p + 1) * LANES] for p in range(nkey // LANES)]
            pieces[0] = jnp.where(mask_l[trows], pieces[0], NEG)
            pieces[2] = jnp.where(mask_r[trows], pieces[2], NEG)
            return pieces

        for g, t0 in chunks:
            rows = slice(g * BLOCK + t0, g * BLOCK + t0 + SOFTMAX_ROWS)
            pieces = masked_pieces(rows, slice(t0, t0 + SOFTMAX_ROWS))
            m = pieces[0]
            for p in pieces[1:]:
                m = jnp.maximum(m, p)
            m = jnp.maximum(jnp.max(m, axis=1, keepdims=True), sink_ref[kvh * GQA_GROUP + g] * LOG2E)
            m_refs[kvh][rows, :] = jnp.broadcast_to(m, (SOFTMAX_ROWS, LANES))
        for g, t0 in chunks:
            rows = slice(g * BLOCK + t0, g * BLOCK + t0 + SOFTMAX_ROWS)
            pieces = masked_pieces(rows, slice(t0, t0 + SOFTMAX_ROWS))
            m = m_refs[kvh][rows, :]
            for p, piece in enumerate(pieces):
                e_refs[kvh][rows, p * LANES:(p + 1) * LANES] = jnp.exp2((piece - m).astype(BF16))

    def weighted_values(j, kvh):
        q0 = pl.multiple_of(j * BLOCK, BLOCK)
        vcols = slice((N_KV_HEADS + kvh) * LANES, (N_KV_HEADS + kvh + 1) * LANES)
        o = (_dot(e_refs[kvh][:, 0:nwin], kv_ext[pl.ds(q0, nwin), vcols])
             + _dot(e_refs[kvh][:, nwin:nkey], kvc_ref[:, vcols]))
        on = []
        for g in range(GQA_GROUP):
            rows = slice(g * BLOCK, (g + 1) * BLOCK)
            sink_term = jnp.exp2(sink_ref[kvh * GQA_GROUP + g] * LOG2E - m_refs[kvh][rows, :])
            on.append(o[rows] * pltpu.roll(1.0 / (o[rows] + sink_term), HEAD_DIM, 1))
        for pair in range(GQA_GROUP // 2):
            tile = jnp.where(low, on[2 * pair], pltpu.roll(on[2 * pair + 1], HEAD_DIM, 1))
            c0 = (kvh * GQA_GROUP + 2 * pair) * HEAD_DIM
            o_ref[pl.ds(q0, BLOCK), c0:c0 + LANES] = tile.astype(BF16)

    def body(j, carry):
        scores(j, 0)
        scores(j, 1)
        softmax(j, 0)
        weighted_values(j, 0)
        softmax(j, 1)
        weighted_values(j, 1)
        return carry

    lax.fori_loop(0, nblk, body, 0)


def _attention(q, kv, kvc, sink, tq):
    nb, seq_len, _ = q.shape
    ctx_len = kvc.shape[1]
    nblk = seq_len // BLOCK
    qpb = tq // BLOCK
    rows = GQA_GROUP * BLOCK
    nkey = 3 * BLOCK + ctx_len
    return pl.pallas_call(
        functools.partial(_attn_kernel, seq_len, tq, ctx_len),
        grid=(nb, seq_len // tq),
        in_specs=[
            pl.BlockSpec(memory_space=pltpu.SMEM),
            pl.BlockSpec((None, tq, Q_DIM), lambda b, i: (b, i, 0)),
            pl.BlockSpec((None, BLOCK, KV_W), lambda b, i: (b, jnp.maximum(i * qpb - 1, 0), 0)),
            pl.BlockSpec((None, tq, KV_W), lambda b, i: (b, i, 0)),
            pl.BlockSpec((None, BLOCK, KV_W),
                         lambda b, i: (b, jnp.minimum((i + 1) * qpb, nblk - 1), 0)),
            pl.BlockSpec((None, ctx_len, KV_W), lambda b, i: (b, 0, 0)),
        ],
        out_specs=pl.BlockSpec((None, tq, Q_DIM), lambda b, i: (b, i, 0)),
        out_shape=jax.ShapeDtypeStruct((nb, seq_len, Q_DIM), BF16),
        scratch_shapes=([pltpu.VMEM((tq + 2 * BLOCK, KV_W), BF16)]
                        + [pltpu.VMEM((rows, nkey), F32)] * N_KV_HEADS
                        + [pltpu.VMEM((rows, nkey), BF16)] * N_KV_HEADS
                        + [pltpu.VMEM((rows, LANES), F32)] * N_KV_HEADS),
        compiler_params=pltpu.CompilerParams(
            dimension_semantics=("arbitrary", "arbitrary"), vmem_limit_bytes=VMEM_LIMIT),
        name="attention",
    )(sink, q, kv, kv, kv, kvc)


def _post_kernel(x_ref, o_ref_in, mod_ref, gmlp_ref, gfin_ref, wo_ref, w1_ref, w2_ref, out_ref):
    g1 = mod_ref[:, 2 * D_MODEL:3 * D_MODEL]
    x1 = x_ref[...] + g1 * _dot(o_ref_in[...], wo_ref[...])
    x2 = _mlp_residual(x1, mod_ref, gmlp_ref, w1_ref, w2_ref)
    y = x2 * lax.rsqrt(jnp.mean(x2 * x2, axis=-1, keepdims=True) + EPS)
    out_ref[...] = y * gfin_ref[...]


def _post(x, o, mod, layer, tm, gmlp, gfin, w_o, w1, w2):
    nb, seq_len, _ = x.shape
    return pl.pallas_call(
        _post_kernel,
        grid=(nb, seq_len // tm),
        in_specs=[
            pl.BlockSpec((None, tm, D_MODEL), lambda b, i: (b, i, 0)),
            pl.BlockSpec((None, tm, Q_DIM), lambda b, i: (b, i, 0)),
            pl.BlockSpec((None, None, 1, N_MOD * D_MODEL), lambda b, i: (layer, b, 0, 0)),
            _const_spec((1, D_MODEL)),
            _const_spec((1, D_MODEL)),
            _const_spec((Q_DIM, D_MODEL)),
            _const_spec((D_MODEL, D_FF)),
            _const_spec((D_FF, D_MODEL)),
        ],
        out_specs=pl.BlockSpec((None, tm, D_MODEL), lambda b, i: (b, i, 0)),
        out_shape=jax.ShapeDtypeStruct(x.shape, F32),
        compiler_params=pltpu.CompilerParams(
            dimension_semantics=("arbitrary", "arbitrary"), vmem_limit_bytes=VMEM_LIMIT),
        name="post_attn",
    )(x, o, mod, gmlp, gfin, w_o, w1, w2)


def _rope_tables(seq_len):
    rows = seq_len // GRID_W
    row = jnp.repeat(jnp.arange(rows), GRID_W).astype(F32)
    col = jnp.tile(jnp.arange(GRID_W), rows).astype(F32)
    inv = ROPE_BASE ** (-jnp.arange(0, AXIS_ROT, 2, dtype=F32) / AXIS_ROT)
    ang = jnp.concatenate([row[:, None] * inv[None], col[:, None] * inv[None]], axis=-1)
    cos = jnp.tile(jnp.cos(ang), (1, LANES // AXIS_ROT))
    sin = jnp.sin(ang)
    return cos, jnp.concatenate([-sin, -sin, sin, sin], axis=-1)


def kernel(x, c, ctx, c_ctx, ada_w, ada_b, norm_mix_g, norm_mlp_g, pool_w_in, pool_w_grp,
           pool_scale, pool_w_out, attn_w_qkv, attn_sink, attn_w_o, mlp_w1, mlp_w2, final_g):
    nb, seq_len, _ = x.shape
    ctx_len = ctx.shape[1]
    ctx_row = nb
    c8 = jnp.concatenate(
        [c, c_ctx[None], jnp.zeros((MOD_ROWS - nb - 1, D_MODEL), F32)], axis=0)
    mod = _modulation(c8, ada_w, ada_b)

    gmix = norm_mix_g.reshape(DEPTH, 1, D_MODEL)
    gmlp = norm_mlp_g.reshape(DEPTH, 1, D_MODEL)

    l0 = (gmix[0], gmlp[0], pool_scale[0].reshape(1, D_MODEL), pool_w_in[0].astype(BF16),
          pool_w_grp[0].astype(BF16), pool_w_out[0].astype(BF16),
          mlp_w1[0].astype(BF16), mlp_w2[0].astype(BF16))
    layer1_weights = ((attn_w_qkv, 0), (attn_w_o, 0), (mlp_w1, 1), (mlp_w2, 1))
    x1, (w_qkv, w_o, w1, w2) = _layer0(x, mod, None, 0, 512, True, layer1_weights, *l0)
    ctx1, _ = _layer0(ctx, mod, ctx_row, 0, ctx_len, False, (), *l0)

    cos, sin = _rope_tables(seq_len)
    q, kv = _qkv(x1, mod, 1, 1024, gmix[1], w_qkv, cos, sin)
    kvc = _ctx_kv(ctx1, mod, ctx_row, 1, gmix[1], w_qkv[:, Q_DIM:])
    o = _attention(q, kv, kvc, attn_sink[0], 2048)
    return _post(x1, o, mod, 1, 512, gmlp[1], final_g.reshape(1, D_MODEL), w_o, w1, w2)
```

```python
import functools

import jax
import jax.numpy as jnp
from jax import lax
from jax.experimental import pallas as pl
from jax.experimental.pallas import tpu as pltpu

D_MODEL = 1024
DEPTH = 2
GRID_W = 64
POOL_WINDOWS = (2, 4, 8, 16)
POOL_GROUP_DIM = D_MODEL // len(POOL_WINDOWS)
POOL_HALO = 8
HEAD_DIM = 64
N_HEADS = D_MODEL // HEAD_DIM
N_KV_HEADS = 2
GQA_GROUP = N_HEADS // N_KV_HEADS
Q_DIM = N_HEADS * HEAD_DIM
KV_DIM = N_KV_HEADS * HEAD_DIM
WINDOW = 128
BLOCK = 128
ROPE_BASE = 10000.0
AXIS_ROT = HEAD_DIM // 2
ROT_HALF = AXIS_ROT // 2
D_FF = 4 * D_MODEL
N_MOD = 6
EPS = 1e-6
NEG = -1e30
MOD_ROWS = 8
LANES = 128
LOG2E = 1.4426950408889634
KV_W = 4 * LANES
SOFTMAX_ROWS = 32
VMEM_LIMIT = 56 * 1024 * 1024

BF16 = jnp.bfloat16
F32 = jnp.float32


def _const_spec(shape):
    nd = len(shape)
    return pl.BlockSpec(shape, lambda *_: (0,) * nd, pipeline_mode=pl.Buffered(1))


def _dot(a, b):
    return jnp.dot(a, b, preferred_element_type=F32)


def _rms_mod(x, g, shift, scale):
    y = x * lax.rsqrt(jnp.mean(x * x, axis=-1, keepdims=True) + EPS)
    return y * (g * (1.0 + scale)) + shift


def _mlp_residual(x, mod_ref, g_ref, w1_ref, w2_ref):
    sh2 = mod_ref[:, 3 * D_MODEL:4 * D_MODEL]
    sc2 = mod_ref[:, 4 * D_MODEL:5 * D_MODEL]
    g2 = mod_ref[:, 5 * D_MODEL:6 * D_MODEL]
    h = _rms_mod(x, g_ref[...], sh2, sc2).astype(BF16)
    a = jnp.maximum(_dot(h, w1_ref[...]), 0.0)
    a = (a * a).astype(BF16)
    return x + g2 * _dot(a, w2_ref[...])


def _staged_steps(step, n_tiles, n_stages, body):
    assert n_tiles >= n_stages
    for t in range(n_stages - 1):
        pl.when(step == t)(functools.partial(body, *[s <= t for s in range(n_stages)]))
    pl.when((step >= n_stages - 1) & (step < n_tiles))(functools.partial(body, *[True] * n_stages))
    for k in range(n_stages - 1):
        pl.when(step == n_tiles + k)(functools.partial(body, *[s > k for s in range(n_stages)]))


def _skewed_steps(step, n_tiles, body):
    _staged_steps(step, n_tiles, 2, body)


def _stage_tile(n_tiles, tiles_per_row, stage):
    def tile(t):
        k = jnp.clip(t - stage, 0, n_tiles - 1)
        return k // tiles_per_row, k % tiles_per_row
    return tile


def _skew_maps(n_tiles, tiles_per_row):
    return _stage_tile(n_tiles, tiles_per_row, 0), _stage_tile(n_tiles, tiles_per_row, 1)


def _mod_kernel(n_prep, c_ref, w_ref, b_ref, *refs):
    prep_in, o_ref, prep_out = refs[:n_prep], refs[n_prep], refs[n_prep + 1:]
    c = c_ref[...]
    s = (c * jax.nn.sigmoid(c)).astype(BF16)
    r = _dot(s, w_ref[...].astype(BF16)) + b_ref[...]
    for row in range(MOD_ROWS):
        o_ref[row] = r[row:row + 1, :]
    for src, dst in zip(prep_in, prep_out):
        dst[...] = src[...].astype(BF16)


def _modulation(c8, ada_w, ada_b, first_layer_weights):
    tn = 1536
    n = N_MOD * D_MODEL
    n_col = n // tn
    steps = DEPTH * n_col
    prep_in_specs, prep_out_specs = [], []
    for w, idx in first_layer_weights:
        rows = w.shape[1] // steps
        prep_in_specs.append(pl.BlockSpec(
            (None, rows, w.shape[2]), lambda i, j, idx=idx: (idx, i * n_col + j, 0)))
        prep_out_specs.append(pl.BlockSpec((rows, w.shape[2]), lambda i, j: (i * n_col + j, 0)))
    outs = pl.pallas_call(
        functools.partial(_mod_kernel, len(first_layer_weights)),
        grid=(DEPTH, n_col),
        in_specs=[
            pl.BlockSpec((MOD_ROWS, D_MODEL), lambda i, j: (0, 0)),
            pl.BlockSpec((None, D_MODEL, tn), lambda i, j: (i, 0, j)),
            pl.BlockSpec((None, 1, tn), lambda i, j: (i, 0, j)),
        ] + prep_in_specs,
        out_specs=[pl.BlockSpec((None, MOD_ROWS, 1, tn), lambda i, j: (i, 0, 0, j))] + prep_out_specs,
        out_shape=[jax.ShapeDtypeStruct((DEPTH, MOD_ROWS, 1, n), F32)]
        + [jax.ShapeDtypeStruct(w.shape[1:], BF16) for w, _ in first_layer_weights],
        compiler_params=pltpu.CompilerParams(
            dimension_semantics=("arbitrary", "arbitrary"), vmem_limit_bytes=VMEM_LIMIT),
        name="modulation",
    )(c8, ada_w, ada_b.reshape(DEPTH, 1, n), *[w for w, _ in first_layer_weights])
    return outs[0], tuple(outs[1:])


def _to_pair_layout(v):
    lane = lax.broadcasted_iota(jnp.int32, v.shape, 1)
    for width in (ROT_HALF, AXIS_ROT):
        hi = (lane // (2 * width)) % 2
        lo = (lane // width) % 2
        v = jnp.where(hi == lo, v, jnp.where(hi == 0, pltpu.roll(v, LANES - width, 1),
                                             pltpu.roll(v, width, 1)))
    return v


def _prepare_next_weights(prep_in, prep_out):
    wqkv_in, wqkv_out = prep_in[0], prep_out[0]
    for cb in range((Q_DIM + KV_DIM) // LANES):
        cols = slice(cb * LANES, (cb + 1) * LANES)
        wqkv_out[:, cols] = _to_pair_layout(wqkv_in[:, cols]).astype(BF16)
    wqkv_out[:, Q_DIM + KV_DIM:] = wqkv_in[:, Q_DIM + KV_DIM:].astype(BF16)
    for src, dst in zip(prep_in[1:], prep_out[1:]):
        dst[...] = src[...].astype(BF16)


def _layer0_kernel(seq_len, tm, n_tiles, skew, n_prep, *refs):
    (x_ref, xp_ref, xn_ref, moda_ref, modb_ref, gmix_ref, gmlp_ref, pscale_ref, win_ref, wgrp_ref,
     wout_ref, w1_ref, w2_ref) = refs[:13]
    prep_in = refs[13:13 + n_prep]
    o_ref = refs[13 + n_prep]
    prep_out = refs[14 + n_prep:14 + 2 * n_prep]
    u_ref, d_ref, x1_ref, h2_ref = refs[14 + 2 * n_prep:]
    step = pl.program_id(0)
    i = jnp.minimum(step, n_tiles - 1) % (seq_len // tm)
    half = D_FF // 2

    def mlp_up(h2, c0, c1):
        a = jnp.maximum(_dot(h2, w1_ref[:, c0:c1]), 0.0)
        return (a * a).astype(BF16)

    def body(do_mixer, do_prev_mlp):
        if do_prev_mlp:
            g2 = modb_ref[:, 5 * D_MODEL:6 * D_MODEL]
            h2_prev = h2_ref[...]
            a_lo = mlp_up(h2_prev, 0, half)

        if do_mixer:
            sh1 = moda_ref[:, 0:D_MODEL]
            sc1 = moda_ref[:, D_MODEL:2 * D_MODEL]
            x = x_ref[...]
            xe = jnp.concatenate([xp_ref[...], x, xn_ref[...]], axis=0)
            u = _dot(_rms_mod(xe, gmix_ref[...], sh1, sc1).astype(BF16), win_ref[...])
            u_ref[0:POOL_HALO, :] = jnp.where(i > 0, u[0:POOL_HALO], 0.0)
            u_ref[POOL_HALO:POOL_HALO + tm, :] = u[POOL_HALO:POOL_HALO + tm]
            u_ref[POOL_HALO + tm:POOL_HALO + tm + POOL_HALO, :] = jnp.where(
                (i + 1) * tm < seq_len, u[POOL_HALO + tm:], 0.0)

        if do_prev_mlp:
            a_hi = mlp_up(h2_prev, half, D_FF)

        if do_mixer:
            t = i * tm + lax.broadcasted_iota(jnp.int32, (tm, 1), 0)
            for g, w in enumerate(POOL_WINDOWS):
                c0, c1 = g * POOL_GROUP_DIM, (g + 1) * POOL_GROUP_DIM
                acc = u_ref[POOL_HALO - w // 2:POOL_HALO - w // 2 + tm, c0:c1]
                for s in range(-w // 2 + 1, w // 2):
                    acc = acc + u_ref[POOL_HALO + s:POOL_HALO + s + tm, c0:c1]
                lo = jnp.clip(t - w // 2, 0, seq_len)
                hi = jnp.clip(t + w // 2, 0, seq_len)
                inv_cnt = 1.0 / (hi - lo).astype(F32)
                d = acc * inv_cnt - u_ref[POOL_HALO:POOL_HALO + tm, c0:c1]
                d_ref[:, c0:c1] = _dot(d.astype(BF16), wgrp_ref[g])
            y = _dot((d_ref[...] * pscale_ref[...]).astype(BF16), wout_ref[...])

        if do_prev_mlp:
            down = _dot(jnp.concatenate([a_lo, a_hi], axis=1), w2_ref[...])

        if do_mixer:
            g1 = moda_ref[:, 2 * D_MODEL:3 * D_MODEL]
            sh2 = moda_ref[:, 3 * D_MODEL:4 * D_MODEL]
            sc2 = moda_ref[:, 4 * D_MODEL:5 * D_MODEL]
            x1 = x + g1 * y
            h2 = _rms_mod(x1, gmlp_ref[...], sh2, sc2).astype(BF16)

        if do_prev_mlp:
            o_ref[...] = x1_ref[...] + g2 * down
        if do_mixer and n_prep:
            _prepare_next_weights(prep_in, prep_out)
        if do_mixer and skew:
            x1_ref[...] = x1
            h2_ref[...] = h2
        if do_mixer and not skew:
            g2_now = moda_ref[:, 5 * D_MODEL:6 * D_MODEL]
            o_ref[...] = x1 + g2_now * _dot(mlp_up(h2, 0, D_FF), w2_ref[...])

    if skew:
        _skewed_steps(step, n_tiles, body)
    else:
        body(True, False)


def _layer0(x, mod, mod_row, layer, tm, skew, next_weights, gmix, gmlp, pscale, w_in, w_grp, w_out, w1, w2):
    nb, seq_len, _ = x.shape
    rows8 = seq_len // POOL_HALO
    tpb = tm // POOL_HALO
    nt = seq_len // tm
    n_tiles = nb * nt
    tile_a, tile_b = _skew_maps(n_tiles, nt)
    if not skew:
        tile_b = tile_a

    def x_map(t):
        b, i = tile_a(t)
        return (b, i, 0)

    def prev_map(t):
        b, i = tile_a(t)
        return (b, jnp.maximum(i * tpb - 1, 0), 0)

    def next_map(t):
        b, i = tile_a(t)
        return (b, jnp.minimum((i + 1) * tpb, rows8 - 1), 0)

    def moda_map(t):
        return (layer, tile_a(t)[0] if mod_row is None else mod_row, 0, 0)

    def modb_map(t):
        return (layer, tile_b(t)[0] if mod_row is None else mod_row, 0, 0)

    def out_map(t):
        b, i = tile_b(t)
        return (b, i, 0)

    prep_in_specs, prep_out_specs = [], []
    for w, idx in next_weights:
        rows = w.shape[1] // n_tiles
        prep_in_specs.append(pl.BlockSpec(
            (None, rows, w.shape[2]), lambda t, idx=idx: (idx, jnp.minimum(t, n_tiles - 1), 0)))
        prep_out_specs.append(pl.BlockSpec(
            (rows, w.shape[2]), lambda t: (jnp.minimum(t, n_tiles - 1), 0)))

    outs = pl.pallas_call(
        functools.partial(_layer0_kernel, seq_len, tm, n_tiles, skew, len(next_weights)),
        grid=(n_tiles + (1 if skew else 0),),
        in_specs=[
            pl.BlockSpec((None, tm, D_MODEL), x_map),
            pl.BlockSpec((None, POOL_HALO, D_MODEL), prev_map),
            pl.BlockSpec((None, POOL_HALO, D_MODEL), next_map),
            pl.BlockSpec((None, None, 1, N_MOD * D_MODEL), moda_map),
            pl.BlockSpec((None, None, 1, N_MOD * D_MODEL), modb_map),
            _const_spec((1, D_MODEL)),
            _const_spec((1, D_MODEL)),
            _const_spec((1, D_MODEL)),
            _const_spec((D_MODEL, D_MODEL)),
            _const_spec((len(POOL_WINDOWS), POOL_GROUP_DIM, POOL_GROUP_DIM)),
            _const_spec((D_MODEL, D_MODEL)),
            _const_spec((D_MODEL, D_FF)),
            _const_spec((D_FF, D_MODEL)),
        ] + prep_in_specs,
        out_specs=[pl.BlockSpec((None, tm, D_MODEL), out_map)] + prep_out_specs,
        out_shape=[jax.ShapeDtypeStruct(x.shape, F32)]
        + [jax.ShapeDtypeStruct(w.shape[1:], BF16) for w, _ in next_weights],
        scratch_shapes=[pltpu.VMEM((tm + 2 * POOL_HALO, D_MODEL), F32),
                        pltpu.VMEM((tm, D_MODEL), F32),
                        pltpu.VMEM((tm, D_MODEL), F32),
                        pltpu.VMEM((tm, D_MODEL), BF16)],
        compiler_params=pltpu.CompilerParams(
            dimension_semantics=("arbitrary",), vmem_limit_bytes=VMEM_LIMIT),
        name="layer0",
    )(x, x, x, mod, mod, gmix, gmlp, pscale, w_in, w_grp, w_out, w1, w2, *[w for w, _ in next_weights])
    return outs[0], tuple(outs[1:])


def _store_kv(kv_ref, k, v):
    lane = lax.broadcasted_iota(jnp.int32, k.shape, 1)
    head_a = (lane % HEAD_DIM) < AXIS_ROT
    kv_ref[:, 0:LANES] = jnp.where(head_a, k, pltpu.roll(k, AXIS_ROT, 1)).astype(BF16)
    kv_ref[:, LANES:2 * LANES] = jnp.where(head_a, pltpu.roll(k, LANES - AXIS_ROT, 1), k).astype(BF16)
    low = lane < HEAD_DIM
    kv_ref[:, 2 * LANES:3 * LANES] = jnp.where(low, v, 1.0).astype(BF16)
    kv_ref[:, 3 * LANES:4 * LANES] = jnp.where(low, pltpu.roll(v, HEAD_DIM, 1), 1.0).astype(BF16)


def _qkv_kernel(n_tiles, x_ref, mod_ref, gmix_ref, w_ref, cos_ref, sin_ref, q_ref, kv_ref, h_ref):
    def body(do_norm, do_project):
        if do_project:
            qkv = _dot(h_ref[...], w_ref[...])
        if do_norm:
            sh1 = mod_ref[:, 0:D_MODEL]
            sc1 = mod_ref[:, D_MODEL:2 * D_MODEL]
            h = _rms_mod(x_ref[...], gmix_ref[...], sh1, sc1).astype(BF16)
        if do_project:
            cos = cos_ref[...]
            sin = sin_ref[...]
            qscale = HEAD_DIM ** -0.5 * LOG2E
            cos_q = cos * qscale
            sin_q = sin * qscale

            def rope(t, c, s):
                return t * c + pltpu.roll(t, HEAD_DIM, 1) * s

            for cb in range(Q_DIM // LANES):
                cols = slice(cb * LANES, (cb + 1) * LANES)
                q_ref[:, cols] = rope(qkv[:, cols], cos_q, sin_q).astype(BF16)
            _store_kv(kv_ref, rope(qkv[:, Q_DIM:Q_DIM + KV_DIM], cos, sin), qkv[:, Q_DIM + KV_DIM:])
        if do_norm:
            h_ref[...] = h

    _skewed_steps(pl.program_id(0), n_tiles, body)


def _qkv(x, mod, layer, tm, gmix, w_qkv, cos, sin):
    nb, seq_len, _ = x.shape
    nt = seq_len // tm
    n_tiles = nb * nt
    tile_a, tile_b = _skew_maps(n_tiles, nt)

    def out_map(t):
        b, i = tile_b(t)
        return (b, i, 0)

    return pl.pallas_call(
        functools.partial(_qkv_kernel, n_tiles),
        grid=(n_tiles + 1,),
        in_specs=[
            pl.BlockSpec((None, tm, D_MODEL), lambda t: (*tile_a(t), 0)),
            pl.BlockSpec((None, None, 1, N_MOD * D_MODEL), lambda t: (layer, tile_a(t)[0], 0, 0)),
            _const_spec((1, D_MODEL)),
            _const_spec((D_MODEL, Q_DIM + 2 * KV_DIM)),
            pl.BlockSpec((tm, LANES), lambda t: (tile_b(t)[1], 0)),
            pl.BlockSpec((tm, LANES), lambda t: (tile_b(t)[1], 0)),
        ],
        out_specs=[pl.BlockSpec((None, tm, Q_DIM), out_map),
                   pl.BlockSpec((None, tm, KV_W), out_map)],
        out_shape=[jax.ShapeDtypeStruct((nb, seq_len, Q_DIM), BF16),
                   jax.ShapeDtypeStruct((nb, seq_len, KV_W), BF16)],
        scratch_shapes=[pltpu.VMEM((tm, D_MODEL), BF16)],
        compiler_params=pltpu.CompilerParams(
            dimension_semantics=("arbitrary",), vmem_limit_bytes=VMEM_LIMIT),
        name="qkv_rope",
    )(x, mod, gmix, w_qkv, cos, sin)


def _ctx_kv_kernel(x_ref, mod_ref, gmix_ref, w_ref, kv_ref):
    sh1 = mod_ref[:, 0:D_MODEL]
    sc1 = mod_ref[:, D_MODEL:2 * D_MODEL]
    h = _rms_mod(x_ref[...], gmix_ref[...], sh1, sc1).astype(BF16)
    kv = _dot(h, w_ref[...])
    _store_kv(kv_ref, kv[:, 0:KV_DIM], kv[:, KV_DIM:])


def _ctx_kv(ctx, mod, mod_row, layer, gmix, w_kv):
    nb, ctx_len, _ = ctx.shape
    return pl.pallas_call(
        _ctx_kv_kernel,
        grid=(nb,),
        in_specs=[
            pl.BlockSpec((None, ctx_len, D_MODEL), lambda b: (b, 0, 0)),
            pl.BlockSpec((None, None, 1, N_MOD * D_MODEL), lambda b: (layer, mod_row, 0, 0)),
            _const_spec((1, D_MODEL)),
            _const_spec((D_MODEL, 2 * KV_DIM)),
        ],
        out_specs=pl.BlockSpec((None, ctx_len, KV_W), lambda b: (b, 0, 0)),
        out_shape=jax.ShapeDtypeStruct((nb, ctx_len, KV_W), BF16),
        compiler_params=pltpu.CompilerParams(
            dimension_semantics=("arbitrary",), vmem_limit_bytes=VMEM_LIMIT),
        name="ctx_kv",
    )(ctx, mod, gmix, w_kv)


def _dot_nt(a, b):
    return lax.dot_general(a, b, (((1,), (1,)), ((), ())), preferred_element_type=F32)


def _attn_kernel(seq_len, tq, ctx_len, sink_ref, q_ref, kvp_ref, kvm_ref, kvn_ref, kvc_ref, o_ref,
                 kv_ext, *bufs):
    s_refs, e_refs, m_refs = bufs[0:2], bufs[2:4], bufs[4:6]
    i = pl.program_id(1)
    kv_ext[0:BLOCK, :] = kvp_ref[...]
    kv_ext[BLOCK:BLOCK + tq, :] = kvm_ref[...]
    kv_ext[BLOCK + tq:2 * BLOCK + tq, :] = kvn_ref[...]
    nblk = tq // BLOCK
    nwin = 3 * BLOCK
    nkey = nwin + ctx_len
    row = lax.broadcasted_iota(jnp.int32, (BLOCK, BLOCK), 0)
    col = lax.broadcasted_iota(jnp.int32, (BLOCK, BLOCK), 1)
    lane = lax.broadcasted_iota(jnp.int32, (BLOCK, LANES), 1)
    low = lane < HEAD_DIM
    head_a = (lane % HEAD_DIM) < AXIS_ROT

    chunks = [(g, t0) for g in range(GQA_GROUP) for t0 in range(0, BLOCK, SOFTMAX_ROWS)]

    def scores(j, kvh):
        q0 = pl.multiple_of(j * BLOCK, BLOCK)
        per_head = []
        for g in range(GQA_GROUP):
            h = kvh * GQA_GROUP + g
            pair = q_ref[pl.ds(q0, BLOCK), (h // 2) * LANES:(h // 2 + 1) * LANES]
            mine = head_a if h % 2 == 0 else jnp.logical_not(head_a)
            per_head.append(jnp.where(mine, pair, jnp.zeros_like(pair)))
        qs = jnp.concatenate(per_head, axis=0)
        kw = kv_ext[pl.ds(q0, nwin), kvh * LANES:(kvh + 1) * LANES]
        s_refs[kvh][:, 0:nwin] = _dot_nt(qs, kw)
        s_refs[kvh][:, nwin:nkey] = _dot_nt(qs, kvc_ref[:, kvh * LANES:(kvh + 1) * LANES])

    def softmax(j, kvh):
        mask_l = (col >= row) & (i * tq + (j - 1) * BLOCK >= 0)
        mask_r = (col <= row) & (i * tq + (j + 1) * BLOCK < seq_len)

        def masked_pieces(rows, trows):
            pieces = [s_refs[kvh][rows, p * LANES:(p + 1) * LANES] for p in range(nkey // LANES)]
            pieces[0] = jnp.where(mask_l[trows], pieces[0], NEG)
            pieces[2] = jnp.where(mask_r[trows], pieces[2], NEG)
            return pieces

        for g, t0 in chunks:
            rows = slice(g * BLOCK + t0, g * BLOCK + t0 + SOFTMAX_ROWS)
            pieces = masked_pieces(rows, slice(t0, t0 + SOFTMAX_ROWS))
            m = pieces[0]
            for p in pieces[1:]:
                m = jnp.maximum(m, p)
            m = jnp.maximum(jnp.max(m, axis=1, keepdims=True), sink_ref[kvh * GQA_GROUP + g] * LOG2E)
            m_refs[kvh][rows, :] = jnp.broadcast_to(m, (SOFTMAX_ROWS, LANES))
        for g, t0 in chunks:
            rows = slice(g * BLOCK + t0, g * BLOCK + t0 + SOFTMAX_ROWS)
            pieces = masked_pieces(rows, slice(t0, t0 + SOFTMAX_ROWS))
            m = m_refs[kvh][rows, :]
            for p, piece in enumerate(pieces):
                e_refs[kvh][rows, p * LANES:(p + 1) * LANES] = jnp.exp2((piece - m).astype(BF16))

    def weighted_values(j, kvh):
        q0 = pl.multiple_of(j * BLOCK, BLOCK)
        vcols = slice((N_KV_HEADS + kvh) * LANES, (N_KV_HEADS + kvh + 1) * LANES)
        values = jnp.concatenate([kv_ext[pl.ds(q0, nwin), vcols], kvc_ref[:, vcols]], axis=0)
        o = _dot(e_refs[kvh][...], values)
        on = []
        for g in range(GQA_GROUP):
            rows = slice(g * BLOCK, (g + 1) * BLOCK)
            sink_term = jnp.exp2(sink_ref[kvh * GQA_GROUP + g] * LOG2E - m_refs[kvh][rows, :])
            on.append(o[rows] * pltpu.roll(1.0 / (o[rows] + sink_term), HEAD_DIM, 1))
        for pair in range(GQA_GROUP // 2):
            tile = jnp.where(low, on[2 * pair], pltpu.roll(on[2 * pair + 1], HEAD_DIM, 1))
            c0 = (kvh * GQA_GROUP + 2 * pair) * HEAD_DIM
            o_ref[pl.ds(q0, BLOCK), c0:c0 + LANES] = tile.astype(BF16)

    def body(j, carry):
        scores(j, 0)
        scores(j, 1)
        softmax(j, 0)
        weighted_values(j, 0)
        softmax(j, 1)
        weighted_values(j, 1)
        return carry

    lax.fori_loop(0, nblk, body, 0)


def _attention(q, kv, kvc, sink, tq):
    nb, seq_len, _ = q.shape
    ctx_len = kvc.shape[1]
    nblk = seq_len // BLOCK
    qpb = tq // BLOCK
    rows = GQA_GROUP * BLOCK
    nkey = 3 * BLOCK + ctx_len
    return pl.pallas_call(
        functools.partial(_attn_kernel, seq_len, tq, ctx_len),
        grid=(nb, seq_len // tq),
        in_specs=[
            pl.BlockSpec(memory_space=pltpu.SMEM),
            pl.BlockSpec((None, tq, Q_DIM), lambda b, i: (b, i, 0)),
            pl.BlockSpec((None, BLOCK, KV_W), lambda b, i: (b, jnp.maximum(i * qpb - 1, 0), 0)),
            pl.BlockSpec((None, tq, KV_W), lambda b, i: (b, i, 0)),
            pl.BlockSpec((None, BLOCK, KV_W),
                         lambda b, i: (b, jnp.minimum((i + 1) * qpb, nblk - 1), 0)),
            pl.BlockSpec((None, ctx_len, KV_W), lambda b, i: (b, 0, 0)),
        ],
        out_specs=pl.BlockSpec((None, tq, Q_DIM), lambda b, i: (b, i, 0)),
        out_shape=jax.ShapeDtypeStruct((nb, seq_len, Q_DIM), BF16),
        scratch_shapes=([pltpu.VMEM((tq + 2 * BLOCK, KV_W), BF16)]
                        + [pltpu.VMEM((rows, nkey), F32)] * N_KV_HEADS
                        + [pltpu.VMEM((rows, nkey), BF16)] * N_KV_HEADS
                        + [pltpu.VMEM((rows, LANES), F32)] * N_KV_HEADS),
        compiler_params=pltpu.CompilerParams(
            dimension_semantics=("arbitrary", "arbitrary"), vmem_limit_bytes=VMEM_LIMIT),
        name="attention",
    )(sink, q, kv, kv, kv, kvc)


def _post_kernel(n_tiles, x_ref, o_ref_in, moda_ref, modb_ref, gmlp_ref, gfin_ref, wo_ref, w1_ref, w2_ref,
                 out_ref, x1_ref, h2_ref, x2_ref):
    half = D_FF // 2

    def mlp_up(h2, c0, c1):
        a = jnp.maximum(_dot(h2, w1_ref[:, c0:c1]), 0.0)
        return (a * a).astype(BF16)

    def body(do_proj, do_mlp, do_final):
        if do_final:
            x2 = x2_ref[...]
            y = x2 * lax.rsqrt(jnp.mean(x2 * x2, axis=-1, keepdims=True) + EPS)
            out_ref[...] = y * gfin_ref[...]
        if do_proj:
            g1 = moda_ref[:, 2 * D_MODEL:3 * D_MODEL]
            sh2 = moda_ref[:, 3 * D_MODEL:4 * D_MODEL]
            sc2 = moda_ref[:, 4 * D_MODEL:5 * D_MODEL]
            x1 = x_ref[...] + g1 * _dot(o_ref_in[...], wo_ref[...])
        if do_mlp:
            h2_prev = h2_ref[...]
            a_lo = mlp_up(h2_prev, 0, half)
        if do_proj:
            h2 = _rms_mod(x1, gmlp_ref[...], sh2, sc2).astype(BF16)
        if do_mlp:
            g2 = modb_ref[:, 5 * D_MODEL:6 * D_MODEL]
            a_hi = mlp_up(h2_prev, half, D_FF)
            down = _dot(jnp.concatenate([a_lo, a_hi], axis=1), w2_ref[...])
            x2_ref[...] = x1_ref[...] + g2 * down
        if do_proj:
            x1_ref[...] = x1
            h2_ref[...] = h2

    _staged_steps(pl.program_id(0), n_tiles, 3, body)


def _post(x, o, mod, layer, tm, gmlp, gfin, w_o, w1, w2):
    nb, seq_len, _ = x.shape
    nt = seq_len // tm
    n_tiles = nb * nt
    tile_a, tile_b, tile_c = (_stage_tile(n_tiles, nt, s) for s in range(3))
    return pl.pallas_call(
        functools.partial(_post_kernel, n_tiles),
        grid=(n_tiles + 2,),
        in_specs=[
            pl.BlockSpec((None, tm, D_MODEL), lambda t: (*tile_a(t), 0)),
            pl.BlockSpec((None, tm, Q_DIM), lambda t: (*tile_a(t), 0)),
            pl.BlockSpec((None, None, 1, N_MOD * D_MODEL), lambda t: (layer, tile_a(t)[0], 0, 0)),
            pl.BlockSpec((None, None, 1, N_MOD * D_MODEL), lambda t: (layer, tile_b(t)[0], 0, 0)),
            _const_spec((1, D_MODEL)),
            _const_spec((1, D_MODEL)),
            _const_spec((Q_DIM, D_MODEL)),
            _const_spec((D_MODEL, D_FF)),
            _const_spec((D_FF, D_MODEL)),
        ],
        out_specs=pl.BlockSpec((None, tm, D_MODEL), lambda t: (*tile_c(t), 0)),
        out_shape=jax.ShapeDtypeStruct(x.shape, F32),
        scratch_shapes=[pltpu.VMEM((tm, D_MODEL), F32),
                        pltpu.VMEM((tm, D_MODEL), BF16),
                        pltpu.VMEM((tm, D_MODEL), F32)],
        compiler_params=pltpu.CompilerParams(
            dimension_semantics=("arbitrary",), vmem_limit_bytes=VMEM_LIMIT),
        name="post_attn",
    )(x, o, mod, mod, gmlp, gfin, w_o, w1, w2)


def _rope_tables(seq_len):
    rows = seq_len // GRID_W
    row = jnp.repeat(jnp.arange(rows), GRID_W).astype(F32)
    col = jnp.tile(jnp.arange(GRID_W), rows).astype(F32)
    inv = ROPE_BASE ** (-jnp.arange(0, AXIS_ROT, 2, dtype=F32) / AXIS_ROT)
    ang = jnp.concatenate([row[:, None] * inv[None], col[:, None] * inv[None]], axis=-1)
    cos = jnp.tile(jnp.cos(ang), (1, LANES // AXIS_ROT))
    sin = jnp.sin(ang)
    return cos, jnp.concatenate([-sin, -sin, sin, sin], axis=-1)


def kernel(x, c, ctx, c_ctx, ada_w, ada_b, norm_mix_g, norm_mlp_g, pool_w_in, pool_w_grp,
           pool_scale, pool_w_out, attn_w_qkv, attn_sink, attn_w_o, mlp_w1, mlp_w2, final_g):
    nb, seq_len, _ = x.shape
    ctx_len = ctx.shape[1]
    ctx_row = nb
    c8 = jnp.concatenate(
        [c, c_ctx[None], jnp.zeros((MOD_ROWS - nb - 1, D_MODEL), F32)], axis=0)
    n_grp = len(POOL_WINDOWS)
    layer0_weights = ((pool_w_in, 0), (pool_w_grp.reshape(-1, n_grp * POOL_GROUP_DIM, POOL_GROUP_DIM), 0),
                      (pool_w_out, 0), (mlp_w1, 0), (mlp_w2, 0))
    mod, (w_in, w_grp, w_out, w1_0, w2_0) = _modulation(c8, ada_w, ada_b, layer0_weights)

    gmix = norm_mix_g.reshape(DEPTH, 1, D_MODEL)
    gmlp = norm_mlp_g.reshape(DEPTH, 1, D_MODEL)

    l0 = (gmix[0], gmlp[0], pool_scale[0].reshape(1, D_MODEL), w_in,
          w_grp.reshape(n_grp, POOL_GROUP_DIM, POOL_GROUP_DIM), w_out, w1_0, w2_0)
    layer1_weights = ((attn_w_qkv, 0), (attn_w_o, 0), (mlp_w1, 1), (mlp_w2, 1))
    x1, (w_qkv, w_o, w1, w2) = _layer0(x, mod, None, 0, 512, True, layer1_weights, *l0)
    ctx1, _ = _layer0(ctx, mod, ctx_row, 0, ctx_len, False, (), *l0)

    cos, sin = _rope_tables(seq_len)
    q, kv = _qkv(x1, mod, 1, 1024, gmix[1], w_qkv, cos, sin)
    kvc = _ctx_kv(ctx1, mod, ctx_row, 1, gmix[1], w_qkv[:, Q_DIM:])
    o = _attention(q, kv, kvc, attn_sink[0], 2048)
    return _post(x1, o, mod, 1, 512, gmlp[1], final_g.reshape(1, D_MODEL), w_o, w1, w2)
```

```python
import functools

import jax
import jax.numpy as jnp
from jax import lax
from jax.experimental import pallas as pl
from jax.experimental.pallas import tpu as pltpu

D_MODEL = 1024
DEPTH = 2
GRID_W = 64
POOL_WINDOWS = (2, 4, 8, 16)
POOL_GROUP_DIM = D_MODEL // len(POOL_WINDOWS)
POOL_HALO = 8
HEAD_DIM = 64
N_HEADS = D_MODEL // HEAD_DIM
N_KV_HEADS = 2
GQA_GROUP = N_HEADS // N_KV_HEADS
Q_DIM = N_HEADS * HEAD_DIM
KV_DIM = N_KV_HEADS * HEAD_DIM
WINDOW = 128
BLOCK = 128
ROPE_BASE = 10000.0
AXIS_ROT = HEAD_DIM // 2
ROT_HALF = AXIS_ROT // 2
D_FF = 4 * D_MODEL
N_MOD = 6
EPS = 1e-6
NEG = -1e30
MOD_ROWS = 8
LANES = 128
LOG2E = 1.4426950408889634
KV_W = 4 * LANES
SOFTMAX_ROWS = 32
MIN_SAFE_DENOMINATOR_LOG2 = -64.0
VMEM_LIMIT = 56 * 1024 * 1024

BF16 = jnp.bfloat16
F32 = jnp.float32


def _const_spec(shape):
    nd = len(shape)
    return pl.BlockSpec(shape, lambda *_: (0,) * nd, pipeline_mode=pl.Buffered(1))


def _dot(a, b):
    return jnp.dot(a, b, preferred_element_type=F32)


def _rms_mod(x, g, shift, scale):
    y = x * lax.rsqrt(jnp.mean(x * x, axis=-1, keepdims=True) + EPS)
    return y * (g * (1.0 + scale)) + shift


def _mlp_residual(x, mod_ref, g_ref, w1_ref, w2_ref):
    sh2 = mod_ref[:, 3 * D_MODEL:4 * D_MODEL]
    sc2 = mod_ref[:, 4 * D_MODEL:5 * D_MODEL]
    g2 = mod_ref[:, 5 * D_MODEL:6 * D_MODEL]
    h = _rms_mod(x, g_ref[...], sh2, sc2).astype(BF16)
    a = jnp.maximum(_dot(h, w1_ref[...]), 0.0)
    a = (a * a).astype(BF16)
    return x + g2 * _dot(a, w2_ref[...])


def _staged_steps(step, n_tiles, n_stages, body):
    assert n_tiles >= n_stages
    for t in range(n_stages - 1):
        pl.when(step == t)(functools.partial(body, *[s <= t for s in range(n_stages)]))
    pl.when((step >= n_stages - 1) & (step < n_tiles))(functools.partial(body, *[True] * n_stages))
    for k in range(n_stages - 1):
        pl.when(step == n_tiles + k)(functools.partial(body, *[s > k for s in range(n_stages)]))


def _skewed_steps(step, n_tiles, body):
    _staged_steps(step, n_tiles, 2, body)


def _stage_tile(n_tiles, tiles_per_row, stage):
    def tile(t):
        k = jnp.clip(t - stage, 0, n_tiles - 1)
        return k // tiles_per_row, k % tiles_per_row
    return tile


def _skew_maps(n_tiles, tiles_per_row):
    return _stage_tile(n_tiles, tiles_per_row, 0), _stage_tile(n_tiles, tiles_per_row, 1)


def _mod_kernel(n_prep, c_ref, w_ref, b_ref, *refs):
    prep_in, o_ref, prep_out = refs[:n_prep], refs[n_prep], refs[n_prep + 1:]
    c = c_ref[...]
    s = (c * jax.nn.sigmoid(c)).astype(BF16)
    r = _dot(s, w_ref[...].astype(BF16)) + b_ref[...]
    for row in range(MOD_ROWS):
        o_ref[row] = r[row:row + 1, :]
    for src, dst in zip(prep_in, prep_out):
        dst[...] = src[...].astype(BF16)


def _modulation(c8, ada_w, ada_b, first_layer_weights):
    tn = 1536
    n = N_MOD * D_MODEL
    n_col = n // tn
    steps = DEPTH * n_col
    prep_in_specs, prep_out_specs = [], []
    for w, idx in first_layer_weights:
        rows = w.shape[1] // steps
        prep_in_specs.append(pl.BlockSpec(
            (None, rows, w.shape[2]), lambda i, j, idx=idx: (idx, i * n_col + j, 0)))
        prep_out_specs.append(pl.BlockSpec((rows, w.shape[2]), lambda i, j: (i * n_col + j, 0)))
    outs = pl.pallas_call(
        functools.partial(_mod_kernel, len(first_layer_weights)),
        grid=(DEPTH, n_col),
        in_specs=[
            pl.BlockSpec((MOD_ROWS, D_MODEL), lambda i, j: (0, 0)),
            pl.BlockSpec((None, D_MODEL, tn), lambda i, j: (i, 0, j)),
            pl.BlockSpec((None, 1, tn), lambda i, j: (i, 0, j)),
        ] + prep_in_specs,
        out_specs=[pl.BlockSpec((None, MOD_ROWS, 1, tn), lambda i, j: (i, 0, 0, j))] + prep_out_specs,
        out_shape=[jax.ShapeDtypeStruct((DEPTH, MOD_ROWS, 1, n), F32)]
        + [jax.ShapeDtypeStruct(w.shape[1:], BF16) for w, _ in first_layer_weights],
        compiler_params=pltpu.CompilerParams(
            dimension_semantics=("arbitrary", "arbitrary"), vmem_limit_bytes=VMEM_LIMIT),
        name="modulation",
    )(c8, ada_w, ada_b.reshape(DEPTH, 1, n), *[w for w, _ in first_layer_weights])
    return outs[0], tuple(outs[1:])


def _to_pair_layout(v):
    lane = lax.broadcasted_iota(jnp.int32, v.shape, 1)
    for width in (ROT_HALF, AXIS_ROT):
        hi = (lane // (2 * width)) % 2
        lo = (lane // width) % 2
        v = jnp.where(hi == lo, v, jnp.where(hi == 0, pltpu.roll(v, LANES - width, 1),
                                             pltpu.roll(v, width, 1)))
    return v


def _prepare_next_weights(prep_in, prep_out):
    wqkv_in, wqkv_out = prep_in[0], prep_out[0]
    for cb in range((Q_DIM + KV_DIM) // LANES):
        cols = slice(cb * LANES, (cb + 1) * LANES)
        wqkv_out[:, cols] = _to_pair_layout(wqkv_in[:, cols]).astype(BF16)
    wqkv_out[:, Q_DIM + KV_DIM:] = wqkv_in[:, Q_DIM + KV_DIM:].astype(BF16)
    for src, dst in zip(prep_in[1:], prep_out[1:]):
        dst[...] = src[...].astype(BF16)


def _layer0_kernel(seq_len, tm, n_tiles, skew, n_prep, *refs):
    (x_ref, xp_ref, xn_ref, moda_ref, modb_ref, gmix_ref, gmlp_ref, pscale_ref, win_ref, wgrp_ref,
     wout_ref, w1_ref, w2_ref) = refs[:13]
    prep_in = refs[13:13 + n_prep]
    o_ref = refs[13 + n_prep]
    prep_out = refs[14 + n_prep:14 + 2 * n_prep]
    u_ref, d_ref, x1_ref, h2_ref = refs[14 + 2 * n_prep:]
    step = pl.program_id(0)
    i = jnp.minimum(step, n_tiles - 1) % (seq_len // tm)
    half = D_FF // 2

    def mlp_up(h2, c0, c1):
        a = jnp.maximum(_dot(h2, w1_ref[:, c0:c1]), 0.0)
        return (a * a).astype(BF16)

    def body(do_mixer, do_prev_mlp):
        if do_prev_mlp:
            g2 = modb_ref[:, 5 * D_MODEL:6 * D_MODEL]
            h2_prev = h2_ref[...]
            a_lo = mlp_up(h2_prev, 0, half)

        if do_mixer:
            sh1 = moda_ref[:, 0:D_MODEL]
            sc1 = moda_ref[:, D_MODEL:2 * D_MODEL]
            x = x_ref[...]
            xe = jnp.concatenate([xp_ref[...], x, xn_ref[...]], axis=0)
            u = _dot(_rms_mod(xe, gmix_ref[...], sh1, sc1).astype(BF16), win_ref[...])
            u_ref[0:POOL_HALO, :] = jnp.where(i > 0, u[0:POOL_HALO], 0.0)
            u_ref[POOL_HALO:POOL_HALO + tm, :] = u[POOL_HALO:POOL_HALO + tm]
            u_ref[POOL_HALO + tm:POOL_HALO + tm + POOL_HALO, :] = jnp.where(
                (i + 1) * tm < seq_len, u[POOL_HALO + tm:], 0.0)

        if do_prev_mlp:
            a_hi = mlp_up(h2_prev, half, D_FF)

        if do_mixer:
            t = i * tm + lax.broadcasted_iota(jnp.int32, (tm, 1), 0)
            for g, w in enumerate(POOL_WINDOWS):
                c0, c1 = g * POOL_GROUP_DIM, (g + 1) * POOL_GROUP_DIM
                acc = u_ref[POOL_HALO - w // 2:POOL_HALO - w // 2 + tm, c0:c1]
                for s in range(-w // 2 + 1, w // 2):
                    acc = acc + u_ref[POOL_HALO + s:POOL_HALO + s + tm, c0:c1]
                lo = jnp.clip(t - w // 2, 0, seq_len)
                hi = jnp.clip(t + w // 2, 0, seq_len)
                inv_cnt = 1.0 / (hi - lo).astype(F32)
                d = acc * inv_cnt - u_ref[POOL_HALO:POOL_HALO + tm, c0:c1]
                d_ref[:, c0:c1] = _dot(d.astype(BF16), wgrp_ref[g])
            y = _dot((d_ref[...] * pscale_ref[...]).astype(BF16), wout_ref[...])

        if do_prev_mlp:
            down = _dot(jnp.concatenate([a_lo, a_hi], axis=1), w2_ref[...])

        if do_mixer:
            g1 = moda_ref[:, 2 * D_MODEL:3 * D_MODEL]
            sh2 = moda_ref[:, 3 * D_MODEL:4 * D_MODEL]
            sc2 = moda_ref[:, 4 * D_MODEL:5 * D_MODEL]
            x1 = x + g1 * y
            h2 = _rms_mod(x1, gmlp_ref[...], sh2, sc2).astype(BF16)

        if do_prev_mlp:
            o_ref[...] = x1_ref[...] + g2 * down
        if do_mixer and n_prep:
            _prepare_next_weights(prep_in, prep_out)
        if do_mixer and skew:
            x1_ref[...] = x1
            h2_ref[...] = h2
        if do_mixer and not skew:
            g2_now = moda_ref[:, 5 * D_MODEL:6 * D_MODEL]
            o_ref[...] = x1 + g2_now * _dot(mlp_up(h2, 0, D_FF), w2_ref[...])

    if skew:
        _skewed_steps(step, n_tiles, body)
    else:
        body(True, False)


def _layer0(x, mod, mod_row, layer, tm, skew, next_weights, gmix, gmlp, pscale, w_in, w_grp, w_out, w1, w2):
    nb, seq_len, _ = x.shape
    rows8 = seq_len // POOL_HALO
    tpb = tm // POOL_HALO
    nt = seq_len // tm
    n_tiles = nb * nt
    tile_a, tile_b = _skew_maps(n_tiles, nt)
    if not skew:
        tile_b = tile_a

    def x_map(t):
        b, i = tile_a(t)
        return (b, i, 0)

    def prev_map(t):
        b, i = tile_a(t)
        return (b, jnp.maximum(i * tpb - 1, 0), 0)

    def next_map(t):
        b, i = tile_a(t)
        return (b, jnp.minimum((i + 1) * tpb, rows8 - 1), 0)

    def moda_map(t):
        return (layer, tile_a(t)[0] if mod_row is None else mod_row, 0, 0)

    def modb_map(t):
        return (layer, tile_b(t)[0] if mod_row is None else mod_row, 0, 0)

    def out_map(t):
        b, i = tile_b(t)
        return (b, i, 0)

    prep_in_specs, prep_out_specs = [], []
    for w, idx in next_weights:
        rows = w.shape[1] // n_tiles
        prep_in_specs.append(pl.BlockSpec(
            (None, rows, w.shape[2]), lambda t, idx=idx: (idx, jnp.minimum(t, n_tiles - 1), 0)))
        prep_out_specs.append(pl.BlockSpec(
            (rows, w.shape[2]), lambda t: (jnp.minimum(t, n_tiles - 1), 0)))

    outs = pl.pallas_call(
        functools.partial(_layer0_kernel, seq_len, tm, n_tiles, skew, len(next_weights)),
        grid=(n_tiles + (1 if skew else 0),),
        in_specs=[
            pl.BlockSpec((None, tm, D_MODEL), x_map),
            pl.BlockSpec((None, POOL_HALO, D_MODEL), prev_map),
            pl.BlockSpec((None, POOL_HALO, D_MODEL), next_map),
            pl.BlockSpec((None, None, 1, N_MOD * D_MODEL), moda_map),
            pl.BlockSpec((None, None, 1, N_MOD * D_MODEL), modb_map),
            _const_spec((1, D_MODEL)),
            _const_spec((1, D_MODEL)),
            _const_spec((1, D_MODEL)),
            _const_spec((D_MODEL, D_MODEL)),
            _const_spec((len(POOL_WINDOWS), POOL_GROUP_DIM, POOL_GROUP_DIM)),
            _const_spec((D_MODEL, D_MODEL)),
            _const_spec((D_MODEL, D_FF)),
            _const_spec((D_FF, D_MODEL)),
        ] + prep_in_specs,
        out_specs=[pl.BlockSpec((None, tm, D_MODEL), out_map)] + prep_out_specs,
        out_shape=[jax.ShapeDtypeStruct(x.shape, F32)]
        + [jax.ShapeDtypeStruct(w.shape[1:], BF16) for w, _ in next_weights],
        scratch_shapes=[pltpu.VMEM((tm + 2 * POOL_HALO, D_MODEL), F32),
                        pltpu.VMEM((tm, D_MODEL), F32),
                        pltpu.VMEM((tm, D_MODEL), F32),
                        pltpu.VMEM((tm, D_MODEL), BF16)],
        compiler_params=pltpu.CompilerParams(
            dimension_semantics=("arbitrary",), vmem_limit_bytes=VMEM_LIMIT),
        name="layer0",
    )(x, x, x, mod, mod, gmix, gmlp, pscale, w_in, w_grp, w_out, w1, w2, *[w for w, _ in next_weights])
    return outs[0], tuple(outs[1:])


def _store_kv(kv_ref, k, v):
    lane = lax.broadcasted_iota(jnp.int32, k.shape, 1)
    head_a = (lane % HEAD_DIM) < AXIS_ROT
    kv_ref[:, 0:LANES] = jnp.where(head_a, k, pltpu.roll(k, AXIS_ROT, 1)).astype(BF16)
    kv_ref[:, LANES:2 * LANES] = jnp.where(head_a, pltpu.roll(k, LANES - AXIS_ROT, 1), k).astype(BF16)
    low = lane < HEAD_DIM
    kv_ref[:, 2 * LANES:3 * LANES] = jnp.where(low, v, 1.0).astype(BF16)
    kv_ref[:, 3 * LANES:4 * LANES] = jnp.where(low, pltpu.roll(v, HEAD_DIM, 1), 1.0).astype(BF16)


def _qkv_kernel(n_tiles, x_ref, mod_ref, gmix_ref, w_ref, cos_ref, sin_ref, q_ref, kv_ref, h_ref):
    def body(do_norm, do_project):
        if do_project:
            qkv = _dot(h_ref[...], w_ref[...])
        if do_norm:
            sh1 = mod_ref[:, 0:D_MODEL]
            sc1 = mod_ref[:, D_MODEL:2 * D_MODEL]
            h = _rms_mod(x_ref[...], gmix_ref[...], sh1, sc1).astype(BF16)
        if do_project:
            cos = cos_ref[...]
            sin = sin_ref[...]
            qscale = HEAD_DIM ** -0.5 * LOG2E
            cos_q = cos * qscale
            sin_q = sin * qscale

            def rope(t, c, s):
                return t * c + pltpu.roll(t, HEAD_DIM, 1) * s

            for cb in range(Q_DIM // LANES):
                cols = slice(cb * LANES, (cb + 1) * LANES)
                q_ref[:, cols] = rope(qkv[:, cols], cos_q, sin_q).astype(BF16)
            _store_kv(kv_ref, rope(qkv[:, Q_DIM:Q_DIM + KV_DIM], cos, sin), qkv[:, Q_DIM + KV_DIM:])
        if do_norm:
            h_ref[...] = h

    _skewed_steps(pl.program_id(0), n_tiles, body)


def _qkv(x, mod, layer, tm, gmix, w_qkv, cos, sin):
    nb, seq_len, _ = x.shape
    nt = seq_len // tm
    n_tiles = nb * nt
    tile_a, tile_b = _skew_maps(n_tiles, nt)

    def out_map(t):
        b, i = tile_b(t)
        return (b, i, 0)

    return pl.pallas_call(
        functools.partial(_qkv_kernel, n_tiles),
        grid=(n_tiles + 1,),
        in_specs=[
            pl.BlockSpec((None, tm, D_MODEL), lambda t: (*tile_a(t), 0)),
            pl.BlockSpec((None, None, 1, N_MOD * D_MODEL), lambda t: (layer, tile_a(t)[0], 0, 0)),
            _const_spec((1, D_MODEL)),
            _const_spec((D_MODEL, Q_DIM + 2 * KV_DIM)),
            pl.BlockSpec((tm, LANES), lambda t: (tile_b(t)[1], 0)),
            pl.BlockSpec((tm, LANES), lambda t: (tile_b(t)[1], 0)),
        ],
        out_specs=[pl.BlockSpec((None, tm, Q_DIM), out_map),
                   pl.BlockSpec((None, tm, KV_W), out_map)],
        out_shape=[jax.ShapeDtypeStruct((nb, seq_len, Q_DIM), BF16),
                   jax.ShapeDtypeStruct((nb, seq_len, KV_W), BF16)],
        scratch_shapes=[pltpu.VMEM((tm, D_MODEL), BF16)],
        compiler_params=pltpu.CompilerParams(
            dimension_semantics=("arbitrary",), vmem_limit_bytes=VMEM_LIMIT),
        name="qkv_rope",
    )(x, mod, gmix, w_qkv, cos, sin)


def _ctx_kv_kernel(x_ref, mod_ref, gmix_ref, w_ref, kv_ref):
    sh1 = mod_ref[:, 0:D_MODEL]
    sc1 = mod_ref[:, D_MODEL:2 * D_MODEL]
    h = _rms_mod(x_ref[...], gmix_ref[...], sh1, sc1).astype(BF16)
    kv = _dot(h, w_ref[...])
    _store_kv(kv_ref, kv[:, 0:KV_DIM], kv[:, KV_DIM:])


def _ctx_kv(ctx, mod, mod_row, layer, gmix, w_kv):
    nb, ctx_len, _ = ctx.shape
    return pl.pallas_call(
        _ctx_kv_kernel,
        grid=(nb,),
        in_specs=[
            pl.BlockSpec((None, ctx_len, D_MODEL), lambda b: (b, 0, 0)),
            pl.BlockSpec((None, None, 1, N_MOD * D_MODEL), lambda b: (layer, mod_row, 0, 0)),
            _const_spec((1, D_MODEL)),
            _const_spec((D_MODEL, 2 * KV_DIM)),
        ],
        out_specs=pl.BlockSpec((None, ctx_len, KV_W), lambda b: (b, 0, 0)),
        out_shape=jax.ShapeDtypeStruct((nb, ctx_len, KV_W), BF16),
        compiler_params=pltpu.CompilerParams(
            dimension_semantics=("arbitrary",), vmem_limit_bytes=VMEM_LIMIT),
        name="ctx_kv",
    )(ctx, mod, gmix, w_kv)


def _dot_nt(a, b):
    return lax.dot_general(a, b, (((1,), (1,)), ((), ())), preferred_element_type=F32)


def _attn_kernel(seq_len, tq, ctx_len, sink_ref, q_ref, kvp_ref, kvm_ref, kvn_ref, kvc_ref, o_ref,
                 kv_ext, *bufs):
    s_refs, e_refs, m_refs = bufs[0:2], bufs[2:6], bufs[6:10]
    kn_ref, minden_ref = bufs[10], bufs[11]
    i = pl.program_id(1)
    kv_ext[0:BLOCK, :] = kvp_ref[...]
    kv_ext[BLOCK:BLOCK + tq, :] = kvm_ref[...]
    kv_ext[BLOCK + tq:2 * BLOCK + tq, :] = kvn_ref[...]
    nblk = tq // BLOCK
    nwin = 3 * BLOCK
    nkey = nwin + ctx_len
    row = lax.broadcasted_iota(jnp.int32, (BLOCK, BLOCK), 0)
    col = lax.broadcasted_iota(jnp.int32, (BLOCK, BLOCK), 1)
    lane = lax.broadcasted_iota(jnp.int32, (BLOCK, LANES), 1)
    low = lane < HEAD_DIM
    head_a = (lane % HEAD_DIM) < AXIS_ROT

    chunks = [(g, t0) for g in range(GQA_GROUP) for t0 in range(0, BLOCK, SOFTMAX_ROWS)]

    def head_queries(j, kvh):
        q0 = pl.multiple_of(j * BLOCK, BLOCK)
        tiles = []
        for g in range(GQA_GROUP):
            h = kvh * GQA_GROUP + g
            pair = q_ref[pl.ds(q0, BLOCK), (h // 2) * LANES:(h // 2 + 1) * LANES]
            mine = head_a if h % 2 == 0 else jnp.logical_not(head_a)
            tiles.append(jnp.where(mine, pair, jnp.zeros_like(pair)))
        return tiles

    def window_masks(j):
        mask_l = (col >= row) & (i * tq + (j - 1) * BLOCK >= 0)
        mask_r = (col <= row) & (i * tq + (j + 1) * BLOCK < seq_len)
        return mask_l, mask_r

    def key_tiles(j, kvh):
        q0 = pl.multiple_of(j * BLOCK, BLOCK)
        return (kv_ext[pl.ds(q0, nwin), kvh * LANES:(kvh + 1) * LANES],
                kvc_ref[:, kvh * LANES:(kvh + 1) * LANES])

    def scores(j, kvh):
        qs = jnp.concatenate(head_queries(j, kvh), axis=0)
        kw, kc = key_tiles(j, kvh)
        s_refs[kvh][:, 0:nwin] = _dot_nt(qs, kw)
        s_refs[kvh][:, nwin:nkey] = _dot_nt(qs, kc)

    def softmax(j, kvh):
        mask_l, mask_r = window_masks(j)

        def masked_pieces(rows, trows):
            pieces = [s_refs[kvh][rows, p * LANES:(p + 1) * LANES] for p in range(nkey // LANES)]
            pieces[0] = jnp.where(mask_l[trows], pieces[0], NEG)
            pieces[2] = jnp.where(mask_r[trows], pieces[2], NEG)
            return pieces

        for g, t0 in chunks:
            rows = slice(g * BLOCK + t0, g * BLOCK + t0 + SOFTMAX_ROWS)
            pieces = masked_pieces(rows, slice(t0, t0 + SOFTMAX_ROWS))
            m = pieces[0]
            for p in pieces[1:]:
                m = jnp.maximum(m, p)
            m = jnp.maximum(jnp.max(m, axis=1, keepdims=True), sink_ref[kvh * GQA_GROUP + g] * LOG2E)
            m_refs[kvh][rows, :] = jnp.broadcast_to(m, (SOFTMAX_ROWS, LANES))
        for g, t0 in chunks:
            rows = slice(g * BLOCK + t0, g * BLOCK + t0 + SOFTMAX_ROWS)
            pieces = masked_pieces(rows, slice(t0, t0 + SOFTMAX_ROWS))
            m = m_refs[kvh][rows, :]
            for p, piece in enumerate(pieces):
                e_refs[kvh][rows, p * LANES:(p + 1) * LANES] = jnp.exp2((piece - m).astype(BF16))

    def weighted_values(j, kvh, slot, track_denominator):
        e_ref, m_ref = e_refs[2 * slot + kvh], m_refs[2 * slot + kvh]
        q0 = pl.multiple_of(j * BLOCK, BLOCK)
        vcols = slice((N_KV_HEADS + kvh) * LANES, (N_KV_HEADS + kvh + 1) * LANES)
        values = jnp.concatenate([kv_ext[pl.ds(q0, nwin), vcols], kvc_ref[:, vcols]], axis=0)
        o = _dot(e_ref[...], values)
        on, smallest = [], None
        for g in range(GQA_GROUP):
            rows = slice(g * BLOCK, (g + 1) * BLOCK)
            sink_term = jnp.exp2(sink_ref[kvh * GQA_GROUP + g] * LOG2E - m_ref[rows, :])
            total = o[rows] + sink_term
            smallest = total if smallest is None else jnp.minimum(smallest, total)
            on.append(o[rows] * pltpu.roll(1.0 / total, HEAD_DIM, 1))
        if track_denominator:
            minden_ref[...] = jnp.minimum(minden_ref[...], smallest)
        for pair in range(GQA_GROUP // 2):
            tile = jnp.where(low, on[2 * pair], pltpu.roll(on[2 * pair + 1], HEAD_DIM, 1))
            c0 = (kvh * GQA_GROUP + 2 * pair) * HEAD_DIM
            o_ref[pl.ds(q0, BLOCK), c0:c0 + LANES] = tile.astype(BF16)

    def exact_body(j, carry):
        scores(j, 0)
        scores(j, 1)
        softmax(j, 0)
        weighted_values(j, 0, 0, False)
        softmax(j, 1)
        weighted_values(j, 1, 0, False)
        return carry

    def max_sq_norm(k_tile):
        kf = k_tile.astype(F32)
        n2 = 0.5 * jnp.sum(kf * kf, axis=1, keepdims=True)
        return jnp.broadcast_to(jnp.max(n2, axis=0, keepdims=True), (8, LANES))

    ctx_norm = []
    for kvh in range(N_KV_HEADS):
        for kb in range(nblk + 2):
            kn_ref[kvh, kb] = max_sq_norm(kv_ext[kb * BLOCK:(kb + 1) * BLOCK, kvh * LANES:(kvh + 1) * LANES])
        n2 = None
        for cb in range(ctx_len // BLOCK):
            t = max_sq_norm(kvc_ref[cb * BLOCK:(cb + 1) * BLOCK, kvh * LANES:(kvh + 1) * LANES])
            n2 = t if n2 is None else jnp.maximum(n2, t)
        ctx_norm.append(n2)
    minden_ref[...] = jnp.full(minden_ref.shape, 3e38, F32)

    def bounded_exp(j, kvh, slot):
        e_ref, m_ref = e_refs[2 * slot + kvh], m_refs[2 * slot + kvh]
        q0 = pl.multiple_of(j * BLOCK, BLOCK)
        kw, kc = key_tiles(j, kvh)
        kn2 = jnp.maximum(jnp.max(kn_ref[kvh, pl.ds(j, 3)], axis=0), ctx_norm[kvh])
        kn = jnp.sqrt(kn2)[0:1, :]
        for pair in range(GQA_GROUP // 2):
            p = kvh * (GQA_GROUP // 2) + pair
            qf = q_ref[pl.ds(q0, BLOCK), p * LANES:(p + 1) * LANES].astype(F32)
            sq = qf * qf
            for half, mine in enumerate((head_a, jnp.logical_not(head_a))):
                g = 2 * pair + half
                qn = jnp.sqrt(jnp.sum(jnp.where(mine, sq, 0.0), axis=1, keepdims=True))
                bound = jnp.maximum(qn * kn, sink_ref[kvh * GQA_GROUP + g] * LOG2E)
                m_ref[g * BLOCK:(g + 1) * BLOCK, :] = bound
        qs = jnp.concatenate(head_queries(j, kvh), axis=0)
        s_w = _dot_nt(qs, kw)
        s_c = _dot_nt(qs, kc)
        mask_l, mask_r = window_masks(j)
        for g, t0 in chunks:
            rows = slice(g * BLOCK + t0, g * BLOCK + t0 + SOFTMAX_ROWS)
            trows = slice(t0, t0 + SOFTMAX_ROWS)
            m = m_ref[rows, :]
            pieces = ([s_w[rows, p * LANES:(p + 1) * LANES] for p in range(nwin // LANES)]
                      + [s_c[rows, p * LANES:(p + 1) * LANES] for p in range(ctx_len // LANES)])
            pieces[0] = jnp.where(mask_l[trows], pieces[0], NEG)
            pieces[2] = jnp.where(mask_r[trows], pieces[2], NEG)
            for p, piece in enumerate(pieces):
                e_ref[rows, p * LANES:(p + 1) * LANES] = jnp.exp2(piece - m).astype(BF16)

    bounded_exp(0, 0, 0)
    bounded_exp(0, 1, 0)

    def fast_body(jj, carry):
        j0 = 2 * jj
        for slot, (j, nxt) in enumerate(((j0, j0 + 1), (j0 + 1, jnp.minimum(j0 + 2, nblk - 1)))):
            for kvh in range(N_KV_HEADS):
                weighted_values(j, kvh, slot, True)
                bounded_exp(nxt, kvh, 1 - slot)
        return carry

    lax.fori_loop(0, nblk // 2, fast_body, 0)
    smallest = jnp.min(jnp.where(low, 3e38, minden_ref[...]))

    @pl.when(smallest < 2.0 ** MIN_SAFE_DENOMINATOR_LOG2)
    def _():
        lax.fori_loop(0, nblk, exact_body, 0)


def _attention(q, kv, kvc, sink, tq):
    nb, seq_len, _ = q.shape
    ctx_len = kvc.shape[1]
    nblk = seq_len // BLOCK
    qpb = tq // BLOCK
    rows = GQA_GROUP * BLOCK
    nkey = 3 * BLOCK + ctx_len
    return pl.pallas_call(
        functools.partial(_attn_kernel, seq_len, tq, ctx_len),
        grid=(nb, seq_len // tq),
        in_specs=[
            pl.BlockSpec(memory_space=pltpu.SMEM),
            pl.BlockSpec((None, tq, Q_DIM), lambda b, i: (b, i, 0)),
            pl.BlockSpec((None, BLOCK, KV_W), lambda b, i: (b, jnp.maximum(i * qpb - 1, 0), 0)),
            pl.BlockSpec((None, tq, KV_W), lambda b, i: (b, i, 0)),
            pl.BlockSpec((None, BLOCK, KV_W),
                         lambda b, i: (b, jnp.minimum((i + 1) * qpb, nblk - 1), 0)),
            pl.BlockSpec((None, ctx_len, KV_W), lambda b, i: (b, 0, 0)),
        ],
        out_specs=pl.BlockSpec((None, tq, Q_DIM), lambda b, i: (b, i, 0)),
        out_shape=jax.ShapeDtypeStruct((nb, seq_len, Q_DIM), BF16),
        scratch_shapes=([pltpu.VMEM((tq + 2 * BLOCK, KV_W), BF16)]
                        + [pltpu.VMEM((rows, nkey), F32)] * N_KV_HEADS
                        + [pltpu.VMEM((rows, nkey), BF16)] * (2 * N_KV_HEADS)
                        + [pltpu.VMEM((rows, LANES), F32)] * (2 * N_KV_HEADS)
                        + [pltpu.VMEM((N_KV_HEADS, qpb + 2, 8, LANES), F32),
                           pltpu.VMEM((BLOCK, LANES), F32)]),
        compiler_params=pltpu.CompilerParams(
            dimension_semantics=("arbitrary", "arbitrary"), vmem_limit_bytes=VMEM_LIMIT),
        name="attention",
    )(sink, q, kv, kv, kv, kvc)


def _post_kernel(x_ref, o_ref_in, mod_ref, gmlp_ref, gfin_ref, wo_ref, w1_ref, w2_ref, out_ref):
    g1 = mod_ref[:, 2 * D_MODEL:3 * D_MODEL]
    x1 = x_ref[...] + g1 * _dot(o_ref_in[...], wo_ref[...])
    x2 = _mlp_residual(x1, mod_ref, gmlp_ref, w1_ref, w2_ref)
    y = x2 * lax.rsqrt(jnp.mean(x2 * x2, axis=-1, keepdims=True) + EPS)
    out_ref[...] = y * gfin_ref[...]


def _post(x, o, mod, layer, tm, gmlp, gfin, w_o, w1, w2):
    nb, seq_len, _ = x.shape
    return pl.pallas_call(
        _post_kernel,
        grid=(nb, seq_len // tm),
        in_specs=[
            pl.BlockSpec((None, tm, D_MODEL), lambda b, i: (b, i, 0)),
            pl.BlockSpec((None, tm, Q_DIM), lambda b, i: (b, i, 0)),
            pl.BlockSpec((None, None, 1, N_MOD * D_MODEL), lambda b, i: (layer, b, 0, 0)),
            _const_spec((1, D_MODEL)),
            _const_spec((1, D_MODEL)),
            _const_spec((Q_DIM, D_MODEL)),
            _const_spec((D_MODEL, D_FF)),
            _const_spec((D_FF, D_MODEL)),
        ],
        out_specs=pl.BlockSpec((None, tm, D_MODEL), lambda b, i: (b, i, 0)),
        out_shape=jax.ShapeDtypeStruct(x.shape, F32),
        compiler_params=pltpu.CompilerParams(
            dimension_semantics=("arbitrary", "arbitrary"), vmem_limit_bytes=VMEM_LIMIT),
        name="post_attn",
    )(x, o, mod, gmlp, gfin, w_o, w1, w2)


def _rope_tables(seq_len):
    rows = seq_len // GRID_W
    row = jnp.repeat(jnp.arange(rows), GRID_W).astype(F32)
    col = jnp.tile(jnp.arange(GRID_W), rows).astype(F32)
    inv = ROPE_BASE ** (-jnp.arange(0, AXIS_ROT, 2, dtype=F32) / AXIS_ROT)
    ang = jnp.concatenate([row[:, None] * inv[None], col[:, None] * inv[None]], axis=-1)
    cos = jnp.tile(jnp.cos(ang), (1, LANES // AXIS_ROT))
    sin = jnp.sin(ang)
    return cos, jnp.concatenate([-sin, -sin, sin, sin], axis=-1)


def kernel(x, c, ctx, c_ctx, ada_w, ada_b, norm_mix_g, norm_mlp_g, pool_w_in, pool_w_grp,
           pool_scale, pool_w_out, attn_w_qkv, attn_sink, attn_w_o, mlp_w1, mlp_w2, final_g):
    nb, seq_len, _ = x.shape
    ctx_len = ctx.shape[1]
    ctx_row = nb
    c8 = jnp.concatenate(
        [c, c_ctx[None], jnp.zeros((MOD_ROWS - nb - 1, D_MODEL), F32)], axis=0)
    n_grp = len(POOL_WINDOWS)
    layer0_weights = ((pool_w_in, 0), (pool_w_grp.reshape(-1, n_grp * POOL_GROUP_DIM, POOL_GROUP_DIM), 0),
                      (pool_w_out, 0), (mlp_w1, 0), (mlp_w2, 0))
    mod, (w_in, w_grp, w_out, w1_0, w2_0) = _modulation(c8, ada_w, ada_b, layer0_weights)

    gmix = norm_mix_g.reshape(DEPTH, 1, D_MODEL)
    gmlp = norm_mlp_g.reshape(DEPTH, 1, D_MODEL)

    l0 = (gmix[0], gmlp[0], pool_scale[0].reshape(1, D_MODEL), w_in,
          w_grp.reshape(n_grp, POOL_GROUP_DIM, POOL_GROUP_DIM), w_out, w1_0, w2_0)
    layer1_weights = ((attn_w_qkv, 0), (attn_w_o, 0), (mlp_w1, 1), (mlp_w2, 1))
    x1, (w_qkv, w_o, w1, w2) = _layer0(x, mod, None, 0, 512, True, layer1_weights, *l0)
    ctx1, _ = _layer0(ctx, mod, ctx_row, 0, ctx_len, False, (), *l0)

    cos, sin = _rope_tables(seq_len)
    q, kv = _qkv(x1, mod, 1, 1024, gmix[1], w_qkv, cos, sin)
    kvc = _ctx_kv(ctx1, mod, ctx_row, 1, gmix[1], w_qkv[:, Q_DIM:])
    o = _attention(q, kv, kvc, attn_sink[0], 2048)
    return _post(x1, o, mod, 1, 512, gmlp[1], final_g.reshape(1, D_MODEL), w_o, w1, w2)
```

```python
import functools

import jax
import jax.numpy as jnp
from jax import lax
from jax.experimental import pallas as pl
from jax.experimental.pallas import tpu as pltpu

D_MODEL = 1024
DEPTH = 2
GRID_W = 64
POOL_WINDOWS = (2, 4, 8, 16)
POOL_GROUP_DIM = D_MODEL // len(POOL_WINDOWS)
POOL_HALO = 8
HEAD_DIM = 64
N_HEADS = D_MODEL // HEAD_DIM
N_KV_HEADS = 2
GQA_GROUP = N_HEADS // N_KV_HEADS
Q_DIM = N_HEADS * HEAD_DIM
KV_DIM = N_KV_HEADS * HEAD_DIM
WINDOW = 128
BLOCK = 128
ROPE_BASE = 10000.0
AXIS_ROT = HEAD_DIM // 2
ROT_HALF = AXIS_ROT // 2
D_FF = 4 * D_MODEL
N_MOD = 6
EPS = 1e-6
NEG = -1e30
MOD_ROWS = 8
LANES = 128
LOG2E = 1.4426950408889634
KV_W = 4 * LANES
SOFTMAX_ROWS = 32
MIN_SAFE_DENOMINATOR_LOG2 = -64.0
FAST_UNROLL = 2
VMEM_LIMIT = 56 * 1024 * 1024

BF16 = jnp.bfloat16
F32 = jnp.float32


def _const_spec(shape):
    nd = len(shape)
    return pl.BlockSpec(shape, lambda *_: (0,) * nd, pipeline_mode=pl.Buffered(1))


def _dot(a, b):
    return jnp.dot(a, b, preferred_element_type=F32)


def _rms_mod(x, g, shift, scale):
    y = x * lax.rsqrt(jnp.mean(x * x, axis=-1, keepdims=True) + EPS)
    return y * (g * (1.0 + scale)) + shift


def _mlp_residual(x, mod_ref, g_ref, w1_ref, w2_ref):
    sh2 = mod_ref[:, 3 * D_MODEL:4 * D_MODEL]
    sc2 = mod_ref[:, 4 * D_MODEL:5 * D_MODEL]
    g2 = mod_ref[:, 5 * D_MODEL:6 * D_MODEL]
    h = _rms_mod(x, g_ref[...], sh2, sc2).astype(BF16)
    a = jnp.maximum(_dot(h, w1_ref[...]), 0.0)
    a = (a * a).astype(BF16)
    return x + g2 * _dot(a, w2_ref[...])


def _staged_steps(step, n_tiles, n_stages, body):
    assert n_tiles >= n_stages
    for t in range(n_stages - 1):
        pl.when(step == t)(functools.partial(body, *[s <= t for s in range(n_stages)]))
    pl.when((step >= n_stages - 1) & (step < n_tiles))(functools.partial(body, *[True] * n_stages))
    for k in range(n_stages - 1):
        pl.when(step == n_tiles + k)(functools.partial(body, *[s > k for s in range(n_stages)]))


def _skewed_steps(step, n_tiles, body):
    _staged_steps(step, n_tiles, 2, body)


def _stage_tile(n_tiles, tiles_per_row, stage):
    def tile(t):
        k = jnp.clip(t - stage, 0, n_tiles - 1)
        return k // tiles_per_row, k % tiles_per_row
    return tile


def _skew_maps(n_tiles, tiles_per_row):
    return _stage_tile(n_tiles, tiles_per_row, 0), _stage_tile(n_tiles, tiles_per_row, 1)


def _mod_kernel(n_prep, c_ref, w_ref, b_ref, *refs):
    prep_in, o_ref, prep_out = refs[:n_prep], refs[n_prep], refs[n_prep + 1:]
    c = c_ref[...]
    s = (c * jax.nn.sigmoid(c)).astype(BF16)
    r = _dot(s, w_ref[...].astype(BF16)) + b_ref[...]
    for row in range(MOD_ROWS):
        o_ref[row] = r[row:row + 1, :]
    for src, dst in zip(prep_in, prep_out):
        dst[...] = src[...].astype(BF16)


def _modulation(c8, ada_w, ada_b, first_layer_weights):
    tn = 1536
    n = N_MOD * D_MODEL
    n_col = n // tn
    steps = DEPTH * n_col
    prep_in_specs, prep_out_specs = [], []
    for w, idx in first_layer_weights:
        rows = w.shape[1] // steps
        prep_in_specs.append(pl.BlockSpec(
            (None, rows, w.shape[2]), lambda i, j, idx=idx: (idx, i * n_col + j, 0)))
        prep_out_specs.append(pl.BlockSpec((rows, w.shape[2]), lambda i, j: (i * n_col + j, 0)))
    outs = pl.pallas_call(
        functools.partial(_mod_kernel, len(first_layer_weights)),
        grid=(DEPTH, n_col),
        in_specs=[
            pl.BlockSpec((MOD_ROWS, D_MODEL), lambda i, j: (0, 0)),
            pl.BlockSpec((None, D_MODEL, tn), lambda i, j: (i, 0, j)),
            pl.BlockSpec((None, 1, tn), lambda i, j: (i, 0, j)),
        ] + prep_in_specs,
        out_specs=[pl.BlockSpec((None, MOD_ROWS, 1, tn), lambda i, j: (i, 0, 0, j))] + prep_out_specs,
        out_shape=[jax.ShapeDtypeStruct((DEPTH, MOD_ROWS, 1, n), F32)]
        + [jax.ShapeDtypeStruct(w.shape[1:], BF16) for w, _ in first_layer_weights],
        compiler_params=pltpu.CompilerParams(
            dimension_semantics=("arbitrary", "arbitrary"), vmem_limit_bytes=VMEM_LIMIT),
        name="modulation",
    )(c8, ada_w, ada_b.reshape(DEPTH, 1, n), *[w for w, _ in first_layer_weights])
    return outs[0], tuple(outs[1:])


def _to_pair_layout(v):
    lane = lax.broadcasted_iota(jnp.int32, v.shape, 1)
    for width in (ROT_HALF, AXIS_ROT):
        hi = (lane // (2 * width)) % 2
        lo = (lane // width) % 2
        v = jnp.where(hi == lo, v, jnp.where(hi == 0, pltpu.roll(v, LANES - width, 1),
                                             pltpu.roll(v, width, 1)))
    return v


def _prepare_next_weights(prep_in, prep_out):
    wqkv_in, wqkv_out = prep_in[0], prep_out[0]
    for cb in range((Q_DIM + KV_DIM) // LANES):
        cols = slice(cb * LANES, (cb + 1) * LANES)
        wqkv_out[:, cols] = _to_pair_layout(wqkv_in[:, cols]).astype(BF16)
    wqkv_out[:, Q_DIM + KV_DIM:] = wqkv_in[:, Q_DIM + KV_DIM:].astype(BF16)
    for src, dst in zip(prep_in[1:], prep_out[1:]):
        dst[...] = src[...].astype(BF16)


def _layer0_kernel(seq_len, tm, n_tiles, skew, n_prep, *refs):
    (x_ref, xp_ref, xn_ref, moda_ref, modb_ref, gmix_ref, gmlp_ref, pscale_ref, win_ref, wgrp_ref,
     wout_ref, w1_ref, w2_ref) = refs[:13]
    prep_in = refs[13:13 + n_prep]
    o_ref = refs[13 + n_prep]
    prep_out = refs[14 + n_prep:14 + 2 * n_prep]
    u_ref, d_ref, x1_ref, h2_ref = refs[14 + 2 * n_prep:]
    step = pl.program_id(0)
    i = jnp.minimum(step, n_tiles - 1) % (seq_len // tm)
    half = D_FF // 2

    def mlp_up(h2, c0, c1):
        a = jnp.maximum(_dot(h2, w1_ref[:, c0:c1]), 0.0)
        return (a * a).astype(BF16)

    def body(do_mixer, do_prev_mlp):
        if do_prev_mlp:
            g2 = modb_ref[:, 5 * D_MODEL:6 * D_MODEL]
            h2_prev = h2_ref[...]
            a_lo = mlp_up(h2_prev, 0, half)

        if do_mixer:
            sh1 = moda_ref[:, 0:D_MODEL]
            sc1 = moda_ref[:, D_MODEL:2 * D_MODEL]
            x = x_ref[...]
            xe = jnp.concatenate([xp_ref[...], x, xn_ref[...]], axis=0)
            u = _dot(_rms_mod(xe, gmix_ref[...], sh1, sc1).astype(BF16), win_ref[...])
            u_ref[0:POOL_HALO, :] = jnp.where(i > 0, u[0:POOL_HALO], 0.0)
            u_ref[POOL_HALO:POOL_HALO + tm, :] = u[POOL_HALO:POOL_HALO + tm]
            u_ref[POOL_HALO + tm:POOL_HALO + tm + POOL_HALO, :] = jnp.where(
                (i + 1) * tm < seq_len, u[POOL_HALO + tm:], 0.0)

        if do_prev_mlp:
            a_hi = mlp_up(h2_prev, half, D_FF)

        if do_mixer:
            t = i * tm + lax.broadcasted_iota(jnp.int32, (tm, 1), 0)
            for g, w in enumerate(POOL_WINDOWS):
                c0, c1 = g * POOL_GROUP_DIM, (g + 1) * POOL_GROUP_DIM
                acc = u_ref[POOL_HALO - w // 2:POOL_HALO - w // 2 + tm, c0:c1]
                for s in range(-w // 2 + 1, w // 2):
                    acc = acc + u_ref[POOL_HALO + s:POOL_HALO + s + tm, c0:c1]
                lo = jnp.clip(t - w // 2, 0, seq_len)
                hi = jnp.clip(t + w // 2, 0, seq_len)
                inv_cnt = 1.0 / (hi - lo).astype(F32)
                d = acc * inv_cnt - u_ref[POOL_HALO:POOL_HALO + tm, c0:c1]
                d_ref[:, c0:c1] = _dot(d.astype(BF16), wgrp_ref[g])
            y = _dot((d_ref[...] * pscale_ref[...]).astype(BF16), wout_ref[...])

        if do_prev_mlp:
            down = _dot(jnp.concatenate([a_lo, a_hi], axis=1), w2_ref[...])

        if do_mixer:
            g1 = moda_ref[:, 2 * D_MODEL:3 * D_MODEL]
            sh2 = moda_ref[:, 3 * D_MODEL:4 * D_MODEL]
            sc2 = moda_ref[:, 4 * D_MODEL:5 * D_MODEL]
            x1 = x + g1 * y
            h2 = _rms_mod(x1, gmlp_ref[...], sh2, sc2).astype(BF16)

        if do_prev_mlp:
            o_ref[...] = x1_ref[...] + g2 * down
        if do_mixer and n_prep:
            _prepare_next_weights(prep_in, prep_out)
        if do_mixer and skew:
            x1_ref[...] = x1
            h2_ref[...] = h2
        if do_mixer and not skew:
            g2_now = moda_ref[:, 5 * D_MODEL:6 * D_MODEL]
            o_ref[...] = x1 + g2_now * _dot(mlp_up(h2, 0, D_FF), w2_ref[...])

    if skew:
        _skewed_steps(step, n_tiles, body)
    else:
        body(True, False)


def _layer0(x, mod, mod_row, layer, tm, skew, next_weights, gmix, gmlp, pscale, w_in, w_grp, w_out, w1, w2):
    nb, seq_len, _ = x.shape
    rows8 = seq_len // POOL_HALO
    tpb = tm // POOL_HALO
    nt = seq_len // tm
    n_tiles = nb * nt
    tile_a, tile_b = _skew_maps(n_tiles, nt)
    if not skew:
        tile_b = tile_a

    def x_map(t):
        b, i = tile_a(t)
        return (b, i, 0)

    def prev_map(t):
        b, i = tile_a(t)
        return (b, jnp.maximum(i * tpb - 1, 0), 0)

    def next_map(t):
        b, i = tile_a(t)
        return (b, jnp.minimum((i + 1) * tpb, rows8 - 1), 0)

    def moda_map(t):
        return (layer, tile_a(t)[0] if mod_row is None else mod_row, 0, 0)

    def modb_map(t):
        return (layer, tile_b(t)[0] if mod_row is None else mod_row, 0, 0)

    def out_map(t):
        b, i = tile_b(t)
        return (b, i, 0)

    prep_in_specs, prep_out_specs = [], []
    for w, idx in next_weights:
        rows = w.shape[1] // n_tiles
        prep_in_specs.append(pl.BlockSpec(
            (None, rows, w.shape[2]), lambda t, idx=idx: (idx, jnp.minimum(t, n_tiles - 1), 0)))
        prep_out_specs.append(pl.BlockSpec(
            (rows, w.shape[2]), lambda t: (jnp.minimum(t, n_tiles - 1), 0)))

    outs = pl.pallas_call(
        functools.partial(_layer0_kernel, seq_len, tm, n_tiles, skew, len(next_weights)),
        grid=(n_tiles + (1 if skew else 0),),
        in_specs=[
            pl.BlockSpec((None, tm, D_MODEL), x_map),
            pl.BlockSpec((None, POOL_HALO, D_MODEL), prev_map),
            pl.BlockSpec((None, POOL_HALO, D_MODEL), next_map),
            pl.BlockSpec((None, None, 1, N_MOD * D_MODEL), moda_map),
            pl.BlockSpec((None, None, 1, N_MOD * D_MODEL), modb_map),
            _const_spec((1, D_MODEL)),
            _const_spec((1, D_MODEL)),
            _const_spec((1, D_MODEL)),
            _const_spec((D_MODEL, D_MODEL)),
            _const_spec((len(POOL_WINDOWS), POOL_GROUP_DIM, POOL_GROUP_DIM)),
            _const_spec((D_MODEL, D_MODEL)),
            _const_spec((D_MODEL, D_FF)),
            _const_spec((D_FF, D_MODEL)),
        ] + prep_in_specs,
        out_specs=[pl.BlockSpec((None, tm, D_MODEL), out_map)] + prep_out_specs,
        out_shape=[jax.ShapeDtypeStruct(x.shape, F32)]
        + [jax.ShapeDtypeStruct(w.shape[1:], BF16) for w, _ in next_weights],
        scratch_shapes=[pltpu.VMEM((tm + 2 * POOL_HALO, D_MODEL), F32),
                        pltpu.VMEM((tm, D_MODEL), F32),
                        pltpu.VMEM((tm, D_MODEL), F32),
                        pltpu.VMEM((tm, D_MODEL), BF16)],
        compiler_params=pltpu.CompilerParams(
            dimension_semantics=("arbitrary",), vmem_limit_bytes=VMEM_LIMIT),
        name="layer0",
    )(x, x, x, mod, mod, gmix, gmlp, pscale, w_in, w_grp, w_out, w1, w2, *[w for w, _ in next_weights])
    return outs[0], tuple(outs[1:])


def _store_kv(kv_ref, k, v):
    lane = lax.broadcasted_iota(jnp.int32, k.shape, 1)
    head_a = (lane % HEAD_DIM) < AXIS_ROT
    kv_ref[:, 0:LANES] = jnp.where(head_a, k, pltpu.roll(k, AXIS_ROT, 1)).astype(BF16)
    kv_ref[:, LANES:2 * LANES] = jnp.where(head_a, pltpu.roll(k, LANES - AXIS_ROT, 1), k).astype(BF16)
    low = lane < HEAD_DIM
    kv_ref[:, 2 * LANES:3 * LANES] = jnp.where(low, v, 1.0).astype(BF16)
    kv_ref[:, 3 * LANES:4 * LANES] = jnp.where(low, pltpu.roll(v, HEAD_DIM, 1), 1.0).astype(BF16)


def _qkv_kernel(n_tiles, x_ref, mod_ref, gmix_ref, w_ref, cos_ref, sin_ref, q_ref, kv_ref, h_ref):
    def body(do_norm, do_project):
        if do_project:
            qkv = _dot(h_ref[...], w_ref[...])
        if do_norm:
            sh1 = mod_ref[:, 0:D_MODEL]
            sc1 = mod_ref[:, D_MODEL:2 * D_MODEL]
            h = _rms_mod(x_ref[...], gmix_ref[...], sh1, sc1).astype(BF16)
        if do_project:
            cos = cos_ref[...]
            sin = sin_ref[...]
            qscale = HEAD_DIM ** -0.5 * LOG2E
            cos_q = cos * qscale
            sin_q = sin * qscale

            def rope(t, c, s):
                return t * c + pltpu.roll(t, HEAD_DIM, 1) * s

            for cb in range(Q_DIM // LANES):
                cols = slice(cb * LANES, (cb + 1) * LANES)
                q_ref[:, cols] = rope(qkv[:, cols], cos_q, sin_q).astype(BF16)
            _store_kv(kv_ref, rope(qkv[:, Q_DIM:Q_DIM + KV_DIM], cos, sin), qkv[:, Q_DIM + KV_DIM:])
        if do_norm:
            h_ref[...] = h

    _skewed_steps(pl.program_id(0), n_tiles, body)


def _qkv(x, mod, layer, tm, gmix, w_qkv, cos, sin):
    nb, seq_len, _ = x.shape
    nt = seq_len // tm
    n_tiles = nb * nt
    tile_a, tile_b = _skew_maps(n_tiles, nt)

    def out_map(t):
        b, i = tile_b(t)
        return (b, i, 0)

    return pl.pallas_call(
        functools.partial(_qkv_kernel, n_tiles),
        grid=(n_tiles + 1,),
        in_specs=[
            pl.BlockSpec((None, tm, D_MODEL), lambda t: (*tile_a(t), 0)),
            pl.BlockSpec((None, None, 1, N_MOD * D_MODEL), lambda t: (layer, tile_a(t)[0], 0, 0)),
            _const_spec((1, D_MODEL)),
            _const_spec((D_MODEL, Q_DIM + 2 * KV_DIM)),
            pl.BlockSpec((tm, LANES), lambda t: (tile_b(t)[1], 0)),
            pl.BlockSpec((tm, LANES), lambda t: (tile_b(t)[1], 0)),
        ],
        out_specs=[pl.BlockSpec((None, tm, Q_DIM), out_map),
                   pl.BlockSpec((None, tm, KV_W), out_map)],
        out_shape=[jax.ShapeDtypeStruct((nb, seq_len, Q_DIM), BF16),
                   jax.ShapeDtypeStruct((nb, seq_len, KV_W), BF16)],
        scratch_shapes=[pltpu.VMEM((tm, D_MODEL), BF16)],
        compiler_params=pltpu.CompilerParams(
            dimension_semantics=("arbitrary",), vmem_limit_bytes=VMEM_LIMIT),
        name="qkv_rope",
    )(x, mod, gmix, w_qkv, cos, sin)


def _ctx_kv_kernel(x_ref, mod_ref, gmix_ref, w_ref, kv_ref):
    sh1 = mod_ref[:, 0:D_MODEL]
    sc1 = mod_ref[:, D_MODEL:2 * D_MODEL]
    h = _rms_mod(x_ref[...], gmix_ref[...], sh1, sc1).astype(BF16)
    kv = _dot(h, w_ref[...])
    _store_kv(kv_ref, kv[:, 0:KV_DIM], kv[:, KV_DIM:])


def _ctx_kv(ctx, mod, mod_row, layer, gmix, w_kv):
    nb, ctx_len, _ = ctx.shape
    return pl.pallas_call(
        _ctx_kv_kernel,
        grid=(nb,),
        in_specs=[
            pl.BlockSpec((None, ctx_len, D_MODEL), lambda b: (b, 0, 0)),
            pl.BlockSpec((None, None, 1, N_MOD * D_MODEL), lambda b: (layer, mod_row, 0, 0)),
            _const_spec((1, D_MODEL)),
            _const_spec((D_MODEL, 2 * KV_DIM)),
        ],
        out_specs=pl.BlockSpec((None, ctx_len, KV_W), lambda b: (b, 0, 0)),
        out_shape=jax.ShapeDtypeStruct((nb, ctx_len, KV_W), BF16),
        compiler_params=pltpu.CompilerParams(
            dimension_semantics=("arbitrary",), vmem_limit_bytes=VMEM_LIMIT),
        name="ctx_kv",
    )(ctx, mod, gmix, w_kv)


def _dot_nt(a, b):
    return lax.dot_general(a, b, (((1,), (1,)), ((), ())), preferred_element_type=F32)


def _attn_kernel(seq_len, tq, ctx_len, sink_ref, q_ref, kvp_ref, kvm_ref, kvn_ref, kvc_ref, o_ref,
                 kv_ext, *bufs):
    s_refs, e_refs, m_refs = bufs[0:2], bufs[2:6], bufs[6:10]
    kn_ref, minden_ref = bufs[10], bufs[11]
    i = pl.program_id(1)
    kv_ext[0:BLOCK, :] = kvp_ref[...]
    kv_ext[BLOCK:BLOCK + tq, :] = kvm_ref[...]
    kv_ext[BLOCK + tq:2 * BLOCK + tq, :] = kvn_ref[...]
    nblk = tq // BLOCK
    nwin = 3 * BLOCK
    nkey = nwin + ctx_len
    row = lax.broadcasted_iota(jnp.int32, (BLOCK, BLOCK), 0)
    col = lax.broadcasted_iota(jnp.int32, (BLOCK, BLOCK), 1)
    lane = lax.broadcasted_iota(jnp.int32, (BLOCK, LANES), 1)
    low = lane < HEAD_DIM
    head_a = (lane % HEAD_DIM) < AXIS_ROT

    chunks = [(g, t0) for g in range(GQA_GROUP) for t0 in range(0, BLOCK, SOFTMAX_ROWS)]

    def head_queries(j, kvh):
        q0 = pl.multiple_of(j * BLOCK, BLOCK)
        tiles = []
        for g in range(GQA_GROUP):
            h = kvh * GQA_GROUP + g
            pair = q_ref[pl.ds(q0, BLOCK), (h // 2) * LANES:(h // 2 + 1) * LANES]
            mine = head_a if h % 2 == 0 else jnp.logical_not(head_a)
            tiles.append(jnp.where(mine, pair, jnp.zeros_like(pair)))
        return tiles

    def window_masks(j):
        mask_l = (col >= row) & (i * tq + (j - 1) * BLOCK >= 0)
        mask_r = (col <= row) & (i * tq + (j + 1) * BLOCK < seq_len)
        return mask_l, mask_r

    def key_tiles(j, kvh):
        q0 = pl.multiple_of(j * BLOCK, BLOCK)
        return (kv_ext[pl.ds(q0, nwin), kvh * LANES:(kvh + 1) * LANES],
                kvc_ref[:, kvh * LANES:(kvh + 1) * LANES])

    def scores(j, kvh):
        qs = jnp.concatenate(head_queries(j, kvh), axis=0)
        kw, kc = key_tiles(j, kvh)
        s_refs[kvh][:, 0:nwin] = _dot_nt(qs, kw)
        s_refs[kvh][:, nwin:nkey] = _dot_nt(qs, kc)

    def softmax(j, kvh):
        mask_l, mask_r = window_masks(j)

        def masked_pieces(rows, trows):
            pieces = [s_refs[kvh][rows, p * LANES:(p + 1) * LANES] for p in range(nkey // LANES)]
            pieces[0] = jnp.where(mask_l[trows], pieces[0], NEG)
            pieces[2] = jnp.where(mask_r[trows], pieces[2], NEG)
            return pieces

        for g, t0 in chunks:
            rows = slice(g * BLOCK + t0, g * BLOCK + t0 + SOFTMAX_ROWS)
            pieces = masked_pieces(rows, slice(t0, t0 + SOFTMAX_ROWS))
            m = pieces[0]
            for p in pieces[1:]:
                m = jnp.maximum(m, p)
            m = jnp.maximum(jnp.max(m, axis=1, keepdims=True), sink_ref[kvh * GQA_GROUP + g] * LOG2E)
            m_refs[kvh][rows, :] = jnp.broadcast_to(m, (SOFTMAX_ROWS, LANES))
        for g, t0 in chunks:
            rows = slice(g * BLOCK + t0, g * BLOCK + t0 + SOFTMAX_ROWS)
            pieces = masked_pieces(rows, slice(t0, t0 + SOFTMAX_ROWS))
            m = m_refs[kvh][rows, :]
            for p, piece in enumerate(pieces):
                e_refs[kvh][rows, p * LANES:(p + 1) * LANES] = jnp.exp2((piece - m).astype(BF16))

    def weighted_values(j, kvh, slot, fast):
        e_ref, m_ref = e_refs[2 * slot + kvh], m_refs[2 * slot + kvh]
        q0 = pl.multiple_of(j * BLOCK, BLOCK)
        vcols = slice((N_KV_HEADS + kvh) * LANES, (N_KV_HEADS + kvh + 1) * LANES)
        values = jnp.concatenate([kv_ext[pl.ds(q0, nwin), vcols], kvc_ref[:, vcols]], axis=0)
        o = _dot(e_ref[...], values)
        on, smallest = [], None
        for g in range(GQA_GROUP):
            rows = slice(g * BLOCK, (g + 1) * BLOCK)
            sink = sink_ref[kvh * GQA_GROUP + g] * LOG2E
            if fast:
                sink_term = jnp.exp2(sink - m_ref[g * BLOCK:g * BLOCK + 1, :])
            else:
                sink_term = jnp.exp2(sink - m_ref[rows, :])
            total = o[rows] + sink_term
            smallest = total if smallest is None else jnp.minimum(smallest, total)
            on.append(o[rows] * pltpu.roll(1.0 / total, HEAD_DIM, 1))
        if fast:
            minden_ref[...] = jnp.minimum(minden_ref[...], smallest)
        for pair in range(GQA_GROUP // 2):
            tile = jnp.where(low, on[2 * pair], pltpu.roll(on[2 * pair + 1], HEAD_DIM, 1))
            c0 = (kvh * GQA_GROUP + 2 * pair) * HEAD_DIM
            o_ref[pl.ds(q0, BLOCK), c0:c0 + LANES] = tile.astype(BF16)

    def exact_body(j, carry):
        scores(j, 0)
        scores(j, 1)
        softmax(j, 0)
        weighted_values(j, 0, 0, False)
        softmax(j, 1)
        weighted_values(j, 1, 0, False)
        return carry

    def max_sq_norm(k_tile):
        kf = k_tile.astype(F32)
        n2 = 0.5 * jnp.sum(kf * kf, axis=1, keepdims=True)
        return jnp.broadcast_to(jnp.max(n2, axis=0, keepdims=True), (8, LANES))

    ctx_norm = []
    for kvh in range(N_KV_HEADS):
        for kb in range(nblk + 2):
            kn_ref[kvh, kb] = max_sq_norm(kv_ext[kb * BLOCK:(kb + 1) * BLOCK, kvh * LANES:(kvh + 1) * LANES])
        n2 = None
        for cb in range(ctx_len // BLOCK):
            t = max_sq_norm(kvc_ref[cb * BLOCK:(cb + 1) * BLOCK, kvh * LANES:(kvh + 1) * LANES])
            n2 = t if n2 is None else jnp.maximum(n2, t)
        ctx_norm.append(n2)
    minden_ref[...] = jnp.full(minden_ref.shape, 3e38, F32)

    def bounded_exp(j, kvh, slot):
        e_ref, m_ref = e_refs[2 * slot + kvh], m_refs[2 * slot + kvh]
        q0 = pl.multiple_of(j * BLOCK, BLOCK)
        kw, kc = key_tiles(j, kvh)
        kn2 = jnp.maximum(jnp.max(kn_ref[kvh, pl.ds(j, 3)], axis=0), ctx_norm[kvh])[0:1, :]
        for pair in range(GQA_GROUP // 2):
            p = kvh * (GQA_GROUP // 2) + pair
            qf = q_ref[pl.ds(q0, BLOCK), p * LANES:(p + 1) * LANES].astype(F32)
            sq = qf * qf
            for half, mine in enumerate((head_a, jnp.logical_not(head_a))):
                g = 2 * pair + half
                qn2 = jnp.max(jnp.sum(jnp.where(mine, sq, 0.0), axis=1, keepdims=True), axis=0, keepdims=True)
                bound = jnp.maximum(jnp.sqrt(qn2 * kn2), sink_ref[kvh * GQA_GROUP + g] * LOG2E)
                m_ref[g * BLOCK:(g + 1) * BLOCK, :] = jnp.broadcast_to(bound, (BLOCK, LANES))
        qs = jnp.concatenate(head_queries(j, kvh), axis=0)
        s_w = _dot_nt(qs, kw)
        s_c = _dot_nt(qs, kc)
        mask_l, mask_r = window_masks(j)
        for g, t0 in chunks:
            rows = slice(g * BLOCK + t0, g * BLOCK + t0 + SOFTMAX_ROWS)
            trows = slice(t0, t0 + SOFTMAX_ROWS)
            m = m_ref[rows, :]
            pieces = ([s_w[rows, p * LANES:(p + 1) * LANES] for p in range(nwin // LANES)]
                      + [s_c[rows, p * LANES:(p + 1) * LANES] for p in range(ctx_len // LANES)])
            pieces[0] = jnp.where(mask_l[trows], pieces[0], NEG)
            pieces[2] = jnp.where(mask_r[trows], pieces[2], NEG)
            for p, piece in enumerate(pieces):
                e_ref[rows, p * LANES:(p + 1) * LANES] = jnp.exp2(piece - m).astype(BF16)

    bounded_exp(0, 0, 0)
    bounded_exp(0, 1, 0)

    def fast_body(jj, carry):
        for k in range(FAST_UNROLL):
            j = FAST_UNROLL * jj + k
            nxt = j + 1 if k + 1 < FAST_UNROLL else jnp.minimum(j + 1, nblk - 1)
            for kvh in range(N_KV_HEADS):
                weighted_values(j, kvh, k % 2, True)
                bounded_exp(nxt, kvh, (k + 1) % 2)
        return carry

    lax.fori_loop(0, nblk // FAST_UNROLL, fast_body, 0)
    smallest = jnp.min(jnp.where(low, 3e38, minden_ref[...]))

    @pl.when(smallest < 2.0 ** MIN_SAFE_DENOMINATOR_LOG2)
    def _():
        lax.fori_loop(0, nblk, exact_body, 0)


def _attention(q, kv, kvc, sink, tq):
    nb, seq_len, _ = q.shape
    ctx_len = kvc.shape[1]
    nblk = seq_len // BLOCK
    qpb = tq // BLOCK
    rows = GQA_GROUP * BLOCK
    nkey = 3 * BLOCK + ctx_len
    return pl.pallas_call(
        functools.partial(_attn_kernel, seq_len, tq, ctx_len),
        grid=(nb, seq_len // tq),
        in_specs=[
            pl.BlockSpec(memory_space=pltpu.SMEM),
            pl.BlockSpec((None, tq, Q_DIM), lambda b, i: (b, i, 0)),
            pl.BlockSpec((None, BLOCK, KV_W), lambda b, i: (b, jnp.maximum(i * qpb - 1, 0), 0)),
            pl.BlockSpec((None, tq, KV_W), lambda b, i: (b, i, 0)),
            pl.BlockSpec((None, BLOCK, KV_W),
                         lambda b, i: (b, jnp.minimum((i + 1) * qpb, nblk - 1), 0)),
            pl.BlockSpec((None, ctx_len, KV_W), lambda b, i: (b, 0, 0)),
        ],
        out_specs=pl.BlockSpec((None, tq, Q_DIM), lambda b, i: (b, i, 0)),
        out_shape=jax.ShapeDtypeStruct((nb, seq_len, Q_DIM), BF16),
        scratch_shapes=([pltpu.VMEM((tq + 2 * BLOCK, KV_W), BF16)]
                        + [pltpu.VMEM((rows, nkey), F32)] * N_KV_HEADS
                        + [pltpu.VMEM((rows, nkey), BF16)] * (2 * N_KV_HEADS)
                        + [pltpu.VMEM((rows, LANES), F32)] * (2 * N_KV_HEADS)
                        + [pltpu.VMEM((N_KV_HEADS, qpb + 2, 8, LANES), F32),
                           pltpu.VMEM((BLOCK, LANES), F32)]),
        compiler_params=pltpu.CompilerParams(
            dimension_semantics=("arbitrary", "arbitrary"), vmem_limit_bytes=VMEM_LIMIT),
        name="attention",
    )(sink, q, kv, kv, kv, kvc)


def _post_kernel(x_ref, o_ref_in, mod_ref, gmlp_ref, gfin_ref, wo_ref, w1_ref, w2_ref, out_ref):
    g1 = mod_ref[:, 2 * D_MODEL:3 * D_MODEL]
    x1 = x_ref[...] + g1 * _dot(o_ref_in[...], wo_ref[...])
    x2 = _mlp_residual(x1, mod_ref, gmlp_ref, w1_ref, w2_ref)
    y = x2 * lax.rsqrt(jnp.mean(x2 * x2, axis=-1, keepdims=True) + EPS)
    out_ref[...] = y * gfin_ref[...]


def _post(x, o, mod, layer, tm, gmlp, gfin, w_o, w1, w2):
    nb, seq_len, _ = x.shape
    return pl.pallas_call(
        _post_kernel,
        grid=(nb, seq_len // tm),
        in_specs=[
            pl.BlockSpec((None, tm, D_MODEL), lambda b, i: (b, i, 0)),
            pl.BlockSpec((None, tm, Q_DIM), lambda b, i: (b, i, 0)),
            pl.BlockSpec((None, None, 1, N_MOD * D_MODEL), lambda b, i: (layer, b, 0, 0)),
            _const_spec((1, D_MODEL)),
            _const_spec((1, D_MODEL)),
            _const_spec((Q_DIM, D_MODEL)),
            _const_spec((D_MODEL, D_FF)),
            _const_spec((D_FF, D_MODEL)),
        ],
        out_specs=pl.BlockSpec((None, tm, D_MODEL), lambda b, i: (b, i, 0)),
        out_shape=jax.ShapeDtypeStruct(x.shape, F32),
        compiler_params=pltpu.CompilerParams(
            dimension_semantics=("arbitrary", "arbitrary"), vmem_limit_bytes=VMEM_LIMIT),
        name="post_attn",
    )(x, o, mod, gmlp, gfin, w_o, w1, w2)


def _rope_tables(seq_len):
    rows = seq_len // GRID_W
    row = jnp.repeat(jnp.arange(rows), GRID_W).astype(F32)
    col = jnp.tile(jnp.arange(GRID_W), rows).astype(F32)
    inv = ROPE_BASE ** (-jnp.arange(0, AXIS_ROT, 2, dtype=F32) / AXIS_ROT)
    ang = jnp.concatenate([row[:, None] * inv[None], col[:, None] * inv[None]], axis=-1)
    cos = jnp.tile(jnp.cos(ang), (1, LANES // AXIS_ROT))
    sin = jnp.sin(ang)
    return cos, jnp.concatenate([-sin, -sin, sin, sin], axis=-1)


def kernel(x, c, ctx, c_ctx, ada_w, ada_b, norm_mix_g, norm_mlp_g, pool_w_in, pool_w_grp,
           pool_scale, pool_w_out, attn_w_qkv, attn_sink, attn_w_o, mlp_w1, mlp_w2, final_g):
    nb, seq_len, _ = x.shape
    ctx_len = ctx.shape[1]
    ctx_row = nb
    c8 = jnp.concatenate(
        [c, c_ctx[None], jnp.zeros((MOD_ROWS - nb - 1, D_MODEL), F32)], axis=0)
    n_grp = len(POOL_WINDOWS)
    layer0_weights = ((pool_w_in, 0), (pool_w_grp.reshape(-1, n_grp * POOL_GROUP_DIM, POOL_GROUP_DIM), 0),
                      (pool_w_out, 0), (mlp_w1, 0), (mlp_w2, 0))
    mod, (w_in, w_grp, w_out, w1_0, w2_0) = _modulation(c8, ada_w, ada_b, layer0_weights)

    gmix = norm_mix_g.reshape(DEPTH, 1, D_MODEL)
    gmlp = norm_mlp_g.reshape(DEPTH, 1, D_MODEL)

    l0 = (gmix[0], gmlp[0], pool_scale[0].reshape(1, D_MODEL), w_in,
          w_grp.reshape(n_grp, POOL_GROUP_DIM, POOL_GROUP_DIM), w_out, w1_0, w2_0)
    layer1_weights = ((attn_w_qkv, 0), (attn_w_o, 0), (mlp_w1, 1), (mlp_w2, 1))
    x1, (w_qkv, w_o, w1, w2) = _layer0(x, mod, None, 0, 512, True, layer1_weights, *l0)
    ctx1, _ = _layer0(ctx, mod, ctx_row, 0, ctx_len, False, (), *l0)

    cos, sin = _rope_tables(seq_len)
    q, kv = _qkv(x1, mod, 1, 1024, gmix[1], w_qkv, cos, sin)
    kvc = _ctx_kv(ctx1, mod, ctx_row, 1, gmix[1], w_qkv[:, Q_DIM:])
    o = _attention(q, kv, kvc, attn_sink[0], 2048)
    return _post(x1, o, mod, 1, 512, gmlp[1], final_g.reshape(1, D_MODEL), w_o, w1, w2)
```

```python
import functools

import jax
import jax.numpy as jnp
from jax import lax
from jax.experimental import pallas as pl
from jax.experimental.pallas import tpu as pltpu

D_MODEL = 1024
DEPTH = 2
GRID_W = 64
POOL_WINDOWS = (2, 4, 8, 16)
POOL_GROUP_DIM = D_MODEL // len(POOL_WINDOWS)
POOL_HALO = 8
HEAD_DIM = 64
N_HEADS = D_MODEL // HEAD_DIM
N_KV_HEADS = 2
GQA_GROUP = N_HEADS // N_KV_HEADS
Q_DIM = N_HEADS * HEAD_DIM
KV_DIM = N_KV_HEADS * HEAD_DIM
WINDOW = 128
BLOCK = 128
ROPE_BASE = 10000.0
AXIS_ROT = HEAD_DIM // 2
ROT_HALF = AXIS_ROT // 2
D_FF = 4 * D_MODEL
N_MOD = 6
EPS = 1e-6
NEG = -1e30
MOD_ROWS = 8
LANES = 128
LOG2E = 1.4426950408889634
KV_W = 3 * LANES
SOFTMAX_ROWS = 32
MIN_SAFE_DENOMINATOR_LOG2 = -64.0
QKV_COL_GROUP = 256
FAST_UNROLL = 2
VMEM_LIMIT = 56 * 1024 * 1024

BF16 = jnp.bfloat16
F32 = jnp.float32


def _const_spec(shape):
    nd = len(shape)
    return pl.BlockSpec(shape, lambda *_: (0,) * nd, pipeline_mode=pl.Buffered(1))


def _dot(a, b):
    return jnp.dot(a, b, preferred_element_type=F32)


def _rms_mod(x, g, shift, scale):
    y = x * lax.rsqrt(jnp.mean(x * x, axis=-1, keepdims=True) + EPS)
    return y * (g * (1.0 + scale)) + shift


def _mlp_residual(x, mod_ref, g_ref, w1_ref, w2_ref):
    sh2 = mod_ref[:, 3 * D_MODEL:4 * D_MODEL]
    sc2 = mod_ref[:, 4 * D_MODEL:5 * D_MODEL]
    g2 = mod_ref[:, 5 * D_MODEL:6 * D_MODEL]
    h = _rms_mod(x, g_ref[...], sh2, sc2).astype(BF16)
    a = jnp.maximum(_dot(h, w1_ref[...]), 0.0)
    a = (a * a).astype(BF16)
    return x + g2 * _dot(a, w2_ref[...])


def _staged_steps(step, n_tiles, n_stages, body):
    assert n_tiles >= n_stages
    for t in range(n_stages - 1):
        pl.when(step == t)(functools.partial(body, *[s <= t for s in range(n_stages)]))
    pl.when((step >= n_stages - 1) & (step < n_tiles))(functools.partial(body, *[True] * n_stages))
    for k in range(n_stages - 1):
        pl.when(step == n_tiles + k)(functools.partial(body, *[s > k for s in range(n_stages)]))


def _skewed_steps(step, n_tiles, body):
    _staged_steps(step, n_tiles, 2, body)


def _stage_tile(n_tiles, tiles_per_row, stage):
    def tile(t):
        k = jnp.clip(t - stage, 0, n_tiles - 1)
        return k // tiles_per_row, k % tiles_per_row
    return tile


def _skew_maps(n_tiles, tiles_per_row):
    return _stage_tile(n_tiles, tiles_per_row, 0), _stage_tile(n_tiles, tiles_per_row, 1)


def _mod_kernel(n_prep, c_ref, w_ref, b_ref, *refs):
    prep_in, o_ref, prep_out = refs[:n_prep], refs[n_prep], refs[n_prep + 1:]
    c = c_ref[...]
    s = (c * jax.nn.sigmoid(c)).astype(BF16)
    r = _dot(s, w_ref[...].astype(BF16)) + b_ref[...]
    for row in range(MOD_ROWS):
        o_ref[row] = r[row:row + 1, :]
    for src, dst in zip(prep_in, prep_out):
        dst[...] = src[...].astype(BF16)


def _modulation(c8, ada_w, ada_b, first_layer_weights):
    tn = 1536
    n = N_MOD * D_MODEL
    n_col = n // tn
    steps = DEPTH * n_col
    prep_in_specs, prep_out_specs = [], []
    for w, idx in first_layer_weights:
        rows = w.shape[1] // steps
        prep_in_specs.append(pl.BlockSpec(
            (None, rows, w.shape[2]), lambda i, j, idx=idx: (idx, i * n_col + j, 0)))
        prep_out_specs.append(pl.BlockSpec((rows, w.shape[2]), lambda i, j: (i * n_col + j, 0)))
    outs = pl.pallas_call(
        functools.partial(_mod_kernel, len(first_layer_weights)),
        grid=(DEPTH, n_col),
        in_specs=[
            pl.BlockSpec((MOD_ROWS, D_MODEL), lambda i, j: (0, 0)),
            pl.BlockSpec((None, D_MODEL, tn), lambda i, j: (i, 0, j)),
            pl.BlockSpec((None, 1, tn), lambda i, j: (i, 0, j)),
        ] + prep_in_specs,
        out_specs=[pl.BlockSpec((None, MOD_ROWS, 1, tn), lambda i, j: (i, 0, 0, j))] + prep_out_specs,
        out_shape=[jax.ShapeDtypeStruct((DEPTH, MOD_ROWS, 1, n), F32)]
        + [jax.ShapeDtypeStruct(w.shape[1:], BF16) for w, _ in first_layer_weights],
        compiler_params=pltpu.CompilerParams(
            dimension_semantics=("arbitrary", "arbitrary"), vmem_limit_bytes=VMEM_LIMIT),
        name="modulation",
    )(c8, ada_w, ada_b.reshape(DEPTH, 1, n), *[w for w, _ in first_layer_weights])
    return outs[0], tuple(outs[1:])


def _to_pair_layout(v):
    lane = lax.broadcasted_iota(jnp.int32, v.shape, 1)
    for width in (ROT_HALF, AXIS_ROT):
        hi = (lane // (2 * width)) % 2
        lo = (lane // width) % 2
        v = jnp.where(hi == lo, v, jnp.where(hi == 0, pltpu.roll(v, LANES - width, 1),
                                             pltpu.roll(v, width, 1)))
    return v


def _prepare_next_weights(prep_in, prep_out):
    wqkv_in, wqkv_out = prep_in[0], prep_out[0]
    low = lax.broadcasted_iota(jnp.int32, (wqkv_in.shape[0], LANES), 1) < HEAD_DIM
    for p in range(GQA_GROUP):
        first = wqkv_in[:, (p // 2) * LANES:(p // 2 + 1) * LANES]
        second = wqkv_in[:, (p // 2 + GQA_GROUP // 2) * LANES:(p // 2 + GQA_GROUP // 2 + 1) * LANES]
        if p % 2 == 0:
            both = jnp.where(low, first, pltpu.roll(second, HEAD_DIM, 1))
        else:
            both = jnp.where(low, pltpu.roll(first, HEAD_DIM, 1), second)
        wqkv_out[:, p * LANES:(p + 1) * LANES] = _to_pair_layout(both).astype(BF16)
    wqkv_out[:, Q_DIM:Q_DIM + KV_DIM] = _to_pair_layout(wqkv_in[:, Q_DIM:Q_DIM + KV_DIM]).astype(BF16)
    wqkv_out[:, Q_DIM + KV_DIM:] = wqkv_in[:, Q_DIM + KV_DIM:].astype(BF16)
    for src, dst in zip(prep_in[1:], prep_out[1:]):
        dst[...] = src[...].astype(BF16)


def _layer0_kernel(seq_len, tm, n_tiles, skew, n_prep, *refs):
    (x_ref, xp_ref, xn_ref, moda_ref, modb_ref, gmix_ref, gmlp_ref, pscale_ref, win_ref, wgrp_ref,
     wout_ref, w1_ref, w2_ref) = refs[:13]
    prep_in = refs[13:13 + n_prep]
    o_ref = refs[13 + n_prep]
    prep_out = refs[14 + n_prep:14 + 2 * n_prep]
    u_ref, d_ref, x1_ref, h2_ref = refs[14 + 2 * n_prep:]
    step = pl.program_id(0)
    i = jnp.minimum(step, n_tiles - 1) % (seq_len // tm)
    half = D_FF // 2

    def mlp_up(h2, c0, c1):
        a = jnp.maximum(_dot(h2, w1_ref[:, c0:c1]), 0.0)
        return (a * a).astype(BF16)

    def body(do_mixer, do_prev_mlp):
        if do_prev_mlp:
            g2 = modb_ref[:, 5 * D_MODEL:6 * D_MODEL]
            h2_prev = h2_ref[...]
            a_lo = mlp_up(h2_prev, 0, half)

        if do_mixer:
            sh1 = moda_ref[:, 0:D_MODEL]
            sc1 = moda_ref[:, D_MODEL:2 * D_MODEL]
            x = x_ref[...]
            xe = jnp.concatenate([xp_ref[...], x, xn_ref[...]], axis=0)
            u = _dot(_rms_mod(xe, gmix_ref[...], sh1, sc1).astype(BF16), win_ref[...])
            u_ref[0:POOL_HALO, :] = jnp.where(i > 0, u[0:POOL_HALO], 0.0)
            u_ref[POOL_HALO:POOL_HALO + tm, :] = u[POOL_HALO:POOL_HALO + tm]
            u_ref[POOL_HALO + tm:POOL_HALO + tm + POOL_HALO, :] = jnp.where(
                (i + 1) * tm < seq_len, u[POOL_HALO + tm:], 0.0)

        if do_prev_mlp:
            a_hi = mlp_up(h2_prev, half, D_FF)

        if do_mixer:
            t = i * tm + lax.broadcasted_iota(jnp.int32, (tm, 1), 0)
            for g, w in enumerate(POOL_WINDOWS):
                c0, c1 = g * POOL_GROUP_DIM, (g + 1) * POOL_GROUP_DIM
                acc = u_ref[POOL_HALO - w // 2:POOL_HALO - w // 2 + tm, c0:c1]
                for s in range(-w // 2 + 1, w // 2):
                    acc = acc + u_ref[POOL_HALO + s:POOL_HALO + s + tm, c0:c1]
                lo = jnp.clip(t - w // 2, 0, seq_len)
                hi = jnp.clip(t + w // 2, 0, seq_len)
                inv_cnt = 1.0 / (hi - lo).astype(F32)
                d = acc * inv_cnt - u_ref[POOL_HALO:POOL_HALO + tm, c0:c1]
                d_ref[:, c0:c1] = _dot(d.astype(BF16), wgrp_ref[g])
            y = _dot((d_ref[...] * pscale_ref[...]).astype(BF16), wout_ref[...])

        if do_prev_mlp:
            down = _dot(jnp.concatenate([a_lo, a_hi], axis=1), w2_ref[...])

        if do_mixer:
            g1 = moda_ref[:, 2 * D_MODEL:3 * D_MODEL]
            sh2 = moda_ref[:, 3 * D_MODEL:4 * D_MODEL]
            sc2 = moda_ref[:, 4 * D_MODEL:5 * D_MODEL]
            x1 = x + g1 * y
            h2 = _rms_mod(x1, gmlp_ref[...], sh2, sc2).astype(BF16)

        if do_prev_mlp:
            o_ref[...] = x1_ref[...] + g2 * down
        if do_mixer and n_prep:
            _prepare_next_weights(prep_in, prep_out)
        if do_mixer and skew:
            x1_ref[...] = x1
            h2_ref[...] = h2
        if do_mixer and not skew:
            g2_now = moda_ref[:, 5 * D_MODEL:6 * D_MODEL]
            o_ref[...] = x1 + g2_now * _dot(mlp_up(h2, 0, D_FF), w2_ref[...])

    if skew:
        _skewed_steps(step, n_tiles, body)
    else:
        body(True, False)


def _layer0(x, mod, mod_row, layer, tm, skew, next_weights, gmix, gmlp, pscale, w_in, w_grp, w_out, w1, w2):
    nb, seq_len, _ = x.shape
    rows8 = seq_len // POOL_HALO
    tpb = tm // POOL_HALO
    nt = seq_len // tm
    n_tiles = nb * nt
    tile_a, tile_b = _skew_maps(n_tiles, nt)
    if not skew:
        tile_b = tile_a

    def x_map(t):
        b, i = tile_a(t)
        return (b, i, 0)

    def prev_map(t):
        b, i = tile_a(t)
        return (b, jnp.maximum(i * tpb - 1, 0), 0)

    def next_map(t):
        b, i = tile_a(t)
        return (b, jnp.minimum((i + 1) * tpb, rows8 - 1), 0)

    def moda_map(t):
        return (layer, tile_a(t)[0] if mod_row is None else mod_row, 0, 0)

    def modb_map(t):
        return (layer, tile_b(t)[0] if mod_row is None else mod_row, 0, 0)

    def out_map(t):
        b, i = tile_b(t)
        return (b, i, 0)

    prep_in_specs, prep_out_specs = [], []
    for w, idx in next_weights:
        rows = w.shape[1] // n_tiles
        prep_in_specs.append(pl.BlockSpec(
            (None, rows, w.shape[2]), lambda t, idx=idx: (idx, jnp.minimum(t, n_tiles - 1), 0)))
        prep_out_specs.append(pl.BlockSpec(
            (rows, w.shape[2]), lambda t: (jnp.minimum(t, n_tiles - 1), 0)))

    outs = pl.pallas_call(
        functools.partial(_layer0_kernel, seq_len, tm, n_tiles, skew, len(next_weights)),
        grid=(n_tiles + (1 if skew else 0),),
        in_specs=[
            pl.BlockSpec((None, tm, D_MODEL), x_map),
            pl.BlockSpec((None, POOL_HALO, D_MODEL), prev_map),
            pl.BlockSpec((None, POOL_HALO, D_MODEL), next_map),
            pl.BlockSpec((None, None, 1, N_MOD * D_MODEL), moda_map),
            pl.BlockSpec((None, None, 1, N_MOD * D_MODEL), modb_map),
            _const_spec((1, D_MODEL)),
            _const_spec((1, D_MODEL)),
            _const_spec((1, D_MODEL)),
            _const_spec((D_MODEL, D_MODEL)),
            _const_spec((len(POOL_WINDOWS), POOL_GROUP_DIM, POOL_GROUP_DIM)),
            _const_spec((D_MODEL, D_MODEL)),
            _const_spec((D_MODEL, D_FF)),
            _const_spec((D_FF, D_MODEL)),
        ] + prep_in_specs,
        out_specs=[pl.BlockSpec((None, tm, D_MODEL), out_map)] + prep_out_specs,
        out_shape=[jax.ShapeDtypeStruct(x.shape, F32)]
        + [jax.ShapeDtypeStruct(w.shape[1:], BF16) for w, _ in next_weights],
        scratch_shapes=[pltpu.VMEM((tm + 2 * POOL_HALO, D_MODEL), F32),
                        pltpu.VMEM((tm, D_MODEL), F32),
                        pltpu.VMEM((tm, D_MODEL), F32),
                        pltpu.VMEM((tm, D_MODEL), BF16)],
        compiler_params=pltpu.CompilerParams(
            dimension_semantics=("arbitrary",), vmem_limit_bytes=VMEM_LIMIT),
        name="layer0",
    )(x, x, x, mod, mod, gmix, gmlp, pscale, w_in, w_grp, w_out, w1, w2, *[w for w, _ in next_weights])
    return outs[0], tuple(outs[1:])


def _store_kv(kv_ref, k, v):
    low = lax.broadcasted_iota(jnp.int32, v.shape, 1) < HEAD_DIM
    kv_ref[:, 0:LANES] = k.astype(BF16)
    kv_ref[:, LANES:2 * LANES] = jnp.where(low, v, 1.0).astype(BF16)
    kv_ref[:, 2 * LANES:3 * LANES] = jnp.where(low, 1.0, v).astype(BF16)


def _qkv_kernel(n_tiles, x_ref, mod_ref, gmix_ref, w_ref, cos_ref, sin_ref, q_ref, kv_ref, h_ref):
    def body(do_norm, do_project):
        if do_project:
            h_prev = h_ref[...]
            kv = _dot(h_prev, w_ref[:, Q_DIM:])
            q_groups = [_dot(h_prev, w_ref[:, c:c + QKV_COL_GROUP]) for c in range(0, Q_DIM, QKV_COL_GROUP)]
        if do_norm:
            sh1 = mod_ref[:, 0:D_MODEL]
            sc1 = mod_ref[:, D_MODEL:2 * D_MODEL]
            h = _rms_mod(x_ref[...], gmix_ref[...], sh1, sc1).astype(BF16)
        if do_project:
            cos = cos_ref[...]
            sin = sin_ref[...]
            qscale = HEAD_DIM ** -0.5 * LOG2E
            cos_q = cos * qscale
            sin_q = sin * qscale

            def rope(t, c, s):
                return t * c + pltpu.roll(t, HEAD_DIM, 1) * s

            _store_kv(kv_ref, rope(kv[:, 0:KV_DIM], cos, sin), kv[:, KV_DIM:])
            for gi, group in enumerate(q_groups):
                for cb in range(QKV_COL_GROUP // LANES):
                    c0 = gi * QKV_COL_GROUP + cb * LANES
                    q_ref[:, c0:c0 + LANES] = rope(group[:, cb * LANES:(cb + 1) * LANES], cos_q, sin_q).astype(BF16)
        if do_norm:
            h_ref[...] = h

    _skewed_steps(pl.program_id(0), n_tiles, body)


def _qkv(x, mod, layer, tm, gmix, w_qkv, cos, sin):
    nb, seq_len, _ = x.shape
    nt = seq_len // tm
    n_tiles = nb * nt
    tile_a, tile_b = _skew_maps(n_tiles, nt)

    def out_map(t):
        b, i = tile_b(t)
        return (b, i, 0)

    return pl.pallas_call(
        functools.partial(_qkv_kernel, n_tiles),
        grid=(n_tiles + 1,),
        in_specs=[
            pl.BlockSpec((None, tm, D_MODEL), lambda t: (*tile_a(t), 0)),
            pl.BlockSpec((None, None, 1, N_MOD * D_MODEL), lambda t: (layer, tile_a(t)[0], 0, 0)),
            _const_spec((1, D_MODEL)),
            _const_spec((D_MODEL, Q_DIM + 2 * KV_DIM)),
            pl.BlockSpec((tm, LANES), lambda t: (tile_b(t)[1], 0)),
            pl.BlockSpec((tm, LANES), lambda t: (tile_b(t)[1], 0)),
        ],
        out_specs=[pl.BlockSpec((None, tm, Q_DIM), out_map),
                   pl.BlockSpec((None, tm, KV_W), out_map)],
        out_shape=[jax.ShapeDtypeStruct((nb, seq_len, Q_DIM), BF16),
                   jax.ShapeDtypeStruct((nb, seq_len, KV_W), BF16)],
        scratch_shapes=[pltpu.VMEM((tm, D_MODEL), BF16)],
        compiler_params=pltpu.CompilerParams(
            dimension_semantics=("arbitrary",), vmem_limit_bytes=VMEM_LIMIT),
        name="qkv_rope",
    )(x, mod, gmix, w_qkv, cos, sin)


def _ctx_kv_kernel(x_ref, mod_ref, gmix_ref, w_ref, kv_ref):
    sh1 = mod_ref[:, 0:D_MODEL]
    sc1 = mod_ref[:, D_MODEL:2 * D_MODEL]
    h = _rms_mod(x_ref[...], gmix_ref[...], sh1, sc1).astype(BF16)
    kv = _dot(h, w_ref[...])
    _store_kv(kv_ref, kv[:, 0:KV_DIM], kv[:, KV_DIM:])


def _ctx_kv(ctx, mod, mod_row, layer, gmix, w_kv):
    nb, ctx_len, _ = ctx.shape
    return pl.pallas_call(
        _ctx_kv_kernel,
        grid=(nb,),
        in_specs=[
            pl.BlockSpec((None, ctx_len, D_MODEL), lambda b: (b, 0, 0)),
            pl.BlockSpec((None, None, 1, N_MOD * D_MODEL), lambda b: (layer, mod_row, 0, 0)),
            _const_spec((1, D_MODEL)),
            _const_spec((D_MODEL, 2 * KV_DIM)),
        ],
        out_specs=pl.BlockSpec((None, ctx_len, KV_W), lambda b: (b, 0, 0)),
        out_shape=jax.ShapeDtypeStruct((nb, ctx_len, KV_W), BF16),
        compiler_params=pltpu.CompilerParams(
            dimension_semantics=("arbitrary",), vmem_limit_bytes=VMEM_LIMIT),
        name="ctx_kv",
    )(ctx, mod, gmix, w_kv)


def _dot_nt(a, b):
    return lax.dot_general(a, b, (((1,), (1,)), ((), ())), preferred_element_type=F32)


def _attn_kernel(seq_len, tq, ctx_len, sink_ref, q_ref, kvp_ref, kvm_ref, kvn_ref, kvc_ref, o_ref,
                 kv_ext, *bufs):
    s_refs, e_refs, m_refs = bufs[0:2], bufs[2:6], bufs[6:10]
    kn_ref, minden_ref = bufs[10], bufs[11]
    i = pl.program_id(1)
    kv_ext[0:BLOCK, :] = kvp_ref[...]
    kv_ext[BLOCK:BLOCK + tq, :] = kvm_ref[...]
    kv_ext[BLOCK + tq:2 * BLOCK + tq, :] = kvn_ref[...]
    nblk = tq // BLOCK
    nwin = 3 * BLOCK
    nkey = nwin + ctx_len
    row = lax.broadcasted_iota(jnp.int32, (BLOCK, BLOCK), 0)
    col = lax.broadcasted_iota(jnp.int32, (BLOCK, BLOCK), 1)
    lane = lax.broadcasted_iota(jnp.int32, (BLOCK, LANES), 1)
    low = lane < HEAD_DIM
    head_a = (lane % HEAD_DIM) < AXIS_ROT
    my_lanes = (head_a, jnp.logical_not(head_a))
    den_lanes = (jnp.logical_not(low), low)

    chunks = [(g, t0) for g in range(GQA_GROUP) for t0 in range(0, BLOCK, SOFTMAX_ROWS)]

    def head_queries(j, kvh):
        q0 = pl.multiple_of(j * BLOCK, BLOCK)
        tiles = []
        for g in range(GQA_GROUP):
            pair = q_ref[pl.ds(q0, BLOCK), g * LANES:(g + 1) * LANES]
            tiles.append(jnp.where(my_lanes[kvh], pair, jnp.zeros_like(pair)))
        return tiles

    def window_masks(j):
        mask_l = (col >= row) & (i * tq + (j - 1) * BLOCK >= 0)
        mask_r = (col <= row) & (i * tq + (j + 1) * BLOCK < seq_len)
        return mask_l, mask_r

    def key_tiles(j, kvh):
        q0 = pl.multiple_of(j * BLOCK, BLOCK)
        return kv_ext[pl.ds(q0, nwin), 0:LANES], kvc_ref[:, 0:LANES]

    def scores(j, kvh):
        qs = jnp.concatenate(head_queries(j, kvh), axis=0)
        kw, kc = key_tiles(j, kvh)
        s_refs[kvh][:, 0:nwin] = _dot_nt(qs, kw)
        s_refs[kvh][:, nwin:nkey] = _dot_nt(qs, kc)

    def softmax(j, kvh):
        mask_l, mask_r = window_masks(j)

        def masked_pieces(rows, trows):
            pieces = [s_refs[kvh][rows, p * LANES:(p + 1) * LANES] for p in range(nkey // LANES)]
            pieces[0] = jnp.where(mask_l[trows], pieces[0], NEG)
            pieces[2] = jnp.where(mask_r[trows], pieces[2], NEG)
            return pieces

        for g, t0 in chunks:
            rows = slice(g * BLOCK + t0, g * BLOCK + t0 + SOFTMAX_ROWS)
            pieces = masked_pieces(rows, slice(t0, t0 + SOFTMAX_ROWS))
            m = pieces[0]
            for p in pieces[1:]:
                m = jnp.maximum(m, p)
            m = jnp.maximum(jnp.max(m, axis=1, keepdims=True), sink_ref[kvh * GQA_GROUP + g] * LOG2E)
            m_refs[kvh][rows, :] = jnp.broadcast_to(m, (SOFTMAX_ROWS, LANES))
        for g, t0 in chunks:
            rows = slice(g * BLOCK + t0, g * BLOCK + t0 + SOFTMAX_ROWS)
            pieces = masked_pieces(rows, slice(t0, t0 + SOFTMAX_ROWS))
            m = m_refs[kvh][rows, :]
            for p, piece in enumerate(pieces):
                e_refs[kvh][rows, p * LANES:(p + 1) * LANES] = jnp.exp2((piece - m).astype(BF16))

    def weighted_values(j, kvh, slot, fast):
        e_ref, m_ref = e_refs[2 * slot + kvh], m_refs[2 * slot + kvh]
        q0 = pl.multiple_of(j * BLOCK, BLOCK)
        vcols = slice((1 + kvh) * LANES, (2 + kvh) * LANES)
        values = jnp.concatenate([kv_ext[pl.ds(q0, nwin), vcols], kvc_ref[:, vcols]], axis=0)
        o = _dot(e_ref[...], values)
        on, smallest = [], None
        for g in range(GQA_GROUP):
            rows = slice(g * BLOCK, (g + 1) * BLOCK)
            sink = sink_ref[kvh * GQA_GROUP + g] * LOG2E
            if fast:
                sink_term = jnp.exp2(sink - m_ref[g * BLOCK:g * BLOCK + 1, :])
            else:
                sink_term = jnp.exp2(sink - m_ref[rows, :])
            total = o[rows] + sink_term
            smallest = total if smallest is None else jnp.minimum(smallest, total)
            on.append(o[rows] * pltpu.roll(1.0 / total, HEAD_DIM, 1))
        if fast:
            minden_ref[...] = jnp.minimum(minden_ref[...], jnp.where(den_lanes[kvh], smallest, 3e38))
        for pair in range(GQA_GROUP // 2):
            if kvh == 0:
                tile = jnp.where(low, on[2 * pair], pltpu.roll(on[2 * pair + 1], HEAD_DIM, 1))
            else:
                tile = jnp.where(low, pltpu.roll(on[2 * pair], HEAD_DIM, 1), on[2 * pair + 1])
            c0 = (kvh * GQA_GROUP + 2 * pair) * HEAD_DIM
            o_ref[pl.ds(q0, BLOCK), c0:c0 + LANES] = tile.astype(BF16)

    def exact_body(j, carry):
        scores(j, 0)
        scores(j, 1)
        softmax(j, 0)
        weighted_values(j, 0, 0, False)
        softmax(j, 1)
        weighted_values(j, 1, 0, False)
        return carry

    def max_sq_norms(k_tile):
        sq = k_tile.astype(F32)
        sq = sq * sq
        return [jnp.broadcast_to(jnp.max(jnp.sum(jnp.where(my_lanes[kvh], sq, 0.0), axis=1, keepdims=True),
                                         axis=0, keepdims=True), (8, LANES)) for kvh in range(N_KV_HEADS)]

    for kb in range(nblk + 2):
        for kvh, n2 in enumerate(max_sq_norms(kv_ext[kb * BLOCK:(kb + 1) * BLOCK, 0:LANES])):
            kn_ref[kvh, kb] = n2
    ctx_norm = None
    for cb in range(ctx_len // BLOCK):
        n2 = max_sq_norms(kvc_ref[cb * BLOCK:(cb + 1) * BLOCK, 0:LANES])
        ctx_norm = n2 if ctx_norm is None else [jnp.maximum(a, b) for a, b in zip(ctx_norm, n2)]
    minden_ref[...] = jnp.full(minden_ref.shape, 3e38, F32)

    def bounded_exp(j, kvh, slot):
        e_ref, m_ref = e_refs[2 * slot + kvh], m_refs[2 * slot + kvh]
        q0 = pl.multiple_of(j * BLOCK, BLOCK)
        kw, kc = key_tiles(j, kvh)
        kn2 = jnp.maximum(jnp.max(kn_ref[kvh, pl.ds(j, 3)], axis=0), ctx_norm[kvh])[0:1, :]
        tiles = head_queries(j, kvh)
        for g in range(GQA_GROUP):
            qf = tiles[g].astype(F32)
            qn2 = jnp.max(jnp.sum(qf * qf, axis=1, keepdims=True), axis=0, keepdims=True)
            bound = jnp.maximum(jnp.sqrt(qn2 * kn2), sink_ref[kvh * GQA_GROUP + g] * LOG2E)
            m_ref[g * BLOCK:(g + 1) * BLOCK, :] = jnp.broadcast_to(bound, (BLOCK, LANES))
        qs = jnp.concatenate(tiles, axis=0)
        s_w = _dot_nt(qs, kw)
        s_c = _dot_nt(qs, kc)
        mask_l, mask_r = window_masks(j)
        for g, t0 in chunks:
            rows = slice(g * BLOCK + t0, g * BLOCK + t0 + SOFTMAX_ROWS)
            trows = slice(t0, t0 + SOFTMAX_ROWS)
            m = m_ref[rows, :]
            pieces = ([s_w[rows, p * LANES:(p + 1) * LANES] for p in range(nwin // LANES)]
                      + [s_c[rows, p * LANES:(p + 1) * LANES] for p in range(ctx_len // LANES)])
            pieces[0] = jnp.where(mask_l[trows], pieces[0], NEG)
            pieces[2] = jnp.where(mask_r[trows], pieces[2], NEG)
            for p, piece in enumerate(pieces):
                e_ref[rows, p * LANES:(p + 1) * LANES] = jnp.exp2(piece - m).astype(BF16)

    bounded_exp(0, 0, 0)
    bounded_exp(0, 1, 0)

    def fast_body(jj, carry):
        for k in range(FAST_UNROLL):
            j = FAST_UNROLL * jj + k
            nxt = j + 1 if k + 1 < FAST_UNROLL else jnp.minimum(j + 1, nblk - 1)
            for kvh in range(N_KV_HEADS):
                weighted_values(j, kvh, k % 2, True)
                bounded_exp(nxt, kvh, (k + 1) % 2)
        return carry

    lax.fori_loop(0, nblk // FAST_UNROLL, fast_body, 0)
    smallest = jnp.min(minden_ref[...])

    @pl.when(smallest < 2.0 ** MIN_SAFE_DENOMINATOR_LOG2)
    def _():
        lax.fori_loop(0, nblk, exact_body, 0)


def _attention(q, kv, kvc, sink, tq):
    nb, seq_len, _ = q.shape
    ctx_len = kvc.shape[1]
    nblk = seq_len // BLOCK
    qpb = tq // BLOCK
    rows = GQA_GROUP * BLOCK
    nkey = 3 * BLOCK + ctx_len
    return pl.pallas_call(
        functools.partial(_attn_kernel, seq_len, tq, ctx_len),
        grid=(nb, seq_len // tq),
        in_specs=[
            pl.BlockSpec(memory_space=pltpu.SMEM),
            pl.BlockSpec((None, tq, Q_DIM), lambda b, i: (b, i, 0)),
            pl.BlockSpec((None, BLOCK, KV_W), lambda b, i: (b, jnp.maximum(i * qpb - 1, 0), 0)),
            pl.BlockSpec((None, tq, KV_W), lambda b, i: (b, i, 0)),
            pl.BlockSpec((None, BLOCK, KV_W),
                         lambda b, i: (b, jnp.minimum((i + 1) * qpb, nblk - 1), 0)),
            pl.BlockSpec((None, ctx_len, KV_W), lambda b, i: (b, 0, 0)),
        ],
        out_specs=pl.BlockSpec((None, tq, Q_DIM), lambda b, i: (b, i, 0)),
        out_shape=jax.ShapeDtypeStruct((nb, seq_len, Q_DIM), BF16),
        scratch_shapes=([pltpu.VMEM((tq + 2 * BLOCK, KV_W), BF16)]
                        + [pltpu.VMEM((rows, nkey), F32)] * N_KV_HEADS
                        + [pltpu.VMEM((rows, nkey), BF16)] * (2 * N_KV_HEADS)
                        + [pltpu.VMEM((rows, LANES), F32)] * (2 * N_KV_HEADS)
                        + [pltpu.VMEM((N_KV_HEADS, qpb + 2, 8, LANES), F32),
                           pltpu.VMEM((BLOCK, LANES), F32)]),
        compiler_params=pltpu.CompilerParams(
            dimension_semantics=("arbitrary", "arbitrary"), vmem_limit_bytes=VMEM_LIMIT),
        name="attention",
    )(sink, q, kv, kv, kv, kvc)


def _post_kernel(x_ref, o_ref_in, mod_ref, gmlp_ref, gfin_ref, wo_ref, w1_ref, w2_ref, out_ref):
    g1 = mod_ref[:, 2 * D_MODEL:3 * D_MODEL]
    x1 = x_ref[...] + g1 * _dot(o_ref_in[...], wo_ref[...])
    x2 = _mlp_residual(x1, mod_ref, gmlp_ref, w1_ref, w2_ref)
    y = x2 * lax.rsqrt(jnp.mean(x2 * x2, axis=-1, keepdims=True) + EPS)
    out_ref[...] = y * gfin_ref[...]


def _post(x, o, mod, layer, tm, gmlp, gfin, w_o, w1, w2):
    nb, seq_len, _ = x.shape
    return pl.pallas_call(
        _post_kernel,
        grid=(nb, seq_len // tm),
        in_specs=[
            pl.BlockSpec((None, tm, D_MODEL), lambda b, i: (b, i, 0)),
            pl.BlockSpec((None, tm, Q_DIM), lambda b, i: (b, i, 0)),
            pl.BlockSpec((None, None, 1, N_MOD * D_MODEL), lambda b, i: (layer, b, 0, 0)),
            _const_spec((1, D_MODEL)),
            _const_spec((1, D_MODEL)),
            _const_spec((Q_DIM, D_MODEL)),
            _const_spec((D_MODEL, D_FF)),
            _const_spec((D_FF, D_MODEL)),
        ],
        out_specs=pl.BlockSpec((None, tm, D_MODEL), lambda b, i: (b, i, 0)),
        out_shape=jax.ShapeDtypeStruct(x.shape, F32),
        compiler_params=pltpu.CompilerParams(
            dimension_semantics=("arbitrary", "arbitrary"), vmem_limit_bytes=VMEM_LIMIT),
        name="post_attn",
    )(x, o, mod, gmlp, gfin, w_o, w1, w2)


def _rope_tables(seq_len):
    rows = seq_len // GRID_W
    row = jnp.repeat(jnp.arange(rows), GRID_W).astype(F32)
    col = jnp.tile(jnp.arange(GRID_W), rows).astype(F32)
    inv = ROPE_BASE ** (-jnp.arange(0, AXIS_ROT, 2, dtype=F32) / AXIS_ROT)
    ang = jnp.concatenate([row[:, None] * inv[None], col[:, None] * inv[None]], axis=-1)
    cos = jnp.tile(jnp.cos(ang), (1, LANES // AXIS_ROT))
    sin = jnp.sin(ang)
    return cos, jnp.concatenate([-sin, -sin, sin, sin], axis=-1)


def kernel(x, c, ctx, c_ctx, ada_w, ada_b, norm_mix_g, norm_mlp_g, pool_w_in, pool_w_grp,
           pool_scale, pool_w_out, attn_w_qkv, attn_sink, attn_w_o, mlp_w1, mlp_w2, final_g):
    nb, seq_len, _ = x.shape
    ctx_len = ctx.shape[1]
    ctx_row = nb
    c8 = jnp.concatenate(
        [c, c_ctx[None], jnp.zeros((MOD_ROWS - nb - 1, D_MODEL), F32)], axis=0)
    n_grp = len(POOL_WINDOWS)
    layer0_weights = ((pool_w_in, 0), (pool_w_grp.reshape(-1, n_grp * POOL_GROUP_DIM, POOL_GROUP_DIM), 0),
                      (pool_w_out, 0), (mlp_w1, 0), (mlp_w2, 0))
    mod, (w_in, w_grp, w_out, w1_0, w2_0) = _modulation(c8, ada_w, ada_b, layer0_weights)

    gmix = norm_mix_g.reshape(DEPTH, 1, D_MODEL)
    gmlp = norm_mlp_g.reshape(DEPTH, 1, D_MODEL)

    l0 = (gmix[0], gmlp[0], pool_scale[0].reshape(1, D_MODEL), w_in,
          w_grp.reshape(n_grp, POOL_GROUP_DIM, POOL_GROUP_DIM), w_out, w1_0, w2_0)
    layer1_weights = ((attn_w_qkv, 0), (attn_w_o, 0), (mlp_w1, 1), (mlp_w2, 1))
    x1, (w_qkv, w_o, w1, w2) = _layer0(x, mod, None, 0, 512, True, layer1_weights, *l0)
    ctx1, _ = _layer0(ctx, mod, ctx_row, 0, ctx_len, False, (), *l0)

    cos, sin = _rope_tables(seq_len)
    q, kv = _qkv(x1, mod, 1, 1024, gmix[1], w_qkv, cos, sin)
    kvc = _ctx_kv(ctx1, mod, ctx_row, 1, gmix[1], w_qkv[:, Q_DIM:])
    o = _attention(q, kv, kvc, attn_sink[0], 2048)
    return _post(x1, o, mod, 1, 512, gmlp[1], final_g.reshape(1, D_MODEL), w_o, w1, w2)
```

```python
import functools

import numpy as np
import jax
import jax.numpy as jnp
from jax import lax
from jax.experimental import pallas as pl
from jax.experimental.pallas import tpu as pltpu

D_MODEL = 1024
DEPTH = 2
GRID_W = 64
POOL_WINDOWS = (2, 4, 8, 16)
POOL_GROUP_DIM = D_MODEL // len(POOL_WINDOWS)
POOL_HALO = 8
HEAD_DIM = 64
N_HEADS = D_MODEL // HEAD_DIM
N_KV_HEADS = 2
GQA_GROUP = N_HEADS // N_KV_HEADS
Q_DIM = N_HEADS * HEAD_DIM
KV_DIM = N_KV_HEADS * HEAD_DIM
WINDOW = 128
BLOCK = 128
ROPE_BASE = 10000.0
AXIS_ROT = HEAD_DIM // 2
ROT_HALF = AXIS_ROT // 2
D_FF = 4 * D_MODEL
N_MOD = 6
EPS = 1e-6
NEG = -1e30
MOD_ROWS = 8
LANES = 128
LOG2E = 1.4426950408889634
KV_W = 3 * LANES
SOFTMAX_ROWS = 32
MIN_SAFE_DENOMINATOR_LOG2 = -64.0
QKV_COL_GROUP = 256
FAST_UNROLL = 2
VMEM_LIMIT = 56 * 1024 * 1024

BF16 = jnp.bfloat16
F32 = jnp.float32


def _const_spec(shape):
    nd = len(shape)
    return pl.BlockSpec(shape, lambda *_: (0,) * nd, pipeline_mode=pl.Buffered(1))


def _layer_row_spec(layer):
    return pl.BlockSpec((None, 1, D_MODEL), lambda *_: (layer, 0, 0), pipeline_mode=pl.Buffered(1))


def _dot(a, b):
    return jnp.dot(a, b, preferred_element_type=F32)


def _rms_mod(x, g, shift, scale):
    y = x * lax.rsqrt(jnp.mean(x * x, axis=-1, keepdims=True) + EPS)
    return y * (g * (1.0 + scale)) + shift


def _mlp_residual(x, mod_ref, g_ref, w1_ref, w2_ref):
    sh2 = mod_ref[:, 3 * D_MODEL:4 * D_MODEL]
    sc2 = mod_ref[:, 4 * D_MODEL:5 * D_MODEL]
    g2 = mod_ref[:, 5 * D_MODEL:6 * D_MODEL]
    h = _rms_mod(x, g_ref[...], sh2, sc2).astype(BF16)
    a = jnp.maximum(_dot(h, w1_ref[...]), 0.0)
    a = (a * a).astype(BF16)
    return x + g2 * _dot(a, w2_ref[...])


def _staged_steps(step, n_tiles, n_stages, body):
    assert n_tiles >= n_stages
    for t in range(n_stages - 1):
        pl.when(step == t)(functools.partial(body, *[s <= t for s in range(n_stages)]))
    pl.when((step >= n_stages - 1) & (step < n_tiles))(functools.partial(body, *[True] * n_stages))
    for k in range(n_stages - 1):
        pl.when(step == n_tiles + k)(functools.partial(body, *[s > k for s in range(n_stages)]))


def _skewed_steps(step, n_tiles, body):
    _staged_steps(step, n_tiles, 2, body)


def _stage_tile(n_tiles, tiles_per_row, stage):
    def tile(t):
        k = jnp.clip(t - stage, 0, n_tiles - 1)
        return k // tiles_per_row, k % tiles_per_row
    return tile


def _skew_maps(n_tiles, tiles_per_row):
    return _stage_tile(n_tiles, tiles_per_row, 0), _stage_tile(n_tiles, tiles_per_row, 1)


def _mod_kernel(n_prep, c_ref, w_ref, b_ref, *refs):
    prep_in, o_ref, prep_out = refs[:n_prep], refs[n_prep], refs[n_prep + 1:]
    c = c_ref[...]
    s = (c * jax.nn.sigmoid(c)).astype(BF16)
    r = _dot(s, w_ref[...].astype(BF16)) + b_ref[...]
    for row in range(MOD_ROWS):
        o_ref[row] = r[row:row + 1, :]
    for src, dst in zip(prep_in, prep_out):
        dst[...] = src[...].astype(BF16)


def _modulation(c8, ada_w, ada_b, first_layer_weights):
    tn = 1536
    n = N_MOD * D_MODEL
    n_col = n // tn
    steps = DEPTH * n_col
    prep_in_specs, prep_out_specs = [], []
    for w, idx in first_layer_weights:
        rows = w.shape[1] // steps
        prep_in_specs.append(pl.BlockSpec(
            (None, rows, w.shape[2]), lambda i, j, idx=idx: (idx, i * n_col + j, 0)))
        prep_out_specs.append(pl.BlockSpec((rows, w.shape[2]), lambda i, j: (i * n_col + j, 0)))
    outs = pl.pallas_call(
        functools.partial(_mod_kernel, len(first_layer_weights)),
        grid=(DEPTH, n_col),
        in_specs=[
            pl.BlockSpec((MOD_ROWS, D_MODEL), lambda i, j: (0, 0)),
            pl.BlockSpec((None, D_MODEL, tn), lambda i, j: (i, 0, j)),
            pl.BlockSpec((None, 1, tn), lambda i, j: (i, 0, j)),
        ] + prep_in_specs,
        out_specs=[pl.BlockSpec((None, MOD_ROWS, 1, tn), lambda i, j: (i, 0, 0, j))] + prep_out_specs,
        out_shape=[jax.ShapeDtypeStruct((DEPTH, MOD_ROWS, 1, n), F32)]
        + [jax.ShapeDtypeStruct(w.shape[1:], BF16) for w, _ in first_layer_weights],
        compiler_params=pltpu.CompilerParams(
            dimension_semantics=("arbitrary", "arbitrary"), vmem_limit_bytes=VMEM_LIMIT),
        name="modulation",
    )(c8, ada_w, ada_b.reshape(DEPTH, 1, n), *[w for w, _ in first_layer_weights])
    return outs[0], tuple(outs[1:])


def _to_pair_layout(v):
    lane = lax.broadcasted_iota(jnp.int32, v.shape, 1)
    for width in (ROT_HALF, AXIS_ROT):
        hi = (lane // (2 * width)) % 2
        lo = (lane // width) % 2
        v = jnp.where(hi == lo, v, jnp.where(hi == 0, pltpu.roll(v, LANES - width, 1),
                                             pltpu.roll(v, width, 1)))
    return v


def _prepare_next_weights(prep_in, prep_out):
    wqkv_in, wqkv_out = prep_in[0], prep_out[0]
    low = lax.broadcasted_iota(jnp.int32, (wqkv_in.shape[0], LANES), 1) < HEAD_DIM
    for p in range(GQA_GROUP):
        first = wqkv_in[:, (p // 2) * LANES:(p // 2 + 1) * LANES]
        second = wqkv_in[:, (p // 2 + GQA_GROUP // 2) * LANES:(p // 2 + GQA_GROUP // 2 + 1) * LANES]
        if p % 2 == 0:
            both = jnp.where(low, first, pltpu.roll(second, HEAD_DIM, 1))
        else:
            both = jnp.where(low, pltpu.roll(first, HEAD_DIM, 1), second)
        wqkv_out[:, p * LANES:(p + 1) * LANES] = _to_pair_layout(both).astype(BF16)
    wqkv_out[:, Q_DIM:Q_DIM + KV_DIM] = _to_pair_layout(wqkv_in[:, Q_DIM:Q_DIM + KV_DIM]).astype(BF16)
    wqkv_out[:, Q_DIM + KV_DIM:] = wqkv_in[:, Q_DIM + KV_DIM:].astype(BF16)
    for src, dst in zip(prep_in[1:], prep_out[1:]):
        dst[...] = src[...].astype(BF16)


def _layer0_kernel(seq_len, tm, n_tiles, skew, n_prep, *refs):
    (x_ref, xp_ref, xn_ref, moda_ref, modb_ref, gmix_ref, gmlp_ref, pscale_ref, win_ref, wgrp_ref,
     wout_ref, w1_ref, w2_ref) = refs[:13]
    prep_in = refs[13:13 + n_prep]
    o_ref = refs[13 + n_prep]
    prep_out = refs[14 + n_prep:14 + 2 * n_prep]
    u_ref, d_ref, x1_ref, h2_ref = refs[14 + 2 * n_prep:]
    step = pl.program_id(0)
    i = jnp.minimum(step, n_tiles - 1) % (seq_len // tm)
    half = D_FF // 2

    def mlp_up(h2, c0, c1):
        a = jnp.maximum(_dot(h2, w1_ref[:, c0:c1]), 0.0)
        return (a * a).astype(BF16)

    def body(do_mixer, do_prev_mlp):
        if do_prev_mlp:
            g2 = modb_ref[:, 5 * D_MODEL:6 * D_MODEL]
            h2_prev = h2_ref[...]
            a_lo = mlp_up(h2_prev, 0, half)

        if do_mixer:
            sh1 = moda_ref[:, 0:D_MODEL]
            sc1 = moda_ref[:, D_MODEL:2 * D_MODEL]
            x = x_ref[...]
            xe = jnp.concatenate([xp_ref[...], x, xn_ref[...]], axis=0)
            u = _dot(_rms_mod(xe, gmix_ref[...], sh1, sc1).astype(BF16), win_ref[...])
            u_ref[0:POOL_HALO, :] = jnp.where(i > 0, u[0:POOL_HALO], 0.0)
            u_ref[POOL_HALO:POOL_HALO + tm, :] = u[POOL_HALO:POOL_HALO + tm]
            u_ref[POOL_HALO + tm:POOL_HALO + tm + POOL_HALO, :] = jnp.where(
                (i + 1) * tm < seq_len, u[POOL_HALO + tm:], 0.0)

        if do_prev_mlp:
            a_hi = mlp_up(h2_prev, half, D_FF)

        if do_mixer:
            t = i * tm + lax.broadcasted_iota(jnp.int32, (tm, 1), 0)
            for g, w in enumerate(POOL_WINDOWS):
                c0, c1 = g * POOL_GROUP_DIM, (g + 1) * POOL_GROUP_DIM
                acc = u_ref[POOL_HALO - w // 2:POOL_HALO - w // 2 + tm, c0:c1]
                for s in range(-w // 2 + 1, w // 2):
                    acc = acc + u_ref[POOL_HALO + s:POOL_HALO + s + tm, c0:c1]
                lo = jnp.clip(t - w // 2, 0, seq_len)
                hi = jnp.clip(t + w // 2, 0, seq_len)
                inv_cnt = 1.0 / (hi - lo).astype(F32)
                d = acc * inv_cnt - u_ref[POOL_HALO:POOL_HALO + tm, c0:c1]
                d_ref[:, c0:c1] = _dot(d.astype(BF16), wgrp_ref[g])
            y = _dot((d_ref[...] * pscale_ref[...]).astype(BF16), wout_ref[...])

        if do_prev_mlp:
            down = _dot(jnp.concatenate([a_lo, a_hi], axis=1), w2_ref[...])

        if do_mixer:
            g1 = moda_ref[:, 2 * D_MODEL:3 * D_MODEL]
            sh2 = moda_ref[:, 3 * D_MODEL:4 * D_MODEL]
            sc2 = moda_ref[:, 4 * D_MODEL:5 * D_MODEL]
            x1 = x + g1 * y
            h2 = _rms_mod(x1, gmlp_ref[...], sh2, sc2).astype(BF16)

        if do_prev_mlp:
            o_ref[...] = x1_ref[...] + g2 * down
        if do_mixer and n_prep:
            _prepare_next_weights(prep_in, prep_out)
        if do_mixer and skew:
            x1_ref[...] = x1
            h2_ref[...] = h2
        if do_mixer and not skew:
            g2_now = moda_ref[:, 5 * D_MODEL:6 * D_MODEL]
            o_ref[...] = x1 + g2_now * _dot(mlp_up(h2, 0, D_FF), w2_ref[...])

    if skew:
        _skewed_steps(step, n_tiles, body)
    else:
        body(True, False)


def _layer0(x, mod, mod_row, layer, tm, skew, next_weights, gmix, gmlp, pscale, w_in, w_grp, w_out, w1, w2):
    nb, seq_len, _ = x.shape
    rows8 = seq_len // POOL_HALO
    tpb = tm // POOL_HALO
    nt = seq_len // tm
    n_tiles = nb * nt
    tile_a, tile_b = _skew_maps(n_tiles, nt)
    if not skew:
        tile_b = tile_a

    def x_map(t):
        b, i = tile_a(t)
        return (b, i, 0)

    def prev_map(t):
        b, i = tile_a(t)
        return (b, jnp.maximum(i * tpb - 1, 0), 0)

    def next_map(t):
        b, i = tile_a(t)
        return (b, jnp.minimum((i + 1) * tpb, rows8 - 1), 0)

    def moda_map(t):
        return (layer, tile_a(t)[0] if mod_row is None else mod_row, 0, 0)

    def modb_map(t):
        return (layer, tile_b(t)[0] if mod_row is None else mod_row, 0, 0)

    def out_map(t):
        b, i = tile_b(t)
        return (b, i, 0)

    prep_in_specs, prep_out_specs = [], []
    for w, idx in next_weights:
        rows = w.shape[1] // n_tiles
        prep_in_specs.append(pl.BlockSpec(
            (None, rows, w.shape[2]), lambda t, idx=idx: (idx, jnp.minimum(t, n_tiles - 1), 0)))
        prep_out_specs.append(pl.BlockSpec(
            (rows, w.shape[2]), lambda t: (jnp.minimum(t, n_tiles - 1), 0)))

    outs = pl.pallas_call(
        functools.partial(_layer0_kernel, seq_len, tm, n_tiles, skew, len(next_weights)),
        grid=(n_tiles + (1 if skew else 0),),
        in_specs=[
            pl.BlockSpec((None, tm, D_MODEL), x_map),
            pl.BlockSpec((None, POOL_HALO, D_MODEL), prev_map),
            pl.BlockSpec((None, POOL_HALO, D_MODEL), next_map),
            pl.BlockSpec((None, None, 1, N_MOD * D_MODEL), moda_map),
            pl.BlockSpec((None, None, 1, N_MOD * D_MODEL), modb_map),
            _layer_row_spec(layer),
            _layer_row_spec(layer),
            _const_spec((1, D_MODEL)),
            _const_spec((D_MODEL, D_MODEL)),
            _const_spec((len(POOL_WINDOWS), POOL_GROUP_DIM, POOL_GROUP_DIM)),
            _const_spec((D_MODEL, D_MODEL)),
            _const_spec((D_MODEL, D_FF)),
            _const_spec((D_FF, D_MODEL)),
        ] + prep_in_specs,
        out_specs=[pl.BlockSpec((None, tm, D_MODEL), out_map)] + prep_out_specs,
        out_shape=[jax.ShapeDtypeStruct(x.shape, F32)]
        + [jax.ShapeDtypeStruct(w.shape[1:], BF16) for w, _ in next_weights],
        scratch_shapes=[pltpu.VMEM((tm + 2 * POOL_HALO, D_MODEL), F32),
                        pltpu.VMEM((tm, D_MODEL), F32),
                        pltpu.VMEM((tm, D_MODEL), F32),
                        pltpu.VMEM((tm, D_MODEL), BF16)],
        compiler_params=pltpu.CompilerParams(
            dimension_semantics=("arbitrary",), vmem_limit_bytes=VMEM_LIMIT),
        name="layer0",
    )(x, x, x, mod, mod, gmix, gmlp, pscale, w_in, w_grp, w_out, w1, w2, *[w for w, _ in next_weights])
    return outs[0], tuple(outs[1:])


def _store_kv(kv_ref, k, v):
    low = lax.broadcasted_iota(jnp.int32, v.shape, 1) < HEAD_DIM
    kv_ref[:, 0:LANES] = k.astype(BF16)
    kv_ref[:, LANES:2 * LANES] = jnp.where(low, v, 1.0).astype(BF16)
    kv_ref[:, 2 * LANES:3 * LANES] = jnp.where(low, 1.0, v).astype(BF16)


def _qkv_kernel(n_tiles, x_ref, mod_ref, gmix_ref, w_ref, cos_ref, sin_ref, q_ref, kv_ref, h_ref):
    def body(do_norm, do_project):
        if do_project:
            h_prev = h_ref[...]
            kv = _dot(h_prev, w_ref[:, Q_DIM:])
            q_groups = [_dot(h_prev, w_ref[:, c:c + QKV_COL_GROUP]) for c in range(0, Q_DIM, QKV_COL_GROUP)]
        if do_norm:
            sh1 = mod_ref[:, 0:D_MODEL]
            sc1 = mod_ref[:, D_MODEL:2 * D_MODEL]
            h = _rms_mod(x_ref[...], gmix_ref[...], sh1, sc1).astype(BF16)
        if do_project:
            cos = cos_ref[...]
            sin = sin_ref[...]
            qscale = HEAD_DIM ** -0.5 * LOG2E
            cos_q = cos * qscale
            sin_q = sin * qscale

            def rope(t, c, s):
                return t * c + pltpu.roll(t, HEAD_DIM, 1) * s

            _store_kv(kv_ref, rope(kv[:, 0:KV_DIM], cos, sin), kv[:, KV_DIM:])
            for gi, group in enumerate(q_groups):
                for cb in range(QKV_COL_GROUP // LANES):
                    c0 = gi * QKV_COL_GROUP + cb * LANES
                    q_ref[:, c0:c0 + LANES] = rope(group[:, cb * LANES:(cb + 1) * LANES], cos_q, sin_q).astype(BF16)
        if do_norm:
            h_ref[...] = h

    _skewed_steps(pl.program_id(0), n_tiles, body)


def _qkv(x, mod, layer, tm, gmix, w_qkv, cos, sin):
    nb, seq_len, _ = x.shape
    nt = seq_len // tm
    n_tiles = nb * nt
    tile_a, tile_b = _skew_maps(n_tiles, nt)

    def out_map(t):
        b, i = tile_b(t)
        return (b, i, 0)

    return pl.pallas_call(
        functools.partial(_qkv_kernel, n_tiles),
        grid=(n_tiles + 1,),
        in_specs=[
            pl.BlockSpec((None, tm, D_MODEL), lambda t: (*tile_a(t), 0)),
            pl.BlockSpec((None, None, 1, N_MOD * D_MODEL), lambda t: (layer, tile_a(t)[0], 0, 0)),
            _layer_row_spec(layer),
            _const_spec((D_MODEL, Q_DIM + 2 * KV_DIM)),
            pl.BlockSpec((tm, LANES), lambda t: (tile_b(t)[1], 0)),
            pl.BlockSpec((tm, LANES), lambda t: (tile_b(t)[1], 0)),
        ],
        out_specs=[pl.BlockSpec((None, tm, Q_DIM), out_map),
                   pl.BlockSpec((None, tm, KV_W), out_map)],
        out_shape=[jax.ShapeDtypeStruct((nb, seq_len, Q_DIM), BF16),
                   jax.ShapeDtypeStruct((nb, seq_len, KV_W), BF16)],
        scratch_shapes=[pltpu.VMEM((tm, D_MODEL), BF16)],
        compiler_params=pltpu.CompilerParams(
            dimension_semantics=("arbitrary",), vmem_limit_bytes=VMEM_LIMIT),
        name="qkv_rope",
    )(x, mod, gmix, w_qkv, cos, sin)


def _ctx_kv_kernel(x_ref, mod_ref, gmix_ref, w_ref, kv_ref):
    sh1 = mod_ref[:, 0:D_MODEL]
    sc1 = mod_ref[:, D_MODEL:2 * D_MODEL]
    h = _rms_mod(x_ref[...], gmix_ref[...], sh1, sc1).astype(BF16)
    kv = _dot(h, w_ref[...])
    _store_kv(kv_ref, kv[:, 0:KV_DIM], kv[:, KV_DIM:])


def _ctx_kv(ctx, mod, mod_row, layer, gmix, w_kv):
    nb, ctx_len, _ = ctx.shape
    return pl.pallas_call(
        _ctx_kv_kernel,
        grid=(nb,),
        in_specs=[
            pl.BlockSpec((None, ctx_len, D_MODEL), lambda b: (b, 0, 0)),
            pl.BlockSpec((None, None, 1, N_MOD * D_MODEL), lambda b: (layer, mod_row, 0, 0)),
            _layer_row_spec(layer),
            pl.BlockSpec((D_MODEL, 2 * KV_DIM), lambda b: (0, Q_DIM // (2 * KV_DIM)),
                         pipeline_mode=pl.Buffered(1)),
        ],
        out_specs=pl.BlockSpec((None, ctx_len, KV_W), lambda b: (b, 0, 0)),
        out_shape=jax.ShapeDtypeStruct((nb, ctx_len, KV_W), BF16),
        compiler_params=pltpu.CompilerParams(
            dimension_semantics=("arbitrary",), vmem_limit_bytes=VMEM_LIMIT),
        name="ctx_kv",
    )(ctx, mod, gmix, w_kv)


def _dot_nt(a, b):
    return lax.dot_general(a, b, (((1,), (1,)), ((), ())), preferred_element_type=F32)


def _attn_kernel(seq_len, tq, ctx_len, sink_ref, q_ref, kvp_ref, kvm_ref, kvn_ref, kvc_ref, o_ref,
                 kv_ext, *bufs):
    s_refs, e_refs, m_refs = bufs[0:2], bufs[2:6], bufs[6:10]
    kn_ref, minden_ref = bufs[10], bufs[11]
    i = pl.program_id(1)
    kv_ext[0:BLOCK, :] = kvp_ref[...]
    kv_ext[BLOCK:BLOCK + tq, :] = kvm_ref[...]
    kv_ext[BLOCK + tq:2 * BLOCK + tq, :] = kvn_ref[...]
    nblk = tq // BLOCK
    nwin = 3 * BLOCK
    nkey = nwin + ctx_len
    row = lax.broadcasted_iota(jnp.int32, (BLOCK, BLOCK), 0)
    col = lax.broadcasted_iota(jnp.int32, (BLOCK, BLOCK), 1)
    lane = lax.broadcasted_iota(jnp.int32, (BLOCK, LANES), 1)
    low = lane < HEAD_DIM
    head_a = (lane % HEAD_DIM) < AXIS_ROT
    my_lanes = (head_a, jnp.logical_not(head_a))
    den_lanes = (jnp.logical_not(low), low)

    chunks = [(g, t0) for g in range(GQA_GROUP) for t0 in range(0, BLOCK, SOFTMAX_ROWS)]

    def head_queries(j, kvh):
        q0 = pl.multiple_of(j * BLOCK, BLOCK)
        tiles = []
        for g in range(GQA_GROUP):
            pair = q_ref[pl.ds(q0, BLOCK), g * LANES:(g + 1) * LANES]
            tiles.append(jnp.where(my_lanes[kvh], pair, jnp.zeros_like(pair)))
        return tiles

    def window_masks(j):
        mask_l = (col >= row) & (i * tq + (j - 1) * BLOCK >= 0)
        mask_r = (col <= row) & (i * tq + (j + 1) * BLOCK < seq_len)
        return mask_l, mask_r

    def key_tiles(j, kvh):
        q0 = pl.multiple_of(j * BLOCK, BLOCK)
        return kv_ext[pl.ds(q0, nwin), 0:LANES], kvc_ref[:, 0:LANES]

    def scores(j, kvh):
        qs = jnp.concatenate(head_queries(j, kvh), axis=0)
        kw, kc = key_tiles(j, kvh)
        s_refs[kvh][:, 0:nwin] = _dot_nt(qs, kw)
        s_refs[kvh][:, nwin:nkey] = _dot_nt(qs, kc)

    def softmax(j, kvh):
        mask_l, mask_r = window_masks(j)

        def masked_pieces(rows, trows):
            pieces = [s_refs[kvh][rows, p * LANES:(p + 1) * LANES] for p in range(nkey // LANES)]
            pieces[0] = jnp.where(mask_l[trows], pieces[0], NEG)
            pieces[2] = jnp.where(mask_r[trows], pieces[2], NEG)
            return pieces

        for g, t0 in chunks:
            rows = slice(g * BLOCK + t0, g * BLOCK + t0 + SOFTMAX_ROWS)
            pieces = masked_pieces(rows, slice(t0, t0 + SOFTMAX_ROWS))
            m = pieces[0]
            for p in pieces[1:]:
                m = jnp.maximum(m, p)
            m = jnp.maximum(jnp.max(m, axis=1, keepdims=True), sink_ref[kvh * GQA_GROUP + g] * LOG2E)
            m_refs[kvh][rows, :] = jnp.broadcast_to(m, (SOFTMAX_ROWS, LANES))
        for g, t0 in chunks:
            rows = slice(g * BLOCK + t0, g * BLOCK + t0 + SOFTMAX_ROWS)
            pieces = masked_pieces(rows, slice(t0, t0 + SOFTMAX_ROWS))
            m = m_refs[kvh][rows, :]
            for p, piece in enumerate(pieces):
                e_refs[kvh][rows, p * LANES:(p + 1) * LANES] = jnp.exp2((piece - m).astype(BF16))

    def weighted_values(j, kvh, slot, fast):
        e_ref, m_ref = e_refs[2 * slot + kvh], m_refs[2 * slot + kvh]
        q0 = pl.multiple_of(j * BLOCK, BLOCK)
        vcols = slice((1 + kvh) * LANES, (2 + kvh) * LANES)
        values = jnp.concatenate([kv_ext[pl.ds(q0, nwin), vcols], kvc_ref[:, vcols]], axis=0)
        o = _dot(e_ref[...], values)
        on, smallest = [], None
        for g in range(GQA_GROUP):
            rows = slice(g * BLOCK, (g + 1) * BLOCK)
            sink = sink_ref[kvh * GQA_GROUP + g] * LOG2E
            if fast:
                sink_term = jnp.exp2(sink - m_ref[g * BLOCK:g * BLOCK + 1, :])
            else:
                sink_term = jnp.exp2(sink - m_ref[rows, :])
            total = o[rows] + sink_term
            smallest = total if smallest is None else jnp.minimum(smallest, total)
            on.append(o[rows] * pltpu.roll(1.0 / total, HEAD_DIM, 1))
        if fast:
            minden_ref[...] = jnp.minimum(minden_ref[...], jnp.where(den_lanes[kvh], smallest, 3e38))
        for pair in range(GQA_GROUP // 2):
            if kvh == 0:
                tile = jnp.where(low, on[2 * pair], pltpu.roll(on[2 * pair + 1], HEAD_DIM, 1))
            else:
                tile = jnp.where(low, pltpu.roll(on[2 * pair], HEAD_DIM, 1), on[2 * pair + 1])
            c0 = (kvh * GQA_GROUP + 2 * pair) * HEAD_DIM
            o_ref[pl.ds(q0, BLOCK), c0:c0 + LANES] = tile.astype(BF16)

    def exact_body(j, carry):
        scores(j, 0)
        scores(j, 1)
        softmax(j, 0)
        weighted_values(j, 0, 0, False)
        softmax(j, 1)
        weighted_values(j, 1, 0, False)
        return carry

    def max_sq_norms(k_tile):
        sq = k_tile.astype(F32)
        sq = sq * sq
        return [jnp.broadcast_to(jnp.max(jnp.sum(jnp.where(my_lanes[kvh], sq, 0.0), axis=1, keepdims=True),
                                         axis=0, keepdims=True), (8, LANES)) for kvh in range(N_KV_HEADS)]

    for kb in range(nblk + 2):
        for kvh, n2 in enumerate(max_sq_norms(kv_ext[kb * BLOCK:(kb + 1) * BLOCK, 0:LANES])):
            kn_ref[kvh, kb] = n2
    ctx_norm = None
    for cb in range(ctx_len // BLOCK):
        n2 = max_sq_norms(kvc_ref[cb * BLOCK:(cb + 1) * BLOCK, 0:LANES])
        ctx_norm = n2 if ctx_norm is None else [jnp.maximum(a, b) for a, b in zip(ctx_norm, n2)]
    minden_ref[...] = jnp.full(minden_ref.shape, 3e38, F32)

    def bounded_exp(j, kvh, slot):
        e_ref, m_ref = e_refs[2 * slot + kvh], m_refs[2 * slot + kvh]
        q0 = pl.multiple_of(j * BLOCK, BLOCK)
        kw, kc = key_tiles(j, kvh)
        kn2 = jnp.maximum(jnp.max(kn_ref[kvh, pl.ds(j, 3)], axis=0), ctx_norm[kvh])[0:1, :]
        tiles = head_queries(j, kvh)
        for g in range(GQA_GROUP):
            qf = tiles[g].astype(F32)
            qn2 = jnp.max(jnp.sum(qf * qf, axis=1, keepdims=True), axis=0, keepdims=True)
            bound = jnp.maximum(jnp.sqrt(qn2 * kn2), sink_ref[kvh * GQA_GROUP + g] * LOG2E)
            m_ref[g * BLOCK:(g + 1) * BLOCK, :] = jnp.broadcast_to(bound, (BLOCK, LANES))
        qs = jnp.concatenate(tiles, axis=0)
        s_w = _dot_nt(qs, kw)
        s_c = _dot_nt(qs, kc)
        mask_l, mask_r = window_masks(j)
        for g, t0 in chunks:
            rows = slice(g * BLOCK + t0, g * BLOCK + t0 + SOFTMAX_ROWS)
            trows = slice(t0, t0 + SOFTMAX_ROWS)
            m = m_ref[rows, :]
            pieces = ([s_w[rows, p * LANES:(p + 1) * LANES] for p in range(nwin // LANES)]
                      + [s_c[rows, p * LANES:(p + 1) * LANES] for p in range(ctx_len // LANES)])
            pieces[0] = jnp.where(mask_l[trows], pieces[0], NEG)
            pieces[2] = jnp.where(mask_r[trows], pieces[2], NEG)
            for p, piece in enumerate(pieces):
                e_ref[rows, p * LANES:(p + 1) * LANES] = jnp.exp2(piece - m).astype(BF16)

    bounded_exp(0, 0, 0)
    bounded_exp(0, 1, 0)

    def fast_body(jj, carry):
        for k in range(FAST_UNROLL):
            j = FAST_UNROLL * jj + k
            nxt = j + 1 if k + 1 < FAST_UNROLL else jnp.minimum(j + 1, nblk - 1)
            for kvh in range(N_KV_HEADS):
                weighted_values(j, kvh, k % 2, True)
                bounded_exp(nxt, kvh, (k + 1) % 2)
        return carry

    lax.fori_loop(0, nblk // FAST_UNROLL, fast_body, 0)
    smallest = jnp.min(minden_ref[...])

    @pl.when(smallest < 2.0 ** MIN_SAFE_DENOMINATOR_LOG2)
    def _():
        lax.fori_loop(0, nblk, exact_body, 0)


def _attention(q, kv, kvc, sink, tq):
    nb, seq_len, _ = q.shape
    ctx_len = kvc.shape[1]
    nblk = seq_len // BLOCK
    qpb = tq // BLOCK
    rows = GQA_GROUP * BLOCK
    nkey = 3 * BLOCK + ctx_len
    return pl.pallas_call(
        functools.partial(_attn_kernel, seq_len, tq, ctx_len),
        grid=(nb, seq_len // tq),
        in_specs=[
            pl.BlockSpec(memory_space=pltpu.SMEM),
            pl.BlockSpec((None, tq, Q_DIM), lambda b, i: (b, i, 0)),
            pl.BlockSpec((None, BLOCK, KV_W), lambda b, i: (b, jnp.maximum(i * qpb - 1, 0), 0)),
            pl.BlockSpec((None, tq, KV_W), lambda b, i: (b, i, 0)),
            pl.BlockSpec((None, BLOCK, KV_W),
                         lambda b, i: (b, jnp.minimum((i + 1) * qpb, nblk - 1), 0)),
            pl.BlockSpec((None, ctx_len, KV_W), lambda b, i: (b, 0, 0)),
        ],
        out_specs=pl.BlockSpec((None, tq, Q_DIM), lambda b, i: (b, i, 0)),
        out_shape=jax.ShapeDtypeStruct((nb, seq_len, Q_DIM), BF16),
        scratch_shapes=([pltpu.VMEM((tq + 2 * BLOCK, KV_W), BF16)]
                        + [pltpu.VMEM((rows, nkey), F32)] * N_KV_HEADS
                        + [pltpu.VMEM((rows, nkey), BF16)] * (2 * N_KV_HEADS)
                        + [pltpu.VMEM((rows, LANES), F32)] * (2 * N_KV_HEADS)
                        + [pltpu.VMEM((N_KV_HEADS, qpb + 2, 8, LANES), F32),
                           pltpu.VMEM((BLOCK, LANES), F32)]),
        compiler_params=pltpu.CompilerParams(
            dimension_semantics=("arbitrary", "arbitrary"), vmem_limit_bytes=VMEM_LIMIT),
        name="attention",
    )(sink, q, kv, kv, kv, kvc)


def _post_kernel(x_ref, o_ref_in, mod_ref, gmlp_ref, gfin_ref, wo_ref, w1_ref, w2_ref, out_ref):
    g1 = mod_ref[:, 2 * D_MODEL:3 * D_MODEL]
    x1 = x_ref[...] + g1 * _dot(o_ref_in[...], wo_ref[...])
    x2 = _mlp_residual(x1, mod_ref, gmlp_ref, w1_ref, w2_ref)
    y = x2 * lax.rsqrt(jnp.mean(x2 * x2, axis=-1, keepdims=True) + EPS)
    out_ref[...] = y * gfin_ref[...]


def _post(x, o, mod, layer, tm, gmlp, gfin, w_o, w1, w2):
    nb, seq_len, _ = x.shape
    return pl.pallas_call(
        _post_kernel,
        grid=(nb, seq_len // tm),
        in_specs=[
            pl.BlockSpec((None, tm, D_MODEL), lambda b, i: (b, i, 0)),
            pl.BlockSpec((None, tm, Q_DIM), lambda b, i: (b, i, 0)),
            pl.BlockSpec((None, None, 1, N_MOD * D_MODEL), lambda b, i: (layer, b, 0, 0)),
            _layer_row_spec(layer),
            _const_spec((1, D_MODEL)),
            _const_spec((Q_DIM, D_MODEL)),
            _const_spec((D_MODEL, D_FF)),
            _const_spec((D_FF, D_MODEL)),
        ],
        out_specs=pl.BlockSpec((None, tm, D_MODEL), lambda b, i: (b, i, 0)),
        out_shape=jax.ShapeDtypeStruct(x.shape, F32),
        compiler_params=pltpu.CompilerParams(
            dimension_semantics=("arbitrary", "arbitrary"), vmem_limit_bytes=VMEM_LIMIT),
        name="post_attn",
    )(x, o, mod, gmlp, gfin, w_o, w1, w2)


def _rope_tables(seq_len):
    pos = np.arange(seq_len)
    inv = ROPE_BASE ** (-np.arange(0, AXIS_ROT, 2, dtype=np.float32) / AXIS_ROT)
    ang = np.concatenate([(pos // GRID_W).astype(np.float32)[:, None] * inv[None],
                          (pos % GRID_W).astype(np.float32)[:, None] * inv[None]], axis=-1)
    cos = np.tile(np.cos(ang.astype(np.float64)), (1, LANES // AXIS_ROT))
    sin = np.sin(ang.astype(np.float64))
    return (jnp.asarray(cos, F32), jnp.asarray(np.concatenate([-sin, -sin, sin, sin], axis=-1), F32))


def kernel(x, c, ctx, c_ctx, ada_w, ada_b, norm_mix_g, norm_mlp_g, pool_w_in, pool_w_grp,
           pool_scale, pool_w_out, attn_w_qkv, attn_sink, attn_w_o, mlp_w1, mlp_w2, final_g):
    nb, seq_len, _ = x.shape
    ctx_len = ctx.shape[1]
    ctx_row = nb
    c8 = jnp.concatenate(
        [c, c_ctx[None], jnp.zeros((MOD_ROWS - nb - 1, D_MODEL), F32)], axis=0)
    n_grp = len(POOL_WINDOWS)
    layer0_weights = ((pool_w_in, 0), (pool_w_grp.reshape(-1, n_grp * POOL_GROUP_DIM, POOL_GROUP_DIM), 0),
                      (pool_w_out, 0), (mlp_w1, 0), (mlp_w2, 0))
    mod, (w_in, w_grp, w_out, w1_0, w2_0) = _modulation(c8, ada_w, ada_b, layer0_weights)

    gmix = norm_mix_g.reshape(DEPTH, 1, D_MODEL)
    gmlp = norm_mlp_g.reshape(DEPTH, 1, D_MODEL)

    l0 = (gmix, gmlp, pool_scale, w_in,
          w_grp.reshape(n_grp, POOL_GROUP_DIM, POOL_GROUP_DIM), w_out, w1_0, w2_0)
    layer1_weights = ((attn_w_qkv, 0), (attn_w_o, 0), (mlp_w1, 1), (mlp_w2, 1))
    x1, (w_qkv, w_o, w1, w2) = _layer0(x, mod, None, 0, 512, True, layer1_weights, *l0)
    ctx1, _ = _layer0(ctx, mod, ctx_row, 0, ctx_len, False, (), *l0)

    cos, sin = _rope_tables(seq_len)
    q, kv = _qkv(x1, mod, 1, 1024, gmix, w_qkv, cos, sin)
    kvc = _ctx_kv(ctx1, mod, ctx_row, 1, gmix, w_qkv)
    o = _attention(q, kv, kvc, attn_sink[0], 2048)
    return _post(x1, o, mod, 1, 512, gmlp, final_g.reshape(1, D_MODEL), w_o, w1, w2)
```

```python
import functools

import numpy as np
import jax
import jax.numpy as jnp
from jax import lax
from jax.experimental import pallas as pl
from jax.experimental.pallas import tpu as pltpu

D_MODEL = 1024
DEPTH = 2
GRID_W = 64
POOL_WINDOWS = (2, 4, 8, 16)
POOL_GROUP_DIM = D_MODEL // len(POOL_WINDOWS)
POOL_HALO = 8
HEAD_DIM = 64
N_HEADS = D_MODEL // HEAD_DIM
N_KV_HEADS = 2
GQA_GROUP = N_HEADS // N_KV_HEADS
Q_DIM = N_HEADS * HEAD_DIM
KV_DIM = N_KV_HEADS * HEAD_DIM
WINDOW = 128
BLOCK = 128
ROPE_BASE = 10000.0
AXIS_ROT = HEAD_DIM // 2
ROT_HALF = AXIS_ROT // 2
D_FF = 4 * D_MODEL
N_MOD = 6
EPS = 1e-6
NEG = -1e30
MOD_ROWS = 8
LANES = 128
LOG2E = 1.4426950408889634
KV_W = 3 * LANES
SOFTMAX_ROWS = 32
MIN_SAFE_DENOMINATOR_LOG2 = -64.0
QKV_COL_GROUP = 256
FAST_UNROLL = 2
VMEM_LIMIT = 56 * 1024 * 1024

BF16 = jnp.bfloat16
F32 = jnp.float32


def _const_spec(shape):
    nd = len(shape)
    return pl.BlockSpec(shape, lambda *_: (0,) * nd, pipeline_mode=pl.Buffered(1))


def _layer_rows_spec():
    return _const_spec((DEPTH, D_MODEL))


def _dot(a, b):
    return jnp.dot(a, b, preferred_element_type=F32)


def _rms_mod(x, g, shift, scale):
    y = x * lax.rsqrt(jnp.mean(x * x, axis=-1, keepdims=True) + EPS)
    return y * (g * (1.0 + scale)) + shift


def _mlp_residual(x, mod_ref, gain, w1_ref, w2_ref):
    sh2 = mod_ref[:, 3 * D_MODEL:4 * D_MODEL]
    sc2 = mod_ref[:, 4 * D_MODEL:5 * D_MODEL]
    g2 = mod_ref[:, 5 * D_MODEL:6 * D_MODEL]
    h = _rms_mod(x, gain, sh2, sc2).astype(BF16)
    a = jnp.maximum(_dot(h, w1_ref[...]), 0.0)
    a = (a * a).astype(BF16)
    return x + g2 * _dot(a, w2_ref[...])


def _staged_steps(step, n_tiles, n_stages, body):
    assert n_tiles >= n_stages
    for t in range(n_stages - 1):
        pl.when(step == t)(functools.partial(body, *[s <= t for s in range(n_stages)]))
    pl.when((step >= n_stages - 1) & (step < n_tiles))(functools.partial(body, *[True] * n_stages))
    for k in range(n_stages - 1):
        pl.when(step == n_tiles + k)(functools.partial(body, *[s > k for s in range(n_stages)]))


def _skewed_steps(step, n_tiles, body):
    _staged_steps(step, n_tiles, 2, body)


def _stage_tile(n_tiles, tiles_per_row, stage):
    def tile(t):
        k = jnp.clip(t - stage, 0, n_tiles - 1)
        return k // tiles_per_row, k % tiles_per_row
    return tile


def _skew_maps(n_tiles, tiles_per_row):
    return _stage_tile(n_tiles, tiles_per_row, 0), _stage_tile(n_tiles, tiles_per_row, 1)


def _mod_kernel(n_prep, c_ref, w_ref, b_ref, *refs):
    prep_in, o_ref, prep_out = refs[:n_prep], refs[n_prep], refs[n_prep + 1:]
    c = c_ref[...]
    s = (c * jax.nn.sigmoid(c)).astype(BF16)
    r = _dot(s, w_ref[...].astype(BF16)) + b_ref[...]
    for row in range(MOD_ROWS):
        o_ref[row] = r[row:row + 1, :]
    for src, dst in zip(prep_in, prep_out):
        dst[...] = src[...].astype(BF16)


def _modulation(c8, ada_w, ada_b, first_layer_weights):
    tn = 1536
    n = N_MOD * D_MODEL
    n_col = n // tn
    steps = DEPTH * n_col
    prep_in_specs, prep_out_specs = [], []
    for w, idx in first_layer_weights:
        rows = w.shape[1] // steps
        prep_in_specs.append(pl.BlockSpec(
            (None, rows, w.shape[2]), lambda i, j, idx=idx: (idx, i * n_col + j, 0)))
        prep_out_specs.append(pl.BlockSpec((rows, w.shape[2]), lambda i, j: (i * n_col + j, 0)))
    outs = pl.pallas_call(
        functools.partial(_mod_kernel, len(first_layer_weights)),
        grid=(DEPTH, n_col),
        in_specs=[
            pl.BlockSpec((MOD_ROWS, D_MODEL), lambda i, j: (0, 0)),
            pl.BlockSpec((None, D_MODEL, tn), lambda i, j: (i, 0, j)),
            pl.BlockSpec((None, 1, tn), lambda i, j: (i, 0, j)),
        ] + prep_in_specs,
        out_specs=[pl.BlockSpec((None, MOD_ROWS, 1, tn), lambda i, j: (i, 0, 0, j))] + prep_out_specs,
        out_shape=[jax.ShapeDtypeStruct((DEPTH, MOD_ROWS, 1, n), F32)]
        + [jax.ShapeDtypeStruct(w.shape[1:], BF16) for w, _ in first_layer_weights],
        compiler_params=pltpu.CompilerParams(
            dimension_semantics=("arbitrary", "arbitrary"), vmem_limit_bytes=VMEM_LIMIT),
        name="modulation",
    )(c8, ada_w, ada_b.reshape(DEPTH, 1, n), *[w for w, _ in first_layer_weights])
    return outs[0], tuple(outs[1:])


def _to_pair_layout(v):
    lane = lax.broadcasted_iota(jnp.int32, v.shape, 1)
    for width in (ROT_HALF, AXIS_ROT):
        hi = (lane // (2 * width)) % 2
        lo = (lane // width) % 2
        v = jnp.where(hi == lo, v, jnp.where(hi == 0, pltpu.roll(v, LANES - width, 1),
                                             pltpu.roll(v, width, 1)))
    return v


def _prepare_next_weights(prep_in, prep_out):
    wqkv_in, wqkv_out = prep_in[0], prep_out[0]
    low = lax.broadcasted_iota(jnp.int32, (wqkv_in.shape[0], LANES), 1) < HEAD_DIM
    for p in range(GQA_GROUP):
        first = wqkv_in[:, (p // 2) * LANES:(p // 2 + 1) * LANES]
        second = wqkv_in[:, (p // 2 + GQA_GROUP // 2) * LANES:(p // 2 + GQA_GROUP // 2 + 1) * LANES]
        if p % 2 == 0:
            both = jnp.where(low, first, pltpu.roll(second, HEAD_DIM, 1))
        else:
            both = jnp.where(low, pltpu.roll(first, HEAD_DIM, 1), second)
        wqkv_out[:, p * LANES:(p + 1) * LANES] = _to_pair_layout(both).astype(BF16)
    wqkv_out[:, Q_DIM:Q_DIM + KV_DIM] = _to_pair_layout(wqkv_in[:, Q_DIM:Q_DIM + KV_DIM]).astype(BF16)
    wqkv_out[:, Q_DIM + KV_DIM:] = wqkv_in[:, Q_DIM + KV_DIM:].astype(BF16)
    for src, dst in zip(prep_in[1:], prep_out[1:]):
        dst[...] = src[...].astype(BF16)


def _layer0_kernel(seq_len, tm, n_tiles, skew, n_prep, layer, *refs):
    (x_ref, xp_ref, xn_ref, moda_ref, modb_ref, gmix_ref, gmlp_ref, pscale_ref, win_ref, wgrp_ref,
     wout_ref, w1_ref, w2_ref) = refs[:13]
    prep_in = refs[13:13 + n_prep]
    o_ref = refs[13 + n_prep]
    prep_out = refs[14 + n_prep:14 + 2 * n_prep]
    u_ref, d_ref, x1_ref, h2_ref = refs[14 + 2 * n_prep:]
    step = pl.program_id(0)
    i = jnp.minimum(step, n_tiles - 1) % (seq_len // tm)
    half = D_FF // 2

    def mlp_up(h2, c0, c1):
        a = jnp.maximum(_dot(h2, w1_ref[:, c0:c1]), 0.0)
        return (a * a).astype(BF16)

    def body(do_mixer, do_prev_mlp):
        if do_prev_mlp:
            g2 = modb_ref[:, 5 * D_MODEL:6 * D_MODEL]
            h2_prev = h2_ref[...]
            a_lo = mlp_up(h2_prev, 0, half)

        if do_mixer:
            sh1 = moda_ref[:, 0:D_MODEL]
            sc1 = moda_ref[:, D_MODEL:2 * D_MODEL]
            x = x_ref[...]
            xe = jnp.concatenate([xp_ref[...], x, xn_ref[...]], axis=0)
            u = _dot(_rms_mod(xe, gmix_ref[layer:layer + 1, :], sh1, sc1).astype(BF16), win_ref[...])
            u_ref[0:POOL_HALO, :] = jnp.where(i > 0, u[0:POOL_HALO], 0.0)
            u_ref[POOL_HALO:POOL_HALO + tm, :] = u[POOL_HALO:POOL_HALO + tm]
            u_ref[POOL_HALO + tm:POOL_HALO + tm + POOL_HALO, :] = jnp.where(
                (i + 1) * tm < seq_len, u[POOL_HALO + tm:], 0.0)

        if do_prev_mlp:
            a_hi = mlp_up(h2_prev, half, D_FF)

        if do_mixer:
            t = i * tm + lax.broadcasted_iota(jnp.int32, (tm, 1), 0)
            for g, w in enumerate(POOL_WINDOWS):
                c0, c1 = g * POOL_GROUP_DIM, (g + 1) * POOL_GROUP_DIM
                acc = u_ref[POOL_HALO - w // 2:POOL_HALO - w // 2 + tm, c0:c1]
                for s in range(-w // 2 + 1, w // 2):
                    acc = acc + u_ref[POOL_HALO + s:POOL_HALO + s + tm, c0:c1]
                lo = jnp.clip(t - w // 2, 0, seq_len)
                hi = jnp.clip(t + w // 2, 0, seq_len)
                inv_cnt = 1.0 / (hi - lo).astype(F32)
                d = acc * inv_cnt - u_ref[POOL_HALO:POOL_HALO + tm, c0:c1]
                d_ref[:, c0:c1] = _dot(d.astype(BF16), wgrp_ref[g])
            y = _dot((d_ref[...] * pscale_ref[...]).astype(BF16), wout_ref[...])

        if do_prev_mlp:
            down = _dot(jnp.concatenate([a_lo, a_hi], axis=1), w2_ref[...])

        if do_mixer:
            g1 = moda_ref[:, 2 * D_MODEL:3 * D_MODEL]
            sh2 = moda_ref[:, 3 * D_MODEL:4 * D_MODEL]
            sc2 = moda_ref[:, 4 * D_MODEL:5 * D_MODEL]
            x1 = x + g1 * y
            h2 = _rms_mod(x1, gmlp_ref[layer:layer + 1, :], sh2, sc2).astype(BF16)

        if do_prev_mlp:
            o_ref[...] = x1_ref[...] + g2 * down
        if do_mixer and n_prep:
            _prepare_next_weights(prep_in, prep_out)
        if do_mixer and skew:
            x1_ref[...] = x1
            h2_ref[...] = h2
        if do_mixer and not skew:
            g2_now = moda_ref[:, 5 * D_MODEL:6 * D_MODEL]
            o_ref[...] = x1 + g2_now * _dot(mlp_up(h2, 0, D_FF), w2_ref[...])

    if skew:
        _skewed_steps(step, n_tiles, body)
    else:
        body(True, False)


def _layer0(x, mod, mod_row, layer, tm, skew, next_weights, gmix, gmlp, pscale, w_in, w_grp, w_out, w1, w2):
    nb, seq_len, _ = x.shape
    rows8 = seq_len // POOL_HALO
    tpb = tm // POOL_HALO
    nt = seq_len // tm
    n_tiles = nb * nt
    tile_a, tile_b = _skew_maps(n_tiles, nt)
    if not skew:
        tile_b = tile_a

    def x_map(t):
        b, i = tile_a(t)
        return (b, i, 0)

    def prev_map(t):
        b, i = tile_a(t)
        return (b, jnp.maximum(i * tpb - 1, 0), 0)

    def next_map(t):
        b, i = tile_a(t)
        return (b, jnp.minimum((i + 1) * tpb, rows8 - 1), 0)

    def moda_map(t):
        return (layer, tile_a(t)[0] if mod_row is None else mod_row, 0, 0)

    def modb_map(t):
        return (layer, tile_b(t)[0] if mod_row is None else mod_row, 0, 0)

    def out_map(t):
        b, i = tile_b(t)
        return (b, i, 0)

    prep_in_specs, prep_out_specs = [], []
    for w, idx in next_weights:
        rows = w.shape[1] // n_tiles
        prep_in_specs.append(pl.BlockSpec(
            (None, rows, w.shape[2]), lambda t, idx=idx: (idx, jnp.minimum(t, n_tiles - 1), 0)))
        prep_out_specs.append(pl.BlockSpec(
            (rows, w.shape[2]), lambda t: (jnp.minimum(t, n_tiles - 1), 0)))

    outs = pl.pallas_call(
        functools.partial(_layer0_kernel, seq_len, tm, n_tiles, skew, len(next_weights), layer),
        grid=(n_tiles + (1 if skew else 0),),
        in_specs=[
            pl.BlockSpec((None, tm, D_MODEL), x_map),
            pl.BlockSpec((None, POOL_HALO, D_MODEL), prev_map),
            pl.BlockSpec((None, POOL_HALO, D_MODEL), next_map),
            pl.BlockSpec((None, None, 1, N_MOD * D_MODEL), moda_map),
            pl.BlockSpec((None, None, 1, N_MOD * D_MODEL), modb_map),
            _layer_rows_spec(),
            _layer_rows_spec(),
            _const_spec((1, D_MODEL)),
            _const_spec((D_MODEL, D_MODEL)),
            _const_spec((len(POOL_WINDOWS), POOL_GROUP_DIM, POOL_GROUP_DIM)),
            _const_spec((D_MODEL, D_MODEL)),
            _const_spec((D_MODEL, D_FF)),
            _const_spec((D_FF, D_MODEL)),
        ] + prep_in_specs,
        out_specs=[pl.BlockSpec((None, tm, D_MODEL), out_map)] + prep_out_specs,
        out_shape=[jax.ShapeDtypeStruct(x.shape, F32)]
        + [jax.ShapeDtypeStruct(w.shape[1:], BF16) for w, _ in next_weights],
        scratch_shapes=[pltpu.VMEM((tm + 2 * POOL_HALO, D_MODEL), F32),
                        pltpu.VMEM((tm, D_MODEL), F32),
                        pltpu.VMEM((tm, D_MODEL), F32),
                        pltpu.VMEM((tm, D_MODEL), BF16)],
        compiler_params=pltpu.CompilerParams(
            dimension_semantics=("arbitrary",), vmem_limit_bytes=VMEM_LIMIT),
        name="layer0",
    )(x, x, x, mod, mod, gmix, gmlp, pscale, w_in, w_grp, w_out, w1, w2, *[w for w, _ in next_weights])
    return outs[0], tuple(outs[1:])


def _store_kv(kv_ref, k, v):
    low = lax.broadcasted_iota(jnp.int32, v.shape, 1) < HEAD_DIM
    kv_ref[:, 0:LANES] = k.astype(BF16)
    kv_ref[:, LANES:2 * LANES] = jnp.where(low, v, 1.0).astype(BF16)
    kv_ref[:, 2 * LANES:3 * LANES] = jnp.where(low, 1.0, v).astype(BF16)


def _qkv_kernel(n_tiles, layer, x_ref, mod_ref, gmix_ref, w_ref, cos_ref, sin_ref, q_ref, kv_ref, h_ref):
    def body(do_norm, do_project):
        if do_project:
            h_prev = h_ref[...]
            q_groups = [_dot(h_prev, w_ref[:, c:c + QKV_COL_GROUP]) for c in range(0, Q_DIM, QKV_COL_GROUP)]
            kv = _dot(h_prev, w_ref[:, Q_DIM:])
        if do_norm:
            sh1 = mod_ref[:, 0:D_MODEL]
            sc1 = mod_ref[:, D_MODEL:2 * D_MODEL]
            h = _rms_mod(x_ref[...], gmix_ref[layer:layer + 1, :], sh1, sc1).astype(BF16)
        if do_project:
            cos = cos_ref[...]
            sin = sin_ref[...]
            qscale = HEAD_DIM ** -0.5 * LOG2E
            cos_q = cos * qscale
            sin_q = sin * qscale

            def rope(t, c, s):
                return t * c + pltpu.roll(t, HEAD_DIM, 1) * s

            for gi, group in enumerate(q_groups):
                for cb in range(QKV_COL_GROUP // LANES):
                    c0 = gi * QKV_COL_GROUP + cb * LANES
                    q_ref[:, c0:c0 + LANES] = rope(group[:, cb * LANES:(cb + 1) * LANES], cos_q, sin_q).astype(BF16)
            _store_kv(kv_ref, rope(kv[:, 0:KV_DIM], cos, sin), kv[:, KV_DIM:])
        if do_norm:
            h_ref[...] = h

    _skewed_steps(pl.program_id(0), n_tiles, body)


def _qkv(x, mod, layer, tm, gmix, w_qkv, cos, sin):
    nb, seq_len, _ = x.shape
    nt = seq_len // tm
    n_tiles = nb * nt
    tile_a, tile_b = _skew_maps(n_tiles, nt)

    def out_map(t):
        b, i = tile_b(t)
        return (b, i, 0)

    return pl.pallas_call(
        functools.partial(_qkv_kernel, n_tiles, layer),
        grid=(n_tiles + 1,),
        in_specs=[
            pl.BlockSpec((None, tm, D_MODEL), lambda t: (*tile_a(t), 0)),
            pl.BlockSpec((None, None, 1, N_MOD * D_MODEL), lambda t: (layer, tile_a(t)[0], 0, 0)),
            _layer_rows_spec(),
            _const_spec((D_MODEL, Q_DIM + 2 * KV_DIM)),
            pl.BlockSpec((tm, LANES), lambda t: (tile_b(t)[1], 0)),
            pl.BlockSpec((tm, LANES), lambda t: (tile_b(t)[1], 0)),
        ],
        out_specs=[pl.BlockSpec((None, tm, Q_DIM), out_map),
                   pl.BlockSpec((None, tm, KV_W), out_map)],
        out_shape=[jax.ShapeDtypeStruct((nb, seq_len, Q_DIM), BF16),
                   jax.ShapeDtypeStruct((nb, seq_len, KV_W), BF16)],
        scratch_shapes=[pltpu.VMEM((tm, D_MODEL), BF16)],
        compiler_params=pltpu.CompilerParams(
            dimension_semantics=("arbitrary",), vmem_limit_bytes=VMEM_LIMIT),
        name="qkv_rope",
    )(x, mod, gmix, w_qkv, cos, sin)


def _ctx_kv_kernel(layer, x_ref, mod_ref, gmix_ref, w_ref, kv_ref):
    sh1 = mod_ref[:, 0:D_MODEL]
    sc1 = mod_ref[:, D_MODEL:2 * D_MODEL]
    h = _rms_mod(x_ref[...], gmix_ref[layer:layer + 1, :], sh1, sc1).astype(BF16)
    kv = _dot(h, w_ref[...])
    _store_kv(kv_ref, kv[:, 0:KV_DIM], kv[:, KV_DIM:])


def _ctx_kv(ctx, mod, mod_row, layer, gmix, w_kv):
    nb, ctx_len, _ = ctx.shape
    return pl.pallas_call(
        functools.partial(_ctx_kv_kernel, layer),
        grid=(nb,),
        in_specs=[
            pl.BlockSpec((None, ctx_len, D_MODEL), lambda b: (b, 0, 0)),
            pl.BlockSpec((None, None, 1, N_MOD * D_MODEL), lambda b: (layer, mod_row, 0, 0)),
            _layer_rows_spec(),
            pl.BlockSpec((D_MODEL, 2 * KV_DIM), lambda b: (0, Q_DIM // (2 * KV_DIM)),
                         pipeline_mode=pl.Buffered(1)),
        ],
        out_specs=pl.BlockSpec((None, ctx_len, KV_W), lambda b: (b, 0, 0)),
        out_shape=jax.ShapeDtypeStruct((nb, ctx_len, KV_W), BF16),
        compiler_params=pltpu.CompilerParams(
            dimension_semantics=("arbitrary",), vmem_limit_bytes=VMEM_LIMIT),
        name="ctx_kv",
    )(ctx, mod, gmix, w_kv)


def _dot_nt(a, b):
    return lax.dot_general(a, b, (((1,), (1,)), ((), ())), preferred_element_type=F32)


def _attn_kernel(seq_len, tq, ctx_len, sink_ref, q_ref, kvp_ref, kvm_ref, kvn_ref, kvc_ref, o_ref,
                 kv_ext, *bufs):
    s_refs, e_refs, m_refs = bufs[0:2], bufs[2:6], bufs[6:10]
    kn_ref, minden_ref = bufs[10], bufs[11]
    i = pl.program_id(1)
    kv_ext[0:BLOCK, :] = kvp_ref[...]
    kv_ext[BLOCK:BLOCK + tq, :] = kvm_ref[...]
    kv_ext[BLOCK + tq:2 * BLOCK + tq, :] = kvn_ref[...]
    nblk = tq // BLOCK
    nwin = 3 * BLOCK
    nkey = nwin + ctx_len
    row = lax.broadcasted_iota(jnp.int32, (BLOCK, BLOCK), 0)
    col = lax.broadcasted_iota(jnp.int32, (BLOCK, BLOCK), 1)
    lane = lax.broadcasted_iota(jnp.int32, (BLOCK, LANES), 1)
    low = lane < HEAD_DIM
    head_a = (lane % HEAD_DIM) < AXIS_ROT
    my_lanes = (head_a, jnp.logical_not(head_a))
    den_lanes = (jnp.logical_not(low), low)

    chunks = [(g, t0) for g in range(GQA_GROUP) for t0 in range(0, BLOCK, SOFTMAX_ROWS)]

    def head_queries(j, kvh):
        q0 = pl.multiple_of(j * BLOCK, BLOCK)
        tiles = []
        for g in range(GQA_GROUP):
            pair = q_ref[pl.ds(q0, BLOCK), g * LANES:(g + 1) * LANES]
            tiles.append(jnp.where(my_lanes[kvh], pair, jnp.zeros_like(pair)))
        return tiles

    def window_masks(j):
        mask_l = (col >= row) & (i * tq + (j - 1) * BLOCK >= 0)
        mask_r = (col <= row) & (i * tq + (j + 1) * BLOCK < seq_len)
        return mask_l, mask_r

    def key_tiles(j, kvh):
        q0 = pl.multiple_of(j * BLOCK, BLOCK)
        return kv_ext[pl.ds(q0, nwin), 0:LANES], kvc_ref[:, 0:LANES]

    def scores(j, kvh):
        qs = jnp.concatenate(head_queries(j, kvh), axis=0)
        kw, kc = key_tiles(j, kvh)
        s_refs[kvh][:, 0:nwin] = _dot_nt(qs, kw)
        s_refs[kvh][:, nwin:nkey] = _dot_nt(qs, kc)

    def softmax(j, kvh):
        mask_l, mask_r = window_masks(j)

        def masked_pieces(rows, trows):
            pieces = [s_refs[kvh][rows, p * LANES:(p + 1) * LANES] for p in range(nkey // LANES)]
            pieces[0] = jnp.where(mask_l[trows], pieces[0], NEG)
            pieces[2] = jnp.where(mask_r[trows], pieces[2], NEG)
            return pieces

        for g, t0 in chunks:
            rows = slice(g * BLOCK + t0, g * BLOCK + t0 + SOFTMAX_ROWS)
            pieces = masked_pieces(rows, slice(t0, t0 + SOFTMAX_ROWS))
            m = pieces[0]
            for p in pieces[1:]:
                m = jnp.maximum(m, p)
            m = jnp.maximum(jnp.max(m, axis=1, keepdims=True), sink_ref[kvh * GQA_GROUP + g] * LOG2E)
            m_refs[kvh][rows, :] = jnp.broadcast_to(m, (SOFTMAX_ROWS, LANES))
        for g, t0 in chunks:
            rows = slice(g * BLOCK + t0, g * BLOCK + t0 + SOFTMAX_ROWS)
            pieces = masked_pieces(rows, slice(t0, t0 + SOFTMAX_ROWS))
            m = m_refs[kvh][rows, :]
            for p, piece in enumerate(pieces):
                e_refs[kvh][rows, p * LANES:(p + 1) * LANES] = jnp.exp2((piece - m).astype(BF16))

    def weighted_values(j, kvh, slot, fast):
        e_ref, m_ref = e_refs[2 * slot + kvh], m_refs[2 * slot + kvh]
        q0 = pl.multiple_of(j * BLOCK, BLOCK)
        vcols = slice((1 + kvh) * LANES, (2 + kvh) * LANES)
        values = jnp.concatenate([kv_ext[pl.ds(q0, nwin), vcols], kvc_ref[:, vcols]], axis=0)
        o = _dot(e_ref[...], values)
        on, smallest = [], None
        for g in range(GQA_GROUP):
            rows = slice(g * BLOCK, (g + 1) * BLOCK)
            sink = sink_ref[kvh * GQA_GROUP + g] * LOG2E
            if fast:
                sink_term = jnp.exp2(sink - m_ref[g * BLOCK:g * BLOCK + 1, :])
            else:
                sink_term = jnp.exp2(sink - m_ref[rows, :])
            total = o[rows] + sink_term
            smallest = total if smallest is None else jnp.minimum(smallest, total)
            on.append(o[rows] * pltpu.roll(1.0 / total, HEAD_DIM, 1))
        if fast:
            minden_ref[...] = jnp.minimum(minden_ref[...], jnp.where(den_lanes[kvh], smallest, 3e38))
        for pair in range(GQA_GROUP // 2):
            if kvh == 0:
                tile = jnp.where(low, on[2 * pair], pltpu.roll(on[2 * pair + 1], HEAD_DIM, 1))
            else:
                tile = jnp.where(low, pltpu.roll(on[2 * pair], HEAD_DIM, 1), on[2 * pair + 1])
            c0 = (kvh * GQA_GROUP + 2 * pair) * HEAD_DIM
            o_ref[pl.ds(q0, BLOCK), c0:c0 + LANES] = tile.astype(BF16)

    def exact_body(j, carry):
        scores(j, 0)
        scores(j, 1)
        softmax(j, 0)
        weighted_values(j, 0, 0, False)
        softmax(j, 1)
        weighted_values(j, 1, 0, False)
        return carry

    def max_sq_norms(k_tile):
        sq = k_tile.astype(F32)
        sq = sq * sq
        return [jnp.broadcast_to(jnp.max(jnp.sum(jnp.where(my_lanes[kvh], sq, 0.0), axis=1, keepdims=True),
                                         axis=0, keepdims=True), (8, LANES)) for kvh in range(N_KV_HEADS)]

    for kb in range(nblk + 2):
        for kvh, n2 in enumerate(max_sq_norms(kv_ext[kb * BLOCK:(kb + 1) * BLOCK, 0:LANES])):
            kn_ref[kvh, kb] = n2
    ctx_norm = None
    for cb in range(ctx_len // BLOCK):
        n2 = max_sq_norms(kvc_ref[cb * BLOCK:(cb + 1) * BLOCK, 0:LANES])
        ctx_norm = n2 if ctx_norm is None else [jnp.maximum(a, b) for a, b in zip(ctx_norm, n2)]
    minden_ref[...] = jnp.full(minden_ref.shape, 3e38, F32)

    def bounded_exp(j, kvh, slot):
        e_ref, m_ref = e_refs[2 * slot + kvh], m_refs[2 * slot + kvh]
        q0 = pl.multiple_of(j * BLOCK, BLOCK)
        kw, kc = key_tiles(j, kvh)
        kn2 = jnp.maximum(jnp.max(kn_ref[kvh, pl.ds(j, 3)], axis=0), ctx_norm[kvh])[0:1, :]
        tiles = head_queries(j, kvh)
        for g in range(GQA_GROUP):
            qf = tiles[g].astype(F32)
            qn2 = jnp.max(jnp.sum(qf * qf, axis=1, keepdims=True), axis=0, keepdims=True)
            bound = jnp.maximum(jnp.sqrt(qn2 * kn2), sink_ref[kvh * GQA_GROUP + g] * LOG2E)
            m_ref[g * BLOCK:(g + 1) * BLOCK, :] = jnp.broadcast_to(bound, (BLOCK, LANES))
        qs = jnp.concatenate(tiles, axis=0)
        s_w = _dot_nt(qs, kw)
        s_c = _dot_nt(qs, kc)
        mask_l, mask_r = window_masks(j)
        for g, t0 in chunks:
            rows = slice(g * BLOCK + t0, g * BLOCK + t0 + SOFTMAX_ROWS)
            trows = slice(t0, t0 + SOFTMAX_ROWS)
            m = m_ref[rows, :]
            pieces = ([s_w[rows, p * LANES:(p + 1) * LANES] for p in range(nwin // LANES)]
                      + [s_c[rows, p * LANES:(p + 1) * LANES] for p in range(ctx_len // LANES)])
            pieces[0] = jnp.where(mask_l[trows], pieces[0], NEG)
            pieces[2] = jnp.where(mask_r[trows], pieces[2], NEG)
            for p, piece in enumerate(pieces):
                e_ref[rows, p * LANES:(p + 1) * LANES] = jnp.exp2(piece - m).astype(BF16)

    bounded_exp(0, 0, 0)
    bounded_exp(0, 1, 0)

    def fast_blocks(j0, last):
        for k in range(FAST_UNROLL):
            for kvh in range(N_KV_HEADS):
                weighted_values(j0 + k, kvh, k % 2, True)
                if not (last and k + 1 == FAST_UNROLL):
                    bounded_exp(j0 + k + 1, kvh, (k + 1) % 2)

    def fast_body(jj, carry):
        fast_blocks(FAST_UNROLL * jj, False)
        return carry

    lax.fori_loop(0, nblk // FAST_UNROLL - 1, fast_body, 0)
    fast_blocks(nblk - FAST_UNROLL, True)
    smallest = jnp.min(minden_ref[...])

    @pl.when(smallest < 2.0 ** MIN_SAFE_DENOMINATOR_LOG2)
    def _():
        lax.fori_loop(0, nblk, exact_body, 0)


def _attention(q, kv, kvc, sink, tq):
    nb, seq_len, _ = q.shape
    ctx_len = kvc.shape[1]
    nblk = seq_len // BLOCK
    qpb = tq // BLOCK
    rows = GQA_GROUP * BLOCK
    nkey = 3 * BLOCK + ctx_len
    return pl.pallas_call(
        functools.partial(_attn_kernel, seq_len, tq, ctx_len),
        grid=(nb, seq_len // tq),
        in_specs=[
            pl.BlockSpec(memory_space=pltpu.SMEM),
            pl.BlockSpec((None, tq, Q_DIM), lambda b, i: (b, i, 0)),
            pl.BlockSpec((None, BLOCK, KV_W), lambda b, i: (b, jnp.maximum(i * qpb - 1, 0), 0)),
            pl.BlockSpec((None, tq, KV_W), lambda b, i: (b, i, 0)),
            pl.BlockSpec((None, BLOCK, KV_W),
                         lambda b, i: (b, jnp.minimum((i + 1) * qpb, nblk - 1), 0)),
            pl.BlockSpec((None, ctx_len, KV_W), lambda b, i: (b, 0, 0)),
        ],
        out_specs=pl.BlockSpec((None, tq, Q_DIM), lambda b, i: (b, i, 0)),
        out_shape=jax.ShapeDtypeStruct((nb, seq_len, Q_DIM), BF16),
        scratch_shapes=([pltpu.VMEM((tq + 2 * BLOCK, KV_W), BF16)]
                        + [pltpu.VMEM((rows, nkey), F32)] * N_KV_HEADS
                        + [pltpu.VMEM((rows, nkey), BF16)] * (2 * N_KV_HEADS)
                        + [pltpu.VMEM((rows, LANES), F32)] * (2 * N_KV_HEADS)
                        + [pltpu.VMEM((N_KV_HEADS, qpb + 2, 8, LANES), F32),
                           pltpu.VMEM((BLOCK, LANES), F32)]),
        compiler_params=pltpu.CompilerParams(
            dimension_semantics=("arbitrary", "arbitrary"), vmem_limit_bytes=VMEM_LIMIT),
        name="attention",
    )(sink, q, kv, kv, kv, kvc)


def _post_kernel(layer, x_ref, o_ref_in, mod_ref, gmlp_ref, gfin_ref, wo_ref, w1_ref, w2_ref, out_ref):
    g1 = mod_ref[:, 2 * D_MODEL:3 * D_MODEL]
    x1 = x_ref[...] + g1 * _dot(o_ref_in[...], wo_ref[...])
    x2 = _mlp_residual(x1, mod_ref, gmlp_ref[layer:layer + 1, :], w1_ref, w2_ref)
    y = x2 * lax.rsqrt(jnp.mean(x2 * x2, axis=-1, keepdims=True) + EPS)
    out_ref[...] = y * gfin_ref[...]


def _post(x, o, mod, layer, tm, gmlp, gfin, w_o, w1, w2):
    nb, seq_len, _ = x.shape
    return pl.pallas_call(
        functools.partial(_post_kernel, layer),
        grid=(nb, seq_len // tm),
        in_specs=[
            pl.BlockSpec((None, tm, D_MODEL), lambda b, i: (b, i, 0)),
            pl.BlockSpec((None, tm, Q_DIM), lambda b, i: (b, i, 0)),
            pl.BlockSpec((None, None, 1, N_MOD * D_MODEL), lambda b, i: (layer, b, 0, 0)),
            _layer_rows_spec(),
            _const_spec((1, D_MODEL)),
            _const_spec((Q_DIM, D_MODEL)),
            _const_spec((D_MODEL, D_FF)),
            _const_spec((D_FF, D_MODEL)),
        ],
        out_specs=pl.BlockSpec((None, tm, D_MODEL), lambda b, i: (b, i, 0)),
        out_shape=jax.ShapeDtypeStruct(x.shape, F32),
        compiler_params=pltpu.CompilerParams(
            dimension_semantics=("arbitrary", "arbitrary"), vmem_limit_bytes=VMEM_LIMIT),
        name="post_attn",
    )(x, o, mod, gmlp, gfin, w_o, w1, w2)


def _rope_tables(seq_len):
    pos = np.arange(seq_len)
    inv = ROPE_BASE ** (-np.arange(0, AXIS_ROT, 2, dtype=np.float32) / AXIS_ROT)
    ang = np.concatenate([(pos // GRID_W).astype(np.float32)[:, None] * inv[None],
                          (pos % GRID_W).astype(np.float32)[:, None] * inv[None]], axis=-1)
    cos = np.tile(np.cos(ang.astype(np.float64)), (1, LANES // AXIS_ROT))
    sin = np.sin(ang.astype(np.float64))
    return (jnp.asarray(cos, F32), jnp.asarray(np.concatenate([-sin, -sin, sin, sin], axis=-1), F32))


def kernel(x, c, ctx, c_ctx, ada_w, ada_b, norm_mix_g, norm_mlp_g, pool_w_in, pool_w_grp,
           pool_scale, pool_w_out, attn_w_qkv, attn_sink, attn_w_o, mlp_w1, mlp_w2, final_g):
    nb, seq_len, _ = x.shape
    ctx_len = ctx.shape[1]
    ctx_row = nb
    c8 = jnp.concatenate(
        [c, c_ctx[None], jnp.zeros((MOD_ROWS - nb - 1, D_MODEL), F32)], axis=0)
    n_grp = len(POOL_WINDOWS)
    layer0_weights = ((pool_w_in, 0), (pool_w_grp.reshape(-1, n_grp * POOL_GROUP_DIM, POOL_GROUP_DIM), 0),
                      (pool_w_out, 0), (mlp_w1, 0), (mlp_w2, 0))
    mod, (w_in, w_grp, w_out, w1_0, w2_0) = _modulation(c8, ada_w, ada_b, layer0_weights)

    gmix, gmlp = norm_mix_g, norm_mlp_g

    l0 = (gmix, gmlp, pool_scale, w_in,
          w_grp.reshape(n_grp, POOL_GROUP_DIM, POOL_GROUP_DIM), w_out, w1_0, w2_0)
    layer1_weights = ((attn_w_qkv, 0), (attn_w_o, 0), (mlp_w1, 1), (mlp_w2, 1))
    x1, (w_qkv, w_o, w1, w2) = _layer0(x, mod, None, 0, 512, True, layer1_weights, *l0)
    ctx1, _ = _layer0(ctx, mod, ctx_row, 0, ctx_len, False, (), *l0)

    cos, sin = _rope_tables(seq_len)
    q, kv = _qkv(x1, mod, 1, 1024, gmix, w_qkv, cos, sin)
    kvc = _ctx_kv(ctx1, mod, ctx_row, 1, gmix, w_qkv)
    o = _attention(q, kv, kvc, attn_sink[0], 2048)
    return _post(x1, o, mod, 1, 512, gmlp, final_g.reshape(1, D_MODEL), w_o, w1, w2)
```

```python
import functools

import numpy as np
import jax
import jax.numpy as jnp
from jax import lax
from jax.experimental import pallas as pl
from jax.experimental.pallas import tpu as pltpu

D_MODEL = 1024
DEPTH = 2
GRID_W = 64
POOL_WINDOWS = (2, 4, 8, 16)
POOL_GROUP_DIM = D_MODEL // len(POOL_WINDOWS)
POOL_HALO = 8
HEAD_DIM = 64
N_HEADS = D_MODEL // HEAD_DIM
N_KV_HEADS = 2
GQA_GROUP = N_HEADS // N_KV_HEADS
Q_DIM = N_HEADS * HEAD_DIM
KV_DIM = N_KV_HEADS * HEAD_DIM
WINDOW = 128
BLOCK = 128
ROPE_BASE = 10000.0
AXIS_ROT = HEAD_DIM // 2
ROT_HALF = AXIS_ROT // 2
D_FF = 4 * D_MODEL
N_MOD = 6
EPS = 1e-6
NEG = -1e30
MOD_ROWS = 8
LANES = 128
LOG2E = 1.4426950408889634
KV_W = 3 * LANES
SOFTMAX_ROWS = 32
MIN_SAFE_DENOMINATOR_LOG2 = -64.0
QKV_COL_GROUP = 256
FAST_UNROLL = 2
VMEM_LIMIT = 56 * 1024 * 1024

BF16 = jnp.bfloat16
F32 = jnp.float32


def _const_spec(shape):
    nd = len(shape)
    return pl.BlockSpec(shape, lambda *_: (0,) * nd, pipeline_mode=pl.Buffered(1))


def _layer_rows_spec():
    return _const_spec((DEPTH, D_MODEL))


def _dot(a, b):
    return jnp.dot(a, b, preferred_element_type=F32)


def _rms_mod(x, g, shift, scale):
    y = x * lax.rsqrt(jnp.mean(x * x, axis=-1, keepdims=True) + EPS)
    return y * (g * (1.0 + scale)) + shift


def _mlp_residual(x, mod_ref, gain, w1_ref, w2_ref):
    sh2 = mod_ref[:, 3 * D_MODEL:4 * D_MODEL]
    sc2 = mod_ref[:, 4 * D_MODEL:5 * D_MODEL]
    g2 = mod_ref[:, 5 * D_MODEL:6 * D_MODEL]
    h = _rms_mod(x, gain, sh2, sc2).astype(BF16)
    a = jnp.maximum(_dot(h, w1_ref[...]), 0.0)
    a = (a * a).astype(BF16)
    return x + g2 * _dot(a, w2_ref[...])


def _staged_steps(step, n_tiles, n_stages, body):
    assert n_tiles >= n_stages
    for t in range(n_stages - 1):
        pl.when(step == t)(functools.partial(body, *[s <= t for s in range(n_stages)]))
    pl.when((step >= n_stages - 1) & (step < n_tiles))(functools.partial(body, *[True] * n_stages))
    for k in range(n_stages - 1):
        pl.when(step == n_tiles + k)(functools.partial(body, *[s > k for s in range(n_stages)]))


def _skewed_steps(step, n_tiles, body):
    _staged_steps(step, n_tiles, 2, body)


def _stage_tile(n_tiles, tiles_per_row, stage):
    def tile(t):
        k = jnp.clip(t - stage, 0, n_tiles - 1)
        return k // tiles_per_row, k % tiles_per_row
    return tile


def _skew_maps(n_tiles, tiles_per_row):
    return _stage_tile(n_tiles, tiles_per_row, 0), _stage_tile(n_tiles, tiles_per_row, 1)


def _mod_kernel(n_prep, c_ref, w_ref, b_ref, *refs):
    prep_in, o_ref, prep_out = refs[:n_prep], refs[n_prep], refs[n_prep + 1:]
    c = c_ref[...]
    s = (c * jax.nn.sigmoid(c)).astype(BF16)
    r = _dot(s, w_ref[...].astype(BF16)) + b_ref[...]
    for row in range(MOD_ROWS):
        o_ref[row] = r[row:row + 1, :]
    for src, dst in zip(prep_in, prep_out):
        dst[...] = src[...].astype(BF16)


def _modulation(c8, ada_w, ada_b, first_layer_weights):
    tn = 1536
    n = N_MOD * D_MODEL
    n_col = n // tn
    steps = DEPTH * n_col
    prep_in_specs, prep_out_specs = [], []
    for w, idx in first_layer_weights:
        rows = w.shape[1] // steps
        prep_in_specs.append(pl.BlockSpec(
            (None, rows, w.shape[2]), lambda i, j, idx=idx: (idx, i * n_col + j, 0)))
        prep_out_specs.append(pl.BlockSpec((rows, w.shape[2]), lambda i, j: (i * n_col + j, 0)))
    outs = pl.pallas_call(
        functools.partial(_mod_kernel, len(first_layer_weights)),
        grid=(DEPTH, n_col),
        in_specs=[
            pl.BlockSpec((MOD_ROWS, D_MODEL), lambda i, j: (0, 0)),
            pl.BlockSpec((None, D_MODEL, tn), lambda i, j: (i, 0, j)),
            pl.BlockSpec((None, 1, tn), lambda i, j: (i, 0, j)),
        ] + prep_in_specs,
        out_specs=[pl.BlockSpec((None, MOD_ROWS, 1, tn), lambda i, j: (i, 0, 0, j))] + prep_out_specs,
        out_shape=[jax.ShapeDtypeStruct((DEPTH, MOD_ROWS, 1, n), F32)]
        + [jax.ShapeDtypeStruct(w.shape[1:], BF16) for w, _ in first_layer_weights],
        compiler_params=pltpu.CompilerParams(
            dimension_semantics=("arbitrary", "arbitrary"), vmem_limit_bytes=VMEM_LIMIT),
        name="modulation",
    )(c8, ada_w, ada_b.reshape(DEPTH, 1, n), *[w for w, _ in first_layer_weights])
    return outs[0], tuple(outs[1:])


def _to_pair_layout(v):
    lane = lax.broadcasted_iota(jnp.int32, v.shape, 1)
    for width in (ROT_HALF, AXIS_ROT):
        hi = (lane // (2 * width)) % 2
        lo = (lane // width) % 2
        v = jnp.where(hi == lo, v, jnp.where(hi == 0, pltpu.roll(v, LANES - width, 1),
                                             pltpu.roll(v, width, 1)))
    return v


def _prepare_next_weights(prep_in, prep_out):
    wqkv_in, wqkv_out = prep_in[0], prep_out[0]
    low = lax.broadcasted_iota(jnp.int32, (wqkv_in.shape[0], LANES), 1) < HEAD_DIM
    for p in range(GQA_GROUP):
        first = wqkv_in[:, (p // 2) * LANES:(p // 2 + 1) * LANES]
        second = wqkv_in[:, (p // 2 + GQA_GROUP // 2) * LANES:(p // 2 + GQA_GROUP // 2 + 1) * LANES]
        if p % 2 == 0:
            both = jnp.where(low, first, pltpu.roll(second, HEAD_DIM, 1))
        else:
            both = jnp.where(low, pltpu.roll(first, HEAD_DIM, 1), second)
        wqkv_out[:, p * LANES:(p + 1) * LANES] = _to_pair_layout(both).astype(BF16)
    wqkv_out[:, Q_DIM:Q_DIM + KV_DIM] = _to_pair_layout(wqkv_in[:, Q_DIM:Q_DIM + KV_DIM]).astype(BF16)
    wqkv_out[:, Q_DIM + KV_DIM:] = wqkv_in[:, Q_DIM + KV_DIM:].astype(BF16)
    for src, dst in zip(prep_in[1:], prep_out[1:]):
        dst[...] = src[...].astype(BF16)


def _layer0_kernel(seq_len, ctx_len, tm, n_lat, n_ctx, n_prep, layer, *refs):
    (x_ref, xp_ref, xn_ref, ctx_ref, moda_ref, modb_ref, gmix_ref, gmlp_ref, pscale_ref, win_ref,
     wgrp_ref, wout_ref, w1_ref, w2_ref) = refs[:14]
    prep_in = refs[14:14 + n_prep]
    out_refs = {"latent": refs[14 + n_prep], "ctx": refs[15 + n_prep]}
    prep_out = refs[16 + n_prep:16 + 2 * n_prep]
    u_ref, d_ref, x1_ref, h2_ref = refs[16 + 2 * n_prep:]
    step = pl.program_id(0)
    i = jnp.minimum(step, n_lat - 1) % (seq_len // tm)
    half = D_FF // 2

    def mlp_up(h2, c0, c1):
        a = jnp.maximum(_dot(h2, w1_ref[:, c0:c1]), 0.0)
        return (a * a).astype(BF16)

    def body(mixer, mlp_out):
        if mlp_out:
            g2 = modb_ref[:, 5 * D_MODEL:6 * D_MODEL]
            h2_prev = h2_ref[...]
            a_lo = mlp_up(h2_prev, 0, half)

        if mixer:
            sh1 = moda_ref[:, 0:D_MODEL]
            sc1 = moda_ref[:, D_MODEL:2 * D_MODEL]
            gmix = gmix_ref[layer:layer + 1, :]
            zero_halo = jnp.zeros((POOL_HALO, D_MODEL), F32)
            if mixer == "latent":
                x = x_ref[...]
                xe = jnp.concatenate([xp_ref[...], x, xn_ref[...]], axis=0)
                u = _dot(_rms_mod(xe, gmix, sh1, sc1).astype(BF16), win_ref[...])
                u_ref[0:POOL_HALO, :] = jnp.where(i > 0, u[0:POOL_HALO], 0.0)
                u_ref[POOL_HALO:POOL_HALO + tm, :] = u[POOL_HALO:POOL_HALO + tm]
                u_ref[POOL_HALO + tm:POOL_HALO + tm + POOL_HALO, :] = jnp.where(
                    (i + 1) * tm < seq_len, u[POOL_HALO + tm:], 0.0)
                seq, pos = seq_len, i * tm + lax.broadcasted_iota(jnp.int32, (tm, 1), 0)
            else:
                x = ctx_ref[...]
                u_ref[0:POOL_HALO, :] = zero_halo
                u_ref[POOL_HALO:POOL_HALO + tm, :] = _dot(_rms_mod(x, gmix, sh1, sc1).astype(BF16), win_ref[...])
                u_ref[POOL_HALO + tm:POOL_HALO + tm + POOL_HALO, :] = zero_halo
                seq, pos = ctx_len, lax.broadcasted_iota(jnp.int32, (tm, 1), 0) % ctx_len

        if mlp_out:
            a_hi = mlp_up(h2_prev, half, D_FF)

        if mixer:
            for g, w in enumerate(POOL_WINDOWS):
                c0, c1 = g * POOL_GROUP_DIM, (g + 1) * POOL_GROUP_DIM
                acc = None
                for s in range(-w // 2, w // 2):
                    term = u_ref[POOL_HALO + s:POOL_HALO + s + tm, c0:c1]
                    if mixer == "ctx" and s != 0:
                        term = jnp.where((pos + s >= 0) & (pos + s < seq), term, 0.0)
                    acc = term if acc is None else acc + term
                lo = jnp.clip(pos - w // 2, 0, seq)
                hi = jnp.clip(pos + w // 2, 0, seq)
                inv_cnt = 1.0 / (hi - lo).astype(F32)
                d = acc * inv_cnt - u_ref[POOL_HALO:POOL_HALO + tm, c0:c1]
                d_ref[:, c0:c1] = _dot(d.astype(BF16), wgrp_ref[g])
            y = _dot((d_ref[...] * pscale_ref[...]).astype(BF16), wout_ref[...])

        if mlp_out:
            down = _dot(jnp.concatenate([a_lo, a_hi], axis=1), w2_ref[...])

        if mixer:
            g1 = moda_ref[:, 2 * D_MODEL:3 * D_MODEL]
            sh2 = moda_ref[:, 3 * D_MODEL:4 * D_MODEL]
            sc2 = moda_ref[:, 4 * D_MODEL:5 * D_MODEL]
            x1 = x + g1 * y
            h2 = _rms_mod(x1, gmlp_ref[layer:layer + 1, :], sh2, sc2).astype(BF16)

        if mlp_out:
            out_refs[mlp_out][...] = x1_ref[...] + g2 * down
        if mixer == "latent" and n_prep:
            _prepare_next_weights(prep_in, prep_out)
        if mixer:
            x1_ref[...] = x1
            h2_ref[...] = h2

    pl.when(step == 0)(functools.partial(body, "latent", None))
    pl.when((step > 0) & (step < n_lat))(functools.partial(body, "latent", "latent"))
    pl.when(step == n_lat)(functools.partial(body, "ctx", "latent"))
    if n_ctx > 1:
        pl.when((step > n_lat) & (step < n_lat + n_ctx))(functools.partial(body, "ctx", "ctx"))
    pl.when(step == n_lat + n_ctx)(functools.partial(body, None, "ctx"))


def _layer0(x, ctx, mod, ctx_row, layer, tm, next_weights, gmix, gmlp, pscale, w_in, w_grp, w_out, w1, w2):
    nb, seq_len, _ = x.shape
    ctx_len = ctx.shape[1]
    assert tm % ctx_len == 0 and (ctx.shape[0] * ctx_len) % tm == 0
    ctx_tiles = ctx.reshape(-1, tm, D_MODEL)
    rows8 = seq_len // POOL_HALO
    tpb = tm // POOL_HALO
    nt = seq_len // tm
    n_lat = nb * nt
    n_ctx = ctx_tiles.shape[0]

    def lat_tile(t, stage):
        k = jnp.clip(t - stage, 0, n_lat - 1)
        return k // nt, k % nt

    def ctx_tile(t, stage):
        return jnp.clip(t - stage - n_lat, 0, n_ctx - 1)

    def mod_map(stage):
        def index(t):
            return (layer, jnp.where(t - stage < n_lat, lat_tile(t, stage)[0], ctx_row), 0, 0)
        return index

    def prev_map(t):
        b, i = lat_tile(t, 0)
        return (b, jnp.maximum(i * tpb - 1, 0), 0)

    def next_map(t):
        b, i = lat_tile(t, 0)
        return (b, jnp.minimum((i + 1) * tpb, rows8 - 1), 0)

    prep_in_specs, prep_out_specs = [], []
    for w, idx in next_weights:
        rows = w.shape[1] // n_lat
        prep_in_specs.append(pl.BlockSpec(
            (None, rows, w.shape[2]), lambda t, idx=idx: (idx, jnp.minimum(t, n_lat - 1), 0)))
        prep_out_specs.append(pl.BlockSpec(
            (rows, w.shape[2]), lambda t: (jnp.minimum(t, n_lat - 1), 0)))

    outs = pl.pallas_call(
        functools.partial(_layer0_kernel, seq_len, ctx_len, tm, n_lat, n_ctx, len(next_weights), layer),
        grid=(n_lat + n_ctx + 1,),
        in_specs=[
            pl.BlockSpec((None, tm, D_MODEL), lambda t: (*lat_tile(t, 0), 0)),
            pl.BlockSpec((None, POOL_HALO, D_MODEL), prev_map),
            pl.BlockSpec((None, POOL_HALO, D_MODEL), next_map),
            pl.BlockSpec((None, tm, D_MODEL), lambda t: (ctx_tile(t, 0), 0, 0)),
            pl.BlockSpec((None, None, 1, N_MOD * D_MODEL), mod_map(0)),
            pl.BlockSpec((None, None, 1, N_MOD * D_MODEL), mod_map(1)),
            _layer_rows_spec(),
            _layer_rows_spec(),
            _const_spec((1, D_MODEL)),
            _const_spec((D_MODEL, D_MODEL)),
            _const_spec((len(POOL_WINDOWS), POOL_GROUP_DIM, POOL_GROUP_DIM)),
            _const_spec((D_MODEL, D_MODEL)),
            _const_spec((D_MODEL, D_FF)),
            _const_spec((D_FF, D_MODEL)),
        ] + prep_in_specs,
        out_specs=[pl.BlockSpec((None, tm, D_MODEL), lambda t: (*lat_tile(t, 1), 0)),
                   pl.BlockSpec((None, tm, D_MODEL), lambda t: (ctx_tile(t, 1), 0, 0))] + prep_out_specs,
        out_shape=[jax.ShapeDtypeStruct(x.shape, F32), jax.ShapeDtypeStruct(ctx_tiles.shape, F32)]
        + [jax.ShapeDtypeStruct(w.shape[1:], BF16) for w, _ in next_weights],
        scratch_shapes=[pltpu.VMEM((tm + 2 * POOL_HALO, D_MODEL), F32),
                        pltpu.VMEM((tm, D_MODEL), F32),
                        pltpu.VMEM((tm, D_MODEL), F32),
                        pltpu.VMEM((tm, D_MODEL), BF16)],
        compiler_params=pltpu.CompilerParams(
            dimension_semantics=("arbitrary",), vmem_limit_bytes=VMEM_LIMIT),
        name="layer0",
    )(x, x, x, ctx_tiles, mod, mod, gmix, gmlp, pscale, w_in, w_grp, w_out, w1, w2,
      *[w for w, _ in next_weights])
    return outs[0], outs[1].reshape(ctx.shape), tuple(outs[2:])


def _store_kv(kv_ref, k, v):
    low = lax.broadcasted_iota(jnp.int32, v.shape, 1) < HEAD_DIM
    kv_ref[:, 0:LANES] = k.astype(BF16)
    kv_ref[:, LANES:2 * LANES] = jnp.where(low, v, 1.0).astype(BF16)
    kv_ref[:, 2 * LANES:3 * LANES] = jnp.where(low, 1.0, v).astype(BF16)


def _qkv_kernel(n_tiles, layer, x_ref, mod_ref, gmix_ref, w_ref, cos_ref, sin_ref, q_ref, kv_ref, h_ref):
    def body(do_norm, do_project):
        if do_project:
            h_prev = h_ref[...]
            q_groups = [_dot(h_prev, w_ref[:, c:c + QKV_COL_GROUP]) for c in range(0, Q_DIM, QKV_COL_GROUP)]
            kv = _dot(h_prev, w_ref[:, Q_DIM:])
        if do_norm:
            sh1 = mod_ref[:, 0:D_MODEL]
            sc1 = mod_ref[:, D_MODEL:2 * D_MODEL]
            h = _rms_mod(x_ref[...], gmix_ref[layer:layer + 1, :], sh1, sc1).astype(BF16)
        if do_project:
            cos = cos_ref[...]
            sin = sin_ref[...]
            qscale = HEAD_DIM ** -0.5 * LOG2E
            cos_q = cos * qscale
            sin_q = sin * qscale

            def rope(t, c, s):
                return t * c + pltpu.roll(t, HEAD_DIM, 1) * s

            for gi, group in enumerate(q_groups):
                for cb in range(QKV_COL_GROUP // LANES):
                    c0 = gi * QKV_COL_GROUP + cb * LANES
                    q_ref[:, c0:c0 + LANES] = rope(group[:, cb * LANES:(cb + 1) * LANES], cos_q, sin_q).astype(BF16)
            _store_kv(kv_ref, rope(kv[:, 0:KV_DIM], cos, sin), kv[:, KV_DIM:])
        if do_norm:
            h_ref[...] = h

    _skewed_steps(pl.program_id(0), n_tiles, body)


def _qkv(x, mod, layer, tm, gmix, w_qkv, cos, sin):
    nb, seq_len, _ = x.shape
    nt = seq_len // tm
    n_tiles = nb * nt
    tile_a, tile_b = _skew_maps(n_tiles, nt)

    def out_map(t):
        b, i = tile_b(t)
        return (b, i, 0)

    return pl.pallas_call(
        functools.partial(_qkv_kernel, n_tiles, layer),
        grid=(n_tiles + 1,),
        in_specs=[
            pl.BlockSpec((None, tm, D_MODEL), lambda t: (*tile_a(t), 0)),
            pl.BlockSpec((None, None, 1, N_MOD * D_MODEL), lambda t: (layer, tile_a(t)[0], 0, 0)),
            _layer_rows_spec(),
            _const_spec((D_MODEL, Q_DIM + 2 * KV_DIM)),
            pl.BlockSpec((tm, LANES), lambda t: (tile_b(t)[1], 0)),
            pl.BlockSpec((tm, LANES), lambda t: (tile_b(t)[1], 0)),
        ],
        out_specs=[pl.BlockSpec((None, tm, Q_DIM), out_map),
                   pl.BlockSpec((None, tm, KV_W), out_map)],
        out_shape=[jax.ShapeDtypeStruct((nb, seq_len, Q_DIM), BF16),
                   jax.ShapeDtypeStruct((nb, seq_len, KV_W), BF16)],
        scratch_shapes=[pltpu.VMEM((tm, D_MODEL), BF16)],
        compiler_params=pltpu.CompilerParams(
            dimension_semantics=("arbitrary",), vmem_limit_bytes=VMEM_LIMIT),
        name="qkv_rope",
    )(x, mod, gmix, w_qkv, cos, sin)


def _ctx_kv_kernel(layer, x_ref, mod_ref, gmix_ref, w_ref, kv_ref):
    sh1 = mod_ref[:, 0:D_MODEL]
    sc1 = mod_ref[:, D_MODEL:2 * D_MODEL]
    h = _rms_mod(x_ref[...], gmix_ref[layer:layer + 1, :], sh1, sc1).astype(BF16)
    kv = _dot(h, w_ref[...])
    _store_kv(kv_ref, kv[:, 0:KV_DIM], kv[:, KV_DIM:])


def _ctx_kv(ctx, mod, mod_row, layer, gmix, w_kv):
    nb, ctx_len, _ = ctx.shape
    return pl.pallas_call(
        functools.partial(_ctx_kv_kernel, layer),
        grid=(nb,),
        in_specs=[
            pl.BlockSpec((None, ctx_len, D_MODEL), lambda b: (b, 0, 0)),
            pl.BlockSpec((None, None, 1, N_MOD * D_MODEL), lambda b: (layer, mod_row, 0, 0)),
            _layer_rows_spec(),
            pl.BlockSpec((D_MODEL, 2 * KV_DIM), lambda b: (0, Q_DIM // (2 * KV_DIM)),
                         pipeline_mode=pl.Buffered(1)),
        ],
        out_specs=pl.BlockSpec((None, ctx_len, KV_W), lambda b: (b, 0, 0)),
        out_shape=jax.ShapeDtypeStruct((nb, ctx_len, KV_W), BF16),
        compiler_params=pltpu.CompilerParams(
            dimension_semantics=("arbitrary",), vmem_limit_bytes=VMEM_LIMIT),
        name="ctx_kv",
    )(ctx, mod, gmix, w_kv)


def _dot_nt(a, b):
    return lax.dot_general(a, b, (((1,), (1,)), ((), ())), preferred_element_type=F32)


def _attn_kernel(seq_len, tq, ctx_len, sink_ref, q_ref, kvp_ref, kvm_ref, kvn_ref, kvc_ref, o_ref,
                 kv_ext, *bufs):
    s_refs, e_refs, m_refs = bufs[0:2], bufs[2:6], bufs[6:10]
    kn_ref, minden_ref = bufs[10], bufs[11]
    i = pl.program_id(1)
    kv_ext[0:BLOCK, :] = kvp_ref[...]
    kv_ext[BLOCK:BLOCK + tq, :] = kvm_ref[...]
    kv_ext[BLOCK + tq:2 * BLOCK + tq, :] = kvn_ref[...]
    nblk = tq // BLOCK
    nwin = 3 * BLOCK
    nkey = nwin + ctx_len
    row = lax.broadcasted_iota(jnp.int32, (BLOCK, BLOCK), 0)
    col = lax.broadcasted_iota(jnp.int32, (BLOCK, BLOCK), 1)
    lane = lax.broadcasted_iota(jnp.int32, (BLOCK, LANES), 1)
    low = lane < HEAD_DIM
    head_a = (lane % HEAD_DIM) < AXIS_ROT
    my_lanes = (head_a, jnp.logical_not(head_a))
    den_lanes = (jnp.logical_not(low), low)

    chunks = [(g, t0) for g in range(GQA_GROUP) for t0 in range(0, BLOCK, SOFTMAX_ROWS)]

    def head_queries(j, kvh):
        q0 = pl.multiple_of(j * BLOCK, BLOCK)
        tiles = []
        for g in range(GQA_GROUP):
            pair = q_ref[pl.ds(q0, BLOCK), g * LANES:(g + 1) * LANES]
            tiles.append(jnp.where(my_lanes[kvh], pair, jnp.zeros_like(pair)))
        return tiles

    def window_masks(j):
        mask_l = (col >= row) & (i * tq + (j - 1) * BLOCK >= 0)
        mask_r = (col <= row) & (i * tq + (j + 1) * BLOCK < seq_len)
        return mask_l, mask_r

    def key_tiles(j, kvh):
        q0 = pl.multiple_of(j * BLOCK, BLOCK)
        return kv_ext[pl.ds(q0, nwin), 0:LANES], kvc_ref[:, 0:LANES]

    def scores(j, kvh):
        qs = jnp.concatenate(head_queries(j, kvh), axis=0)
        kw, kc = key_tiles(j, kvh)
        s_refs[kvh][:, 0:nwin] = _dot_nt(qs, kw)
        s_refs[kvh][:, nwin:nkey] = _dot_nt(qs, kc)

    def softmax(j, kvh):
        mask_l, mask_r = window_masks(j)

        def masked_pieces(rows, trows):
            pieces = [s_refs[kvh][rows, p * LANES:(p + 1) * LANES] for p in range(nkey // LANES)]
            pieces[0] = jnp.where(mask_l[trows], pieces[0], NEG)
            pieces[2] = jnp.where(mask_r[trows], pieces[2], NEG)
            return pieces

        for g, t0 in chunks:
            rows = slice(g * BLOCK + t0, g * BLOCK + t0 + SOFTMAX_ROWS)
            pieces = masked_pieces(rows, slice(t0, t0 + SOFTMAX_ROWS))
            m = pieces[0]
            for p in pieces[1:]:
                m = jnp.maximum(m, p)
            m = jnp.maximum(jnp.max(m, axis=1, keepdims=True), sink_ref[kvh * GQA_GROUP + g] * LOG2E)
            m_refs[kvh][rows, :] = jnp.broadcast_to(m, (SOFTMAX_ROWS, LANES))
        for g, t0 in chunks:
            rows = slice(g * BLOCK + t0, g * BLOCK + t0 + SOFTMAX_ROWS)
            pieces = masked_pieces(rows, slice(t0, t0 + SOFTMAX_ROWS))
            m = m_refs[kvh][rows, :]
            for p, piece in enumerate(pieces):
                e_refs[kvh][rows, p * LANES:(p + 1) * LANES] = jnp.exp2((piece - m).astype(BF16))

    def weighted_values(j, kvh, slot, fast):
        e_ref, m_ref = e_refs[2 * slot + kvh], m_refs[2 * slot + kvh]
        q0 = pl.multiple_of(j * BLOCK, BLOCK)
        vcols = slice((1 + kvh) * LANES, (2 + kvh) * LANES)
        values = jnp.concatenate([kv_ext[pl.ds(q0, nwin), vcols], kvc_ref[:, vcols]], axis=0)
        o = _dot(e_ref[...], values)
        on, smallest = [], None
        for g in range(GQA_GROUP):
            rows = slice(g * BLOCK, (g + 1) * BLOCK)
            sink = sink_ref[kvh * GQA_GROUP + g] * LOG2E
            if fast:
                sink_term = jnp.exp2(sink - m_ref[g * BLOCK:g * BLOCK + 1, :])
            else:
                sink_term = jnp.exp2(sink - m_ref[rows, :])
            total = o[rows] + sink_term
            smallest = total if smallest is None else jnp.minimum(smallest, total)
            on.append(o[rows] * pltpu.roll(1.0 / total, HEAD_DIM, 1))
        if fast:
            minden_ref[...] = jnp.minimum(minden_ref[...], jnp.where(den_lanes[kvh], smallest, 3e38))
        for pair in range(GQA_GROUP // 2):
            if kvh == 0:
                tile = jnp.where(low, on[2 * pair], pltpu.roll(on[2 * pair + 1], HEAD_DIM, 1))
            else:
                tile = jnp.where(low, pltpu.roll(on[2 * pair], HEAD_DIM, 1), on[2 * pair + 1])
            c0 = (kvh * GQA_GROUP + 2 * pair) * HEAD_DIM
            o_ref[pl.ds(q0, BLOCK), c0:c0 + LANES] = tile.astype(BF16)

    def exact_body(j, carry):
        scores(j, 0)
        scores(j, 1)
        softmax(j, 0)
        weighted_values(j, 0, 0, False)
        softmax(j, 1)
        weighted_values(j, 1, 0, False)
        return carry

    def max_sq_norms(k_tile):
        sq = k_tile.astype(F32)
        sq = sq * sq
        return [jnp.broadcast_to(jnp.max(jnp.sum(jnp.where(my_lanes[kvh], sq, 0.0), axis=1, keepdims=True),
                                         axis=0, keepdims=True), (8, LANES)) for kvh in range(N_KV_HEADS)]

    for kb in range(nblk + 2):
        for kvh, n2 in enumerate(max_sq_norms(kv_ext[kb * BLOCK:(kb + 1) * BLOCK, 0:LANES])):
            kn_ref[kvh, kb] = n2
    ctx_norm = None
    for cb in range(ctx_len // BLOCK):
        n2 = max_sq_norms(kvc_ref[cb * BLOCK:(cb + 1) * BLOCK, 0:LANES])
        ctx_norm = n2 if ctx_norm is None else [jnp.maximum(a, b) for a, b in zip(ctx_norm, n2)]
    minden_ref[...] = jnp.full(minden_ref.shape, 3e38, F32)

    def bounded_exp(j, kvh, slot):
        e_ref, m_ref = e_refs[2 * slot + kvh], m_refs[2 * slot + kvh]
        q0 = pl.multiple_of(j * BLOCK, BLOCK)
        kw, kc = key_tiles(j, kvh)
        kn2 = jnp.maximum(jnp.max(kn_ref[kvh, pl.ds(j, 3)], axis=0), ctx_norm[kvh])[0:1, :]
        tiles = head_queries(j, kvh)
        for g in range(GQA_GROUP):
            qf = tiles[g].astype(F32)
            qn2 = jnp.max(jnp.sum(qf * qf, axis=1, keepdims=True), axis=0, keepdims=True)
            bound = jnp.maximum(jnp.sqrt(qn2 * kn2), sink_ref[kvh * GQA_GROUP + g] * LOG2E)
            m_ref[g * BLOCK:(g + 1) * BLOCK, :] = jnp.broadcast_to(bound, (BLOCK, LANES))
        qs = jnp.concatenate(tiles, axis=0)
        s_w = _dot_nt(qs, kw)
        s_c = _dot_nt(qs, kc)
        mask_l, mask_r = window_masks(j)
        for g, t0 in chunks:
            rows = slice(g * BLOCK + t0, g * BLOCK + t0 + SOFTMAX_ROWS)
            trows = slice(t0, t0 + SOFTMAX_ROWS)
            m = m_ref[rows, :]
            pieces = ([s_w[rows, p * LANES:(p + 1) * LANES] for p in range(nwin // LANES)]
                      + [s_c[rows, p * LANES:(p + 1) * LANES] for p in range(ctx_len // LANES)])
            pieces[0] = jnp.where(mask_l[trows], pieces[0], NEG)
            pieces[2] = jnp.where(mask_r[trows], pieces[2], NEG)
            for p, piece in enumerate(pieces):
                e_ref[rows, p * LANES:(p + 1) * LANES] = jnp.exp2(piece - m).astype(BF16)

    bounded_exp(0, 0, 0)
    bounded_exp(0, 1, 0)

    def fast_blocks(j0, last):
        for k in range(FAST_UNROLL):
            for kvh in range(N_KV_HEADS):
                weighted_values(j0 + k, kvh, k % 2, True)
                if not (last and k + 1 == FAST_UNROLL):
                    bounded_exp(j0 + k + 1, kvh, (k + 1) % 2)

    def fast_body(jj, carry):
        fast_blocks(FAST_UNROLL * jj, False)
        return carry

    lax.fori_loop(0, nblk // FAST_UNROLL - 1, fast_body, 0)
    fast_blocks(nblk - FAST_UNROLL, True)
    smallest = jnp.min(minden_ref[...])

    @pl.when(smallest < 2.0 ** MIN_SAFE_DENOMINATOR_LOG2)
    def _():
        lax.fori_loop(0, nblk, exact_body, 0)


def _attention(q, kv, kvc, sink, tq):
    nb, seq_len, _ = q.shape
    ctx_len = kvc.shape[1]
    nblk = seq_len // BLOCK
    qpb = tq // BLOCK
    rows = GQA_GROUP * BLOCK
    nkey = 3 * BLOCK + ctx_len
    return pl.pallas_call(
        functools.partial(_attn_kernel, seq_len, tq, ctx_len),
        grid=(nb, seq_len // tq),
        in_specs=[
            pl.BlockSpec(memory_space=pltpu.SMEM),
            pl.BlockSpec((None, tq, Q_DIM), lambda b, i: (b, i, 0)),
            pl.BlockSpec((None, BLOCK, KV_W), lambda b, i: (b, jnp.maximum(i * qpb - 1, 0), 0)),
            pl.BlockSpec((None, tq, KV_W), lambda b, i: (b, i, 0)),
            pl.BlockSpec((None, BLOCK, KV_W),
                         lambda b, i: (b, jnp.minimum((i + 1) * qpb, nblk - 1), 0)),
            pl.BlockSpec((None, ctx_len, KV_W), lambda b, i: (b, 0, 0)),
        ],
        out_specs=pl.BlockSpec((None, tq, Q_DIM), lambda b, i: (b, i, 0)),
        out_shape=jax.ShapeDtypeStruct((nb, seq_len, Q_DIM), BF16),
        scratch_shapes=([pltpu.VMEM((tq + 2 * BLOCK, KV_W), BF16)]
                        + [pltpu.VMEM((rows, nkey), F32)] * N_KV_HEADS
                        + [pltpu.VMEM((rows, nkey), BF16)] * (2 * N_KV_HEADS)
                        + [pltpu.VMEM((rows, LANES), F32)] * (2 * N_KV_HEADS)
                        + [pltpu.VMEM((N_KV_HEADS, qpb + 2, 8, LANES), F32),
                           pltpu.VMEM((BLOCK, LANES), F32)]),
        compiler_params=pltpu.CompilerParams(
            dimension_semantics=("arbitrary", "arbitrary"), vmem_limit_bytes=VMEM_LIMIT),
        name="attention",
    )(sink, q, kv, kv, kv, kvc)


def _post_kernel(layer, x_ref, o_ref_in, mod_ref, gmlp_ref, gfin_ref, wo_ref, w1_ref, w2_ref, out_ref):
    g1 = mod_ref[:, 2 * D_MODEL:3 * D_MODEL]
    x1 = x_ref[...] + g1 * _dot(o_ref_in[...], wo_ref[...])
    x2 = _mlp_residual(x1, mod_ref, gmlp_ref[layer:layer + 1, :], w1_ref, w2_ref)
    y = x2 * lax.rsqrt(jnp.mean(x2 * x2, axis=-1, keepdims=True) + EPS)
    out_ref[...] = y * gfin_ref[...]


def _post(x, o, mod, layer, tm, gmlp, gfin, w_o, w1, w2):
    nb, seq_len, _ = x.shape
    return pl.pallas_call(
        functools.partial(_post_kernel, layer),
        grid=(nb, seq_len // tm),
        in_specs=[
            pl.BlockSpec((None, tm, D_MODEL), lambda b, i: (b, i, 0)),
            pl.BlockSpec((None, tm, Q_DIM), lambda b, i: (b, i, 0)),
            pl.BlockSpec((None, None, 1, N_MOD * D_MODEL), lambda b, i: (layer, b, 0, 0)),
            _layer_rows_spec(),
            _const_spec((1, D_MODEL)),
            _const_spec((Q_DIM, D_MODEL)),
            _const_spec((D_MODEL, D_FF)),
            _const_spec((D_FF, D_MODEL)),
        ],
        out_specs=pl.BlockSpec((None, tm, D_MODEL), lambda b, i: (b, i, 0)),
        out_shape=jax.ShapeDtypeStruct(x.shape, F32),
        compiler_params=pltpu.CompilerParams(
            dimension_semantics=("arbitrary", "arbitrary"), vmem_limit_bytes=VMEM_LIMIT),
        name="post_attn",
    )(x, o, mod, gmlp, gfin, w_o, w1, w2)


def _rope_tables(seq_len):
    pos = np.arange(seq_len)
    inv = ROPE_BASE ** (-np.arange(0, AXIS_ROT, 2, dtype=np.float32) / AXIS_ROT)
    ang = np.concatenate([(pos // GRID_W).astype(np.float32)[:, None] * inv[None],
                          (pos % GRID_W).astype(np.float32)[:, None] * inv[None]], axis=-1)
    cos = np.tile(np.cos(ang.astype(np.float64)), (1, LANES // AXIS_ROT))
    sin = np.sin(ang.astype(np.float64))
    return (jnp.asarray(cos, F32), jnp.asarray(np.concatenate([-sin, -sin, sin, sin], axis=-1), F32))


def kernel(x, c, ctx, c_ctx, ada_w, ada_b, norm_mix_g, norm_mlp_g, pool_w_in, pool_w_grp,
           pool_scale, pool_w_out, attn_w_qkv, attn_sink, attn_w_o, mlp_w1, mlp_w2, final_g):
    nb, seq_len, _ = x.shape
    ctx_len = ctx.shape[1]
    ctx_row = nb
    c8 = jnp.concatenate(
        [c, c_ctx[None], jnp.zeros((MOD_ROWS - nb - 1, D_MODEL), F32)], axis=0)
    n_grp = len(POOL_WINDOWS)
    layer0_weights = ((pool_w_in, 0), (pool_w_grp.reshape(-1, n_grp * POOL_GROUP_DIM, POOL_GROUP_DIM), 0),
                      (pool_w_out, 0), (mlp_w1, 0), (mlp_w2, 0))
    mod, (w_in, w_grp, w_out, w1_0, w2_0) = _modulation(c8, ada_w, ada_b, layer0_weights)

    gmix, gmlp = norm_mix_g, norm_mlp_g

    l0 = (gmix, gmlp, pool_scale, w_in,
          w_grp.reshape(n_grp, POOL_GROUP_DIM, POOL_GROUP_DIM), w_out, w1_0, w2_0)
    layer1_weights = ((attn_w_qkv, 0), (attn_w_o, 0), (mlp_w1, 1), (mlp_w2, 1))
    x1, ctx1, (w_qkv, w_o, w1, w2) = _layer0(x, ctx, mod, ctx_row, 0, 512, layer1_weights, *l0)

    cos, sin = _rope_tables(seq_len)
    q, kv = _qkv(x1, mod, 1, 1024, gmix, w_qkv, cos, sin)
    kvc = _ctx_kv(ctx1, mod, ctx_row, 1, gmix, w_qkv)
    o = _attention(q, kv, kvc, attn_sink[0], 2048)
    return _post(x1, o, mod, 1, 512, gmlp, final_g.reshape(1, D_MODEL), w_o, w1, w2)
```

```python
import functools

import numpy as np
import jax
import jax.numpy as jnp
from jax import lax
from jax.experimental import pallas as pl
from jax.experimental.pallas import tpu as pltpu

D_MODEL = 1024
DEPTH = 2
GRID_W = 64
POOL_WINDOWS = (2, 4, 8, 16)
POOL_GROUP_DIM = D_MODEL // len(POOL_WINDOWS)
POOL_HALO = 8
HEAD_DIM = 64
N_HEADS = D_MODEL // HEAD_DIM
N_KV_HEADS = 2
GQA_GROUP = N_HEADS // N_KV_HEADS
Q_DIM = N_HEADS * HEAD_DIM
KV_DIM = N_KV_HEADS * HEAD_DIM
WINDOW = 128
BLOCK = 128
ROPE_BASE = 10000.0
AXIS_ROT = HEAD_DIM // 2
ROT_HALF = AXIS_ROT // 2
D_FF = 4 * D_MODEL
N_MOD = 6
EPS = 1e-6
NEG = -1e30
MOD_ROWS = 8
LANES = 128
LOG2E = 1.4426950408889634
KV_W = 3 * LANES
SOFTMAX_ROWS = 32
MIN_SAFE_DENOMINATOR_LOG2 = -64.0
QKV_COL_GROUP = 256
FAST_UNROLL = 2
VMEM_LIMIT = 56 * 1024 * 1024

BF16 = jnp.bfloat16
F32 = jnp.float32


def _const_spec(shape):
    nd = len(shape)
    return pl.BlockSpec(shape, lambda *_: (0,) * nd, pipeline_mode=pl.Buffered(1))


def _layer_rows_spec():
    return _const_spec((DEPTH, D_MODEL))


def _dot(a, b):
    return jnp.dot(a, b, preferred_element_type=F32)


def _rms_mod(x, g, shift, scale):
    y = x * lax.rsqrt(jnp.mean(x * x, axis=-1, keepdims=True) + EPS)
    return y * (g * (1.0 + scale)) + shift


def _mlp_residual(x, mod_ref, gain, w1_ref, w2_ref):
    sh2 = mod_ref[:, 3 * D_MODEL:4 * D_MODEL]
    sc2 = mod_ref[:, 4 * D_MODEL:5 * D_MODEL]
    g2 = mod_ref[:, 5 * D_MODEL:6 * D_MODEL]
    h = _rms_mod(x, gain, sh2, sc2).astype(BF16)
    a = jnp.maximum(_dot(h, w1_ref[...]), 0.0)
    a = (a * a).astype(BF16)
    return x + g2 * _dot(a, w2_ref[...])


def _staged_steps(step, n_tiles, n_stages, body):
    assert n_tiles >= n_stages
    for t in range(n_stages - 1):
        pl.when(step == t)(functools.partial(body, *[s <= t for s in range(n_stages)]))
    pl.when((step >= n_stages - 1) & (step < n_tiles))(functools.partial(body, *[True] * n_stages))
    for k in range(n_stages - 1):
        pl.when(step == n_tiles + k)(functools.partial(body, *[s > k for s in range(n_stages)]))


def _skewed_steps(step, n_tiles, body):
    _staged_steps(step, n_tiles, 2, body)


def _stage_tile(n_tiles, tiles_per_row, stage):
    def tile(t):
        k = jnp.clip(t - stage, 0, n_tiles - 1)
        return k // tiles_per_row, k % tiles_per_row
    return tile


def _skew_maps(n_tiles, tiles_per_row):
    return _stage_tile(n_tiles, tiles_per_row, 0), _stage_tile(n_tiles, tiles_per_row, 1)


def _mod_kernel(n_prep, c_ref, w_ref, b_ref, *refs):
    prep_in, o_ref, prep_out = refs[:n_prep], refs[n_prep], refs[n_prep + 1:]
    c = c_ref[...]
    s = (c * jax.nn.sigmoid(c)).astype(BF16)
    r = _dot(s, w_ref[...].astype(BF16)) + b_ref[...]
    for row in range(MOD_ROWS):
        o_ref[row] = r[row:row + 1, :]
    for src, dst in zip(prep_in, prep_out):
        dst[...] = src[...].astype(BF16)


def _modulation(c8, ada_w, ada_b, first_layer_weights):
    tn = 1536
    n = N_MOD * D_MODEL
    n_col = n // tn
    steps = DEPTH * n_col
    prep_in_specs, prep_out_specs = [], []
    for w, idx in first_layer_weights:
        rows = w.shape[1] // steps
        prep_in_specs.append(pl.BlockSpec(
            (None, rows, w.shape[2]), lambda i, j, idx=idx: (idx, i * n_col + j, 0)))
        prep_out_specs.append(pl.BlockSpec((rows, w.shape[2]), lambda i, j: (i * n_col + j, 0)))
    outs = pl.pallas_call(
        functools.partial(_mod_kernel, len(first_layer_weights)),
        grid=(DEPTH, n_col),
        in_specs=[
            pl.BlockSpec((MOD_ROWS, D_MODEL), lambda i, j: (0, 0)),
            pl.BlockSpec((None, D_MODEL, tn), lambda i, j: (i, 0, j)),
            pl.BlockSpec((None, 1, tn), lambda i, j: (i, 0, j)),
        ] + prep_in_specs,
        out_specs=[pl.BlockSpec((None, MOD_ROWS, 1, tn), lambda i, j: (i, 0, 0, j))] + prep_out_specs,
        out_shape=[jax.ShapeDtypeStruct((DEPTH, MOD_ROWS, 1, n), F32)]
        + [jax.ShapeDtypeStruct(w.shape[1:], BF16) for w, _ in first_layer_weights],
        compiler_params=pltpu.CompilerParams(
            dimension_semantics=("arbitrary", "arbitrary"), vmem_limit_bytes=VMEM_LIMIT),
        name="modulation",
    )(c8, ada_w, ada_b.reshape(DEPTH, 1, n), *[w for w, _ in first_layer_weights])
    return outs[0], tuple(outs[1:])


def _to_pair_layout(v):
    lane = lax.broadcasted_iota(jnp.int32, v.shape, 1)
    for width in (ROT_HALF, AXIS_ROT):
        hi = (lane // (2 * width)) % 2
        lo = (lane // width) % 2
        v = jnp.where(hi == lo, v, jnp.where(hi == 0, pltpu.roll(v, LANES - width, 1),
                                             pltpu.roll(v, width, 1)))
    return v


def _prepare_next_weights(prep_in, prep_out):
    wqkv_in, wqkv_out = prep_in[0], prep_out[0]
    low = lax.broadcasted_iota(jnp.int32, (wqkv_in.shape[0], LANES), 1) < HEAD_DIM
    for p in range(GQA_GROUP):
        first = wqkv_in[:, (p // 2) * LANES:(p // 2 + 1) * LANES]
        second = wqkv_in[:, (p // 2 + GQA_GROUP // 2) * LANES:(p // 2 + GQA_GROUP // 2 + 1) * LANES]
        if p % 2 == 0:
            both = jnp.where(low, first, pltpu.roll(second, HEAD_DIM, 1))
        else:
            both = jnp.where(low, pltpu.roll(first, HEAD_DIM, 1), second)
        wqkv_out[:, p * LANES:(p + 1) * LANES] = _to_pair_layout(both).astype(BF16)
    wqkv_out[:, Q_DIM:Q_DIM + KV_DIM] = _to_pair_layout(wqkv_in[:, Q_DIM:Q_DIM + KV_DIM]).astype(BF16)
    wqkv_out[:, Q_DIM + KV_DIM:] = wqkv_in[:, Q_DIM + KV_DIM:].astype(BF16)
    for src, dst in zip(prep_in[1:], prep_out[1:]):
        dst[...] = src[...].astype(BF16)


def _layer0_kernel(seq_len, tm, n_tiles, skew, n_prep, layer, *refs):
    (x_ref, xp_ref, xn_ref, moda_ref, modb_ref, gmix_ref, gmlp_ref, pscale_ref, win_ref, wgrp_ref,
     wout_ref, w1_ref, w2_ref) = refs[:13]
    prep_in = refs[13:13 + n_prep]
    o_ref = refs[13 + n_prep]
    prep_out = refs[14 + n_prep:14 + 2 * n_prep]
    u_ref, d_ref, x1_ref, h2_ref = refs[14 + 2 * n_prep:]
    step = pl.program_id(0)
    i = jnp.minimum(step, n_tiles - 1) % (seq_len // tm)
    half = D_FF // 2

    def mlp_up(h2, c0, c1):
        a = jnp.maximum(_dot(h2, w1_ref[:, c0:c1]), 0.0)
        return (a * a).astype(BF16)

    def body(do_mixer, do_prev_mlp):
        if do_prev_mlp:
            g2 = modb_ref[:, 5 * D_MODEL:6 * D_MODEL]
            h2_prev = h2_ref[...]
            a_lo = mlp_up(h2_prev, 0, half)

        if do_mixer:
            sh1 = moda_ref[:, 0:D_MODEL]
            sc1 = moda_ref[:, D_MODEL:2 * D_MODEL]
            x = x_ref[...]
            xe = jnp.concatenate([xp_ref[...], x, xn_ref[...]], axis=0)
            u = _dot(_rms_mod(xe, gmix_ref[layer:layer + 1, :], sh1, sc1).astype(BF16), win_ref[...])
            u_ref[0:POOL_HALO, :] = jnp.where(i > 0, u[0:POOL_HALO], 0.0)
            u_ref[POOL_HALO:POOL_HALO + tm, :] = u[POOL_HALO:POOL_HALO + tm]
            u_ref[POOL_HALO + tm:POOL_HALO + tm + POOL_HALO, :] = jnp.where(
                (i + 1) * tm < seq_len, u[POOL_HALO + tm:], 0.0)

        if do_prev_mlp:
            a_hi = mlp_up(h2_prev, half, D_FF)

        if do_mixer:
            t = i * tm + lax.broadcasted_iota(jnp.int32, (tm, 1), 0)
            for g, w in enumerate(POOL_WINDOWS):
                c0, c1 = g * POOL_GROUP_DIM, (g + 1) * POOL_GROUP_DIM
                acc = u_ref[POOL_HALO - w // 2:POOL_HALO - w // 2 + tm, c0:c1]
                for s in range(-w // 2 + 1, w // 2):
                    acc = acc + u_ref[POOL_HALO + s:POOL_HALO + s + tm, c0:c1]
                lo = jnp.clip(t - w // 2, 0, seq_len)
                hi = jnp.clip(t + w // 2, 0, seq_len)
                inv_cnt = 1.0 / (hi - lo).astype(F32)
                d = acc * inv_cnt - u_ref[POOL_HALO:POOL_HALO + tm, c0:c1]
                d_ref[:, c0:c1] = _dot(d.astype(BF16), wgrp_ref[g])
            y = _dot((d_ref[...] * pscale_ref[...]).astype(BF16), wout_ref[...])

        if do_prev_mlp:
            down = _dot(jnp.concatenate([a_lo, a_hi], axis=1), w2_ref[...])

        if do_mixer:
            g1 = moda_ref[:, 2 * D_MODEL:3 * D_MODEL]
            sh2 = moda_ref[:, 3 * D_MODEL:4 * D_MODEL]
            sc2 = moda_ref[:, 4 * D_MODEL:5 * D_MODEL]
            x1 = x + g1 * y
            h2 = _rms_mod(x1, gmlp_ref[layer:layer + 1, :], sh2, sc2).astype(BF16)

        if do_prev_mlp:
            o_ref[...] = x1_ref[...] + g2 * down
        if do_mixer and n_prep:
            _prepare_next_weights(prep_in, prep_out)
        if do_mixer and skew:
            x1_ref[...] = x1
            h2_ref[...] = h2
        if do_mixer and not skew:
            g2_now = moda_ref[:, 5 * D_MODEL:6 * D_MODEL]
            o_ref[...] = x1 + g2_now * _dot(mlp_up(h2, 0, D_FF), w2_ref[...])

    if skew:
        @pl.when(step == 0)
        def _():
            x1_ref[...] = jnp.zeros(x1_ref.shape, F32)
            h2_ref[...] = jnp.zeros(h2_ref.shape, BF16)

        body(True, True)
    else:
        body(True, False)


def _layer0(x, mod, mod_row, layer, tm, skew, next_weights, gmix, gmlp, pscale, w_in, w_grp, w_out, w1, w2):
    nb, seq_len, _ = x.shape
    rows8 = seq_len // POOL_HALO
    tpb = tm // POOL_HALO
    nt = seq_len // tm
    n_tiles = nb * nt
    tile_a, tile_b = _skew_maps(n_tiles, nt)
    if not skew:
        tile_b = tile_a

    def x_map(t):
        b, i = tile_a(t)
        return (b, i, 0)

    def prev_map(t):
        b, i = tile_a(t)
        return (b, jnp.maximum(i * tpb - 1, 0), 0)

    def next_map(t):
        b, i = tile_a(t)
        return (b, jnp.minimum((i + 1) * tpb, rows8 - 1), 0)

    def moda_map(t):
        return (layer, tile_a(t)[0] if mod_row is None else mod_row, 0, 0)

    def modb_map(t):
        return (layer, tile_b(t)[0] if mod_row is None else mod_row, 0, 0)

    def out_map(t):
        b, i = tile_b(t)
        return (b, i, 0)

    prep_in_specs, prep_out_specs = [], []
    for w, idx in next_weights:
        rows = w.shape[1] // n_tiles
        prep_in_specs.append(pl.BlockSpec(
            (None, rows, w.shape[2]), lambda t, idx=idx: (idx, jnp.minimum(t, n_tiles - 1), 0)))
        prep_out_specs.append(pl.BlockSpec(
            (rows, w.shape[2]), lambda t: (jnp.minimum(t, n_tiles - 1), 0)))

    outs = pl.pallas_call(
        functools.partial(_layer0_kernel, seq_len, tm, n_tiles, skew, len(next_weights), layer),
        grid=(n_tiles + (1 if skew else 0),),
        in_specs=[
            pl.BlockSpec((None, tm, D_MODEL), x_map),
            pl.BlockSpec((None, POOL_HALO, D_MODEL), prev_map),
            pl.BlockSpec((None, POOL_HALO, D_MODEL), next_map),
            pl.BlockSpec((None, None, 1, N_MOD * D_MODEL), moda_map),
            pl.BlockSpec((None, None, 1, N_MOD * D_MODEL), modb_map),
            _layer_rows_spec(),
            _layer_rows_spec(),
            _const_spec((1, D_MODEL)),
            _const_spec((D_MODEL, D_MODEL)),
            _const_spec((len(POOL_WINDOWS), POOL_GROUP_DIM, POOL_GROUP_DIM)),
            _const_spec((D_MODEL, D_MODEL)),
            _const_spec((D_MODEL, D_FF)),
            _const_spec((D_FF, D_MODEL)),
        ] + prep_in_specs,
        out_specs=[pl.BlockSpec((None, tm, D_MODEL), out_map)] + prep_out_specs,
        out_shape=[jax.ShapeDtypeStruct(x.shape, F32)]
        + [jax.ShapeDtypeStruct(w.shape[1:], BF16) for w, _ in next_weights],
        scratch_shapes=[pltpu.VMEM((tm + 2 * POOL_HALO, D_MODEL), F32),
                        pltpu.VMEM((tm, D_MODEL), F32),
                        pltpu.VMEM((tm, D_MODEL), F32),
                        pltpu.VMEM((tm, D_MODEL), BF16)],
        compiler_params=pltpu.CompilerParams(
            dimension_semantics=("arbitrary",), vmem_limit_bytes=VMEM_LIMIT),
        name="layer0",
    )(x, x, x, mod, mod, gmix, gmlp, pscale, w_in, w_grp, w_out, w1, w2, *[w for w, _ in next_weights])
    return outs[0], tuple(outs[1:])


def _store_kv(kv_ref, k, v):
    low = lax.broadcasted_iota(jnp.int32, v.shape, 1) < HEAD_DIM
    kv_ref[:, 0:LANES] = k.astype(BF16)
    kv_ref[:, LANES:2 * LANES] = jnp.where(low, v, 1.0).astype(BF16)
    kv_ref[:, 2 * LANES:3 * LANES] = jnp.where(low, 1.0, v).astype(BF16)


def _qkv_kernel(n_tiles, layer, x_ref, mod_ref, gmix_ref, w_ref, cos_ref, sin_ref, q_ref, kv_ref, h_ref):
    def body(do_norm, do_project):
        if do_project:
            h_prev = h_ref[...]
            q_groups = [_dot(h_prev, w_ref[:, c:c + QKV_COL_GROUP]) for c in range(0, Q_DIM, QKV_COL_GROUP)]
            kv = _dot(h_prev, w_ref[:, Q_DIM:])
        if do_norm:
            sh1 = mod_ref[:, 0:D_MODEL]
            sc1 = mod_ref[:, D_MODEL:2 * D_MODEL]
            h = _rms_mod(x_ref[...], gmix_ref[layer:layer + 1, :], sh1, sc1).astype(BF16)
        if do_project:
            cos = cos_ref[...]
            sin = sin_ref[...]
            qscale = HEAD_DIM ** -0.5 * LOG2E
            cos_q = cos * qscale
            sin_q = sin * qscale

            def rope(t, c, s):
                return t * c + pltpu.roll(t, HEAD_DIM, 1) * s

            for gi, group in enumerate(q_groups):
                for cb in range(QKV_COL_GROUP // LANES):
                    c0 = gi * QKV_COL_GROUP + cb * LANES
                    q_ref[:, c0:c0 + LANES] = rope(group[:, cb * LANES:(cb + 1) * LANES], cos_q, sin_q).astype(BF16)
            _store_kv(kv_ref, rope(kv[:, 0:KV_DIM], cos, sin), kv[:, KV_DIM:])
        if do_norm:
            h_ref[...] = h

    _skewed_steps(pl.program_id(0), n_tiles, body)


def _qkv(x, mod, layer, tm, gmix, w_qkv, cos, sin):
    nb, seq_len, _ = x.shape
    nt = seq_len // tm
    n_tiles = nb * nt
    tile_a, tile_b = _skew_maps(n_tiles, nt)

    def out_map(t):
        b, i = tile_b(t)
        return (b, i, 0)

    return pl.pallas_call(
        functools.partial(_qkv_kernel, n_tiles, layer),
        grid=(n_tiles + 1,),
        in_specs=[
            pl.BlockSpec((None, tm, D_MODEL), lambda t: (*tile_a(t), 0)),
            pl.BlockSpec((None, None, 1, N_MOD * D_MODEL), lambda t: (layer, tile_a(t)[0], 0, 0)),
            _layer_rows_spec(),
            _const_spec((D_MODEL, Q_DIM + 2 * KV_DIM)),
            pl.BlockSpec((tm, LANES), lambda t: (tile_b(t)[1], 0)),
            pl.BlockSpec((tm, LANES), lambda t: (tile_b(t)[1], 0)),
        ],
        out_specs=[pl.BlockSpec((None, tm, Q_DIM), out_map),
                   pl.BlockSpec((None, tm, KV_W), out_map)],
        out_shape=[jax.ShapeDtypeStruct((nb, seq_len, Q_DIM), BF16),
                   jax.ShapeDtypeStruct((nb, seq_len, KV_W), BF16)],
        scratch_shapes=[pltpu.VMEM((tm, D_MODEL), BF16)],
        compiler_params=pltpu.CompilerParams(
            dimension_semantics=("arbitrary",), vmem_limit_bytes=VMEM_LIMIT),
        name="qkv_rope",
    )(x, mod, gmix, w_qkv, cos, sin)


def _ctx_kv_kernel(layer, x_ref, mod_ref, gmix_ref, w_ref, kv_ref):
    sh1 = mod_ref[:, 0:D_MODEL]
    sc1 = mod_ref[:, D_MODEL:2 * D_MODEL]
    h = _rms_mod(x_ref[...], gmix_ref[layer:layer + 1, :], sh1, sc1).astype(BF16)
    kv = _dot(h, w_ref[...])
    _store_kv(kv_ref, kv[:, 0:KV_DIM], kv[:, KV_DIM:])


def _ctx_kv(ctx, mod, mod_row, layer, gmix, w_kv):
    nb, ctx_len, _ = ctx.shape
    return pl.pallas_call(
        functools.partial(_ctx_kv_kernel, layer),
        grid=(nb,),
        in_specs=[
            pl.BlockSpec((None, ctx_len, D_MODEL), lambda b: (b, 0, 0)),
            pl.BlockSpec((None, None, 1, N_MOD * D_MODEL), lambda b: (layer, mod_row, 0, 0)),
            _layer_rows_spec(),
            pl.BlockSpec((D_MODEL, 2 * KV_DIM), lambda b: (0, Q_DIM // (2 * KV_DIM)),
                         pipeline_mode=pl.Buffered(1)),
        ],
        out_specs=pl.BlockSpec((None, ctx_len, KV_W), lambda b: (b, 0, 0)),
        out_shape=jax.ShapeDtypeStruct((nb, ctx_len, KV_W), BF16),
        compiler_params=pltpu.CompilerParams(
            dimension_semantics=("arbitrary",), vmem_limit_bytes=VMEM_LIMIT),
        name="ctx_kv",
    )(ctx, mod, gmix, w_kv)


def _dot_nt(a, b):
    return lax.dot_general(a, b, (((1,), (1,)), ((), ())), preferred_element_type=F32)


def _attn_kernel(seq_len, tq, ctx_len, sink_ref, q_ref, kvp_ref, kvm_ref, kvn_ref, kvc_ref, o_ref,
                 kv_ext, *bufs):
    s_refs, e_refs, m_refs = bufs[0:2], bufs[2:6], bufs[6:10]
    kn_ref, minden_ref = bufs[10], bufs[11]
    i = pl.program_id(1)
    kv_ext[0:BLOCK, :] = kvp_ref[...]
    kv_ext[BLOCK:BLOCK + tq, :] = kvm_ref[...]
    kv_ext[BLOCK + tq:2 * BLOCK + tq, :] = kvn_ref[...]
    nblk = tq // BLOCK
    nwin = 3 * BLOCK
    nkey = nwin + ctx_len
    row = lax.broadcasted_iota(jnp.int32, (BLOCK, BLOCK), 0)
    col = lax.broadcasted_iota(jnp.int32, (BLOCK, BLOCK), 1)
    lane = lax.broadcasted_iota(jnp.int32, (BLOCK, LANES), 1)
    low = lane < HEAD_DIM
    head_a = (lane % HEAD_DIM) < AXIS_ROT
    my_lanes = (head_a, jnp.logical_not(head_a))
    den_lanes = (jnp.logical_not(low), low)

    chunks = [(g, t0) for g in range(GQA_GROUP) for t0 in range(0, BLOCK, SOFTMAX_ROWS)]

    def head_queries(j, kvh):
        q0 = pl.multiple_of(j * BLOCK, BLOCK)
        tiles = []
        for g in range(GQA_GROUP):
            pair = q_ref[pl.ds(q0, BLOCK), g * LANES:(g + 1) * LANES]
            tiles.append(jnp.where(my_lanes[kvh], pair, jnp.zeros_like(pair)))
        return tiles

    def window_masks(j):
        mask_l = (col >= row) & (i * tq + (j - 1) * BLOCK >= 0)
        mask_r = (col <= row) & (i * tq + (j + 1) * BLOCK < seq_len)
        return mask_l, mask_r

    def key_tiles(j, kvh):
        q0 = pl.multiple_of(j * BLOCK, BLOCK)
        return kv_ext[pl.ds(q0, nwin), 0:LANES], kvc_ref[:, 0:LANES]

    def scores(j, kvh):
        qs = jnp.concatenate(head_queries(j, kvh), axis=0)
        kw, kc = key_tiles(j, kvh)
        s_refs[kvh][:, 0:nwin] = _dot_nt(qs, kw)
        s_refs[kvh][:, nwin:nkey] = _dot_nt(qs, kc)

    def softmax(j, kvh):
        mask_l, mask_r = window_masks(j)

        def masked_pieces(rows, trows):
            pieces = [s_refs[kvh][rows, p * LANES:(p + 1) * LANES] for p in range(nkey // LANES)]
            pieces[0] = jnp.where(mask_l[trows], pieces[0], NEG)
            pieces[2] = jnp.where(mask_r[trows], pieces[2], NEG)
            return pieces

        for g, t0 in chunks:
            rows = slice(g * BLOCK + t0, g * BLOCK + t0 + SOFTMAX_ROWS)
            pieces = masked_pieces(rows, slice(t0, t0 + SOFTMAX_ROWS))
            m = pieces[0]
            for p in pieces[1:]:
                m = jnp.maximum(m, p)
            m = jnp.maximum(jnp.max(m, axis=1, keepdims=True), sink_ref[kvh * GQA_GROUP + g] * LOG2E)
            m_refs[kvh][rows, :] = jnp.broadcast_to(m, (SOFTMAX_ROWS, LANES))
        for g, t0 in chunks:
            rows = slice(g * BLOCK + t0, g * BLOCK + t0 + SOFTMAX_ROWS)
            pieces = masked_pieces(rows, slice(t0, t0 + SOFTMAX_ROWS))
            m = m_refs[kvh][rows, :]
            for p, piece in enumerate(pieces):
                e_refs[kvh][rows, p * LANES:(p + 1) * LANES] = jnp.exp2((piece - m).astype(BF16))

    def weighted_values(j, kvh, slot, fast):
        e_ref, m_ref = e_refs[2 * slot + kvh], m_refs[2 * slot + kvh]
        q0 = pl.multiple_of(j * BLOCK, BLOCK)
        vcols = slice((1 + kvh) * LANES, (2 + kvh) * LANES)
        values = jnp.concatenate([kv_ext[pl.ds(q0, nwin), vcols], kvc_ref[:, vcols]], axis=0)
        o = _dot(e_ref[...], values)
        on, smallest = [], None
        for g in range(GQA_GROUP):
            rows = slice(g * BLOCK, (g + 1) * BLOCK)
            sink = sink_ref[kvh * GQA_GROUP + g] * LOG2E
            if fast:
                sink_term = jnp.exp2(sink - m_ref[g * BLOCK:g * BLOCK + 1, :])
            else:
                sink_term = jnp.exp2(sink - m_ref[rows, :])
            total = o[rows] + sink_term
            smallest = total if smallest is None else jnp.minimum(smallest, total)
            on.append(o[rows] * pltpu.roll(1.0 / total, HEAD_DIM, 1))
        if fast:
            minden_ref[...] = jnp.minimum(minden_ref[...], jnp.where(den_lanes[kvh], smallest, 3e38))
        for pair in range(GQA_GROUP // 2):
            if kvh == 0:
                tile = jnp.where(low, on[2 * pair], pltpu.roll(on[2 * pair + 1], HEAD_DIM, 1))
            else:
                tile = jnp.where(low, pltpu.roll(on[2 * pair], HEAD_DIM, 1), on[2 * pair + 1])
            c0 = (kvh * GQA_GROUP + 2 * pair) * HEAD_DIM
            o_ref[pl.ds(q0, BLOCK), c0:c0 + LANES] = tile.astype(BF16)

    def exact_body(j, carry):
        scores(j, 0)
        scores(j, 1)
        softmax(j, 0)
        weighted_values(j, 0, 0, False)
        softmax(j, 1)
        weighted_values(j, 1, 0, False)
        return carry

    def max_sq_norms(k_tile):
        sq = k_tile.astype(F32)
        sq = sq * sq
        return [jnp.broadcast_to(jnp.max(jnp.sum(jnp.where(my_lanes[kvh], sq, 0.0), axis=1, keepdims=True),
                                         axis=0, keepdims=True), (8, LANES)) for kvh in range(N_KV_HEADS)]

    for kb in range(nblk + 2):
        for kvh, n2 in enumerate(max_sq_norms(kv_ext[kb * BLOCK:(kb + 1) * BLOCK, 0:LANES])):
            kn_ref[kvh, kb] = n2
    ctx_norm = None
    for cb in range(ctx_len // BLOCK):
        n2 = max_sq_norms(kvc_ref[cb * BLOCK:(cb + 1) * BLOCK, 0:LANES])
        ctx_norm = n2 if ctx_norm is None else [jnp.maximum(a, b) for a, b in zip(ctx_norm, n2)]
    minden_ref[...] = jnp.full(minden_ref.shape, 3e38, F32)

    def bounded_exp(j, kvh, slot):
        e_ref, m_ref = e_refs[2 * slot + kvh], m_refs[2 * slot + kvh]
        q0 = pl.multiple_of(j * BLOCK, BLOCK)
        kw, kc = key_tiles(j, kvh)
        kn2 = jnp.maximum(jnp.max(kn_ref[kvh, pl.ds(j, 3)], axis=0), ctx_norm[kvh])[0:1, :]
        tiles = head_queries(j, kvh)
        for g in range(GQA_GROUP):
            qf = tiles[g].astype(F32)
            qn2 = jnp.max(jnp.sum(qf * qf, axis=1, keepdims=True), axis=0, keepdims=True)
            bound = jnp.maximum(jnp.sqrt(qn2 * kn2), sink_ref[kvh * GQA_GROUP + g] * LOG2E)
            m_ref[g * BLOCK:(g + 1) * BLOCK, :] = jnp.broadcast_to(bound, (BLOCK, LANES))
        qs = jnp.concatenate(tiles, axis=0)
        s_w = _dot_nt(qs, kw)
        s_c = _dot_nt(qs, kc)
        mask_l, mask_r = window_masks(j)
        for g, t0 in chunks:
            rows = slice(g * BLOCK + t0, g * BLOCK + t0 + SOFTMAX_ROWS)
            trows = slice(t0, t0 + SOFTMAX_ROWS)
            m = m_ref[rows, :]
            pieces = ([s_w[rows, p * LANES:(p + 1) * LANES] for p in range(nwin // LANES)]
                      + [s_c[rows, p * LANES:(p + 1) * LANES] for p in range(ctx_len // LANES)])
            pieces[0] = jnp.where(mask_l[trows], pieces[0], NEG)
            pieces[2] = jnp.where(mask_r[trows], pieces[2], NEG)
            for p, piece in enumerate(pieces):
                e_ref[rows, p * LANES:(p + 1) * LANES] = jnp.exp2(piece - m).astype(BF16)

    bounded_exp(0, 0, 0)
    bounded_exp(0, 1, 0)

    def fast_blocks(j0, last):
        for k in range(FAST_UNROLL):
            for kvh in range(N_KV_HEADS):
                weighted_values(j0 + k, kvh, k % 2, True)
                if not (last and k + 1 == FAST_UNROLL):
                    bounded_exp(j0 + k + 1, kvh, (k + 1) % 2)

    def fast_body(jj, carry):
        fast_blocks(FAST_UNROLL * jj, False)
        return carry

    lax.fori_loop(0, nblk // FAST_UNROLL - 1, fast_body, 0)
    fast_blocks(nblk - FAST_UNROLL, True)
    smallest = jnp.min(minden_ref[...])

    @pl.when(smallest < 2.0 ** MIN_SAFE_DENOMINATOR_LOG2)
    def _():
        lax.fori_loop(0, nblk, exact_body, 0)


def _attention(q, kv, kvc, sink, tq):
    nb, seq_len, _ = q.shape
    ctx_len = kvc.shape[1]
    nblk = seq_len // BLOCK
    qpb = tq // BLOCK
    rows = GQA_GROUP * BLOCK
    nkey = 3 * BLOCK + ctx_len
    return pl.pallas_call(
        functools.partial(_attn_kernel, seq_len, tq, ctx_len),
        grid=(nb, seq_len // tq),
        in_specs=[
            pl.BlockSpec(memory_space=pltpu.SMEM),
            pl.BlockSpec((None, tq, Q_DIM), lambda b, i: (b, i, 0)),
            pl.BlockSpec((None, BLOCK, KV_W), lambda b, i: (b, jnp.maximum(i * qpb - 1, 0), 0)),
            pl.BlockSpec((None, tq, KV_W), lambda b, i: (b, i, 0)),
            pl.BlockSpec((None, BLOCK, KV_W),
                         lambda b, i: (b, jnp.minimum((i + 1) * qpb, nblk - 1), 0)),
            pl.BlockSpec((None, ctx_len, KV_W), lambda b, i: (b, 0, 0)),
        ],
        out_specs=pl.BlockSpec((None, tq, Q_DIM), lambda b, i: (b, i, 0)),
        out_shape=jax.ShapeDtypeStruct((nb, seq_len, Q_DIM), BF16),
        scratch_shapes=([pltpu.VMEM((tq + 2 * BLOCK, KV_W), BF16)]
                        + [pltpu.VMEM((rows, nkey), F32)] * N_KV_HEADS
                        + [pltpu.VMEM((rows, nkey), BF16)] * (2 * N_KV_HEADS)
                        + [pltpu.VMEM((rows, LANES), F32)] * (2 * N_KV_HEADS)
                        + [pltpu.VMEM((N_KV_HEADS, qpb + 2, 8, LANES), F32),
                           pltpu.VMEM((BLOCK, LANES), F32)]),
        compiler_params=pltpu.CompilerParams(
            dimension_semantics=("arbitrary", "arbitrary"), vmem_limit_bytes=VMEM_LIMIT),
        name="attention",
    )(sink, q, kv, kv, kv, kvc)


def _post_kernel(layer, x_ref, o_ref_in, mod_ref, gmlp_ref, gfin_ref, wo_ref, w1_ref, w2_ref, out_ref):
    g1 = mod_ref[:, 2 * D_MODEL:3 * D_MODEL]
    x1 = x_ref[...] + g1 * _dot(o_ref_in[...], wo_ref[...])
    x2 = _mlp_residual(x1, mod_ref, gmlp_ref[layer:layer + 1, :], w1_ref, w2_ref)
    y = x2 * lax.rsqrt(jnp.mean(x2 * x2, axis=-1, keepdims=True) + EPS)
    out_ref[...] = y * gfin_ref[...]


def _post(x, o, mod, layer, tm, gmlp, gfin, w_o, w1, w2):
    nb, seq_len, _ = x.shape
    return pl.pallas_call(
        functools.partial(_post_kernel, layer),
        grid=(nb, seq_len // tm),
        in_specs=[
            pl.BlockSpec((None, tm, D_MODEL), lambda b, i: (b, i, 0)),
            pl.BlockSpec((None, tm, Q_DIM), lambda b, i: (b, i, 0)),
            pl.BlockSpec((None, None, 1, N_MOD * D_MODEL), lambda b, i: (layer, b, 0, 0)),
            _layer_rows_spec(),
            _const_spec((1, D_MODEL)),
            _const_spec((Q_DIM, D_MODEL)),
            _const_spec((D_MODEL, D_FF)),
            _const_spec((D_FF, D_MODEL)),
        ],
        out_specs=pl.BlockSpec((None, tm, D_MODEL), lambda b, i: (b, i, 0)),
        out_shape=jax.ShapeDtypeStruct(x.shape, F32),
        compiler_params=pltpu.CompilerParams(
            dimension_semantics=("arbitrary", "arbitrary"), vmem_limit_bytes=VMEM_LIMIT),
        name="post_attn",
    )(x, o, mod, gmlp, gfin, w_o, w1, w2)


def _rope_tables(seq_len):
    pos = np.arange(seq_len)
    inv = ROPE_BASE ** (-np.arange(0, AXIS_ROT, 2, dtype=np.float32) / AXIS_ROT)
    ang = np.concatenate([(pos // GRID_W).astype(np.float32)[:, None] * inv[None],
                          (pos % GRID_W).astype(np.float32)[:, None] * inv[None]], axis=-1)
    cos = np.tile(np.cos(ang.astype(np.float64)), (1, LANES // AXIS_ROT))
    sin = np.sin(ang.astype(np.float64))
    return (jnp.asarray(cos, F32), jnp.asarray(np.concatenate([-sin, -sin, sin, sin], axis=-1), F32))


def kernel(x, c, ctx, c_ctx, ada_w, ada_b, norm_mix_g, norm_mlp_g, pool_w_in, pool_w_grp,
           pool_scale, pool_w_out, attn_w_qkv, attn_sink, attn_w_o, mlp_w1, mlp_w2, final_g):
    nb, seq_len, _ = x.shape
    ctx_len = ctx.shape[1]
    ctx_row = nb
    c8 = jnp.concatenate(
        [c, c_ctx[None], jnp.zeros((MOD_ROWS - nb - 1, D_MODEL), F32)], axis=0)
    n_grp = len(POOL_WINDOWS)
    layer0_weights = ((pool_w_in, 0), (pool_w_grp.reshape(-1, n_grp * POOL_GROUP_DIM, POOL_GROUP_DIM), 0),
                      (pool_w_out, 0), (mlp_w1, 0), (mlp_w2, 0))
    mod, (w_in, w_grp, w_out, w1_0, w2_0) = _modulation(c8, ada_w, ada_b, layer0_weights)

    gmix, gmlp = norm_mix_g, norm_mlp_g

    l0 = (gmix, gmlp, pool_scale, w_in,
          w_grp.reshape(n_grp, POOL_GROUP_DIM, POOL_GROUP_DIM), w_out, w1_0, w2_0)
    layer1_weights = ((attn_w_qkv, 0), (attn_w_o, 0), (mlp_w1, 1), (mlp_w2, 1))
    x1, (w_qkv, w_o, w1, w2) = _layer0(x, mod, None, 0, 512, True, layer1_weights, *l0)
    ctx1, _ = _layer0(ctx, mod, ctx_row, 0, ctx_len, False, (), *l0)

    cos, sin = _rope_tables(seq_len)
    q, kv = _qkv(x1, mod, 1, 1024, gmix, w_qkv, cos, sin)
    kvc = _ctx_kv(ctx1, mod, ctx_row, 1, gmix, w_qkv)
    o = _attention(q, kv, kvc, attn_sink[0], 2048)
    return _post(x1, o, mod, 1, 512, gmlp, final_g.reshape(1, D_MODEL), w_o, w1, w2)
```

```python
import functools

import numpy as np
import jax
import jax.numpy as jnp
from jax import lax
from jax.experimental import pallas as pl
from jax.experimental.pallas import tpu as pltpu

D_MODEL = 1024
DEPTH = 2
GRID_W = 64
POOL_WINDOWS = (2, 4, 8, 16)
POOL_GROUP_DIM = D_MODEL // len(POOL_WINDOWS)
POOL_HALO = 8
HEAD_DIM = 64
N_HEADS = D_MODEL // HEAD_DIM
N_KV_HEADS = 2
GQA_GROUP = N_HEADS // N_KV_HEADS
Q_DIM = N_HEADS * HEAD_DIM
KV_DIM = N_KV_HEADS * HEAD_DIM
WINDOW = 128
BLOCK = 128
ROPE_BASE = 10000.0
AXIS_ROT = HEAD_DIM // 2
ROT_HALF = AXIS_ROT // 2
D_FF = 4 * D_MODEL
N_MOD = 6
EPS = 1e-6
NEG = -1e30
MOD_ROWS = 8
LANES = 128
LOG2E = 1.4426950408889634
KV_W = 3 * LANES
SOFTMAX_ROWS = 32
MIN_SAFE_DENOMINATOR_LOG2 = -64.0
QKV_COL_GROUP = 256
FAST_UNROLL = 2
VMEM_LIMIT = 56 * 1024 * 1024

LAYER_TILE = 512
QKV_TILE = 1024
ATTN_TILE = 2048
MOD_COLS = 1536

BF16 = jnp.bfloat16
F32 = jnp.float32


def _const_spec(shape):
    nd = len(shape)
    return pl.BlockSpec(shape, lambda *_: (0,) * nd, pipeline_mode=pl.Buffered(1))


def _layer_rows_spec():
    return _const_spec((DEPTH, D_MODEL))


def _dot(a, b):
    return jnp.dot(a, b, preferred_element_type=F32)


def _rms_mod(x, g, shift, scale):
    y = x * lax.rsqrt(jnp.mean(x * x, axis=-1, keepdims=True) + EPS)
    return y * (g * (1.0 + scale)) + shift


def _mlp_residual(x, mod_ref, gain, w1_ref, w2_ref):
    sh2 = mod_ref[:, 3 * D_MODEL:4 * D_MODEL]
    sc2 = mod_ref[:, 4 * D_MODEL:5 * D_MODEL]
    g2 = mod_ref[:, 5 * D_MODEL:6 * D_MODEL]
    h = _rms_mod(x, gain, sh2, sc2).astype(BF16)
    a = jnp.maximum(_dot(h, w1_ref[...]), 0.0)
    a = (a * a).astype(BF16)
    return x + g2 * _dot(a, w2_ref[...])


def _staged_steps(step, n_tiles, n_stages, body):
    assert n_tiles >= n_stages
    for t in range(n_stages - 1):
        pl.when(step == t)(functools.partial(body, *[s <= t for s in range(n_stages)]))
    pl.when((step >= n_stages - 1) & (step < n_tiles))(functools.partial(body, *[True] * n_stages))
    for k in range(n_stages - 1):
        pl.when(step == n_tiles + k)(functools.partial(body, *[s > k for s in range(n_stages)]))


def _skewed_steps(step, n_tiles, body):
    _staged_steps(step, n_tiles, 2, body)


def _stage_tile(n_tiles, tiles_per_row, stage):
    def tile(t):
        k = jnp.clip(t - stage, 0, n_tiles - 1)
        return k // tiles_per_row, k % tiles_per_row
    return tile


def _skew_maps(n_tiles, tiles_per_row):
    return _stage_tile(n_tiles, tiles_per_row, 0), _stage_tile(n_tiles, tiles_per_row, 1)


def _mod_kernel(n_prep, c_ref, w_ref, b_ref, *refs):
    prep_in, o_ref, prep_out = refs[:n_prep], refs[n_prep], refs[n_prep + 1:]
    c = c_ref[...]
    s = (c * jax.nn.sigmoid(c)).astype(BF16)
    r = _dot(s, w_ref[...].astype(BF16)) + b_ref[...]
    for row in range(MOD_ROWS):
        o_ref[row] = r[row:row + 1, :]
    for src, dst in zip(prep_in, prep_out):
        dst[...] = src[...].astype(BF16)


def _modulation(c8, ada_w, ada_b, first_layer_weights):
    tn = MOD_COLS
    n = N_MOD * D_MODEL
    n_col = n // tn
    steps = DEPTH * n_col
    prep_in_specs, prep_out_specs = [], []
    for w, idx in first_layer_weights:
        rows = w.shape[1] // steps
        prep_in_specs.append(pl.BlockSpec(
            (None, rows, w.shape[2]), lambda i, j, idx=idx: (idx, i * n_col + j, 0)))
        prep_out_specs.append(pl.BlockSpec((rows, w.shape[2]), lambda i, j: (i * n_col + j, 0)))
    outs = pl.pallas_call(
        functools.partial(_mod_kernel, len(first_layer_weights)),
        grid=(DEPTH, n_col),
        in_specs=[
            pl.BlockSpec((MOD_ROWS, D_MODEL), lambda i, j: (0, 0)),
            pl.BlockSpec((None, D_MODEL, tn), lambda i, j: (i, 0, j)),
            pl.BlockSpec((None, 1, tn), lambda i, j: (i, 0, j)),
        ] + prep_in_specs,
        out_specs=[pl.BlockSpec((None, MOD_ROWS, 1, tn), lambda i, j: (i, 0, 0, j))] + prep_out_specs,
        out_shape=[jax.ShapeDtypeStruct((DEPTH, MOD_ROWS, 1, n), F32)]
        + [jax.ShapeDtypeStruct(w.shape[1:], BF16) for w, _ in first_layer_weights],
        compiler_params=pltpu.CompilerParams(
            dimension_semantics=("arbitrary", "arbitrary"), vmem_limit_bytes=VMEM_LIMIT),
        name="modulation",
    )(c8, ada_w, ada_b.reshape(DEPTH, 1, n), *[w for w, _ in first_layer_weights])
    return outs[0], tuple(outs[1:])


def _to_pair_layout(v):
    lane = lax.broadcasted_iota(jnp.int32, v.shape, 1)
    for width in (ROT_HALF, AXIS_ROT):
        hi = (lane // (2 * width)) % 2
        lo = (lane // width) % 2
        v = jnp.where(hi == lo, v, jnp.where(hi == 0, pltpu.roll(v, LANES - width, 1),
                                             pltpu.roll(v, width, 1)))
    return v


def _prepare_next_weights(prep_in, prep_out):
    wqkv_in, wqkv_out = prep_in[0], prep_out[0]
    low = lax.broadcasted_iota(jnp.int32, (wqkv_in.shape[0], LANES), 1) < HEAD_DIM
    for p in range(GQA_GROUP):
        first = wqkv_in[:, (p // 2) * LANES:(p // 2 + 1) * LANES]
        second = wqkv_in[:, (p // 2 + GQA_GROUP // 2) * LANES:(p // 2 + GQA_GROUP // 2 + 1) * LANES]
        if p % 2 == 0:
            both = jnp.where(low, first, pltpu.roll(second, HEAD_DIM, 1))
        else:
            both = jnp.where(low, pltpu.roll(first, HEAD_DIM, 1), second)
        wqkv_out[:, p * LANES:(p + 1) * LANES] = _to_pair_layout(both).astype(BF16)
    wqkv_out[:, Q_DIM:Q_DIM + KV_DIM] = _to_pair_layout(wqkv_in[:, Q_DIM:Q_DIM + KV_DIM]).astype(BF16)
    wqkv_out[:, Q_DIM + KV_DIM:] = wqkv_in[:, Q_DIM + KV_DIM:].astype(BF16)
    for src, dst in zip(prep_in[1:], prep_out[1:]):
        dst[...] = src[...].astype(BF16)


def _layer0_kernel(seq_len, tm, n_tiles, skew, n_prep, layer, *refs):
    (x_ref, xp_ref, xn_ref, moda_ref, modb_ref, gmix_ref, gmlp_ref, pscale_ref, win_ref, wgrp_ref,
     wout_ref, w1_ref, w2_ref) = refs[:13]
    prep_in = refs[13:13 + n_prep]
    o_ref = refs[13 + n_prep]
    prep_out = refs[14 + n_prep:14 + 2 * n_prep]
    u_ref, d_ref, x1_ref, h2_ref = refs[14 + 2 * n_prep:]
    step = pl.program_id(0)
    i = jnp.minimum(step, n_tiles - 1) % (seq_len // tm)
    half = D_FF // 2

    def mlp_up(h2, c0, c1):
        a = jnp.maximum(_dot(h2, w1_ref[:, c0:c1]), 0.0)
        return (a * a).astype(BF16)

    def body(do_mixer, do_prev_mlp):
        if do_prev_mlp:
            g2 = modb_ref[:, 5 * D_MODEL:6 * D_MODEL]
            h2_prev = h2_ref[...]
            a_lo = mlp_up(h2_prev, 0, half)

        if do_mixer:
            sh1 = moda_ref[:, 0:D_MODEL]
            sc1 = moda_ref[:, D_MODEL:2 * D_MODEL]
            x = x_ref[...]
            xe = jnp.concatenate([xp_ref[...], x, xn_ref[...]], axis=0)
            u = _dot(_rms_mod(xe, gmix_ref[layer:layer + 1, :], sh1, sc1).astype(BF16), win_ref[...])
            u_ref[0:POOL_HALO, :] = jnp.where(i > 0, u[0:POOL_HALO], 0.0)
            u_ref[POOL_HALO:POOL_HALO + tm, :] = u[POOL_HALO:POOL_HALO + tm]
            u_ref[POOL_HALO + tm:POOL_HALO + tm + POOL_HALO, :] = jnp.where(
                (i + 1) * tm < seq_len, u[POOL_HALO + tm:], 0.0)

        if do_prev_mlp:
            a_hi = mlp_up(h2_prev, half, D_FF)

        if do_mixer:
            t = i * tm + lax.broadcasted_iota(jnp.int32, (tm, 1), 0)
            for g, w in enumerate(POOL_WINDOWS):
                c0, c1 = g * POOL_GROUP_DIM, (g + 1) * POOL_GROUP_DIM
                acc = u_ref[POOL_HALO - w // 2:POOL_HALO - w // 2 + tm, c0:c1]
                for s in range(-w // 2 + 1, w // 2):
                    acc = acc + u_ref[POOL_HALO + s:POOL_HALO + s + tm, c0:c1]
                lo = jnp.clip(t - w // 2, 0, seq_len)
                hi = jnp.clip(t + w // 2, 0, seq_len)
                inv_cnt = 1.0 / (hi - lo).astype(F32)
                d = acc * inv_cnt - u_ref[POOL_HALO:POOL_HALO + tm, c0:c1]
                d_ref[:, c0:c1] = _dot(d.astype(BF16), wgrp_ref[g])
            y = _dot((d_ref[...] * pscale_ref[...]).astype(BF16), wout_ref[...])

        if do_prev_mlp:
            down = _dot(jnp.concatenate([a_lo, a_hi], axis=1), w2_ref[...])

        if do_mixer:
            g1 = moda_ref[:, 2 * D_MODEL:3 * D_MODEL]
            sh2 = moda_ref[:, 3 * D_MODEL:4 * D_MODEL]
            sc2 = moda_ref[:, 4 * D_MODEL:5 * D_MODEL]
            x1 = x + g1 * y
            h2 = _rms_mod(x1, gmlp_ref[layer:layer + 1, :], sh2, sc2).astype(BF16)

        if do_prev_mlp:
            o_ref[...] = x1_ref[...] + g2 * down
        if do_mixer and n_prep:
            _prepare_next_weights(prep_in, prep_out)
        if do_mixer and skew:
            x1_ref[...] = x1
            h2_ref[...] = h2
        if do_mixer and not skew:
            g2_now = moda_ref[:, 5 * D_MODEL:6 * D_MODEL]
            o_ref[...] = x1 + g2_now * _dot(mlp_up(h2, 0, D_FF), w2_ref[...])

    if skew:
        _skewed_steps(step, n_tiles, body)
    else:
        body(True, False)


def _layer0(x, mod, mod_row, layer, tm, skew, next_weights, gmix, gmlp, pscale, w_in, w_grp, w_out, w1, w2):
    nb, seq_len, _ = x.shape
    rows8 = seq_len // POOL_HALO
    tpb = tm // POOL_HALO
    nt = seq_len // tm
    n_tiles = nb * nt
    tile_a, tile_b = _skew_maps(n_tiles, nt)
    if not skew:
        tile_b = tile_a

    def x_map(t):
        b, i = tile_a(t)
        return (b, i, 0)

    def prev_map(t):
        b, i = tile_a(t)
        return (b, jnp.maximum(i * tpb - 1, 0), 0)

    def next_map(t):
        b, i = tile_a(t)
        return (b, jnp.minimum((i + 1) * tpb, rows8 - 1), 0)

    def moda_map(t):
        return (layer, tile_a(t)[0] if mod_row is None else mod_row, 0, 0)

    def modb_map(t):
        return (layer, tile_b(t)[0] if mod_row is None else mod_row, 0, 0)

    def out_map(t):
        b, i = tile_b(t)
        return (b, i, 0)

    prep_in_specs, prep_out_specs = [], []
    for w, idx in next_weights:
        rows = w.shape[1] // n_tiles
        prep_in_specs.append(pl.BlockSpec(
            (None, rows, w.shape[2]), lambda t, idx=idx: (idx, jnp.minimum(t, n_tiles - 1), 0)))
        prep_out_specs.append(pl.BlockSpec(
            (rows, w.shape[2]), lambda t: (jnp.minimum(t, n_tiles - 1), 0)))

    outs = pl.pallas_call(
        functools.partial(_layer0_kernel, seq_len, tm, n_tiles, skew, len(next_weights), layer),
        grid=(n_tiles + (1 if skew else 0),),
        in_specs=[
            pl.BlockSpec((None, tm, D_MODEL), x_map),
            pl.BlockSpec((None, POOL_HALO, D_MODEL), prev_map),
            pl.BlockSpec((None, POOL_HALO, D_MODEL), next_map),
            pl.BlockSpec((None, None, 1, N_MOD * D_MODEL), moda_map),
            pl.BlockSpec((None, None, 1, N_MOD * D_MODEL), modb_map),
            _layer_rows_spec(),
            _layer_rows_spec(),
            _const_spec((1, D_MODEL)),
            _const_spec((D_MODEL, D_MODEL)),
            _const_spec((len(POOL_WINDOWS), POOL_GROUP_DIM, POOL_GROUP_DIM)),
            _const_spec((D_MODEL, D_MODEL)),
            _const_spec((D_MODEL, D_FF)),
            _const_spec((D_FF, D_MODEL)),
        ] + prep_in_specs,
        out_specs=[pl.BlockSpec((None, tm, D_MODEL), out_map)] + prep_out_specs,
        out_shape=[jax.ShapeDtypeStruct(x.shape, F32)]
        + [jax.ShapeDtypeStruct(w.shape[1:], BF16) for w, _ in next_weights],
        scratch_shapes=[pltpu.VMEM((tm + 2 * POOL_HALO, D_MODEL), F32),
                        pltpu.VMEM((tm, D_MODEL), F32),
                        pltpu.VMEM((tm, D_MODEL), F32),
                        pltpu.VMEM((tm, D_MODEL), BF16)],
        compiler_params=pltpu.CompilerParams(
            dimension_semantics=("arbitrary",), vmem_limit_bytes=VMEM_LIMIT),
        name="layer0",
    )(x, x, x, mod, mod, gmix, gmlp, pscale, w_in, w_grp, w_out, w1, w2, *[w for w, _ in next_weights])
    return outs[0], tuple(outs[1:])


def _store_kv(kv_ref, k, v):
    low = lax.broadcasted_iota(jnp.int32, v.shape, 1) < HEAD_DIM
    kv_ref[:, 0:LANES] = k.astype(BF16)
    kv_ref[:, LANES:2 * LANES] = jnp.where(low, v, 1.0).astype(BF16)
    kv_ref[:, 2 * LANES:3 * LANES] = jnp.where(low, 1.0, v).astype(BF16)


def _qkv_kernel(n_tiles, layer, x_ref, mod_ref, gmix_ref, w_ref, cos_ref, sin_ref, q_ref, kv_ref, h_ref):
    def body(do_norm, do_project):
        if do_project:
            h_prev = h_ref[...]
            q_groups = [_dot(h_prev, w_ref[:, c:c + QKV_COL_GROUP]) for c in range(0, Q_DIM, QKV_COL_GROUP)]
            kv = _dot(h_prev, w_ref[:, Q_DIM:])
        if do_norm:
            sh1 = mod_ref[:, 0:D_MODEL]
            sc1 = mod_ref[:, D_MODEL:2 * D_MODEL]
            h = _rms_mod(x_ref[...], gmix_ref[layer:layer + 1, :], sh1, sc1).astype(BF16)
        if do_project:
            cos = cos_ref[...]
            sin = sin_ref[...]
            qscale = HEAD_DIM ** -0.5 * LOG2E
            cos_q = cos * qscale
            sin_q = sin * qscale

            def rope(t, c, s):
                return t * c + pltpu.roll(t, HEAD_DIM, 1) * s

            for gi, group in enumerate(q_groups):
                for cb in range(QKV_COL_GROUP // LANES):
                    c0 = gi * QKV_COL_GROUP + cb * LANES
                    q_ref[:, c0:c0 + LANES] = rope(group[:, cb * LANES:(cb + 1) * LANES], cos_q, sin_q).astype(BF16)
            _store_kv(kv_ref, rope(kv[:, 0:KV_DIM], cos, sin), kv[:, KV_DIM:])
        if do_norm:
            h_ref[...] = h

    _skewed_steps(pl.program_id(0), n_tiles, body)


def _qkv(x, mod, layer, tm, gmix, w_qkv, cos, sin):
    nb, seq_len, _ = x.shape
    nt = seq_len // tm
    n_tiles = nb * nt
    tile_a, tile_b = _skew_maps(n_tiles, nt)

    def out_map(t):
        b, i = tile_b(t)
        return (b, i, 0)

    return pl.pallas_call(
        functools.partial(_qkv_kernel, n_tiles, layer),
        grid=(n_tiles + 1,),
        in_specs=[
            pl.BlockSpec((None, tm, D_MODEL), lambda t: (*tile_a(t), 0)),
            pl.BlockSpec((None, None, 1, N_MOD * D_MODEL), lambda t: (layer, tile_a(t)[0], 0, 0)),
            _layer_rows_spec(),
            _const_spec((D_MODEL, Q_DIM + 2 * KV_DIM)),
            pl.BlockSpec((tm, LANES), lambda t: (tile_b(t)[1], 0)),
            pl.BlockSpec((tm, LANES), lambda t: (tile_b(t)[1], 0)),
        ],
        out_specs=[pl.BlockSpec((None, tm, Q_DIM), out_map),
                   pl.BlockSpec((None, tm, KV_W), out_map)],
        out_shape=[jax.ShapeDtypeStruct((nb, seq_len, Q_DIM), BF16),
                   jax.ShapeDtypeStruct((nb, seq_len, KV_W), BF16)],
        scratch_shapes=[pltpu.VMEM((tm, D_MODEL), BF16)],
        compiler_params=pltpu.CompilerParams(
            dimension_semantics=("arbitrary",), vmem_limit_bytes=VMEM_LIMIT),
        name="qkv_rope",
    )(x, mod, gmix, w_qkv, cos, sin)


def _ctx_kv_kernel(layer, x_ref, mod_ref, gmix_ref, w_ref, kv_ref):
    sh1 = mod_ref[:, 0:D_MODEL]
    sc1 = mod_ref[:, D_MODEL:2 * D_MODEL]
    h = _rms_mod(x_ref[...], gmix_ref[layer:layer + 1, :], sh1, sc1).astype(BF16)
    kv = _dot(h, w_ref[...])
    _store_kv(kv_ref, kv[:, 0:KV_DIM], kv[:, KV_DIM:])


def _ctx_kv(ctx, mod, mod_row, layer, gmix, w_kv):
    nb, ctx_len, _ = ctx.shape
    return pl.pallas_call(
        functools.partial(_ctx_kv_kernel, layer),
        grid=(nb,),
        in_specs=[
            pl.BlockSpec((None, ctx_len, D_MODEL), lambda b: (b, 0, 0)),
            pl.BlockSpec((None, None, 1, N_MOD * D_MODEL), lambda b: (layer, mod_row, 0, 0)),
            _layer_rows_spec(),
            pl.BlockSpec((D_MODEL, 2 * KV_DIM), lambda b: (0, Q_DIM // (2 * KV_DIM)),
                         pipeline_mode=pl.Buffered(1)),
        ],
        out_specs=pl.BlockSpec((None, ctx_len, KV_W), lambda b: (b, 0, 0)),
        out_shape=jax.ShapeDtypeStruct((nb, ctx_len, KV_W), BF16),
        compiler_params=pltpu.CompilerParams(
            dimension_semantics=("arbitrary",), vmem_limit_bytes=VMEM_LIMIT),
        name="ctx_kv",
    )(ctx, mod, gmix, w_kv)


def _dot_nt(a, b):
    return lax.dot_general(a, b, (((1,), (1,)), ((), ())), preferred_element_type=F32)


def _attn_kernel(seq_len, tq, ctx_len, sink_ref, q_ref, kvp_ref, kvm_ref, kvn_ref, kvc_ref, o_ref,
                 kv_ext, *bufs):
    s_refs, e_refs, m_refs = bufs[0:2], bufs[2:6], bufs[6:10]
    kn_ref, minden_ref = bufs[10], bufs[11]
    i = pl.program_id(1)
    kv_ext[0:BLOCK, :] = kvp_ref[...]
    kv_ext[BLOCK:BLOCK + tq, :] = kvm_ref[...]
    kv_ext[BLOCK + tq:2 * BLOCK + tq, :] = kvn_ref[...]
    nblk = tq // BLOCK
    nwin = 3 * BLOCK
    nkey = nwin + ctx_len
    row = lax.broadcasted_iota(jnp.int32, (BLOCK, BLOCK), 0)
    col = lax.broadcasted_iota(jnp.int32, (BLOCK, BLOCK), 1)
    lane = lax.broadcasted_iota(jnp.int32, (BLOCK, LANES), 1)
    low = lane < HEAD_DIM
    head_a = (lane % HEAD_DIM) < AXIS_ROT
    my_lanes = (head_a, jnp.logical_not(head_a))
    den_lanes = (jnp.logical_not(low), low)

    chunks = [(g, t0) for g in range(GQA_GROUP) for t0 in range(0, BLOCK, SOFTMAX_ROWS)]

    def head_queries(j, kvh):
        q0 = pl.multiple_of(j * BLOCK, BLOCK)
        tiles = []
        for g in range(GQA_GROUP):
            pair = q_ref[pl.ds(q0, BLOCK), g * LANES:(g + 1) * LANES]
            tiles.append(jnp.where(my_lanes[kvh], pair, jnp.zeros_like(pair)))
        return tiles

    def window_masks(j):
        mask_l = (col >= row) & (i * tq + (j - 1) * BLOCK >= 0)
        mask_r = (col <= row) & (i * tq + (j + 1) * BLOCK < seq_len)
        return mask_l, mask_r

    def key_tiles(j, kvh):
        q0 = pl.multiple_of(j * BLOCK, BLOCK)
        return kv_ext[pl.ds(q0, nwin), 0:LANES], kvc_ref[:, 0:LANES]

    def scores(j, kvh):
        qs = jnp.concatenate(head_queries(j, kvh), axis=0)
        kw, kc = key_tiles(j, kvh)
        s_refs[kvh][:, 0:nwin] = _dot_nt(qs, kw)
        s_refs[kvh][:, nwin:nkey] = _dot_nt(qs, kc)

    def softmax(j, kvh):
        mask_l, mask_r = window_masks(j)

        def masked_pieces(rows, trows):
            pieces = [s_refs[kvh][rows, p * LANES:(p + 1) * LANES] for p in range(nkey // LANES)]
            pieces[0] = jnp.where(mask_l[trows], pieces[0], NEG)
            pieces[2] = jnp.where(mask_r[trows], pieces[2], NEG)
            return pieces

        for g, t0 in chunks:
            rows = slice(g * BLOCK + t0, g * BLOCK + t0 + SOFTMAX_ROWS)
            pieces = masked_pieces(rows, slice(t0, t0 + SOFTMAX_ROWS))
            m = pieces[0]
            for p in pieces[1:]:
                m = jnp.maximum(m, p)
            m = jnp.maximum(jnp.max(m, axis=1, keepdims=True), sink_ref[kvh * GQA_GROUP + g] * LOG2E)
            m_refs[kvh][rows, :] = jnp.broadcast_to(m, (SOFTMAX_ROWS, LANES))
        for g, t0 in chunks:
            rows = slice(g * BLOCK + t0, g * BLOCK + t0 + SOFTMAX_ROWS)
            pieces = masked_pieces(rows, slice(t0, t0 + SOFTMAX_ROWS))
            m = m_refs[kvh][rows, :]
            for p, piece in enumerate(pieces):
                e_refs[kvh][rows, p * LANES:(p + 1) * LANES] = jnp.exp2((piece - m).astype(BF16))

    def weighted_values(j, kvh, slot, fast):
        e_ref, m_ref = e_refs[2 * slot + kvh], m_refs[2 * slot + kvh]
        q0 = pl.multiple_of(j * BLOCK, BLOCK)
        vcols = slice((1 + kvh) * LANES, (2 + kvh) * LANES)
        values = jnp.concatenate([kv_ext[pl.ds(q0, nwin), vcols], kvc_ref[:, vcols]], axis=0)
        o = _dot(e_ref[...], values)
        on, smallest = [], None
        for g in range(GQA_GROUP):
            rows = slice(g * BLOCK, (g + 1) * BLOCK)
            sink = sink_ref[kvh * GQA_GROUP + g] * LOG2E
            if fast:
                sink_term = jnp.exp2(sink - m_ref[g * BLOCK:g * BLOCK + 1, :])
            else:
                sink_term = jnp.exp2(sink - m_ref[rows, :])
            total = o[rows] + sink_term
            smallest = total if smallest is None else jnp.minimum(smallest, total)
            on.append(o[rows] * pltpu.roll(1.0 / total, HEAD_DIM, 1))
        if fast:
            minden_ref[...] = jnp.minimum(minden_ref[...], jnp.where(den_lanes[kvh], smallest, 3e38))
        for pair in range(GQA_GROUP // 2):
            if kvh == 0:
                tile = jnp.where(low, on[2 * pair], pltpu.roll(on[2 * pair + 1], HEAD_DIM, 1))
            else:
                tile = jnp.where(low, pltpu.roll(on[2 * pair], HEAD_DIM, 1), on[2 * pair + 1])
            c0 = (kvh * GQA_GROUP + 2 * pair) * HEAD_DIM
            o_ref[pl.ds(q0, BLOCK), c0:c0 + LANES] = tile.astype(BF16)

    def exact_body(j, carry):
        scores(j, 0)
        scores(j, 1)
        softmax(j, 0)
        weighted_values(j, 0, 0, False)
        softmax(j, 1)
        weighted_values(j, 1, 0, False)
        return carry

    def max_sq_norms(k_tile):
        sq = k_tile.astype(F32)
        sq = sq * sq
        return [jnp.broadcast_to(jnp.max(jnp.sum(jnp.where(my_lanes[kvh], sq, 0.0), axis=1, keepdims=True),
                                         axis=0, keepdims=True), (8, LANES)) for kvh in range(N_KV_HEADS)]

    for kb in range(nblk + 2):
        for kvh, n2 in enumerate(max_sq_norms(kv_ext[kb * BLOCK:(kb + 1) * BLOCK, 0:LANES])):
            kn_ref[kvh, kb] = n2
    ctx_norm = None
    for cb in range(ctx_len // BLOCK):
        n2 = max_sq_norms(kvc_ref[cb * BLOCK:(cb + 1) * BLOCK, 0:LANES])
        ctx_norm = n2 if ctx_norm is None else [jnp.maximum(a, b) for a, b in zip(ctx_norm, n2)]
    minden_ref[...] = jnp.full(minden_ref.shape, 3e38, F32)

    def bounded_exp(j, kvh, slot):
        e_ref, m_ref = e_refs[2 * slot + kvh], m_refs[2 * slot + kvh]
        q0 = pl.multiple_of(j * BLOCK, BLOCK)
        kw, kc = key_tiles(j, kvh)
        kn2 = jnp.maximum(jnp.max(kn_ref[kvh, pl.ds(j, 3)], axis=0), ctx_norm[kvh])[0:1, :]
        tiles = head_queries(j, kvh)
        for g in range(GQA_GROUP):
            qf = tiles[g].astype(F32)
            qn2 = jnp.max(jnp.sum(qf * qf, axis=1, keepdims=True), axis=0, keepdims=True)
            bound = jnp.maximum(jnp.sqrt(qn2 * kn2), sink_ref[kvh * GQA_GROUP + g] * LOG2E)
            m_ref[g * BLOCK:(g + 1) * BLOCK, :] = jnp.broadcast_to(bound, (BLOCK, LANES))
        qs = jnp.concatenate(tiles, axis=0)
        s_w = _dot_nt(qs, kw)
        s_c = _dot_nt(qs, kc)
        mask_l, mask_r = window_masks(j)
        for g, t0 in chunks:
            rows = slice(g * BLOCK + t0, g * BLOCK + t0 + SOFTMAX_ROWS)
            trows = slice(t0, t0 + SOFTMAX_ROWS)
            m = m_ref[rows, :]
            pieces = ([s_w[rows, p * LANES:(p + 1) * LANES] for p in range(nwin // LANES)]
                      + [s_c[rows, p * LANES:(p + 1) * LANES] for p in range(ctx_len // LANES)])
            pieces[0] = jnp.where(mask_l[trows], pieces[0], NEG)
            pieces[2] = jnp.where(mask_r[trows], pieces[2], NEG)
            for p, piece in enumerate(pieces):
                e_ref[rows, p * LANES:(p + 1) * LANES] = jnp.exp2(piece - m).astype(BF16)

    bounded_exp(0, 0, 0)
    bounded_exp(0, 1, 0)

    def fast_blocks(j0, last):
        for k in range(FAST_UNROLL):
            for kvh in range(N_KV_HEADS):
                weighted_values(j0 + k, kvh, k % 2, True)
                if not (last and k + 1 == FAST_UNROLL):
                    bounded_exp(j0 + k + 1, kvh, (k + 1) % 2)

    def fast_body(jj, carry):
        fast_blocks(FAST_UNROLL * jj, False)
        return carry

    lax.fori_loop(0, nblk // FAST_UNROLL - 1, fast_body, 0)
    fast_blocks(nblk - FAST_UNROLL, True)
    smallest = jnp.min(minden_ref[...])

    @pl.when(smallest < 2.0 ** MIN_SAFE_DENOMINATOR_LOG2)
    def _():
        lax.fori_loop(0, nblk, exact_body, 0)


def _attention(q, kv, kvc, sink, tq):
    nb, seq_len, _ = q.shape
    ctx_len = kvc.shape[1]
    nblk = seq_len // BLOCK
    qpb = tq // BLOCK
    rows = GQA_GROUP * BLOCK
    nkey = 3 * BLOCK + ctx_len
    return pl.pallas_call(
        functools.partial(_attn_kernel, seq_len, tq, ctx_len),
        grid=(nb, seq_len // tq),
        in_specs=[
            pl.BlockSpec(memory_space=pltpu.SMEM),
            pl.BlockSpec((None, tq, Q_DIM), lambda b, i: (b, i, 0)),
            pl.BlockSpec((None, BLOCK, KV_W), lambda b, i: (b, jnp.maximum(i * qpb - 1, 0), 0)),
            pl.BlockSpec((None, tq, KV_W), lambda b, i: (b, i, 0)),
            pl.BlockSpec((None, BLOCK, KV_W),
                         lambda b, i: (b, jnp.minimum((i + 1) * qpb, nblk - 1), 0)),
            pl.BlockSpec((None, ctx_len, KV_W), lambda b, i: (b, 0, 0)),
        ],
        out_specs=pl.BlockSpec((None, tq, Q_DIM), lambda b, i: (b, i, 0)),
        out_shape=jax.ShapeDtypeStruct((nb, seq_len, Q_DIM), BF16),
        scratch_shapes=([pltpu.VMEM((tq + 2 * BLOCK, KV_W), BF16)]
                        + [pltpu.VMEM((rows, nkey), F32)] * N_KV_HEADS
                        + [pltpu.VMEM((rows, nkey), BF16)] * (2 * N_KV_HEADS)
                        + [pltpu.VMEM((rows, LANES), F32)] * (2 * N_KV_HEADS)
                        + [pltpu.VMEM((N_KV_HEADS, qpb + 2, 8, LANES), F32),
                           pltpu.VMEM((BLOCK, LANES), F32)]),
        compiler_params=pltpu.CompilerParams(
            dimension_semantics=("arbitrary", "arbitrary"), vmem_limit_bytes=VMEM_LIMIT),
        name="attention",
    )(sink, q, kv, kv, kv, kvc)


def _post_kernel(layer, x_ref, o_ref_in, mod_ref, gmlp_ref, gfin_ref, wo_ref, w1_ref, w2_ref, out_ref):
    g1 = mod_ref[:, 2 * D_MODEL:3 * D_MODEL]
    x1 = x_ref[...] + g1 * _dot(o_ref_in[...], wo_ref[...])
    x2 = _mlp_residual(x1, mod_ref, gmlp_ref[layer:layer + 1, :], w1_ref, w2_ref)
    y = x2 * lax.rsqrt(jnp.mean(x2 * x2, axis=-1, keepdims=True) + EPS)
    out_ref[...] = y * gfin_ref[...]


def _post(x, o, mod, layer, tm, gmlp, gfin, w_o, w1, w2):
    nb, seq_len, _ = x.shape
    return pl.pallas_call(
        functools.partial(_post_kernel, layer),
        grid=(nb, seq_len // tm),
        in_specs=[
            pl.BlockSpec((None, tm, D_MODEL), lambda b, i: (b, i, 0)),
            pl.BlockSpec((None, tm, Q_DIM), lambda b, i: (b, i, 0)),
            pl.BlockSpec((None, None, 1, N_MOD * D_MODEL), lambda b, i: (layer, b, 0, 0)),
            _layer_rows_spec(),
            _const_spec((1, D_MODEL)),
            _const_spec((Q_DIM, D_MODEL)),
            _const_spec((D_MODEL, D_FF)),
            _const_spec((D_FF, D_MODEL)),
        ],
        out_specs=pl.BlockSpec((None, tm, D_MODEL), lambda b, i: (b, i, 0)),
        out_shape=jax.ShapeDtypeStruct(x.shape, F32),
        compiler_params=pltpu.CompilerParams(
            dimension_semantics=("arbitrary", "arbitrary"), vmem_limit_bytes=VMEM_LIMIT),
        name="post_attn",
    )(x, o, mod, gmlp, gfin, w_o, w1, w2)


def _rope_tables(seq_len):
    pos = np.arange(seq_len)
    inv = ROPE_BASE ** (-np.arange(0, AXIS_ROT, 2, dtype=np.float32) / AXIS_ROT)
    ang = np.concatenate([(pos // GRID_W).astype(np.float32)[:, None] * inv[None],
                          (pos % GRID_W).astype(np.float32)[:, None] * inv[None]], axis=-1)
    cos = np.tile(np.cos(ang.astype(np.float64)), (1, LANES // AXIS_ROT))
    sin = np.sin(ang.astype(np.float64))
    return (jnp.asarray(cos, F32), jnp.asarray(np.concatenate([-sin, -sin, sin, sin], axis=-1), F32))


def kernel(x, c, ctx, c_ctx, ada_w, ada_b, norm_mix_g, norm_mlp_g, pool_w_in, pool_w_grp,
           pool_scale, pool_w_out, attn_w_qkv, attn_sink, attn_w_o, mlp_w1, mlp_w2, final_g):
    nb, seq_len, _ = x.shape
    ctx_len = ctx.shape[1]
    ctx_row = nb
    c8 = jnp.concatenate(
        [c, c_ctx[None], jnp.zeros((MOD_ROWS - nb - 1, D_MODEL), F32)], axis=0)
    n_grp = len(POOL_WINDOWS)
    layer0_weights = ((pool_w_in, 0), (pool_w_grp.reshape(-1, n_grp * POOL_GROUP_DIM, POOL_GROUP_DIM), 0),
                      (pool_w_out, 0), (mlp_w1, 0), (mlp_w2, 0))
    mod, (w_in, w_grp, w_out, w1_0, w2_0) = _modulation(c8, ada_w, ada_b, layer0_weights)

    gmix, gmlp = norm_mix_g, norm_mlp_g

    l0 = (gmix, gmlp, pool_scale, w_in,
          w_grp.reshape(n_grp, POOL_GROUP_DIM, POOL_GROUP_DIM), w_out, w1_0, w2_0)
    layer1_weights = ((attn_w_qkv, 0), (attn_w_o, 0), (mlp_w1, 1), (mlp_w2, 1))
    x1, (w_qkv, w_o, w1, w2) = _layer0(x, mod, None, 0, LAYER_TILE, True, layer1_weights, *l0)
    ctx1, _ = _layer0(ctx, mod, ctx_row, 0, ctx_len, False, (), *l0)

    cos, sin = _rope_tables(seq_len)
    q, kv = _qkv(x1, mod, 1, QKV_TILE, gmix, w_qkv, cos, sin)
    kvc = _ctx_kv(ctx1, mod, ctx_row, 1, gmix, w_qkv)
    o = _attention(q, kv, kvc, attn_sink[0], ATTN_TILE)
    return _post(x1, o, mod, 1, LAYER_TILE, gmlp, final_g.reshape(1, D_MODEL), w_o, w1, w2)
```

```python
import functools

import numpy as np
import jax
import jax.numpy as jnp
from jax import lax
from jax.experimental import pallas as pl
from jax.experimental.pallas import tpu as pltpu

D_MODEL = 1024
DEPTH = 2
GRID_W = 64
POOL_WINDOWS = (2, 4, 8, 16)
POOL_GROUP_DIM = D_MODEL // len(POOL_WINDOWS)
POOL_HALO = 8
HEAD_DIM = 64
N_HEADS = D_MODEL // HEAD_DIM
N_KV_HEADS = 2
GQA_GROUP = N_HEADS // N_KV_HEADS
Q_DIM = N_HEADS * HEAD_DIM
KV_DIM = N_KV_HEADS * HEAD_DIM
WINDOW = 128
BLOCK = 128
ROPE_BASE = 10000.0
AXIS_ROT = HEAD_DIM // 2
ROT_HALF = AXIS_ROT // 2
D_FF = 4 * D_MODEL
N_MOD = 6
EPS = 1e-6
NEG = -1e30
MOD_ROWS = 8
LANES = 128
LOG2E = 1.4426950408889634
KV_W = 3 * LANES
SOFTMAX_ROWS = 32
MIN_SAFE_DENOMINATOR_LOG2 = -64.0
QKV_COL_GROUP = 256
FAST_UNROLL = 2
VMEM_LIMIT = 56 * 1024 * 1024

LAYER_TILE = 512
QKV_TILE = 1024
ATTN_TILE = 2048
MOD_COLS = 1536

BF16 = jnp.bfloat16
F32 = jnp.float32


def _const_spec(shape):
    nd = len(shape)
    return pl.BlockSpec(shape, lambda *_: (0,) * nd, pipeline_mode=pl.Buffered(1))


def _layer_rows_spec():
    return _const_spec((DEPTH, D_MODEL))


def _dot(a, b):
    return jnp.dot(a, b, preferred_element_type=F32)


def _rms_mod(x, g, shift, scale):
    y = x * lax.rsqrt(jnp.mean(x * x, axis=-1, keepdims=True) + EPS)
    return y * (g * (1.0 + scale)) + shift


def _mlp_residual(x, mod_ref, gain, w1_ref, w2_ref):
    sh2 = mod_ref[:, 3 * D_MODEL:4 * D_MODEL]
    sc2 = mod_ref[:, 4 * D_MODEL:5 * D_MODEL]
    g2 = mod_ref[:, 5 * D_MODEL:6 * D_MODEL]
    h = _rms_mod(x, gain, sh2, sc2).astype(BF16)
    a = jnp.maximum(_dot(h, w1_ref[...]), 0.0)
    a = (a * a).astype(BF16)
    return x + g2 * _dot(a, w2_ref[...])


def _staged_steps(step, n_tiles, n_stages, body):
    assert n_tiles >= n_stages
    for t in range(n_stages - 1):
        pl.when(step == t)(functools.partial(body, *[s <= t for s in range(n_stages)]))
    pl.when((step >= n_stages - 1) & (step < n_tiles))(functools.partial(body, *[True] * n_stages))
    for k in range(n_stages - 1):
        pl.when(step == n_tiles + k)(functools.partial(body, *[s > k for s in range(n_stages)]))


def _skewed_steps(step, n_tiles, body):
    _staged_steps(step, n_tiles, 2, body)


def _stage_tile(n_tiles, tiles_per_row, stage):
    def tile(t):
        k = jnp.clip(t - stage, 0, n_tiles - 1)
        return k // tiles_per_row, k % tiles_per_row
    return tile


def _skew_maps(n_tiles, tiles_per_row):
    return _stage_tile(n_tiles, tiles_per_row, 0), _stage_tile(n_tiles, tiles_per_row, 1)


def _mod_kernel(n_prep, c_ref, w_ref, b_ref, *refs):
    prep_in, o_ref, prep_out = refs[:n_prep], refs[n_prep], refs[n_prep + 1:]
    c = c_ref[...]
    s = (c * jax.nn.sigmoid(c)).astype(BF16)
    r = _dot(s, w_ref[...].astype(BF16)) + b_ref[...]
    for row in range(MOD_ROWS):
        o_ref[row] = r[row:row + 1, :]
    for src, dst in zip(prep_in, prep_out):
        dst[...] = src[...].astype(BF16)


def _modulation(c8, ada_w, ada_b, first_layer_weights):
    tn = MOD_COLS
    n = N_MOD * D_MODEL
    n_col = n // tn
    steps = DEPTH * n_col
    prep_in_specs, prep_out_specs = [], []
    for w, idx in first_layer_weights:
        rows = w.shape[1] // steps
        prep_in_specs.append(pl.BlockSpec(
            (None, rows, w.shape[2]), lambda i, j, idx=idx: (idx, i * n_col + j, 0)))
        prep_out_specs.append(pl.BlockSpec((rows, w.shape[2]), lambda i, j: (i * n_col + j, 0)))
    outs = pl.pallas_call(
        functools.partial(_mod_kernel, len(first_layer_weights)),
        grid=(DEPTH, n_col),
        in_specs=[
            pl.BlockSpec((MOD_ROWS, D_MODEL), lambda i, j: (0, 0)),
            pl.BlockSpec((None, D_MODEL, tn), lambda i, j: (i, 0, j)),
            pl.BlockSpec((None, 1, tn), lambda i, j: (i, 0, j)),
        ] + prep_in_specs,
        out_specs=[pl.BlockSpec((None, MOD_ROWS, 1, tn), lambda i, j: (i, 0, 0, j))] + prep_out_specs,
        out_shape=[jax.ShapeDtypeStruct((DEPTH, MOD_ROWS, 1, n), F32)]
        + [jax.ShapeDtypeStruct(w.shape[1:], BF16) for w, _ in first_layer_weights],
        compiler_params=pltpu.CompilerParams(
            dimension_semantics=("arbitrary", "arbitrary"), vmem_limit_bytes=VMEM_LIMIT),
        name="modulation",
    )(c8, ada_w, ada_b.reshape(DEPTH, 1, n), *[w for w, _ in first_layer_weights])
    return outs[0], tuple(outs[1:])


def _to_pair_layout(v):
    lane = lax.broadcasted_iota(jnp.int32, v.shape, 1)
    for width in (ROT_HALF, AXIS_ROT):
        hi = (lane // (2 * width)) % 2
        lo = (lane // width) % 2
        v = jnp.where(hi == lo, v, jnp.where(hi == 0, pltpu.roll(v, LANES - width, 1),
                                             pltpu.roll(v, width, 1)))
    return v


def _prepare_next_weights(prep_in, prep_out):
    wqkv_in, wqkv_out = prep_in[0], prep_out[0]
    low = lax.broadcasted_iota(jnp.int32, (wqkv_in.shape[0], LANES), 1) < HEAD_DIM
    for p in range(GQA_GROUP):
        first = wqkv_in[:, (p // 2) * LANES:(p // 2 + 1) * LANES]
        second = wqkv_in[:, (p // 2 + GQA_GROUP // 2) * LANES:(p // 2 + GQA_GROUP // 2 + 1) * LANES]
        if p % 2 == 0:
            both = jnp.where(low, first, pltpu.roll(second, HEAD_DIM, 1))
        else:
            both = jnp.where(low, pltpu.roll(first, HEAD_DIM, 1), second)
        wqkv_out[:, p * LANES:(p + 1) * LANES] = _to_pair_layout(both).astype(BF16)
    wqkv_out[:, Q_DIM:Q_DIM + KV_DIM] = _to_pair_layout(wqkv_in[:, Q_DIM:Q_DIM + KV_DIM]).astype(BF16)
    wqkv_out[:, Q_DIM + KV_DIM:] = wqkv_in[:, Q_DIM + KV_DIM:].astype(BF16)
    for src, dst in zip(prep_in[1:], prep_out[1:]):
        dst[...] = src[...].astype(BF16)


def _layer0_kernel(seq_len, tm, n_tiles, skew, n_prep, layer, *refs):
    (x_ref, xp_ref, xn_ref, moda_ref, modb_ref, gmix_ref, gmlp_ref, pscale_ref, win_ref, wgrp_ref,
     wout_ref, w1_ref, w2_ref) = refs[:13]
    prep_in = refs[13:13 + n_prep]
    o_ref = refs[13 + n_prep]
    prep_out = refs[14 + n_prep:14 + 2 * n_prep]
    u_ref, d_ref, x1_ref, h2_ref = refs[14 + 2 * n_prep:]
    step = pl.program_id(0)
    i = jnp.minimum(step, n_tiles - 1) % (seq_len // tm)
    half = D_FF // 2

    def mlp_up(h2, c0, c1):
        a = jnp.maximum(_dot(h2, w1_ref[:, c0:c1]), 0.0)
        return (a * a).astype(BF16)

    def body(do_mixer, do_prev_mlp):
        if do_prev_mlp:
            g2 = modb_ref[:, 5 * D_MODEL:6 * D_MODEL]
            h2_prev = h2_ref[...]
            a_lo = mlp_up(h2_prev, 0, half)

        if do_mixer:
            sh1 = moda_ref[:, 0:D_MODEL]
            sc1 = moda_ref[:, D_MODEL:2 * D_MODEL]
            x = x_ref[...]
            xe = jnp.concatenate([xp_ref[...], x, xn_ref[...]], axis=0)
            u = _dot(_rms_mod(xe, gmix_ref[layer:layer + 1, :], sh1, sc1).astype(BF16), win_ref[...])
            u_ref[0:POOL_HALO, :] = jnp.where(i > 0, u[0:POOL_HALO], 0.0)
            u_ref[POOL_HALO:POOL_HALO + tm, :] = u[POOL_HALO:POOL_HALO + tm]
            u_ref[POOL_HALO + tm:POOL_HALO + tm + POOL_HALO, :] = jnp.where(
                (i + 1) * tm < seq_len, u[POOL_HALO + tm:], 0.0)

        if do_prev_mlp:
            a_hi = mlp_up(h2_prev, half, D_FF)

        if do_mixer:
            t = i * tm + lax.broadcasted_iota(jnp.int32, (tm, 1), 0)
            for g, w in enumerate(POOL_WINDOWS):
                c0, c1 = g * POOL_GROUP_DIM, (g + 1) * POOL_GROUP_DIM
                acc = u_ref[POOL_HALO - w // 2:POOL_HALO - w // 2 + tm, c0:c1]
                for s in range(-w // 2 + 1, w // 2):
                    acc = acc + u_ref[POOL_HALO + s:POOL_HALO + s + tm, c0:c1]
                lo = jnp.clip(t - w // 2, 0, seq_len)
                hi = jnp.clip(t + w // 2, 0, seq_len)
                inv_cnt = 1.0 / (hi - lo).astype(F32)
                d = acc * inv_cnt - u_ref[POOL_HALO:POOL_HALO + tm, c0:c1]
                d_ref[:, c0:c1] = _dot(d.astype(BF16), wgrp_ref[g])
            y = _dot((d_ref[...] * pscale_ref[...]).astype(BF16), wout_ref[...])

        if do_prev_mlp:
            down = _dot(jnp.concatenate([a_lo, a_hi], axis=1), w2_ref[...])

        if do_mixer:
            g1 = moda_ref[:, 2 * D_MODEL:3 * D_MODEL]
            sh2 = moda_ref[:, 3 * D_MODEL:4 * D_MODEL]
            sc2 = moda_ref[:, 4 * D_MODEL:5 * D_MODEL]
            x1 = x + g1 * y
            h2 = _rms_mod(x1, gmlp_ref[layer:layer + 1, :], sh2, sc2).astype(BF16)

        if do_prev_mlp:
            o_ref[...] = x1_ref[...] + g2 * down
        if do_mixer and n_prep:
            _prepare_next_weights(prep_in, prep_out)
        if do_mixer and skew:
            x1_ref[...] = x1
            h2_ref[...] = h2
        if do_mixer and not skew:
            g2_now = moda_ref[:, 5 * D_MODEL:6 * D_MODEL]
            o_ref[...] = x1 + g2_now * _dot(mlp_up(h2, 0, D_FF), w2_ref[...])

    if skew:
        pl.when(step == 0)(functools.partial(body, True, False))
        pl.when(step > 0)(functools.partial(body, True, True))
    else:
        body(True, False)


def _layer0(x, mod, mod_row, layer, tm, skew, next_weights, gmix, gmlp, pscale, w_in, w_grp, w_out, w1, w2):
    nb, seq_len, _ = x.shape
    rows8 = seq_len // POOL_HALO
    tpb = tm // POOL_HALO
    nt = seq_len // tm
    n_tiles = nb * nt
    tile_a, tile_b = _skew_maps(n_tiles, nt)
    if not skew:
        tile_b = tile_a

    def x_map(t):
        b, i = tile_a(t)
        return (b, i, 0)

    def prev_map(t):
        b, i = tile_a(t)
        return (b, jnp.maximum(i * tpb - 1, 0), 0)

    def next_map(t):
        b, i = tile_a(t)
        return (b, jnp.minimum((i + 1) * tpb, rows8 - 1), 0)

    def moda_map(t):
        return (layer, tile_a(t)[0] if mod_row is None else mod_row, 0, 0)

    def modb_map(t):
        return (layer, tile_b(t)[0] if mod_row is None else mod_row, 0, 0)

    def out_map(t):
        b, i = tile_b(t)
        return (b, i, 0)

    prep_in_specs, prep_out_specs = [], []
    for w, idx in next_weights:
        rows = w.shape[1] // n_tiles
        prep_in_specs.append(pl.BlockSpec(
            (None, rows, w.shape[2]), lambda t, idx=idx: (idx, jnp.minimum(t, n_tiles - 1), 0)))
        prep_out_specs.append(pl.BlockSpec(
            (rows, w.shape[2]), lambda t: (jnp.minimum(t, n_tiles - 1), 0)))

    outs = pl.pallas_call(
        functools.partial(_layer0_kernel, seq_len, tm, n_tiles, skew, len(next_weights), layer),
        grid=(n_tiles + (1 if skew else 0),),
        in_specs=[
            pl.BlockSpec((None, tm, D_MODEL), x_map),
            pl.BlockSpec((None, POOL_HALO, D_MODEL), prev_map),
            pl.BlockSpec((None, POOL_HALO, D_MODEL), next_map),
            pl.BlockSpec((None, None, 1, N_MOD * D_MODEL), moda_map),
            pl.BlockSpec((None, None, 1, N_MOD * D_MODEL), modb_map),
            _layer_rows_spec(),
            _layer_rows_spec(),
            _const_spec((1, D_MODEL)),
            _const_spec((D_MODEL, D_MODEL)),
            _const_spec((len(POOL_WINDOWS), POOL_GROUP_DIM, POOL_GROUP_DIM)),
            _const_spec((D_MODEL, D_MODEL)),
            _const_spec((D_MODEL, D_FF)),
            _const_spec((D_FF, D_MODEL)),
        ] + prep_in_specs,
        out_specs=[pl.BlockSpec((None, tm, D_MODEL), out_map)] + prep_out_specs,
        out_shape=[jax.ShapeDtypeStruct(x.shape, F32)]
        + [jax.ShapeDtypeStruct(w.shape[1:], BF16) for w, _ in next_weights],
        scratch_shapes=[pltpu.VMEM((tm + 2 * POOL_HALO, D_MODEL), F32),
                        pltpu.VMEM((tm, D_MODEL), F32),
                        pltpu.VMEM((tm, D_MODEL), F32),
                        pltpu.VMEM((tm, D_MODEL), BF16)],
        compiler_params=pltpu.CompilerParams(
            dimension_semantics=("arbitrary",), vmem_limit_bytes=VMEM_LIMIT),
        name="layer0",
    )(x, x, x, mod, mod, gmix, gmlp, pscale, w_in, w_grp, w_out, w1, w2, *[w for w, _ in next_weights])
    return outs[0], tuple(outs[1:])


def _store_kv(kv_ref, k, v):
    low = lax.broadcasted_iota(jnp.int32, v.shape, 1) < HEAD_DIM
    kv_ref[:, 0:LANES] = k.astype(BF16)
    kv_ref[:, LANES:2 * LANES] = jnp.where(low, v, 1.0).astype(BF16)
    kv_ref[:, 2 * LANES:3 * LANES] = jnp.where(low, 1.0, v).astype(BF16)


def _qkv_kernel(n_tiles, layer, x_ref, mod_ref, gmix_ref, w_ref, cos_ref, sin_ref, q_ref, kv_ref, h_ref):
    def body(do_norm, do_project):
        if do_project:
            h_prev = h_ref[...]
            q_groups = [_dot(h_prev, w_ref[:, c:c + QKV_COL_GROUP]) for c in range(0, Q_DIM, QKV_COL_GROUP)]
            kv = _dot(h_prev, w_ref[:, Q_DIM:])
        if do_norm:
            sh1 = mod_ref[:, 0:D_MODEL]
            sc1 = mod_ref[:, D_MODEL:2 * D_MODEL]
            h = _rms_mod(x_ref[...], gmix_ref[layer:layer + 1, :], sh1, sc1).astype(BF16)
        if do_project:
            cos = cos_ref[...]
            sin = sin_ref[...]
            qscale = HEAD_DIM ** -0.5 * LOG2E
            cos_q = cos * qscale
            sin_q = sin * qscale

            def rope(t, c, s):
                return t * c + pltpu.roll(t, HEAD_DIM, 1) * s

            for gi, group in enumerate(q_groups):
                for cb in range(QKV_COL_GROUP // LANES):
                    c0 = gi * QKV_COL_GROUP + cb * LANES
                    q_ref[:, c0:c0 + LANES] = rope(group[:, cb * LANES:(cb + 1) * LANES], cos_q, sin_q).astype(BF16)
            _store_kv(kv_ref, rope(kv[:, 0:KV_DIM], cos, sin), kv[:, KV_DIM:])
        if do_norm:
            h_ref[...] = h

    _skewed_steps(pl.program_id(0), n_tiles, body)


def _qkv(x, mod, layer, tm, gmix, w_qkv, cos, sin):
    nb, seq_len, _ = x.shape
    nt = seq_len // tm
    n_tiles = nb * nt
    tile_a, tile_b = _skew_maps(n_tiles, nt)

    def out_map(t):
        b, i = tile_b(t)
        return (b, i, 0)

    return pl.pallas_call(
        functools.partial(_qkv_kernel, n_tiles, layer),
        grid=(n_tiles + 1,),
        in_specs=[
            pl.BlockSpec((None, tm, D_MODEL), lambda t: (*tile_a(t), 0)),
            pl.BlockSpec((None, None, 1, N_MOD * D_MODEL), lambda t: (layer, tile_a(t)[0], 0, 0)),
            _layer_rows_spec(),
            _const_spec((D_MODEL, Q_DIM + 2 * KV_DIM)),
            pl.BlockSpec((tm, LANES), lambda t: (tile_b(t)[1], 0)),
            pl.BlockSpec((tm, LANES), lambda t: (tile_b(t)[1], 0)),
        ],
        out_specs=[pl.BlockSpec((None, tm, Q_DIM), out_map),
                   pl.BlockSpec((None, tm, KV_W), out_map)],
        out_shape=[jax.ShapeDtypeStruct((nb, seq_len, Q_DIM), BF16),
                   jax.ShapeDtypeStruct((nb, seq_len, KV_W), BF16)],
        scratch_shapes=[pltpu.VMEM((tm, D_MODEL), BF16)],
        compiler_params=pltpu.CompilerParams(
            dimension_semantics=("arbitrary",), vmem_limit_bytes=VMEM_LIMIT),
        name="qkv_rope",
    )(x, mod, gmix, w_qkv, cos, sin)


def _ctx_kv_kernel(layer, x_ref, mod_ref, gmix_ref, w_ref, kv_ref):
    sh1 = mod_ref[:, 0:D_MODEL]
    sc1 = mod_ref[:, D_MODEL:2 * D_MODEL]
    h = _rms_mod(x_ref[...], gmix_ref[layer:layer + 1, :], sh1, sc1).astype(BF16)
    kv = _dot(h, w_ref[...])
    _store_kv(kv_ref, kv[:, 0:KV_DIM], kv[:, KV_DIM:])


def _ctx_kv(ctx, mod, mod_row, layer, gmix, w_kv):
    nb, ctx_len, _ = ctx.shape
    return pl.pallas_call(
        functools.partial(_ctx_kv_kernel, layer),
        grid=(nb,),
        in_specs=[
            pl.BlockSpec((None, ctx_len, D_MODEL), lambda b: (b, 0, 0)),
            pl.BlockSpec((None, None, 1, N_MOD * D_MODEL), lambda b: (layer, mod_row, 0, 0)),
            _layer_rows_spec(),
            pl.BlockSpec((D_MODEL, 2 * KV_DIM), lambda b: (0, Q_DIM // (2 * KV_DIM)),
                         pipeline_mode=pl.Buffered(1)),
        ],
        out_specs=pl.BlockSpec((None, ctx_len, KV_W), lambda b: (b, 0, 0)),
        out_shape=jax.ShapeDtypeStruct((nb, ctx_len, KV_W), BF16),
        compiler_params=pltpu.CompilerParams(
            dimension_semantics=("arbitrary",), vmem_limit_bytes=VMEM_LIMIT),
        name="ctx_kv",
    )(ctx, mod, gmix, w_kv)


def _dot_nt(a, b):
    return lax.dot_general(a, b, (((1,), (1,)), ((), ())), preferred_element_type=F32)


def _attn_kernel(seq_len, tq, ctx_len, sink_ref, q_ref, kvp_ref, kvm_ref, kvn_ref, kvc_ref, o_ref,
                 kv_ext, *bufs):
    s_refs, e_refs, m_refs = bufs[0:2], bufs[2:6], bufs[6:10]
    kn_ref, minden_ref = bufs[10], bufs[11]
    i = pl.program_id(1)
    kv_ext[0:BLOCK, :] = kvp_ref[...]
    kv_ext[BLOCK:BLOCK + tq, :] = kvm_ref[...]
    kv_ext[BLOCK + tq:2 * BLOCK + tq, :] = kvn_ref[...]
    nblk = tq // BLOCK
    nwin = 3 * BLOCK
    nkey = nwin + ctx_len
    row = lax.broadcasted_iota(jnp.int32, (BLOCK, BLOCK), 0)
    col = lax.broadcasted_iota(jnp.int32, (BLOCK, BLOCK), 1)
    lane = lax.broadcasted_iota(jnp.int32, (BLOCK, LANES), 1)
    low = lane < HEAD_DIM
    head_a = (lane % HEAD_DIM) < AXIS_ROT
    my_lanes = (head_a, jnp.logical_not(head_a))
    den_lanes = (jnp.logical_not(low), low)

    chunks = [(g, t0) for g in range(GQA_GROUP) for t0 in range(0, BLOCK, SOFTMAX_ROWS)]

    def head_queries(j, kvh):
        q0 = pl.multiple_of(j * BLOCK, BLOCK)
        tiles = []
        for g in range(GQA_GROUP):
            pair = q_ref[pl.ds(q0, BLOCK), g * LANES:(g + 1) * LANES]
            tiles.append(jnp.where(my_lanes[kvh], pair, jnp.zeros_like(pair)))
        return tiles

    def window_masks(j):
        mask_l = (col >= row) & (i * tq + (j - 1) * BLOCK >= 0)
        mask_r = (col <= row) & (i * tq + (j + 1) * BLOCK < seq_len)
        return mask_l, mask_r

    def key_tiles(j, kvh):
        q0 = pl.multiple_of(j * BLOCK, BLOCK)
        return kv_ext[pl.ds(q0, nwin), 0:LANES], kvc_ref[:, 0:LANES]

    def scores(j, kvh):
        qs = jnp.concatenate(head_queries(j, kvh), axis=0)
        kw, kc = key_tiles(j, kvh)
        s_refs[kvh][:, 0:nwin] = _dot_nt(qs, kw)
        s_refs[kvh][:, nwin:nkey] = _dot_nt(qs, kc)

    def softmax(j, kvh):
        mask_l, mask_r = window_masks(j)

        def masked_pieces(rows, trows):
            pieces = [s_refs[kvh][rows, p * LANES:(p + 1) * LANES] for p in range(nkey // LANES)]
            pieces[0] = jnp.where(mask_l[trows], pieces[0], NEG)
            pieces[2] = jnp.where(mask_r[trows], pieces[2], NEG)
            return pieces

        for g, t0 in chunks:
            rows = slice(g * BLOCK + t0, g * BLOCK + t0 + SOFTMAX_ROWS)
            pieces = masked_pieces(rows, slice(t0, t0 + SOFTMAX_ROWS))
            m = pieces[0]
            for p in pieces[1:]:
                m = jnp.maximum(m, p)
            m = jnp.maximum(jnp.max(m, axis=1, keepdims=True), sink_ref[kvh * GQA_GROUP + g] * LOG2E)
            m_refs[kvh][rows, :] = jnp.broadcast_to(m, (SOFTMAX_ROWS, LANES))
        for g, t0 in chunks:
            rows = slice(g * BLOCK + t0, g * BLOCK + t0 + SOFTMAX_ROWS)
            pieces = masked_pieces(rows, slice(t0, t0 + SOFTMAX_ROWS))
            m = m_refs[kvh][rows, :]
            for p, piece in enumerate(pieces):
                e_refs[kvh][rows, p * LANES:(p + 1) * LANES] = jnp.exp2((piece - m).astype(BF16))

    def weighted_values(j, kvh, slot, fast):
        e_ref, m_ref = e_refs[2 * slot + kvh], m_refs[2 * slot + kvh]
        q0 = pl.multiple_of(j * BLOCK, BLOCK)
        vcols = slice((1 + kvh) * LANES, (2 + kvh) * LANES)
        values = jnp.concatenate([kv_ext[pl.ds(q0, nwin), vcols], kvc_ref[:, vcols]], axis=0)
        o = _dot(e_ref[...], values)
        on, smallest = [], None
        for g in range(GQA_GROUP):
            rows = slice(g * BLOCK, (g + 1) * BLOCK)
            sink = sink_ref[kvh * GQA_GROUP + g] * LOG2E
            if fast:
                sink_term = jnp.exp2(sink - m_ref[g * BLOCK:g * BLOCK + 1, :])
            else:
                sink_term = jnp.exp2(sink - m_ref[rows, :])
            total = o[rows] + sink_term
            smallest = total if smallest is None else jnp.minimum(smallest, total)
            on.append(o[rows] * pltpu.roll(1.0 / total, HEAD_DIM, 1))
        if fast:
            minden_ref[...] = jnp.minimum(minden_ref[...], jnp.where(den_lanes[kvh], smallest, 3e38))
        for pair in range(GQA_GROUP // 2):
            if kvh == 0:
                tile = jnp.where(low, on[2 * pair], pltpu.roll(on[2 * pair + 1], HEAD_DIM, 1))
            else:
                tile = jnp.where(low, pltpu.roll(on[2 * pair], HEAD_DIM, 1), on[2 * pair + 1])
            c0 = (kvh * GQA_GROUP + 2 * pair) * HEAD_DIM
            o_ref[pl.ds(q0, BLOCK), c0:c0 + LANES] = tile.astype(BF16)

    def exact_body(j, carry):
        scores(j, 0)
        scores(j, 1)
        softmax(j, 0)
        weighted_values(j, 0, 0, False)
        softmax(j, 1)
        weighted_values(j, 1, 0, False)
        return carry

    def max_sq_norms(k_tile):
        sq = k_tile.astype(F32)
        sq = sq * sq
        return [jnp.broadcast_to(jnp.max(jnp.sum(jnp.where(my_lanes[kvh], sq, 0.0), axis=1, keepdims=True),
                                         axis=0, keepdims=True), (8, LANES)) for kvh in range(N_KV_HEADS)]

    for kb in range(nblk + 2):
        for kvh, n2 in enumerate(max_sq_norms(kv_ext[kb * BLOCK:(kb + 1) * BLOCK, 0:LANES])):
            kn_ref[kvh, kb] = n2
    ctx_norm = None
    for cb in range(ctx_len // BLOCK):
        n2 = max_sq_norms(kvc_ref[cb * BLOCK:(cb + 1) * BLOCK, 0:LANES])
        ctx_norm = n2 if ctx_norm is None else [jnp.maximum(a, b) for a, b in zip(ctx_norm, n2)]
    minden_ref[...] = jnp.full(minden_ref.shape, 3e38, F32)

    def bounded_exp(j, kvh, slot):
        e_ref, m_ref = e_refs[2 * slot + kvh], m_refs[2 * slot + kvh]
        q0 = pl.multiple_of(j * BLOCK, BLOCK)
        kw, kc = key_tiles(j, kvh)
        kn2 = jnp.maximum(jnp.max(kn_ref[kvh, pl.ds(j, 3)], axis=0), ctx_norm[kvh])[0:1, :]
        tiles = head_queries(j, kvh)
        for g in range(GQA_GROUP):
            qf = tiles[g].astype(F32)
            qn2 = jnp.max(jnp.sum(qf * qf, axis=1, keepdims=True), axis=0, keepdims=True)
            bound = jnp.maximum(jnp.sqrt(qn2 * kn2), sink_ref[kvh * GQA_GROUP + g] * LOG2E)
            m_ref[g * BLOCK:(g + 1) * BLOCK, :] = jnp.broadcast_to(bound, (BLOCK, LANES))
        qs = jnp.concatenate(tiles, axis=0)
        s_w = _dot_nt(qs, kw)
        s_c = _dot_nt(qs, kc)
        mask_l, mask_r = window_masks(j)
        for g, t0 in chunks:
            rows = slice(g * BLOCK + t0, g * BLOCK + t0 + SOFTMAX_ROWS)
            trows = slice(t0, t0 + SOFTMAX_ROWS)
            m = m_ref[rows, :]
            pieces = ([s_w[rows, p * LANES:(p + 1) * LANES] for p in range(nwin // LANES)]
                      + [s_c[rows, p * LANES:(p + 1) * LANES] for p in range(ctx_len // LANES)])
            pieces[0] = jnp.where(mask_l[trows], pieces[0], NEG)
            pieces[2] = jnp.where(mask_r[trows], pieces[2], NEG)
            for p, piece in enumerate(pieces):
                e_ref[rows, p * LANES:(p + 1) * LANES] = jnp.exp2(piece - m).astype(BF16)

    bounded_exp(0, 0, 0)
    bounded_exp(0, 1, 0)

    def fast_blocks(j0, last):
        for k in range(FAST_UNROLL):
            for kvh in range(N_KV_HEADS):
                weighted_values(j0 + k, kvh, k % 2, True)
                if not (last and k + 1 == FAST_UNROLL):
                    bounded_exp(j0 + k + 1, kvh, (k + 1) % 2)

    def fast_body(jj, carry):
        fast_blocks(FAST_UNROLL * jj, False)
        return carry

    lax.fori_loop(0, nblk // FAST_UNROLL - 1, fast_body, 0)
    fast_blocks(nblk - FAST_UNROLL, True)
    smallest = jnp.min(minden_ref[...])

    @pl.when(smallest < 2.0 ** MIN_SAFE_DENOMINATOR_LOG2)
    def _():
        lax.fori_loop(0, nblk, exact_body, 0)


def _attention(q, kv, kvc, sink, tq):
    nb, seq_len, _ = q.shape
    ctx_len = kvc.shape[1]
    nblk = seq_len // BLOCK
    qpb = tq // BLOCK
    rows = GQA_GROUP * BLOCK
    nkey = 3 * BLOCK + ctx_len
    return pl.pallas_call(
        functools.partial(_attn_kernel, seq_len, tq, ctx_len),
        grid=(nb, seq_len // tq),
        in_specs=[
            pl.BlockSpec(memory_space=pltpu.SMEM),
            pl.BlockSpec((None, tq, Q_DIM), lambda b, i: (b, i, 0)),
            pl.BlockSpec((None, BLOCK, KV_W), lambda b, i: (b, jnp.maximum(i * qpb - 1, 0), 0)),
            pl.BlockSpec((None, tq, KV_W), lambda b, i: (b, i, 0)),
            pl.BlockSpec((None, BLOCK, KV_W),
                         lambda b, i: (b, jnp.minimum((i + 1) * qpb, nblk - 1), 0)),
            pl.BlockSpec((None, ctx_len, KV_W), lambda b, i: (b, 0, 0)),
        ],
        out_specs=pl.BlockSpec((None, tq, Q_DIM), lambda b, i: (b, i, 0)),
        out_shape=jax.ShapeDtypeStruct((nb, seq_len, Q_DIM), BF16),
        scratch_shapes=([pltpu.VMEM((tq + 2 * BLOCK, KV_W), BF16)]
                        + [pltpu.VMEM((rows, nkey), F32)] * N_KV_HEADS
                        + [pltpu.VMEM((rows, nkey), BF16)] * (2 * N_KV_HEADS)
                        + [pltpu.VMEM((rows, LANES), F32)] * (2 * N_KV_HEADS)
                        + [pltpu.VMEM((N_KV_HEADS, qpb + 2, 8, LANES), F32),
                           pltpu.VMEM((BLOCK, LANES), F32)]),
        compiler_params=pltpu.CompilerParams(
            dimension_semantics=("arbitrary", "arbitrary"), vmem_limit_bytes=VMEM_LIMIT),
        name="attention",
    )(sink, q, kv, kv, kv, kvc)


def _post_kernel(layer, x_ref, o_ref_in, mod_ref, gmlp_ref, gfin_ref, wo_ref, w1_ref, w2_ref, out_ref):
    g1 = mod_ref[:, 2 * D_MODEL:3 * D_MODEL]
    x1 = x_ref[...] + g1 * _dot(o_ref_in[...], wo_ref[...])
    x2 = _mlp_residual(x1, mod_ref, gmlp_ref[layer:layer + 1, :], w1_ref, w2_ref)
    y = x2 * lax.rsqrt(jnp.mean(x2 * x2, axis=-1, keepdims=True) + EPS)
    out_ref[...] = y * gfin_ref[...]


def _post(x, o, mod, layer, tm, gmlp, gfin, w_o, w1, w2):
    nb, seq_len, _ = x.shape
    return pl.pallas_call(
        functools.partial(_post_kernel, layer),
        grid=(nb, seq_len // tm),
        in_specs=[
            pl.BlockSpec((None, tm, D_MODEL), lambda b, i: (b, i, 0)),
            pl.BlockSpec((None, tm, Q_DIM), lambda b, i: (b, i, 0)),
            pl.BlockSpec((None, None, 1, N_MOD * D_MODEL), lambda b, i: (layer, b, 0, 0)),
            _layer_rows_spec(),
            _const_spec((1, D_MODEL)),
            _const_spec((Q_DIM, D_MODEL)),
            _const_spec((D_MODEL, D_FF)),
            _const_spec((D_FF, D_MODEL)),
        ],
        out_specs=pl.BlockSpec((None, tm, D_MODEL), lambda b, i: (b, i, 0)),
        out_shape=jax.ShapeDtypeStruct(x.shape, F32),
        compiler_params=pltpu.CompilerParams(
            dimension_semantics=("arbitrary", "arbitrary"), vmem_limit_bytes=VMEM_LIMIT),
        name="post_attn",
    )(x, o, mod, gmlp, gfin, w_o, w1, w2)


def _rope_tables(seq_len):
    pos = np.arange(seq_len)
    inv = ROPE_BASE ** (-np.arange(0, AXIS_ROT, 2, dtype=np.float32) / AXIS_ROT)
    ang = np.concatenate([(pos // GRID_W).astype(np.float32)[:, None] * inv[None],
                          (pos % GRID_W).astype(np.float32)[:, None] * inv[None]], axis=-1)
    cos = np.tile(np.cos(ang.astype(np.float64)), (1, LANES // AXIS_ROT))
    sin = np.sin(ang.astype(np.float64))
    return (jnp.asarray(cos, F32), jnp.asarray(np.concatenate([-sin, -sin, sin, sin], axis=-1), F32))


def kernel(x, c, ctx, c_ctx, ada_w, ada_b, norm_mix_g, norm_mlp_g, pool_w_in, pool_w_grp,
           pool_scale, pool_w_out, attn_w_qkv, attn_sink, attn_w_o, mlp_w1, mlp_w2, final_g):
    nb, seq_len, _ = x.shape
    ctx_len = ctx.shape[1]
    ctx_row = nb
    c8 = jnp.concatenate(
        [c, c_ctx[None], jnp.zeros((MOD_ROWS - nb - 1, D_MODEL), F32)], axis=0)
    n_grp = len(POOL_WINDOWS)
    layer0_weights = ((pool_w_in, 0), (pool_w_grp.reshape(-1, n_grp * POOL_GROUP_DIM, POOL_GROUP_DIM), 0),
                      (pool_w_out, 0), (mlp_w1, 0), (mlp_w2, 0))
    mod, (w_in, w_grp, w_out, w1_0, w2_0) = _modulation(c8, ada_w, ada_b, layer0_weights)

    gmix, gmlp = norm_mix_g, norm_mlp_g

    l0 = (gmix, gmlp, pool_scale, w_in,
          w_grp.reshape(n_grp, POOL_GROUP_DIM, POOL_GROUP_DIM), w_out, w1_0, w2_0)
    layer1_weights = ((attn_w_qkv, 0), (attn_w_o, 0), (mlp_w1, 1), (mlp_w2, 1))
    x1, (w_qkv, w_o, w1, w2) = _layer0(x, mod, None, 0, LAYER_TILE, True, layer1_weights, *l0)
    ctx1, _ = _layer0(ctx, mod, ctx_row, 0, ctx_len, False, (), *l0)

    cos, sin = _rope_tables(seq_len)
    q, kv = _qkv(x1, mod, 1, QKV_TILE, gmix, w_qkv, cos, sin)
    kvc = _ctx_kv(ctx1, mod, ctx_row, 1, gmix, w_qkv)
    o = _attention(q, kv, kvc, attn_sink[0], ATTN_TILE)
    return _post(x1, o, mod, 1, LAYER_TILE, gmlp, final_g.reshape(1, D_MODEL), w_o, w1, w2)
```

```python
import functools

import numpy as np
import jax
import jax.numpy as jnp
from jax import lax
from jax.experimental import pallas as pl
from jax.experimental.pallas import tpu as pltpu

D_MODEL = 1024
DEPTH = 2
GRID_W = 64
POOL_WINDOWS = (2, 4, 8, 16)
POOL_GROUP_DIM = D_MODEL // len(POOL_WINDOWS)
POOL_HALO = 8
HEAD_DIM = 64
N_HEADS = D_MODEL // HEAD_DIM
N_KV_HEADS = 2
GQA_GROUP = N_HEADS // N_KV_HEADS
Q_DIM = N_HEADS * HEAD_DIM
KV_DIM = N_KV_HEADS * HEAD_DIM
WINDOW = 128
BLOCK = 128
ROPE_BASE = 10000.0
AXIS_ROT = HEAD_DIM // 2
ROT_HALF = AXIS_ROT // 2
D_FF = 4 * D_MODEL
N_MOD = 6
EPS = 1e-6
NEG = -1e30
MOD_ROWS = 8
LANES = 128
LOG2E = 1.4426950408889634
KV_W = 3 * LANES
SOFTMAX_ROWS = 32
MIN_SAFE_DENOMINATOR_LOG2 = -64.0
QKV_COL_GROUP = 256
FAST_UNROLL = 2
VMEM_LIMIT = 56 * 1024 * 1024

LAYER_TILE = 512
QKV_TILE = 1024
ATTN_TILE = 2048
MOD_COLS = 1536

BF16 = jnp.bfloat16
F32 = jnp.float32


def _const_spec(shape):
    nd = len(shape)
    return pl.BlockSpec(shape, lambda *_: (0,) * nd, pipeline_mode=pl.Buffered(1))


def _layer_rows_spec():
    return _const_spec((DEPTH, D_MODEL))


def _dot(a, b):
    return jnp.dot(a, b, preferred_element_type=F32)


def _rms_mod(x, g, shift, scale):
    y = x * lax.rsqrt(jnp.mean(x * x, axis=-1, keepdims=True) + EPS)
    return y * (g * (1.0 + scale)) + shift


def _mlp_residual(x, mod_ref, gain, w1_ref, w2_ref):
    sh2 = mod_ref[:, 3 * D_MODEL:4 * D_MODEL]
    sc2 = mod_ref[:, 4 * D_MODEL:5 * D_MODEL]
    g2 = mod_ref[:, 5 * D_MODEL:6 * D_MODEL]
    h = _rms_mod(x, gain, sh2, sc2).astype(BF16)
    a = jnp.maximum(_dot(h, w1_ref[...]), 0.0)
    a = (a * a).astype(BF16)
    return x + g2 * _dot(a, w2_ref[...])


def _staged_steps(step, n_tiles, n_stages, body):
    assert n_tiles >= n_stages
    for t in range(n_stages - 1):
        pl.when(step == t)(functools.partial(body, *[s <= t for s in range(n_stages)]))
    pl.when((step >= n_stages - 1) & (step < n_tiles))(functools.partial(body, *[True] * n_stages))
    for k in range(n_stages - 1):
        pl.when(step == n_tiles + k)(functools.partial(body, *[s > k for s in range(n_stages)]))


def _skewed_steps(step, n_tiles, body):
    _staged_steps(step, n_tiles, 2, body)


def _stage_tile(n_tiles, tiles_per_row, stage):
    def tile(t):
        k = jnp.clip(t - stage, 0, n_tiles - 1)
        return k // tiles_per_row, k % tiles_per_row
    return tile


def _skew_maps(n_tiles, tiles_per_row):
    return _stage_tile(n_tiles, tiles_per_row, 0), _stage_tile(n_tiles, tiles_per_row, 1)


def _mod_kernel(n_prep, c_ref, w_ref, b_ref, *refs):
    prep_in, o_ref, prep_out = refs[:n_prep], refs[n_prep], refs[n_prep + 1:]
    c = c_ref[...]
    s = (c * jax.nn.sigmoid(c)).astype(BF16)
    r = _dot(s, w_ref[...].astype(BF16)) + b_ref[...]
    for row in range(MOD_ROWS):
        o_ref[row] = r[row:row + 1, :]
    for src, dst in zip(prep_in, prep_out):
        dst[...] = src[...].astype(BF16)


def _modulation(c8, ada_w, ada_b, first_layer_weights):
    tn = MOD_COLS
    n = N_MOD * D_MODEL
    n_col = n // tn
    steps = DEPTH * n_col
    prep_in_specs, prep_out_specs = [], []
    for w, idx in first_layer_weights:
        rows = w.shape[1] // steps
        prep_in_specs.append(pl.BlockSpec(
            (None, rows, w.shape[2]), lambda i, j, idx=idx: (idx, i * n_col + j, 0)))
        prep_out_specs.append(pl.BlockSpec((rows, w.shape[2]), lambda i, j: (i * n_col + j, 0)))
    outs = pl.pallas_call(
        functools.partial(_mod_kernel, len(first_layer_weights)),
        grid=(DEPTH, n_col),
        in_specs=[
            pl.BlockSpec((MOD_ROWS, D_MODEL), lambda i, j: (0, 0)),
            pl.BlockSpec((None, D_MODEL, tn), lambda i, j: (i, 0, j)),
            pl.BlockSpec((None, 1, tn), lambda i, j: (i, 0, j)),
        ] + prep_in_specs,
        out_specs=[pl.BlockSpec((None, MOD_ROWS, 1, tn), lambda i, j: (i, 0, 0, j))] + prep_out_specs,
        out_shape=[jax.ShapeDtypeStruct((DEPTH, MOD_ROWS, 1, n), F32)]
        + [jax.ShapeDtypeStruct(w.shape[1:], BF16) for w, _ in first_layer_weights],
        compiler_params=pltpu.CompilerParams(
            dimension_semantics=("arbitrary", "arbitrary"), vmem_limit_bytes=VMEM_LIMIT),
        name="modulation",
    )(c8, ada_w, ada_b.reshape(DEPTH, 1, n), *[w for w, _ in first_layer_weights])
    return outs[0], tuple(outs[1:])


def _to_pair_layout(v):
    lane = lax.broadcasted_iota(jnp.int32, v.shape, 1)
    for width in (ROT_HALF, AXIS_ROT):
        hi = (lane // (2 * width)) % 2
        lo = (lane // width) % 2
        v = jnp.where(hi == lo, v, jnp.where(hi == 0, pltpu.roll(v, LANES - width, 1),
                                             pltpu.roll(v, width, 1)))
    return v


def _prepare_next_weights(prep_in, prep_out):
    wqkv_in, wqkv_out = prep_in[0], prep_out[0]
    low = lax.broadcasted_iota(jnp.int32, (wqkv_in.shape[0], LANES), 1) < HEAD_DIM
    for p in range(GQA_GROUP):
        first = wqkv_in[:, (p // 2) * LANES:(p // 2 + 1) * LANES]
        second = wqkv_in[:, (p // 2 + GQA_GROUP // 2) * LANES:(p // 2 + GQA_GROUP // 2 + 1) * LANES]
        if p % 2 == 0:
            both = jnp.where(low, first, pltpu.roll(second, HEAD_DIM, 1))
        else:
            both = jnp.where(low, pltpu.roll(first, HEAD_DIM, 1), second)
        wqkv_out[:, p * LANES:(p + 1) * LANES] = _to_pair_layout(both).astype(BF16)
    wqkv_out[:, Q_DIM:Q_DIM + KV_DIM] = _to_pair_layout(wqkv_in[:, Q_DIM:Q_DIM + KV_DIM]).astype(BF16)
    wqkv_out[:, Q_DIM + KV_DIM:] = wqkv_in[:, Q_DIM + KV_DIM:].astype(BF16)
    for src, dst in zip(prep_in[1:], prep_out[1:]):
        dst[...] = src[...].astype(BF16)


def _layer0_kernel(seq_len, tm, n_tiles, skew, n_prep, layer, *refs):
    (x_ref, xp_ref, xn_ref, moda_ref, modb_ref, gmix_ref, gmlp_ref, pscale_ref, win_ref, wgrp_ref,
     wout_ref, w1_ref, w2_ref) = refs[:13]
    prep_in = refs[13:13 + n_prep]
    o_ref = refs[13 + n_prep]
    prep_out = refs[14 + n_prep:14 + 2 * n_prep]
    u_ref, d_ref, x1_ref, h2_ref = refs[14 + 2 * n_prep:]
    step = pl.program_id(0)
    i = jnp.minimum(step, n_tiles - 1) % (seq_len // tm)
    half = D_FF // 2

    def mlp_up(h2, c0, c1):
        a = jnp.maximum(_dot(h2, w1_ref[:, c0:c1]), 0.0)
        return (a * a).astype(BF16)

    def body(do_mixer, do_prev_mlp):
        if do_prev_mlp:
            g2 = modb_ref[:, 5 * D_MODEL:6 * D_MODEL]
            h2_prev = h2_ref[...]
            a_lo = mlp_up(h2_prev, 0, half)

        if do_mixer:
            sh1 = moda_ref[:, 0:D_MODEL]
            sc1 = moda_ref[:, D_MODEL:2 * D_MODEL]
            x = x_ref[...]
            xe = jnp.concatenate([xp_ref[...], x, xn_ref[...]], axis=0)
            u = _dot(_rms_mod(xe, gmix_ref[layer:layer + 1, :], sh1, sc1).astype(BF16), win_ref[...])
            u_ref[0:POOL_HALO, :] = jnp.where(i > 0, u[0:POOL_HALO], 0.0)
            u_ref[POOL_HALO:POOL_HALO + tm, :] = u[POOL_HALO:POOL_HALO + tm]
            u_ref[POOL_HALO + tm:POOL_HALO + tm + POOL_HALO, :] = jnp.where(
                (i + 1) * tm < seq_len, u[POOL_HALO + tm:], 0.0)

        if do_prev_mlp:
            a_hi = mlp_up(h2_prev, half, D_FF)

        if do_mixer:
            t = i * tm + lax.broadcasted_iota(jnp.int32, (tm, 1), 0)
            for g, w in enumerate(POOL_WINDOWS):
                c0, c1 = g * POOL_GROUP_DIM, (g + 1) * POOL_GROUP_DIM
                acc = u_ref[POOL_HALO - w // 2:POOL_HALO - w // 2 + tm, c0:c1]
                for s in range(-w // 2 + 1, w // 2):
                    acc = acc + u_ref[POOL_HALO + s:POOL_HALO + s + tm, c0:c1]
                lo = jnp.clip(t - w // 2, 0, seq_len)
                hi = jnp.clip(t + w // 2, 0, seq_len)
                inv_cnt = 1.0 / (hi - lo).astype(F32)
                d = acc * inv_cnt - u_ref[POOL_HALO:POOL_HALO + tm, c0:c1]
                d_ref[:, c0:c1] = _dot(d.astype(BF16), wgrp_ref[g])
            y = _dot((d_ref[...] * pscale_ref[...]).astype(BF16), wout_ref[...])

        if do_prev_mlp:
            down = _dot(jnp.concatenate([a_lo, a_hi], axis=1), w2_ref[...])

        if do_mixer:
            g1 = moda_ref[:, 2 * D_MODEL:3 * D_MODEL]
            sh2 = moda_ref[:, 3 * D_MODEL:4 * D_MODEL]
            sc2 = moda_ref[:, 4 * D_MODEL:5 * D_MODEL]
            x1 = x + g1 * y
            h2 = _rms_mod(x1, gmlp_ref[layer:layer + 1, :], sh2, sc2).astype(BF16)

        if do_prev_mlp:
            o_ref[...] = x1_ref[...] + g2 * down
        if do_mixer and n_prep:
            _prepare_next_weights(prep_in, prep_out)
        if do_mixer and skew:
            x1_ref[...] = x1
            h2_ref[...] = h2
        if do_mixer and not skew:
            g2_now = moda_ref[:, 5 * D_MODEL:6 * D_MODEL]
            o_ref[...] = x1 + g2_now * _dot(mlp_up(h2, 0, D_FF), w2_ref[...])

    if skew:
        _skewed_steps(step, n_tiles, body)
    else:
        body(True, False)


def _layer0(x, mod, mod_row, layer, tm, skew, next_weights, gmix, gmlp, pscale, w_in, w_grp, w_out, w1, w2):
    nb, seq_len, _ = x.shape
    rows8 = seq_len // POOL_HALO
    tpb = tm // POOL_HALO
    nt = seq_len // tm
    n_tiles = nb * nt
    tile_a, tile_b = _skew_maps(n_tiles, nt)
    if not skew:
        tile_b = tile_a

    def x_map(t):
        b, i = tile_a(t)
        return (b, i, 0)

    def prev_map(t):
        b, i = tile_a(t)
        return (b, jnp.maximum(i * tpb - 1, 0), 0)

    def next_map(t):
        b, i = tile_a(t)
        return (b, jnp.minimum((i + 1) * tpb, rows8 - 1), 0)

    def moda_map(t):
        return (layer, tile_a(t)[0] if mod_row is None else mod_row, 0, 0)

    def modb_map(t):
        return (layer, tile_b(t)[0] if mod_row is None else mod_row, 0, 0)

    def out_map(t):
        b, i = tile_b(t)
        return (b, i, 0)

    prep_in_specs, prep_out_specs = [], []
    for w, idx in next_weights:
        rows = w.shape[1] // n_tiles
        prep_in_specs.append(pl.BlockSpec(
            (None, rows, w.shape[2]), lambda t, idx=idx: (idx, jnp.minimum(t, n_tiles - 1), 0)))
        prep_out_specs.append(pl.BlockSpec(
            (rows, w.shape[2]), lambda t: (jnp.minimum(t, n_tiles - 1), 0)))

    outs = pl.pallas_call(
        functools.partial(_layer0_kernel, seq_len, tm, n_tiles, skew, len(next_weights), layer),
        grid=(n_tiles + (1 if skew else 0),),
        in_specs=[
            pl.BlockSpec((None, tm, D_MODEL), x_map),
            pl.BlockSpec((None, POOL_HALO, D_MODEL), prev_map),
            pl.BlockSpec((None, POOL_HALO, D_MODEL), next_map),
            pl.BlockSpec((None, None, 1, N_MOD * D_MODEL), moda_map),
            pl.BlockSpec((None, None, 1, N_MOD * D_MODEL), modb_map),
            _layer_rows_spec(),
            _layer_rows_spec(),
            _const_spec((1, D_MODEL)),
            _const_spec((D_MODEL, D_MODEL)),
            _const_spec((len(POOL_WINDOWS), POOL_GROUP_DIM, POOL_GROUP_DIM)),
            _const_spec((D_MODEL, D_MODEL)),
            _const_spec((D_MODEL, D_FF)),
            _const_spec((D_FF, D_MODEL)),
        ] + prep_in_specs,
        out_specs=[pl.BlockSpec((None, tm, D_MODEL), out_map)] + prep_out_specs,
        out_shape=[jax.ShapeDtypeStruct(x.shape, F32)]
        + [jax.ShapeDtypeStruct(w.shape[1:], BF16) for w, _ in next_weights],
        scratch_shapes=[pltpu.VMEM((tm + 2 * POOL_HALO, D_MODEL), F32),
                        pltpu.VMEM((tm, D_MODEL), F32),
                        pltpu.VMEM((tm, D_MODEL), F32),
                        pltpu.VMEM((tm, D_MODEL), BF16)],
        compiler_params=pltpu.CompilerParams(
            dimension_semantics=("arbitrary",), vmem_limit_bytes=VMEM_LIMIT),
        name="layer0",
    )(x, x, x, mod, mod, gmix, gmlp, pscale, w_in, w_grp, w_out, w1, w2, *[w for w, _ in next_weights])
    return outs[0], tuple(outs[1:])


def _store_kv(kv_ref, k, v):
    low = lax.broadcasted_iota(jnp.int32, v.shape, 1) < HEAD_DIM
    kv_ref[:, 0:LANES] = k.astype(BF16)
    kv_ref[:, LANES:2 * LANES] = jnp.where(low, v, 1.0).astype(BF16)
    kv_ref[:, 2 * LANES:3 * LANES] = jnp.where(low, 1.0, v).astype(BF16)


def _qkv_kernel(n_tiles, layer, x_ref, mod_ref, gmix_ref, w_ref, cos_ref, sin_ref, q_ref, kv_ref, h_ref):
    def body(do_norm, do_project):
        if do_project:
            h_prev = h_ref[...]
            q_groups = [_dot(h_prev, w_ref[:, c:c + QKV_COL_GROUP]) for c in range(0, Q_DIM, QKV_COL_GROUP)]
            kv = _dot(h_prev, w_ref[:, Q_DIM:])
        if do_norm:
            sh1 = mod_ref[:, 0:D_MODEL]
            sc1 = mod_ref[:, D_MODEL:2 * D_MODEL]
            h = _rms_mod(x_ref[...], gmix_ref[layer:layer + 1, :], sh1, sc1).astype(BF16)
        if do_project:
            cos = cos_ref[...]
            sin = sin_ref[...]
            qscale = HEAD_DIM ** -0.5 * LOG2E
            cos_q = cos * qscale
            sin_q = sin * qscale

            def rope(t, c, s):
                return t * c + pltpu.roll(t, HEAD_DIM, 1) * s

            for gi, group in enumerate(q_groups):
                for cb in range(QKV_COL_GROUP // LANES):
                    c0 = gi * QKV_COL_GROUP + cb * LANES
                    q_ref[:, c0:c0 + LANES] = rope(group[:, cb * LANES:(cb + 1) * LANES], cos_q, sin_q).astype(BF16)
            _store_kv(kv_ref, rope(kv[:, 0:KV_DIM], cos, sin), kv[:, KV_DIM:])
        if do_norm:
            h_ref[...] = h

    _skewed_steps(pl.program_id(0), n_tiles, body)


def _qkv(x, mod, layer, tm, gmix, w_qkv, cos, sin):
    nb, seq_len, _ = x.shape
    nt = seq_len // tm
    n_tiles = nb * nt
    tile_a, tile_b = _skew_maps(n_tiles, nt)

    def out_map(t):
        b, i = tile_b(t)
        return (b, i, 0)

    return pl.pallas_call(
        functools.partial(_qkv_kernel, n_tiles, layer),
        grid=(n_tiles + 1,),
        in_specs=[
            pl.BlockSpec((None, tm, D_MODEL), lambda t: (*tile_a(t), 0)),
            pl.BlockSpec((None, None, 1, N_MOD * D_MODEL), lambda t: (layer, tile_a(t)[0], 0, 0)),
            _layer_rows_spec(),
            _const_spec((D_MODEL, Q_DIM + 2 * KV_DIM)),
            pl.BlockSpec((tm, LANES), lambda t: (tile_b(t)[1], 0)),
            pl.BlockSpec((tm, LANES), lambda t: (tile_b(t)[1], 0)),
        ],
        out_specs=[pl.BlockSpec((None, tm, Q_DIM), out_map),
                   pl.BlockSpec((None, tm, KV_W), out_map)],
        out_shape=[jax.ShapeDtypeStruct((nb, seq_len, Q_DIM), BF16),
                   jax.ShapeDtypeStruct((nb, seq_len, KV_W), BF16)],
        scratch_shapes=[pltpu.VMEM((tm, D_MODEL), BF16)],
        compiler_params=pltpu.CompilerParams(
            dimension_semantics=("arbitrary",), vmem_limit_bytes=VMEM_LIMIT),
        name="qkv_rope",
    )(x, mod, gmix, w_qkv, cos, sin)


def _ctx_kv_kernel(layer, x_ref, mod_ref, gmix_ref, w_ref, kv_ref):
    sh1 = mod_ref[:, 0:D_MODEL]
    sc1 = mod_ref[:, D_MODEL:2 * D_MODEL]
    h = _rms_mod(x_ref[...], gmix_ref[layer:layer + 1, :], sh1, sc1).astype(BF16)
    kv = _dot(h, w_ref[...])
    _store_kv(kv_ref, kv[:, 0:KV_DIM], kv[:, KV_DIM:])


def _ctx_kv(ctx, mod, mod_row, layer, gmix, w_kv):
    nb, ctx_len, _ = ctx.shape
    return pl.pallas_call(
        functools.partial(_ctx_kv_kernel, layer),
        grid=(nb,),
        in_specs=[
            pl.BlockSpec((None, ctx_len, D_MODEL), lambda b: (b, 0, 0)),
            pl.BlockSpec((None, None, 1, N_MOD * D_MODEL), lambda b: (layer, mod_row, 0, 0)),
            _layer_rows_spec(),
            pl.BlockSpec((D_MODEL, 2 * KV_DIM), lambda b: (0, Q_DIM // (2 * KV_DIM)),
                         pipeline_mode=pl.Buffered(1)),
        ],
        out_specs=pl.BlockSpec((None, ctx_len, KV_W), lambda b: (b, 0, 0)),
        out_shape=jax.ShapeDtypeStruct((nb, ctx_len, KV_W), BF16),
        compiler_params=pltpu.CompilerParams(
            dimension_semantics=("arbitrary",), vmem_limit_bytes=VMEM_LIMIT),
        name="ctx_kv",
    )(ctx, mod, gmix, w_kv)


def _dot_nt(a, b):
    return lax.dot_general(a, b, (((1,), (1,)), ((), ())), preferred_element_type=F32)


def _attn_kernel(seq_len, tq, ctx_len, sink_ref, q_ref, kvp_ref, kvm_ref, kvn_ref, kvc_ref, o_ref,
                 kv_ext, *bufs):
    s_refs, e_refs, m_refs = bufs[0:2], bufs[2:6], bufs[6:10]
    kn_ref, minden_ref = bufs[10], bufs[11]
    i = pl.program_id(1)
    kv_ext[0:BLOCK, :] = kvp_ref[...]
    kv_ext[BLOCK:BLOCK + tq, :] = kvm_ref[...]
    kv_ext[BLOCK + tq:2 * BLOCK + tq, :] = kvn_ref[...]
    nblk = tq // BLOCK
    nwin = 3 * BLOCK
    nkey = nwin + ctx_len
    row = lax.broadcasted_iota(jnp.int32, (BLOCK, BLOCK), 0)
    col = lax.broadcasted_iota(jnp.int32, (BLOCK, BLOCK), 1)
    lane = lax.broadcasted_iota(jnp.int32, (BLOCK, LANES), 1)
    low = lane < HEAD_DIM
    head_a = (lane % HEAD_DIM) < AXIS_ROT
    my_lanes = (head_a, jnp.logical_not(head_a))
    den_lanes = (jnp.logical_not(low), low)

    chunks = [(g, t0) for g in range(GQA_GROUP) for t0 in range(0, BLOCK, SOFTMAX_ROWS)]

    def head_queries(j, kvh):
        q0 = pl.multiple_of(j * BLOCK, BLOCK)
        tiles = []
        for g in range(GQA_GROUP):
            pair = q_ref[pl.ds(q0, BLOCK), g * LANES:(g + 1) * LANES]
            tiles.append(jnp.where(my_lanes[kvh], pair, jnp.zeros_like(pair)))
        return tiles

    def window_masks(j):
        mask_l = (col >= row) & (i * tq + (j - 1) * BLOCK >= 0)
        mask_r = (col <= row) & (i * tq + (j + 1) * BLOCK < seq_len)
        return mask_l, mask_r

    def key_tiles(j, kvh):
        q0 = pl.multiple_of(j * BLOCK, BLOCK)
        return kv_ext[pl.ds(q0, nwin), 0:LANES], kvc_ref[:, 0:LANES]

    def scores(j, kvh):
        qs = jnp.concatenate(head_queries(j, kvh), axis=0)
        kw, kc = key_tiles(j, kvh)
        s_refs[kvh][:, 0:nwin] = _dot_nt(qs, kw)
        s_refs[kvh][:, nwin:nkey] = _dot_nt(qs, kc)

    def softmax(j, kvh):
        mask_l, mask_r = window_masks(j)

        def masked_pieces(rows, trows):
            pieces = [s_refs[kvh][rows, p * LANES:(p + 1) * LANES] for p in range(nkey // LANES)]
            pieces[0] = jnp.where(mask_l[trows], pieces[0], NEG)
            pieces[2] = jnp.where(mask_r[trows], pieces[2], NEG)
            return pieces

        for g, t0 in chunks:
            rows = slice(g * BLOCK + t0, g * BLOCK + t0 + SOFTMAX_ROWS)
            pieces = masked_pieces(rows, slice(t0, t0 + SOFTMAX_ROWS))
            m = pieces[0]
            for p in pieces[1:]:
                m = jnp.maximum(m, p)
            m = jnp.maximum(jnp.max(m, axis=1, keepdims=True), sink_ref[kvh * GQA_GROUP + g] * LOG2E)
            m_refs[kvh][rows, :] = jnp.broadcast_to(m, (SOFTMAX_ROWS, LANES))
        for g, t0 in chunks:
            rows = slice(g * BLOCK + t0, g * BLOCK + t0 + SOFTMAX_ROWS)
            pieces = masked_pieces(rows, slice(t0, t0 + SOFTMAX_ROWS))
            m = m_refs[kvh][rows, :]
            for p, piece in enumerate(pieces):
                e_refs[kvh][rows, p * LANES:(p + 1) * LANES] = jnp.exp2((piece - m).astype(BF16))

    def weighted_values(j, kvh, slot, fast):
        e_ref, m_ref = e_refs[2 * slot + kvh], m_refs[2 * slot + kvh]
        q0 = pl.multiple_of(j * BLOCK, BLOCK)
        vcols = slice((1 + kvh) * LANES, (2 + kvh) * LANES)
        values = jnp.concatenate([kv_ext[pl.ds(q0, nwin), vcols], kvc_ref[:, vcols]], axis=0)
        o = _dot(e_ref[...], values)
        on, smallest = [], None
        for g in range(GQA_GROUP):
            rows = slice(g * BLOCK, (g + 1) * BLOCK)
            sink = sink_ref[kvh * GQA_GROUP + g] * LOG2E
            if fast:
                sink_term = jnp.exp2(sink - m_ref[g * BLOCK:g * BLOCK + 1, :])
            else:
                sink_term = jnp.exp2(sink - m_ref[rows, :])
            total = o[rows] + sink_term
            smallest = total if smallest is None else jnp.minimum(smallest, total)
            on.append(o[rows] * pltpu.roll(1.0 / total, HEAD_DIM, 1))
        if fast:
            minden_ref[...] = jnp.minimum(minden_ref[...], jnp.where(den_lanes[kvh], smallest, 3e38))
        for pair in range(GQA_GROUP // 2):
            if kvh == 0:
                tile = jnp.where(low, on[2 * pair], pltpu.roll(on[2 * pair + 1], HEAD_DIM, 1))
            else:
                tile = jnp.where(low, pltpu.roll(on[2 * pair], HEAD_DIM, 1), on[2 * pair + 1])
            c0 = (kvh * GQA_GROUP + 2 * pair) * HEAD_DIM
            o_ref[pl.ds(q0, BLOCK), c0:c0 + LANES] = tile.astype(BF16)

    def exact_body(j, carry):
        scores(j, 0)
        scores(j, 1)
        softmax(j, 0)
        weighted_values(j, 0, 0, False)
        softmax(j, 1)
        weighted_values(j, 1, 0, False)
        return carry

    def max_sq_norms(k_tile):
        sq = k_tile.astype(F32)
        sq = sq * sq
        return [jnp.broadcast_to(jnp.max(jnp.sum(jnp.where(my_lanes[kvh], sq, 0.0), axis=1, keepdims=True),
                                         axis=0, keepdims=True), (8, LANES)) for kvh in range(N_KV_HEADS)]

    def store_key_norms(first, last):
        for kb in range(first, last):
            for kvh, n2 in enumerate(max_sq_norms(kv_ext[kb * BLOCK:(kb + 1) * BLOCK, 0:LANES])):
                kn_ref[kvh, kb] = n2

    store_key_norms(0, 3)
    ctx_norm = None
    for cb in range(ctx_len // BLOCK):
        n2 = max_sq_norms(kvc_ref[cb * BLOCK:(cb + 1) * BLOCK, 0:LANES])
        ctx_norm = n2 if ctx_norm is None else [jnp.maximum(a, b) for a, b in zip(ctx_norm, n2)]
    minden_ref[...] = jnp.full(minden_ref.shape, 3e38, F32)

    def bounded_exp(j, kvh, slot):
        e_ref, m_ref = e_refs[2 * slot + kvh], m_refs[2 * slot + kvh]
        q0 = pl.multiple_of(j * BLOCK, BLOCK)
        kw, kc = key_tiles(j, kvh)
        kn2 = jnp.maximum(jnp.max(kn_ref[kvh, pl.ds(j, 3)], axis=0), ctx_norm[kvh])[0:1, :]
        tiles = head_queries(j, kvh)
        for g in range(GQA_GROUP):
            qf = tiles[g].astype(F32)
            qn2 = jnp.max(jnp.sum(qf * qf, axis=1, keepdims=True), axis=0, keepdims=True)
            bound = jnp.maximum(jnp.sqrt(qn2 * kn2), sink_ref[kvh * GQA_GROUP + g] * LOG2E)
            m_ref[g * BLOCK:(g + 1) * BLOCK, :] = jnp.broadcast_to(bound, (BLOCK, LANES))
        qs = jnp.concatenate(tiles, axis=0)
        s_w = _dot_nt(qs, kw)
        s_c = _dot_nt(qs, kc)
        mask_l, mask_r = window_masks(j)
        for g, t0 in chunks:
            rows = slice(g * BLOCK + t0, g * BLOCK + t0 + SOFTMAX_ROWS)
            trows = slice(t0, t0 + SOFTMAX_ROWS)
            m = m_ref[rows, :]
            pieces = ([s_w[rows, p * LANES:(p + 1) * LANES] for p in range(nwin // LANES)]
                      + [s_c[rows, p * LANES:(p + 1) * LANES] for p in range(ctx_len // LANES)])
            pieces[0] = jnp.where(mask_l[trows], pieces[0], NEG)
            pieces[2] = jnp.where(mask_r[trows], pieces[2], NEG)
            for p, piece in enumerate(pieces):
                e_ref[rows, p * LANES:(p + 1) * LANES] = jnp.exp2(piece - m).astype(BF16)

    bounded_exp(0, 0, 0)
    bounded_exp(0, 1, 0)
    store_key_norms(3, nblk + 2)

    def fast_blocks(j0, last):
        for k in range(FAST_UNROLL):
            for kvh in range(N_KV_HEADS):
                weighted_values(j0 + k, kvh, k % 2, True)
                if not (last and k + 1 == FAST_UNROLL):
                    bounded_exp(j0 + k + 1, kvh, (k + 1) % 2)

    def fast_body(jj, carry):
        fast_blocks(FAST_UNROLL * jj, False)
        return carry

    lax.fori_loop(0, nblk // FAST_UNROLL - 1, fast_body, 0)
    fast_blocks(nblk - FAST_UNROLL, True)
    smallest = jnp.min(minden_ref[...])

    @pl.when(jnp.logical_not(smallest >= 2.0 ** MIN_SAFE_DENOMINATOR_LOG2))
    def _():
        lax.fori_loop(0, nblk, exact_body, 0)


def _attention(q, kv, kvc, sink, tq):
    nb, seq_len, _ = q.shape
    ctx_len = kvc.shape[1]
    nblk = seq_len // BLOCK
    qpb = tq // BLOCK
    rows = GQA_GROUP * BLOCK
    nkey = 3 * BLOCK + ctx_len
    return pl.pallas_call(
        functools.partial(_attn_kernel, seq_len, tq, ctx_len),
        grid=(nb, seq_len // tq),
        in_specs=[
            pl.BlockSpec(memory_space=pltpu.SMEM),
            pl.BlockSpec((None, tq, Q_DIM), lambda b, i: (b, i, 0)),
            pl.BlockSpec((None, BLOCK, KV_W), lambda b, i: (b, jnp.maximum(i * qpb - 1, 0), 0)),
            pl.BlockSpec((None, tq, KV_W), lambda b, i: (b, i, 0)),
            pl.BlockSpec((None, BLOCK, KV_W),
                         lambda b, i: (b, jnp.minimum((i + 1) * qpb, nblk - 1), 0)),
            pl.BlockSpec((None, ctx_len, KV_W), lambda b, i: (b, 0, 0)),
        ],
        out_specs=pl.BlockSpec((None, tq, Q_DIM), lambda b, i: (b, i, 0)),
        out_shape=jax.ShapeDtypeStruct((nb, seq_len, Q_DIM), BF16),
        scratch_shapes=([pltpu.VMEM((tq + 2 * BLOCK, KV_W), BF16)]
                        + [pltpu.VMEM((rows, nkey), F32)] * N_KV_HEADS
                        + [pltpu.VMEM((rows, nkey), BF16)] * (2 * N_KV_HEADS)
                        + [pltpu.VMEM((rows, LANES), F32)] * (2 * N_KV_HEADS)
                        + [pltpu.VMEM((N_KV_HEADS, qpb + 2, 8, LANES), F32),
                           pltpu.VMEM((BLOCK, LANES), F32)]),
        compiler_params=pltpu.CompilerParams(
            dimension_semantics=("arbitrary", "arbitrary"), vmem_limit_bytes=VMEM_LIMIT),
        name="attention",
    )(sink, q, kv, kv, kv, kvc)


def _post_kernel(layer, x_ref, o_ref_in, mod_ref, gmlp_ref, gfin_ref, wo_ref, w1_ref, w2_ref, out_ref):
    g1 = mod_ref[:, 2 * D_MODEL:3 * D_MODEL]
    sh2 = mod_ref[:, 3 * D_MODEL:4 * D_MODEL]
    sc2 = mod_ref[:, 4 * D_MODEL:5 * D_MODEL]
    g2 = mod_ref[:, 5 * D_MODEL:6 * D_MODEL]
    gain = gmlp_ref[layer:layer + 1, :]
    tm = x_ref.shape[0]
    halves = [slice(0, tm // 2), slice(tm // 2, tm)]
    x1 = [x_ref[r, :] + g1 * _dot(o_ref_in[r, :], wo_ref[...]) for r in halves]
    h = [_rms_mod(v, gain, sh2, sc2).astype(BF16) for v in x1]
    a = []
    for hv in h:
        up = jnp.maximum(_dot(hv, w1_ref[...]), 0.0)
        a.append((up * up).astype(BF16))
    for r, v, av in zip(halves, x1, a):
        x2 = v + g2 * _dot(av, w2_ref[...])
        y = x2 * lax.rsqrt(jnp.mean(x2 * x2, axis=-1, keepdims=True) + EPS)
        out_ref[r, :] = y * gfin_ref[...]


def _post(x, o, mod, layer, tm, gmlp, gfin, w_o, w1, w2):
    nb, seq_len, _ = x.shape
    return pl.pallas_call(
        functools.partial(_post_kernel, layer),
        grid=(nb, seq_len // tm),
        in_specs=[
            pl.BlockSpec((None, tm, D_MODEL), lambda b, i: (b, i, 0)),
            pl.BlockSpec((None, tm, Q_DIM), lambda b, i: (b, i, 0)),
            pl.BlockSpec((None, None, 1, N_MOD * D_MODEL), lambda b, i: (layer, b, 0, 0)),
            _layer_rows_spec(),
            _const_spec((1, D_MODEL)),
            _const_spec((Q_DIM, D_MODEL)),
            _const_spec((D_MODEL, D_FF)),
            _const_spec((D_FF, D_MODEL)),
        ],
        out_specs=pl.BlockSpec((None, tm, D_MODEL), lambda b, i: (b, i, 0)),
        out_shape=jax.ShapeDtypeStruct(x.shape, F32),
        compiler_params=pltpu.CompilerParams(
            dimension_semantics=("arbitrary", "arbitrary"), vmem_limit_bytes=VMEM_LIMIT),
        name="post_attn",
    )(x, o, mod, gmlp, gfin, w_o, w1, w2)


def _rope_tables(seq_len):
    pos = np.arange(seq_len)
    inv = ROPE_BASE ** (-np.arange(0, AXIS_ROT, 2, dtype=np.float32) / AXIS_ROT)
    ang = np.concatenate([(pos // GRID_W).astype(np.float32)[:, None] * inv[None],
                          (pos % GRID_W).astype(np.float32)[:, None] * inv[None]], axis=-1)
    cos = np.tile(np.cos(ang.astype(np.float64)), (1, LANES // AXIS_ROT))
    sin = np.sin(ang.astype(np.float64))
    return (jnp.asarray(cos, F32), jnp.asarray(np.concatenate([-sin, -sin, sin, sin], axis=-1), F32))


def kernel(x, c, ctx, c_ctx, ada_w, ada_b, norm_mix_g, norm_mlp_g, pool_w_in, pool_w_grp,
           pool_scale, pool_w_out, attn_w_qkv, attn_sink, attn_w_o, mlp_w1, mlp_w2, final_g):
    nb, seq_len, _ = x.shape
    ctx_len = ctx.shape[1]
    ctx_row = nb
    c8 = jnp.concatenate(
        [c, c_ctx[None], jnp.zeros((MOD_ROWS - nb - 1, D_MODEL), F32)], axis=0)
    n_grp = len(POOL_WINDOWS)
    layer0_weights = ((pool_w_in, 0), (pool_w_grp.reshape(-1, n_grp * POOL_GROUP_DIM, POOL_GROUP_DIM), 0),
                      (pool_w_out, 0), (mlp_w1, 0), (mlp_w2, 0))
    mod, (w_in, w_grp, w_out, w1_0, w2_0) = _modulation(c8, ada_w, ada_b, layer0_weights)

    gmix, gmlp = norm_mix_g, norm_mlp_g

    l0 = (gmix, gmlp, pool_scale, w_in,
          w_grp.reshape(n_grp, POOL_GROUP_DIM, POOL_GROUP_DIM), w_out, w1_0, w2_0)
    layer1_weights = ((attn_w_qkv, 0), (attn_w_o, 0), (mlp_w1, 1), (mlp_w2, 1))
    x1, (w_qkv, w_o, w1, w2) = _layer0(x, mod, None, 0, LAYER_TILE, True, layer1_weights, *l0)
    ctx1, _ = _layer0(ctx, mod, ctx_row, 0, ctx_len, False, (), *l0)

    cos, sin = _rope_tables(seq_len)
    q, kv = _qkv(x1, mod, 1, QKV_TILE, gmix, w_qkv, cos, sin)
    kvc = _ctx_kv(ctx1, mod, ctx_row, 1, gmix, w_qkv)
    o = _attention(q, kv, kvc, attn_sink[0], ATTN_TILE)
    return _post(x1, o, mod, 1, LAYER_TILE, gmlp, final_g.reshape(1, D_MODEL), w_o, w1, w2)
```

```python
import functools

import numpy as np
import jax
import jax.numpy as jnp
from jax import lax
from jax.experimental import pallas as pl
from jax.experimental.pallas import tpu as pltpu

D_MODEL = 1024
DEPTH = 2
GRID_W = 64
POOL_WINDOWS = (2, 4, 8, 16)
POOL_GROUP_DIM = D_MODEL // len(POOL_WINDOWS)
POOL_HALO = 8
HEAD_DIM = 64
N_HEADS = D_MODEL // HEAD_DIM
N_KV_HEADS = 2
GQA_GROUP = N_HEADS // N_KV_HEADS
Q_DIM = N_HEADS * HEAD_DIM
KV_DIM = N_KV_HEADS * HEAD_DIM
WINDOW = 128
BLOCK = 128
ROPE_BASE = 10000.0
AXIS_ROT = HEAD_DIM // 2
ROT_HALF = AXIS_ROT // 2
D_FF = 4 * D_MODEL
N_MOD = 6
EPS = 1e-6
NEG = -1e30
MOD_ROWS = 8
LANES = 128
LOG2E = 1.4426950408889634
KV_W = 3 * LANES
SOFTMAX_ROWS = 32
MIN_SAFE_DENOMINATOR_LOG2 = -64.0
QKV_COL_GROUP = 256
FAST_UNROLL = 2
VMEM_LIMIT = 56 * 1024 * 1024

LAYER_TILE = 512
QKV_TILE = 1024
ATTN_TILE = 2048
MOD_COLS = 1536

BF16 = jnp.bfloat16
F32 = jnp.float32


def _const_spec(shape):
    nd = len(shape)
    return pl.BlockSpec(shape, lambda *_: (0,) * nd, pipeline_mode=pl.Buffered(1))


def _layer_rows_spec():
    return _const_spec((DEPTH, D_MODEL))


def _dot(a, b):
    return jnp.dot(a, b, preferred_element_type=F32)


def _rms_mod(x, g, shift, scale):
    y = x * lax.rsqrt(jnp.mean(x * x, axis=-1, keepdims=True) + EPS)
    return y * (g * (1.0 + scale)) + shift


def _mlp_residual(x, mod_ref, gain, w1_ref, w2_ref):
    sh2 = mod_ref[:, 3 * D_MODEL:4 * D_MODEL]
    sc2 = mod_ref[:, 4 * D_MODEL:5 * D_MODEL]
    g2 = mod_ref[:, 5 * D_MODEL:6 * D_MODEL]
    h = _rms_mod(x, gain, sh2, sc2).astype(BF16)
    a = jnp.maximum(_dot(h, w1_ref[...]), 0.0)
    a = (a * a).astype(BF16)
    return x + g2 * _dot(a, w2_ref[...])


def _staged_steps(step, n_tiles, n_stages, body):
    assert n_tiles >= n_stages
    for t in range(n_stages - 1):
        pl.when(step == t)(functools.partial(body, *[s <= t for s in range(n_stages)]))
    pl.when((step >= n_stages - 1) & (step < n_tiles))(functools.partial(body, *[True] * n_stages))
    for k in range(n_stages - 1):
        pl.when(step == n_tiles + k)(functools.partial(body, *[s > k for s in range(n_stages)]))


def _skewed_steps(step, n_tiles, body):
    _staged_steps(step, n_tiles, 2, body)


def _stage_tile(n_tiles, tiles_per_row, stage):
    def tile(t):
        k = jnp.clip(t - stage, 0, n_tiles - 1)
        return k // tiles_per_row, k % tiles_per_row
    return tile


def _skew_maps(n_tiles, tiles_per_row):
    return _stage_tile(n_tiles, tiles_per_row, 0), _stage_tile(n_tiles, tiles_per_row, 1)


def _mod_kernel(n_prep, c_ref, w_ref, b_ref, *refs):
    prep_in, o_ref, prep_out = refs[:n_prep], refs[n_prep], refs[n_prep + 1:]
    c = c_ref[...]
    s = (c * jax.nn.sigmoid(c)).astype(BF16)
    r = _dot(s, w_ref[...].astype(BF16)) + b_ref[...]
    for row in range(MOD_ROWS):
        o_ref[row] = r[row:row + 1, :]
    for src, dst in zip(prep_in, prep_out):
        dst[...] = src[...].astype(BF16)


def _modulation(c8, ada_w, ada_b, first_layer_weights):
    tn = MOD_COLS
    n = N_MOD * D_MODEL
    n_col = n // tn
    steps = DEPTH * n_col
    prep_in_specs, prep_out_specs = [], []
    for w, idx in first_layer_weights:
        rows = w.shape[1] // steps
        prep_in_specs.append(pl.BlockSpec(
            (None, rows, w.shape[2]), lambda i, j, idx=idx: (idx, i * n_col + j, 0)))
        prep_out_specs.append(pl.BlockSpec((rows, w.shape[2]), lambda i, j: (i * n_col + j, 0)))
    outs = pl.pallas_call(
        functools.partial(_mod_kernel, len(first_layer_weights)),
        grid=(DEPTH, n_col),
        in_specs=[
            pl.BlockSpec((MOD_ROWS, D_MODEL), lambda i, j: (0, 0)),
            pl.BlockSpec((None, D_MODEL, tn), lambda i, j: (i, 0, j)),
            pl.BlockSpec((None, 1, tn), lambda i, j: (i, 0, j)),
        ] + prep_in_specs,
        out_specs=[pl.BlockSpec((None, MOD_ROWS, 1, tn), lambda i, j: (i, 0, 0, j))] + prep_out_specs,
        out_shape=[jax.ShapeDtypeStruct((DEPTH, MOD_ROWS, 1, n), F32)]
        + [jax.ShapeDtypeStruct(w.shape[1:], BF16) for w, _ in first_layer_weights],
        compiler_params=pltpu.CompilerParams(
            dimension_semantics=("arbitrary", "arbitrary"), vmem_limit_bytes=VMEM_LIMIT),
        name="modulation",
    )(c8, ada_w, ada_b.reshape(DEPTH, 1, n), *[w for w, _ in first_layer_weights])
    return outs[0], tuple(outs[1:])


def _to_pair_layout(v):
    lane = lax.broadcasted_iota(jnp.int32, v.shape, 1)
    for width in (ROT_HALF, AXIS_ROT):
        hi = (lane // (2 * width)) % 2
        lo = (lane // width) % 2
        v = jnp.where(hi == lo, v, jnp.where(hi == 0, pltpu.roll(v, LANES - width, 1),
                                             pltpu.roll(v, width, 1)))
    return v


def _prepare_next_weights(prep_in, prep_out):
    wqkv_in, wqkv_out = prep_in[0], prep_out[0]
    low = lax.broadcasted_iota(jnp.int32, (wqkv_in.shape[0], LANES), 1) < HEAD_DIM
    for p in range(GQA_GROUP):
        first = wqkv_in[:, (p // 2) * LANES:(p // 2 + 1) * LANES]
        second = wqkv_in[:, (p // 2 + GQA_GROUP // 2) * LANES:(p // 2 + GQA_GROUP // 2 + 1) * LANES]
        if p % 2 == 0:
            both = jnp.where(low, first, pltpu.roll(second, HEAD_DIM, 1))
        else:
            both = jnp.where(low, pltpu.roll(first, HEAD_DIM, 1), second)
        wqkv_out[:, p * LANES:(p + 1) * LANES] = _to_pair_layout(both).astype(BF16)
    wqkv_out[:, Q_DIM:Q_DIM + KV_DIM] = _to_pair_layout(wqkv_in[:, Q_DIM:Q_DIM + KV_DIM]).astype(BF16)
    wqkv_out[:, Q_DIM + KV_DIM:] = wqkv_in[:, Q_DIM + KV_DIM:].astype(BF16)
    for src, dst in zip(prep_in[1:], prep_out[1:]):
        dst[...] = src[...].astype(BF16)


def _layer0_kernel(seq_len, tm, n_tiles, skew, n_prep, layer, *refs):
    (x_ref, xp_ref, xn_ref, moda_ref, modb_ref, gmix_ref, gmlp_ref, pscale_ref, win_ref, wgrp_ref,
     wout_ref, w1_ref, w2_ref) = refs[:13]
    prep_in = refs[13:13 + n_prep]
    o_ref = refs[13 + n_prep]
    prep_out = refs[14 + n_prep:14 + 2 * n_prep]
    u_ref, d_ref, x1_ref, h2_ref = refs[14 + 2 * n_prep:]
    step = pl.program_id(0)
    i = jnp.minimum(step, n_tiles - 1) % (seq_len // tm)
    half = D_FF // 2

    def mlp_up(h2, c0, c1):
        a = jnp.maximum(_dot(h2, w1_ref[:, c0:c1]), 0.0)
        return (a * a).astype(BF16)

    def body(do_mixer, do_prev_mlp):
        if do_prev_mlp:
            g2 = modb_ref[:, 5 * D_MODEL:6 * D_MODEL]
            h2_prev = h2_ref[...]
            a_lo = mlp_up(h2_prev, 0, half)

        if do_mixer:
            sh1 = moda_ref[:, 0:D_MODEL]
            sc1 = moda_ref[:, D_MODEL:2 * D_MODEL]
            x = x_ref[...]
            xe = jnp.concatenate([xp_ref[...], x, xn_ref[...]], axis=0)
            u = _dot(_rms_mod(xe, gmix_ref[layer:layer + 1, :], sh1, sc1).astype(BF16), win_ref[...])
            u_ref[0:POOL_HALO, :] = jnp.where(i > 0, u[0:POOL_HALO], 0.0)
            u_ref[POOL_HALO:POOL_HALO + tm, :] = u[POOL_HALO:POOL_HALO + tm]
            u_ref[POOL_HALO + tm:POOL_HALO + tm + POOL_HALO, :] = jnp.where(
                (i + 1) * tm < seq_len, u[POOL_HALO + tm:], 0.0)

        if do_prev_mlp:
            a_hi = mlp_up(h2_prev, half, D_FF)

        if do_mixer:
            t = i * tm + lax.broadcasted_iota(jnp.int32, (tm, 1), 0)
            for g, w in enumerate(POOL_WINDOWS):
                c0, c1 = g * POOL_GROUP_DIM, (g + 1) * POOL_GROUP_DIM
                acc = u_ref[POOL_HALO - w // 2:POOL_HALO - w // 2 + tm, c0:c1]
                for s in range(-w // 2 + 1, w // 2):
                    acc = acc + u_ref[POOL_HALO + s:POOL_HALO + s + tm, c0:c1]
                lo = jnp.clip(t - w // 2, 0, seq_len)
                hi = jnp.clip(t + w // 2, 0, seq_len)
                inv_cnt = 1.0 / (hi - lo).astype(F32)
                d = acc * inv_cnt - u_ref[POOL_HALO:POOL_HALO + tm, c0:c1]
                d_ref[:, c0:c1] = _dot(d.astype(BF16), wgrp_ref[g])
            y = _dot((d_ref[...] * pscale_ref[...]).astype(BF16), wout_ref[...])

        if do_prev_mlp:
            down = _dot(jnp.concatenate([a_lo, a_hi], axis=1), w2_ref[...])

        if do_mixer:
            g1 = moda_ref[:, 2 * D_MODEL:3 * D_MODEL]
            sh2 = moda_ref[:, 3 * D_MODEL:4 * D_MODEL]
            sc2 = moda_ref[:, 4 * D_MODEL:5 * D_MODEL]
            x1 = x + g1 * y
            h2 = _rms_mod(x1, gmlp_ref[layer:layer + 1, :], sh2, sc2).astype(BF16)

        if do_prev_mlp:
            o_ref[...] = x1_ref[...] + g2 * down
        if do_mixer and n_prep:
            _prepare_next_weights(prep_in, prep_out)
        if do_mixer and skew:
            x1_ref[...] = x1
            h2_ref[...] = h2
        if do_mixer and not skew:
            g2_now = moda_ref[:, 5 * D_MODEL:6 * D_MODEL]
            o_ref[...] = x1 + g2_now * _dot(mlp_up(h2, 0, D_FF), w2_ref[...])

    if skew:
        _skewed_steps(step, n_tiles, body)
    else:
        body(True, False)


def _layer0(x, mod, mod_row, layer, tm, skew, next_weights, gmix, gmlp, pscale, w_in, w_grp, w_out, w1, w2):
    nb, seq_len, _ = x.shape
    rows8 = seq_len // POOL_HALO
    tpb = tm // POOL_HALO
    nt = seq_len // tm
    n_tiles = nb * nt
    tile_a, tile_b = _skew_maps(n_tiles, nt)
    if not skew:
        tile_b = tile_a

    def x_map(t):
        b, i = tile_a(t)
        return (b, i, 0)

    def prev_map(t):
        b, i = tile_a(t)
        return (b, jnp.maximum(i * tpb - 1, 0), 0)

    def next_map(t):
        b, i = tile_a(t)
        return (b, jnp.minimum((i + 1) * tpb, rows8 - 1), 0)

    def moda_map(t):
        return (layer, tile_a(t)[0] if mod_row is None else mod_row, 0, 0)

    def modb_map(t):
        return (layer, tile_b(t)[0] if mod_row is None else mod_row, 0, 0)

    def out_map(t):
        b, i = tile_b(t)
        return (b, i, 0)

    prep_in_specs, prep_out_specs = [], []
    for w, idx in next_weights:
        rows = w.shape[1] // n_tiles
        prep_in_specs.append(pl.BlockSpec(
            (None, rows, w.shape[2]), lambda t, idx=idx: (idx, jnp.minimum(t, n_tiles - 1), 0)))
        prep_out_specs.append(pl.BlockSpec(
            (rows, w.shape[2]), lambda t: (jnp.minimum(t, n_tiles - 1), 0)))

    outs = pl.pallas_call(
        functools.partial(_layer0_kernel, seq_len, tm, n_tiles, skew, len(next_weights), layer),
        grid=(n_tiles + (1 if skew else 0),),
        in_specs=[
            pl.BlockSpec((None, tm, D_MODEL), x_map),
            pl.BlockSpec((None, POOL_HALO, D_MODEL), prev_map),
            pl.BlockSpec((None, POOL_HALO, D_MODEL), next_map),
            pl.BlockSpec((None, None, 1, N_MOD * D_MODEL), moda_map),
            pl.BlockSpec((None, None, 1, N_MOD * D_MODEL), modb_map),
            _layer_rows_spec(),
            _layer_rows_spec(),
            _const_spec((1, D_MODEL)),
            _const_spec((D_MODEL, D_MODEL)),
            _const_spec((len(POOL_WINDOWS), POOL_GROUP_DIM, POOL_GROUP_DIM)),
            _const_spec((D_MODEL, D_MODEL)),
            _const_spec((D_MODEL, D_FF)),
            _const_spec((D_FF, D_MODEL)),
        ] + prep_in_specs,
        out_specs=[pl.BlockSpec((None, tm, D_MODEL), out_map)] + prep_out_specs,
        out_shape=[jax.ShapeDtypeStruct(x.shape, F32)]
        + [jax.ShapeDtypeStruct(w.shape[1:], BF16) for w, _ in next_weights],
        scratch_shapes=[pltpu.VMEM((tm + 2 * POOL_HALO, D_MODEL), F32),
                        pltpu.VMEM((tm, D_MODEL), F32),
                        pltpu.VMEM((tm, D_MODEL), F32),
                        pltpu.VMEM((tm, D_MODEL), BF16)],
        compiler_params=pltpu.CompilerParams(
            dimension_semantics=("arbitrary",), vmem_limit_bytes=VMEM_LIMIT),
        name="layer0",
    )(x, x, x, mod, mod, gmix, gmlp, pscale, w_in, w_grp, w_out, w1, w2, *[w for w, _ in next_weights])
    return outs[0], tuple(outs[1:])


def _store_kv(kv_ref, k, v):
    low = lax.broadcasted_iota(jnp.int32, v.shape, 1) < HEAD_DIM
    kv_ref[:, 0:LANES] = k.astype(BF16)
    kv_ref[:, LANES:2 * LANES] = jnp.where(low, v, 1.0).astype(BF16)
    kv_ref[:, 2 * LANES:3 * LANES] = jnp.where(low, 1.0, v).astype(BF16)


def _qkv_kernel(layer, x_ref, mod_ref, gmix_ref, w_ref, cos_ref, sin_ref, q_ref, kv_ref):
    sh1 = mod_ref[:, 0:D_MODEL]
    sc1 = mod_ref[:, D_MODEL:2 * D_MODEL]
    gmix = gmix_ref[layer:layer + 1, :]
    tm = x_ref.shape[0]
    halves = [slice(0, tm // 2), slice(tm // 2, tm)]
    h = [_rms_mod(x_ref[r, :], gmix, sh1, sc1).astype(BF16) for r in halves]
    q_groups = [[_dot(hv, w_ref[:, c:c + QKV_COL_GROUP]) for c in range(0, Q_DIM, QKV_COL_GROUP)] for hv in h]
    kv = [_dot(hv, w_ref[:, Q_DIM:]) for hv in h]
    qscale = HEAD_DIM ** -0.5 * LOG2E

    def rope(t, c, s):
        return t * c + pltpu.roll(t, HEAD_DIM, 1) * s

    for r, groups, kv_half in zip(halves, q_groups, kv):
        cos = cos_ref[r, :]
        sin = sin_ref[r, :]
        cos_q = cos * qscale
        sin_q = sin * qscale
        for gi, group in enumerate(groups):
            for cb in range(QKV_COL_GROUP // LANES):
                c0 = gi * QKV_COL_GROUP + cb * LANES
                q_ref[r, c0:c0 + LANES] = rope(group[:, cb * LANES:(cb + 1) * LANES], cos_q, sin_q).astype(BF16)
        _store_kv(kv_ref.at[r, :], rope(kv_half[:, 0:KV_DIM], cos, sin), kv_half[:, KV_DIM:])


def _qkv(x, mod, layer, tm, gmix, w_qkv, cos, sin):
    nb, seq_len, _ = x.shape
    return pl.pallas_call(
        functools.partial(_qkv_kernel, layer),
        grid=(nb, seq_len // tm),
        in_specs=[
            pl.BlockSpec((None, tm, D_MODEL), lambda b, i: (b, i, 0)),
            pl.BlockSpec((None, None, 1, N_MOD * D_MODEL), lambda b, i: (layer, b, 0, 0)),
            _layer_rows_spec(),
            _const_spec((D_MODEL, Q_DIM + 2 * KV_DIM)),
            pl.BlockSpec((tm, LANES), lambda b, i: (i, 0)),
            pl.BlockSpec((tm, LANES), lambda b, i: (i, 0)),
        ],
        out_specs=[pl.BlockSpec((None, tm, Q_DIM), lambda b, i: (b, i, 0)),
                   pl.BlockSpec((None, tm, KV_W), lambda b, i: (b, i, 0))],
        out_shape=[jax.ShapeDtypeStruct((nb, seq_len, Q_DIM), BF16),
                   jax.ShapeDtypeStruct((nb, seq_len, KV_W), BF16)],
        compiler_params=pltpu.CompilerParams(
            dimension_semantics=("arbitrary", "arbitrary"), vmem_limit_bytes=VMEM_LIMIT),
        name="qkv_rope",
    )(x, mod, gmix, w_qkv, cos, sin)


def _ctx_kv_kernel(layer, x_ref, mod_ref, gmix_ref, w_ref, kv_ref):
    sh1 = mod_ref[:, 0:D_MODEL]
    sc1 = mod_ref[:, D_MODEL:2 * D_MODEL]
    h = _rms_mod(x_ref[...], gmix_ref[layer:layer + 1, :], sh1, sc1).astype(BF16)
    kv = _dot(h, w_ref[...])
    _store_kv(kv_ref, kv[:, 0:KV_DIM], kv[:, KV_DIM:])


def _ctx_kv(ctx, mod, mod_row, layer, gmix, w_kv):
    nb, ctx_len, _ = ctx.shape
    return pl.pallas_call(
        functools.partial(_ctx_kv_kernel, layer),
        grid=(nb,),
        in_specs=[
            pl.BlockSpec((None, ctx_len, D_MODEL), lambda b: (b, 0, 0)),
            pl.BlockSpec((None, None, 1, N_MOD * D_MODEL), lambda b: (layer, mod_row, 0, 0)),
            _layer_rows_spec(),
            pl.BlockSpec((D_MODEL, 2 * KV_DIM), lambda b: (0, Q_DIM // (2 * KV_DIM)),
                         pipeline_mode=pl.Buffered(1)),
        ],
        out_specs=pl.BlockSpec((None, ctx_len, KV_W), lambda b: (b, 0, 0)),
        out_shape=jax.ShapeDtypeStruct((nb, ctx_len, KV_W), BF16),
        compiler_params=pltpu.CompilerParams(
            dimension_semantics=("arbitrary",), vmem_limit_bytes=VMEM_LIMIT),
        name="ctx_kv",
    )(ctx, mod, gmix, w_kv)


def _dot_nt(a, b):
    return lax.dot_general(a, b, (((1,), (1,)), ((), ())), preferred_element_type=F32)


def _attn_kernel(seq_len, tq, ctx_len, sink_ref, q_ref, kvp_ref, kvm_ref, kvn_ref, kvc_ref, o_ref,
                 kv_ext, *bufs):
    s_refs, e_refs, m_refs = bufs[0:2], bufs[2:6], bufs[6:10]
    kn_ref, minden_ref = bufs[10], bufs[11]
    i = pl.program_id(1)
    kv_ext[0:BLOCK, :] = kvp_ref[...]
    kv_ext[BLOCK:BLOCK + tq, :] = kvm_ref[...]
    kv_ext[BLOCK + tq:2 * BLOCK + tq, :] = kvn_ref[...]
    nblk = tq // BLOCK
    nwin = 3 * BLOCK
    nkey = nwin + ctx_len
    row = lax.broadcasted_iota(jnp.int32, (BLOCK, BLOCK), 0)
    col = lax.broadcasted_iota(jnp.int32, (BLOCK, BLOCK), 1)
    lane = lax.broadcasted_iota(jnp.int32, (BLOCK, LANES), 1)
    low = lane < HEAD_DIM
    head_a = (lane % HEAD_DIM) < AXIS_ROT
    my_lanes = (head_a, jnp.logical_not(head_a))
    den_lanes = (jnp.logical_not(low), low)

    chunks = [(g, t0) for g in range(GQA_GROUP) for t0 in range(0, BLOCK, SOFTMAX_ROWS)]

    def head_queries(j, kvh):
        q0 = pl.multiple_of(j * BLOCK, BLOCK)
        tiles = []
        for g in range(GQA_GROUP):
            pair = q_ref[pl.ds(q0, BLOCK), g * LANES:(g + 1) * LANES]
            tiles.append(jnp.where(my_lanes[kvh], pair, jnp.zeros_like(pair)))
        return tiles

    def window_masks(j):
        mask_l = (col >= row) & (i * tq + (j - 1) * BLOCK >= 0)
        mask_r = (col <= row) & (i * tq + (j + 1) * BLOCK < seq_len)
        return mask_l, mask_r

    def key_tiles(j, kvh):
        q0 = pl.multiple_of(j * BLOCK, BLOCK)
        return kv_ext[pl.ds(q0, nwin), 0:LANES], kvc_ref[:, 0:LANES]

    def scores(j, kvh):
        qs = jnp.concatenate(head_queries(j, kvh), axis=0)
        kw, kc = key_tiles(j, kvh)
        s_refs[kvh][:, 0:nwin] = _dot_nt(qs, kw)
        s_refs[kvh][:, nwin:nkey] = _dot_nt(qs, kc)

    def softmax(j, kvh):
        mask_l, mask_r = window_masks(j)

        def masked_pieces(rows, trows):
            pieces = [s_refs[kvh][rows, p * LANES:(p + 1) * LANES] for p in range(nkey // LANES)]
            pieces[0] = jnp.where(mask_l[trows], pieces[0], NEG)
            pieces[2] = jnp.where(mask_r[trows], pieces[2], NEG)
            return pieces

        for g, t0 in chunks:
            rows = slice(g * BLOCK + t0, g * BLOCK + t0 + SOFTMAX_ROWS)
            pieces = masked_pieces(rows, slice(t0, t0 + SOFTMAX_ROWS))
            m = pieces[0]
            for p in pieces[1:]:
                m = jnp.maximum(m, p)
            m = jnp.maximum(jnp.max(m, axis=1, keepdims=True), sink_ref[kvh * GQA_GROUP + g] * LOG2E)
            m_refs[kvh][rows, :] = jnp.broadcast_to(m, (SOFTMAX_ROWS, LANES))
        for g, t0 in chunks:
            rows = slice(g * BLOCK + t0, g * BLOCK + t0 + SOFTMAX_ROWS)
            pieces = masked_pieces(rows, slice(t0, t0 + SOFTMAX_ROWS))
            m = m_refs[kvh][rows, :]
            for p, piece in enumerate(pieces):
                e_refs[kvh][rows, p * LANES:(p + 1) * LANES] = jnp.exp2((piece - m).astype(BF16))

    def weighted_values(j, kvh, slot, fast):
        e_ref, m_ref = e_refs[2 * slot + kvh], m_refs[2 * slot + kvh]
        q0 = pl.multiple_of(j * BLOCK, BLOCK)
        vcols = slice((1 + kvh) * LANES, (2 + kvh) * LANES)
        values = jnp.concatenate([kv_ext[pl.ds(q0, nwin), vcols], kvc_ref[:, vcols]], axis=0)
        o = _dot(e_ref[...], values)
        on, smallest = [], None
        for g in range(GQA_GROUP):
            rows = slice(g * BLOCK, (g + 1) * BLOCK)
            sink = sink_ref[kvh * GQA_GROUP + g] * LOG2E
            if fast:
                sink_term = jnp.exp2(sink - m_ref[g * BLOCK:g * BLOCK + 1, :])
            else:
                sink_term = jnp.exp2(sink - m_ref[rows, :])
            total = o[rows] + sink_term
            smallest = total if smallest is None else jnp.minimum(smallest, total)
            on.append(o[rows] * pltpu.roll(1.0 / total, HEAD_DIM, 1))
        if fast:
            minden_ref[...] = jnp.minimum(minden_ref[...], jnp.where(den_lanes[kvh], smallest, 3e38))
        for pair in range(GQA_GROUP // 2):
            if kvh == 0:
                tile = jnp.where(low, on[2 * pair], pltpu.roll(on[2 * pair + 1], HEAD_DIM, 1))
            else:
                tile = jnp.where(low, pltpu.roll(on[2 * pair], HEAD_DIM, 1), on[2 * pair + 1])
            c0 = (kvh * GQA_GROUP + 2 * pair) * HEAD_DIM
            o_ref[pl.ds(q0, BLOCK), c0:c0 + LANES] = tile.astype(BF16)

    def exact_body(j, carry):
        scores(j, 0)
        scores(j, 1)
        softmax(j, 0)
        weighted_values(j, 0, 0, False)
        softmax(j, 1)
        weighted_values(j, 1, 0, False)
        return carry

    def max_sq_norms(k_tile):
        sq = k_tile.astype(F32)
        sq = sq * sq
        return [jnp.broadcast_to(jnp.max(jnp.sum(jnp.where(my_lanes[kvh], sq, 0.0), axis=1, keepdims=True),
                                         axis=0, keepdims=True), (8, LANES)) for kvh in range(N_KV_HEADS)]

    def store_key_norms(first, last):
        for kb in range(first, last):
            for kvh, n2 in enumerate(max_sq_norms(kv_ext[kb * BLOCK:(kb + 1) * BLOCK, 0:LANES])):
                kn_ref[kvh, kb] = n2

    store_key_norms(0, 3)
    ctx_norm = None
    for cb in range(ctx_len // BLOCK):
        n2 = max_sq_norms(kvc_ref[cb * BLOCK:(cb + 1) * BLOCK, 0:LANES])
        ctx_norm = n2 if ctx_norm is None else [jnp.maximum(a, b) for a, b in zip(ctx_norm, n2)]
    minden_ref[...] = jnp.full(minden_ref.shape, 3e38, F32)

    def bounded_exp(j, kvh, slot):
        e_ref, m_ref = e_refs[2 * slot + kvh], m_refs[2 * slot + kvh]
        q0 = pl.multiple_of(j * BLOCK, BLOCK)
        kw, kc = key_tiles(j, kvh)
        kn2 = jnp.maximum(jnp.max(kn_ref[kvh, pl.ds(j, 3)], axis=0), ctx_norm[kvh])[0:1, :]
        tiles = head_queries(j, kvh)
        for g in range(GQA_GROUP):
            qf = tiles[g].astype(F32)
            qn2 = jnp.max(jnp.sum(qf * qf, axis=1, keepdims=True), axis=0, keepdims=True)
            bound = jnp.maximum(jnp.sqrt(qn2 * kn2), sink_ref[kvh * GQA_GROUP + g] * LOG2E)
            m_ref[g * BLOCK:(g + 1) * BLOCK, :] = jnp.broadcast_to(bound, (BLOCK, LANES))
        qs = jnp.concatenate(tiles, axis=0)
        s_w = _dot_nt(qs, kw)
        s_c = _dot_nt(qs, kc)
        mask_l, mask_r = window_masks(j)
        for g, t0 in chunks:
            rows = slice(g * BLOCK + t0, g * BLOCK + t0 + SOFTMAX_ROWS)
            trows = slice(t0, t0 + SOFTMAX_ROWS)
            m = m_ref[rows, :]
            pieces = ([s_w[rows, p * LANES:(p + 1) * LANES] for p in range(nwin // LANES)]
                      + [s_c[rows, p * LANES:(p + 1) * LANES] for p in range(ctx_len // LANES)])
            pieces[0] = jnp.where(mask_l[trows], pieces[0], NEG)
            pieces[2] = jnp.where(mask_r[trows], pieces[2], NEG)
            for p, piece in enumerate(pieces):
                e_ref[rows, p * LANES:(p + 1) * LANES] = jnp.exp2(piece - m).astype(BF16)

    bounded_exp(0, 0, 0)
    bounded_exp(0, 1, 0)
    store_key_norms(3, nblk + 2)

    def fast_blocks(j0, last):
        for k in range(FAST_UNROLL):
            for kvh in range(N_KV_HEADS):
                weighted_values(j0 + k, kvh, k % 2, True)
                if not (last and k + 1 == FAST_UNROLL):
                    bounded_exp(j0 + k + 1, kvh, (k + 1) % 2)

    def fast_body(jj, carry):
        fast_blocks(FAST_UNROLL * jj, False)
        return carry

    lax.fori_loop(0, nblk // FAST_UNROLL - 1, fast_body, 0)
    fast_blocks(nblk - FAST_UNROLL, True)
    smallest = jnp.min(minden_ref[...])

    @pl.when(jnp.logical_not(smallest >= 2.0 ** MIN_SAFE_DENOMINATOR_LOG2))
    def _():
        lax.fori_loop(0, nblk, exact_body, 0)


def _attention(q, kv, kvc, sink, tq):
    nb, seq_len, _ = q.shape
    ctx_len = kvc.shape[1]
    nblk = seq_len // BLOCK
    qpb = tq // BLOCK
    rows = GQA_GROUP * BLOCK
    nkey = 3 * BLOCK + ctx_len
    return pl.pallas_call(
        functools.partial(_attn_kernel, seq_len, tq, ctx_len),
        grid=(nb, seq_len // tq),
        in_specs=[
            pl.BlockSpec(memory_space=pltpu.SMEM),
            pl.BlockSpec((None, tq, Q_DIM), lambda b, i: (b, i, 0)),
            pl.BlockSpec((None, BLOCK, KV_W), lambda b, i: (b, jnp.maximum(i * qpb - 1, 0), 0)),
            pl.BlockSpec((None, tq, KV_W), lambda b, i: (b, i, 0)),
            pl.BlockSpec((None, BLOCK, KV_W),
                         lambda b, i: (b, jnp.minimum((i + 1) * qpb, nblk - 1), 0)),
            pl.BlockSpec((None, ctx_len, KV_W), lambda b, i: (b, 0, 0)),
        ],
        out_specs=pl.BlockSpec((None, tq, Q_DIM), lambda b, i: (b, i, 0)),
        out_shape=jax.ShapeDtypeStruct((nb, seq_len, Q_DIM), BF16),
        scratch_shapes=([pltpu.VMEM((tq + 2 * BLOCK, KV_W), BF16)]
                        + [pltpu.VMEM((rows, nkey), F32)] * N_KV_HEADS
                        + [pltpu.VMEM((rows, nkey), BF16)] * (2 * N_KV_HEADS)
                        + [pltpu.VMEM((rows, LANES), F32)] * (2 * N_KV_HEADS)
                        + [pltpu.VMEM((N_KV_HEADS, qpb + 2, 8, LANES), F32),
                           pltpu.VMEM((BLOCK, LANES), F32)]),
        compiler_params=pltpu.CompilerParams(
            dimension_semantics=("arbitrary", "arbitrary"), vmem_limit_bytes=VMEM_LIMIT),
        name="attention",
    )(sink, q, kv, kv, kv, kvc)


def _post_kernel(layer, x_ref, o_ref_in, mod_ref, gmlp_ref, gfin_ref, wo_ref, w1_ref, w2_ref, out_ref):
    g1 = mod_ref[:, 2 * D_MODEL:3 * D_MODEL]
    sh2 = mod_ref[:, 3 * D_MODEL:4 * D_MODEL]
    sc2 = mod_ref[:, 4 * D_MODEL:5 * D_MODEL]
    g2 = mod_ref[:, 5 * D_MODEL:6 * D_MODEL]
    gain = gmlp_ref[layer:layer + 1, :]
    tm = x_ref.shape[0]
    halves = [slice(0, tm // 2), slice(tm // 2, tm)]
    x1 = [x_ref[r, :] + g1 * _dot(o_ref_in[r, :], wo_ref[...]) for r in halves]
    h = [_rms_mod(v, gain, sh2, sc2).astype(BF16) for v in x1]
    a = []
    for hv in h:
        up = jnp.maximum(_dot(hv, w1_ref[...]), 0.0)
        a.append((up * up).astype(BF16))
    for r, v, av in zip(halves, x1, a):
        x2 = v + g2 * _dot(av, w2_ref[...])
        y = x2 * lax.rsqrt(jnp.mean(x2 * x2, axis=-1, keepdims=True) + EPS)
        out_ref[r, :] = y * gfin_ref[...]


def _post(x, o, mod, layer, tm, gmlp, gfin, w_o, w1, w2):
    nb, seq_len, _ = x.shape
    return pl.pallas_call(
        functools.partial(_post_kernel, layer),
        grid=(nb, seq_len // tm),
        in_specs=[
            pl.BlockSpec((None, tm, D_MODEL), lambda b, i: (b, i, 0)),
            pl.BlockSpec((None, tm, Q_DIM), lambda b, i: (b, i, 0)),
            pl.BlockSpec((None, None, 1, N_MOD * D_MODEL), lambda b, i: (layer, b, 0, 0)),
            _layer_rows_spec(),
            _const_spec((1, D_MODEL)),
            _const_spec((Q_DIM, D_MODEL)),
            _const_spec((D_MODEL, D_FF)),
            _const_spec((D_FF, D_MODEL)),
        ],
        out_specs=pl.BlockSpec((None, tm, D_MODEL), lambda b, i: (b, i, 0)),
        out_shape=jax.ShapeDtypeStruct(x.shape, F32),
        compiler_params=pltpu.CompilerParams(
            dimension_semantics=("arbitrary", "arbitrary"), vmem_limit_bytes=VMEM_LIMIT),
        name="post_attn",
    )(x, o, mod, gmlp, gfin, w_o, w1, w2)


def _rope_tables(seq_len):
    pos = np.arange(seq_len)
    inv = ROPE_BASE ** (-np.arange(0, AXIS_ROT, 2, dtype=np.float32) / AXIS_ROT)
    ang = np.concatenate([(pos // GRID_W).astype(np.float32)[:, None] * inv[None],
                          (pos % GRID_W).astype(np.float32)[:, None] * inv[None]], axis=-1)
    cos = np.tile(np.cos(ang.astype(np.float64)), (1, LANES // AXIS_ROT))
    sin = np.sin(ang.astype(np.float64))
    return (jnp.asarray(cos, F32), jnp.asarray(np.concatenate([-sin, -sin, sin, sin], axis=-1), F32))


def kernel(x, c, ctx, c_ctx, ada_w, ada_b, norm_mix_g, norm_mlp_g, pool_w_in, pool_w_grp,
           pool_scale, pool_w_out, attn_w_qkv, attn_sink, attn_w_o, mlp_w1, mlp_w2, final_g):
    nb, seq_len, _ = x.shape
    ctx_len = ctx.shape[1]
    ctx_row = nb
    c8 = jnp.concatenate(
        [c, c_ctx[None], jnp.zeros((MOD_ROWS - nb - 1, D_MODEL), F32)], axis=0)
    n_grp = len(POOL_WINDOWS)
    layer0_weights = ((pool_w_in, 0), (pool_w_grp.reshape(-1, n_grp * POOL_GROUP_DIM, POOL_GROUP_DIM), 0),
                      (pool_w_out, 0), (mlp_w1, 0), (mlp_w2, 0))
    mod, (w_in, w_grp, w_out, w1_0, w2_0) = _modulation(c8, ada_w, ada_b, layer0_weights)

    gmix, gmlp = norm_mix_g, norm_mlp_g

    l0 = (gmix, gmlp, pool_scale, w_in,
          w_grp.reshape(n_grp, POOL_GROUP_DIM, POOL_GROUP_DIM), w_out, w1_0, w2_0)
    layer1_weights = ((attn_w_qkv, 0), (attn_w_o, 0), (mlp_w1, 1), (mlp_w2, 1))
    x1, (w_qkv, w_o, w1, w2) = _layer0(x, mod, None, 0, LAYER_TILE, True, layer1_weights, *l0)
    ctx1, _ = _layer0(ctx, mod, ctx_row, 0, ctx_len, False, (), *l0)

    cos, sin = _rope_tables(seq_len)
    q, kv = _qkv(x1, mod, 1, QKV_TILE, gmix, w_qkv, cos, sin)
    kvc = _ctx_kv(ctx1, mod, ctx_row, 1, gmix, w_qkv)
    o = _attention(q, kv, kvc, attn_sink[0], ATTN_TILE)
    return _post(x1, o, mod, 1, LAYER_TILE, gmlp, final_g.reshape(1, D_MODEL), w_o, w1, w2)
```

```python
import functools

import numpy as np
import jax
import jax.numpy as jnp
from jax import lax
from jax.experimental import pallas as pl
from jax.experimental.pallas import tpu as pltpu

D_MODEL = 1024
DEPTH = 2
GRID_W = 64
POOL_WINDOWS = (2, 4, 8, 16)
POOL_GROUP_DIM = D_MODEL // len(POOL_WINDOWS)
POOL_HALO = 8
HEAD_DIM = 64
N_HEADS = D_MODEL // HEAD_DIM
N_KV_HEADS = 2
GQA_GROUP = N_HEADS // N_KV_HEADS
Q_DIM = N_HEADS * HEAD_DIM
KV_DIM = N_KV_HEADS * HEAD_DIM
WINDOW = 128
BLOCK = 128
ROPE_BASE = 10000.0
AXIS_ROT = HEAD_DIM // 2
ROT_HALF = AXIS_ROT // 2
D_FF = 4 * D_MODEL
N_MOD = 6
EPS = 1e-6
NEG = -1e30
MOD_ROWS = 8
LANES = 128
LOG2E = 1.4426950408889634
KV_W = 3 * LANES
SOFTMAX_ROWS = 32
MIN_SAFE_DENOMINATOR_LOG2 = -64.0
QKV_COL_GROUP = 256
FAST_UNROLL = 2
VMEM_LIMIT = 56 * 1024 * 1024

LAYER_TILE = 512
QKV_TILE = 2048
QKV_PIECE = 512
ATTN_TILE = 2048
MOD_COLS = 1536

BF16 = jnp.bfloat16
F32 = jnp.float32


def _const_spec(shape):
    nd = len(shape)
    return pl.BlockSpec(shape, lambda *_: (0,) * nd, pipeline_mode=pl.Buffered(1))


def _layer_rows_spec():
    return _const_spec((DEPTH, D_MODEL))


def _dot(a, b):
    return jnp.dot(a, b, preferred_element_type=F32)


def _rms_mod(x, g, shift, scale):
    y = x * lax.rsqrt(jnp.mean(x * x, axis=-1, keepdims=True) + EPS)
    return y * (g * (1.0 + scale)) + shift


def _mlp_residual(x, mod_ref, gain, w1_ref, w2_ref):
    sh2 = mod_ref[:, 3 * D_MODEL:4 * D_MODEL]
    sc2 = mod_ref[:, 4 * D_MODEL:5 * D_MODEL]
    g2 = mod_ref[:, 5 * D_MODEL:6 * D_MODEL]
    h = _rms_mod(x, gain, sh2, sc2).astype(BF16)
    a = jnp.maximum(_dot(h, w1_ref[...]), 0.0)
    a = (a * a).astype(BF16)
    return x + g2 * _dot(a, w2_ref[...])


def _staged_steps(step, n_tiles, n_stages, body):
    assert n_tiles >= n_stages
    for t in range(n_stages - 1):
        pl.when(step == t)(functools.partial(body, *[s <= t for s in range(n_stages)]))
    pl.when((step >= n_stages - 1) & (step < n_tiles))(functools.partial(body, *[True] * n_stages))
    for k in range(n_stages - 1):
        pl.when(step == n_tiles + k)(functools.partial(body, *[s > k for s in range(n_stages)]))


def _skewed_steps(step, n_tiles, body):
    _staged_steps(step, n_tiles, 2, body)


def _stage_tile(n_tiles, tiles_per_row, stage):
    def tile(t):
        k = jnp.clip(t - stage, 0, n_tiles - 1)
        return k // tiles_per_row, k % tiles_per_row
    return tile


def _skew_maps(n_tiles, tiles_per_row):
    return _stage_tile(n_tiles, tiles_per_row, 0), _stage_tile(n_tiles, tiles_per_row, 1)


def _mod_kernel(n_prep, c_ref, w_ref, b_ref, *refs):
    prep_in, o_ref, prep_out = refs[:n_prep], refs[n_prep], refs[n_prep + 1:]
    c = c_ref[...]
    s = (c * jax.nn.sigmoid(c)).astype(BF16)
    r = _dot(s, w_ref[...].astype(BF16)) + b_ref[...]
    for row in range(MOD_ROWS):
        o_ref[row] = r[row:row + 1, :]
    for src, dst in zip(prep_in, prep_out):
        dst[...] = src[...].astype(BF16)


def _modulation(c8, ada_w, ada_b, first_layer_weights):
    tn = MOD_COLS
    n = N_MOD * D_MODEL
    n_col = n // tn
    steps = DEPTH * n_col
    prep_in_specs, prep_out_specs = [], []
    for w, idx in first_layer_weights:
        rows = w.shape[1] // steps
        prep_in_specs.append(pl.BlockSpec(
            (None, rows, w.shape[2]), lambda i, j, idx=idx: (idx, i * n_col + j, 0)))
        prep_out_specs.append(pl.BlockSpec((rows, w.shape[2]), lambda i, j: (i * n_col + j, 0)))
    outs = pl.pallas_call(
        functools.partial(_mod_kernel, len(first_layer_weights)),
        grid=(DEPTH, n_col),
        in_specs=[
            pl.BlockSpec((MOD_ROWS, D_MODEL), lambda i, j: (0, 0)),
            pl.BlockSpec((None, D_MODEL, tn), lambda i, j: (i, 0, j)),
            pl.BlockSpec((None, 1, tn), lambda i, j: (i, 0, j)),
        ] + prep_in_specs,
        out_specs=[pl.BlockSpec((None, MOD_ROWS, 1, tn), lambda i, j: (i, 0, 0, j))] + prep_out_specs,
        out_shape=[jax.ShapeDtypeStruct((DEPTH, MOD_ROWS, 1, n), F32)]
        + [jax.ShapeDtypeStruct(w.shape[1:], BF16) for w, _ in first_layer_weights],
        compiler_params=pltpu.CompilerParams(
            dimension_semantics=("arbitrary", "arbitrary"), vmem_limit_bytes=VMEM_LIMIT),
        name="modulation",
    )(c8, ada_w, ada_b.reshape(DEPTH, 1, n), *[w for w, _ in first_layer_weights])
    return outs[0], tuple(outs[1:])


def _to_pair_layout(v):
    lane = lax.broadcasted_iota(jnp.int32, v.shape, 1)
    for width in (ROT_HALF, AXIS_ROT):
        hi = (lane // (2 * width)) % 2
        lo = (lane // width) % 2
        v = jnp.where(hi == lo, v, jnp.where(hi == 0, pltpu.roll(v, LANES - width, 1),
                                             pltpu.roll(v, width, 1)))
    return v


def _prepare_next_weights(prep_in, prep_out):
    wqkv_in, wqkv_out = prep_in[0], prep_out[0]
    low = lax.broadcasted_iota(jnp.int32, (wqkv_in.shape[0], LANES), 1) < HEAD_DIM
    for p in range(GQA_GROUP):
        first = wqkv_in[:, (p // 2) * LANES:(p // 2 + 1) * LANES]
        second = wqkv_in[:, (p // 2 + GQA_GROUP // 2) * LANES:(p // 2 + GQA_GROUP // 2 + 1) * LANES]
        if p % 2 == 0:
            both = jnp.where(low, first, pltpu.roll(second, HEAD_DIM, 1))
        else:
            both = jnp.where(low, pltpu.roll(first, HEAD_DIM, 1), second)
        wqkv_out[:, p * LANES:(p + 1) * LANES] = _to_pair_layout(both).astype(BF16)
    wqkv_out[:, Q_DIM:Q_DIM + KV_DIM] = _to_pair_layout(wqkv_in[:, Q_DIM:Q_DIM + KV_DIM]).astype(BF16)
    wqkv_out[:, Q_DIM + KV_DIM:] = wqkv_in[:, Q_DIM + KV_DIM:].astype(BF16)
    for src, dst in zip(prep_in[1:], prep_out[1:]):
        dst[...] = src[...].astype(BF16)


def _layer0_kernel(seq_len, tm, n_tiles, skew, n_prep, layer, *refs):
    (x_ref, xp_ref, xn_ref, moda_ref, modb_ref, gmix_ref, gmlp_ref, pscale_ref, win_ref, wgrp_ref,
     wout_ref, w1_ref, w2_ref) = refs[:13]
    prep_in = refs[13:13 + n_prep]
    o_ref = refs[13 + n_prep]
    prep_out = refs[14 + n_prep:14 + 2 * n_prep]
    u_ref, d_ref, x1_ref, h2_ref = refs[14 + 2 * n_prep:]
    step = pl.program_id(0)
    i = jnp.minimum(step, n_tiles - 1) % (seq_len // tm)
    half = D_FF // 2

    def mlp_up(h2, c0, c1):
        a = jnp.maximum(_dot(h2, w1_ref[:, c0:c1]), 0.0)
        return (a * a).astype(BF16)

    def body(do_mixer, do_prev_mlp):
        if do_prev_mlp:
            g2 = modb_ref[:, 5 * D_MODEL:6 * D_MODEL]
            h2_prev = h2_ref[...]
            a_lo = mlp_up(h2_prev, 0, half)

        if do_mixer:
            sh1 = moda_ref[:, 0:D_MODEL]
            sc1 = moda_ref[:, D_MODEL:2 * D_MODEL]
            x = x_ref[...]
            xe = jnp.concatenate([xp_ref[...], x, xn_ref[...]], axis=0)
            u = _dot(_rms_mod(xe, gmix_ref[layer:layer + 1, :], sh1, sc1).astype(BF16), win_ref[...])
            u_ref[0:POOL_HALO, :] = jnp.where(i > 0, u[0:POOL_HALO], 0.0)
            u_ref[POOL_HALO:POOL_HALO + tm, :] = u[POOL_HALO:POOL_HALO + tm]
            u_ref[POOL_HALO + tm:POOL_HALO + tm + POOL_HALO, :] = jnp.where(
                (i + 1) * tm < seq_len, u[POOL_HALO + tm:], 0.0)

        if do_prev_mlp:
            a_hi = mlp_up(h2_prev, half, D_FF)

        if do_mixer:
            t = i * tm + lax.broadcasted_iota(jnp.int32, (tm, 1), 0)
            for g, w in enumerate(POOL_WINDOWS):
                c0, c1 = g * POOL_GROUP_DIM, (g + 1) * POOL_GROUP_DIM
                acc = u_ref[POOL_HALO - w // 2:POOL_HALO - w // 2 + tm, c0:c1]
                for s in range(-w // 2 + 1, w // 2):
                    acc = acc + u_ref[POOL_HALO + s:POOL_HALO + s + tm, c0:c1]
                lo = jnp.clip(t - w // 2, 0, seq_len)
                hi = jnp.clip(t + w // 2, 0, seq_len)
                inv_cnt = 1.0 / (hi - lo).astype(F32)
                d = acc * inv_cnt - u_ref[POOL_HALO:POOL_HALO + tm, c0:c1]
                d_ref[:, c0:c1] = _dot(d.astype(BF16), wgrp_ref[g])
            y = _dot((d_ref[...] * pscale_ref[...]).astype(BF16), wout_ref[...])

        if do_prev_mlp:
            down = _dot(jnp.concatenate([a_lo, a_hi], axis=1), w2_ref[...])

        if do_mixer:
            g1 = moda_ref[:, 2 * D_MODEL:3 * D_MODEL]
            sh2 = moda_ref[:, 3 * D_MODEL:4 * D_MODEL]
            sc2 = moda_ref[:, 4 * D_MODEL:5 * D_MODEL]
            x1 = x + g1 * y
            h2 = _rms_mod(x1, gmlp_ref[layer:layer + 1, :], sh2, sc2).astype(BF16)

        if do_prev_mlp:
            o_ref[...] = x1_ref[...] + g2 * down
        if do_mixer and n_prep:
            _prepare_next_weights(prep_in, prep_out)
        if do_mixer and skew:
            x1_ref[...] = x1
            h2_ref[...] = h2
        if do_mixer and not skew:
            g2_now = moda_ref[:, 5 * D_MODEL:6 * D_MODEL]
            o_ref[...] = x1 + g2_now * _dot(mlp_up(h2, 0, D_FF), w2_ref[...])

    if skew:
        _skewed_steps(step, n_tiles, body)
    else:
        body(True, False)


def _layer0(x, mod, mod_row, layer, tm, skew, next_weights, gmix, gmlp, pscale, w_in, w_grp, w_out, w1, w2):
    nb, seq_len, _ = x.shape
    rows8 = seq_len // POOL_HALO
    tpb = tm // POOL_HALO
    nt = seq_len // tm
    n_tiles = nb * nt
    tile_a, tile_b = _skew_maps(n_tiles, nt)
    if not skew:
        tile_b = tile_a

    def x_map(t):
        b, i = tile_a(t)
        return (b, i, 0)

    def prev_map(t):
        b, i = tile_a(t)
        return (b, jnp.maximum(i * tpb - 1, 0), 0)

    def next_map(t):
        b, i = tile_a(t)
        return (b, jnp.minimum((i + 1) * tpb, rows8 - 1), 0)

    def moda_map(t):
        return (layer, tile_a(t)[0] if mod_row is None else mod_row, 0, 0)

    def modb_map(t):
        return (layer, tile_b(t)[0] if mod_row is None else mod_row, 0, 0)

    def out_map(t):
        b, i = tile_b(t)
        return (b, i, 0)

    prep_in_specs, prep_out_specs = [], []
    for w, idx in next_weights:
        rows = w.shape[1] // n_tiles
        prep_in_specs.append(pl.BlockSpec(
            (None, rows, w.shape[2]), lambda t, idx=idx: (idx, jnp.minimum(t, n_tiles - 1), 0)))
        prep_out_specs.append(pl.BlockSpec(
            (rows, w.shape[2]), lambda t: (jnp.minimum(t, n_tiles - 1), 0)))

    outs = pl.pallas_call(
        functools.partial(_layer0_kernel, seq_len, tm, n_tiles, skew, len(next_weights), layer),
        grid=(n_tiles + (1 if skew else 0),),
        in_specs=[
            pl.BlockSpec((None, tm, D_MODEL), x_map),
            pl.BlockSpec((None, POOL_HALO, D_MODEL), prev_map),
            pl.BlockSpec((None, POOL_HALO, D_MODEL), next_map),
            pl.BlockSpec((None, None, 1, N_MOD * D_MODEL), moda_map),
            pl.BlockSpec((None, None, 1, N_MOD * D_MODEL), modb_map),
            _layer_rows_spec(),
            _layer_rows_spec(),
            _const_spec((1, D_MODEL)),
            _const_spec((D_MODEL, D_MODEL)),
            _const_spec((len(POOL_WINDOWS), POOL_GROUP_DIM, POOL_GROUP_DIM)),
            _const_spec((D_MODEL, D_MODEL)),
            _const_spec((D_MODEL, D_FF)),
            _const_spec((D_FF, D_MODEL)),
        ] + prep_in_specs,
        out_specs=[pl.BlockSpec((None, tm, D_MODEL), out_map)] + prep_out_specs,
        out_shape=[jax.ShapeDtypeStruct(x.shape, F32)]
        + [jax.ShapeDtypeStruct(w.shape[1:], BF16) for w, _ in next_weights],
        scratch_shapes=[pltpu.VMEM((tm + 2 * POOL_HALO, D_MODEL), F32),
                        pltpu.VMEM((tm, D_MODEL), F32),
                        pltpu.VMEM((tm, D_MODEL), F32),
                        pltpu.VMEM((tm, D_MODEL), BF16)],
        compiler_params=pltpu.CompilerParams(
            dimension_semantics=("arbitrary",), vmem_limit_bytes=VMEM_LIMIT),
        name="layer0",
    )(x, x, x, mod, mod, gmix, gmlp, pscale, w_in, w_grp, w_out, w1, w2, *[w for w, _ in next_weights])
    return outs[0], tuple(outs[1:])


def _store_kv(kv_ref, k, v):
    low = lax.broadcasted_iota(jnp.int32, v.shape, 1) < HEAD_DIM
    kv_ref[:, 0:LANES] = k.astype(BF16)
    kv_ref[:, LANES:2 * LANES] = jnp.where(low, v, 1.0).astype(BF16)
    kv_ref[:, 2 * LANES:3 * LANES] = jnp.where(low, 1.0, v).astype(BF16)


def _qkv_kernel(layer, x_ref, mod_ref, gmix_ref, w_ref, cos_ref, sin_ref, q_ref, kv_ref):
    sh1 = mod_ref[:, 0:D_MODEL]
    sc1 = mod_ref[:, D_MODEL:2 * D_MODEL]
    gmix = gmix_ref[layer:layer + 1, :]
    pieces = [slice(r, r + QKV_PIECE) for r in range(0, x_ref.shape[0], QKV_PIECE)]
    h = [_rms_mod(x_ref[r, :], gmix, sh1, sc1).astype(BF16) for r in pieces]
    q_groups = [[_dot(hv, w_ref[:, c:c + QKV_COL_GROUP]) for c in range(0, Q_DIM, QKV_COL_GROUP)] for hv in h]
    kv = [_dot(hv, w_ref[:, Q_DIM:]) for hv in h]
    qscale = HEAD_DIM ** -0.5 * LOG2E

    def rope(t, c, s):
        return t * c + pltpu.roll(t, HEAD_DIM, 1) * s

    for r, groups, kv_piece in zip(pieces, q_groups, kv):
        cos = cos_ref[r, :]
        sin = sin_ref[r, :]
        cos_q = cos * qscale
        sin_q = sin * qscale
        for gi, group in enumerate(groups):
            for cb in range(QKV_COL_GROUP // LANES):
                c0 = gi * QKV_COL_GROUP + cb * LANES
                q_ref[r, c0:c0 + LANES] = rope(group[:, cb * LANES:(cb + 1) * LANES], cos_q, sin_q).astype(BF16)
        _store_kv(kv_ref.at[r, :], rope(kv_piece[:, 0:KV_DIM], cos, sin), kv_piece[:, KV_DIM:])


def _qkv(x, mod, layer, tm, gmix, w_qkv, cos, sin):
    nb, seq_len, _ = x.shape
    return pl.pallas_call(
        functools.partial(_qkv_kernel, layer),
        grid=(nb, seq_len // tm),
        in_specs=[
            pl.BlockSpec((None, tm, D_MODEL), lambda b, i: (b, i, 0)),
            pl.BlockSpec((None, None, 1, N_MOD * D_MODEL), lambda b, i: (layer, b, 0, 0)),
            _layer_rows_spec(),
            _const_spec((D_MODEL, Q_DIM + 2 * KV_DIM)),
            pl.BlockSpec((tm, LANES), lambda b, i: (i, 0)),
            pl.BlockSpec((tm, LANES), lambda b, i: (i, 0)),
        ],
        out_specs=[pl.BlockSpec((None, tm, Q_DIM), lambda b, i: (b, i, 0)),
                   pl.BlockSpec((None, tm, KV_W), lambda b, i: (b, i, 0))],
        out_shape=[jax.ShapeDtypeStruct((nb, seq_len, Q_DIM), BF16),
                   jax.ShapeDtypeStruct((nb, seq_len, KV_W), BF16)],
        compiler_params=pltpu.CompilerParams(
            dimension_semantics=("arbitrary", "arbitrary"), vmem_limit_bytes=VMEM_LIMIT),
        name="qkv_rope",
    )(x, mod, gmix, w_qkv, cos, sin)


def _ctx_kv_kernel(layer, x_ref, mod_ref, gmix_ref, w_ref, kv_ref):
    sh1 = mod_ref[:, 0:D_MODEL]
    sc1 = mod_ref[:, D_MODEL:2 * D_MODEL]
    h = _rms_mod(x_ref[...], gmix_ref[layer:layer + 1, :], sh1, sc1).astype(BF16)
    kv = _dot(h, w_ref[...])
    _store_kv(kv_ref, kv[:, 0:KV_DIM], kv[:, KV_DIM:])


def _ctx_kv(ctx, mod, mod_row, layer, gmix, w_kv):
    nb, ctx_len, _ = ctx.shape
    return pl.pallas_call(
        functools.partial(_ctx_kv_kernel, layer),
        grid=(nb,),
        in_specs=[
            pl.BlockSpec((None, ctx_len, D_MODEL), lambda b: (b, 0, 0)),
            pl.BlockSpec((None, None, 1, N_MOD * D_MODEL), lambda b: (layer, mod_row, 0, 0)),
            _layer_rows_spec(),
            pl.BlockSpec((D_MODEL, 2 * KV_DIM), lambda b: (0, Q_DIM // (2 * KV_DIM)),
                         pipeline_mode=pl.Buffered(1)),
        ],
        out_specs=pl.BlockSpec((None, ctx_len, KV_W), lambda b: (b, 0, 0)),
        out_shape=jax.ShapeDtypeStruct((nb, ctx_len, KV_W), BF16),
        compiler_params=pltpu.CompilerParams(
            dimension_semantics=("arbitrary",), vmem_limit_bytes=VMEM_LIMIT),
        name="ctx_kv",
    )(ctx, mod, gmix, w_kv)


def _dot_nt(a, b):
    return lax.dot_general(a, b, (((1,), (1,)), ((), ())), preferred_element_type=F32)


def _attn_kernel(seq_len, tq, ctx_len, sink_ref, q_ref, kvp_ref, kvm_ref, kvn_ref, kvc_ref, o_ref,
                 kv_ext, *bufs):
    s_refs, e_refs, m_refs = bufs[0:2], bufs[2:6], bufs[6:10]
    kn_ref, minden_ref = bufs[10], bufs[11]
    i = pl.program_id(1)
    kv_ext[0:BLOCK, :] = kvp_ref[...]
    kv_ext[BLOCK:BLOCK + tq, :] = kvm_ref[...]
    kv_ext[BLOCK + tq:2 * BLOCK + tq, :] = kvn_ref[...]
    nblk = tq // BLOCK
    nwin = 3 * BLOCK
    nkey = nwin + ctx_len
    row = lax.broadcasted_iota(jnp.int32, (BLOCK, BLOCK), 0)
    col = lax.broadcasted_iota(jnp.int32, (BLOCK, BLOCK), 1)
    lane = lax.broadcasted_iota(jnp.int32, (BLOCK, LANES), 1)
    low = lane < HEAD_DIM
    head_a = (lane % HEAD_DIM) < AXIS_ROT
    my_lanes = (head_a, jnp.logical_not(head_a))
    den_lanes = (jnp.logical_not(low), low)

    chunks = [(g, t0) for g in range(GQA_GROUP) for t0 in range(0, BLOCK, SOFTMAX_ROWS)]

    def head_queries(j, kvh):
        q0 = pl.multiple_of(j * BLOCK, BLOCK)
        tiles = []
        for g in range(GQA_GROUP):
            pair = q_ref[pl.ds(q0, BLOCK), g * LANES:(g + 1) * LANES]
            tiles.append(jnp.where(my_lanes[kvh], pair, jnp.zeros_like(pair)))
        return tiles

    def window_masks(j):
        mask_l = (col >= row) & (i * tq + (j - 1) * BLOCK >= 0)
        mask_r = (col <= row) & (i * tq + (j + 1) * BLOCK < seq_len)
        return mask_l, mask_r

    def key_tiles(j, kvh):
        q0 = pl.multiple_of(j * BLOCK, BLOCK)
        return kv_ext[pl.ds(q0, nwin), 0:LANES], kvc_ref[:, 0:LANES]

    def scores(j, kvh):
        qs = jnp.concatenate(head_queries(j, kvh), axis=0)
        kw, kc = key_tiles(j, kvh)
        s_refs[kvh][:, 0:nwin] = _dot_nt(qs, kw)
        s_refs[kvh][:, nwin:nkey] = _dot_nt(qs, kc)

    def softmax(j, kvh):
        mask_l, mask_r = window_masks(j)

        def masked_pieces(rows, trows):
            pieces = [s_refs[kvh][rows, p * LANES:(p + 1) * LANES] for p in range(nkey // LANES)]
            pieces[0] = jnp.where(mask_l[trows], pieces[0], NEG)
            pieces[2] = jnp.where(mask_r[trows], pieces[2], NEG)
            return pieces

        for g, t0 in chunks:
            rows = slice(g * BLOCK + t0, g * BLOCK + t0 + SOFTMAX_ROWS)
            pieces = masked_pieces(rows, slice(t0, t0 + SOFTMAX_ROWS))
            m = pieces[0]
            for p in pieces[1:]:
                m = jnp.maximum(m, p)
            m = jnp.maximum(jnp.max(m, axis=1, keepdims=True), sink_ref[kvh * GQA_GROUP + g] * LOG2E)
            m_refs[kvh][rows, :] = jnp.broadcast_to(m, (SOFTMAX_ROWS, LANES))
        for g, t0 in chunks:
            rows = slice(g * BLOCK + t0, g * BLOCK + t0 + SOFTMAX_ROWS)
            pieces = masked_pieces(rows, slice(t0, t0 + SOFTMAX_ROWS))
            m = m_refs[kvh][rows, :]
            for p, piece in enumerate(pieces):
                e_refs[kvh][rows, p * LANES:(p + 1) * LANES] = jnp.exp2((piece - m).astype(BF16))

    def weighted_values(j, kvh, slot, fast):
        e_ref, m_ref = e_refs[2 * slot + kvh], m_refs[2 * slot + kvh]
        q0 = pl.multiple_of(j * BLOCK, BLOCK)
        vcols = slice((1 + kvh) * LANES, (2 + kvh) * LANES)
        values = jnp.concatenate([kv_ext[pl.ds(q0, nwin), vcols], kvc_ref[:, vcols]], axis=0)
        o = _dot(e_ref[...], values)
        on, smallest = [], None
        for g in range(GQA_GROUP):
            rows = slice(g * BLOCK, (g + 1) * BLOCK)
            sink = sink_ref[kvh * GQA_GROUP + g] * LOG2E
            if fast:
                sink_term = jnp.exp2(sink - m_ref[g * BLOCK:g * BLOCK + 1, :])
            else:
                sink_term = jnp.exp2(sink - m_ref[rows, :])
            total = o[rows] + sink_term
            smallest = total if smallest is None else jnp.minimum(smallest, total)
            on.append(o[rows] * pltpu.roll(1.0 / total, HEAD_DIM, 1))
        if fast:
            minden_ref[...] = jnp.minimum(minden_ref[...], jnp.where(den_lanes[kvh], smallest, 3e38))
        for pair in range(GQA_GROUP // 2):
            if kvh == 0:
                tile = jnp.where(low, on[2 * pair], pltpu.roll(on[2 * pair + 1], HEAD_DIM, 1))
            else:
                tile = jnp.where(low, pltpu.roll(on[2 * pair], HEAD_DIM, 1), on[2 * pair + 1])
            c0 = (kvh * GQA_GROUP + 2 * pair) * HEAD_DIM
            o_ref[pl.ds(q0, BLOCK), c0:c0 + LANES] = tile.astype(BF16)

    def exact_body(j, carry):
        scores(j, 0)
        scores(j, 1)
        softmax(j, 0)
        weighted_values(j, 0, 0, False)
        softmax(j, 1)
        weighted_values(j, 1, 0, False)
        return carry

    def max_sq_norms(k_tile):
        sq = k_tile.astype(F32)
        sq = sq * sq
        return [jnp.broadcast_to(jnp.max(jnp.sum(jnp.where(my_lanes[kvh], sq, 0.0), axis=1, keepdims=True),
                                         axis=0, keepdims=True), (8, LANES)) for kvh in range(N_KV_HEADS)]

    def store_key_norms(first, last):
        for kb in range(first, last):
            for kvh, n2 in enumerate(max_sq_norms(kv_ext[kb * BLOCK:(kb + 1) * BLOCK, 0:LANES])):
                kn_ref[kvh, kb] = n2

    store_key_norms(0, 3)
    ctx_norm = None
    for cb in range(ctx_len // BLOCK):
        n2 = max_sq_norms(kvc_ref[cb * BLOCK:(cb + 1) * BLOCK, 0:LANES])
        ctx_norm = n2 if ctx_norm is None else [jnp.maximum(a, b) for a, b in zip(ctx_norm, n2)]
    minden_ref[...] = jnp.full(minden_ref.shape, 3e38, F32)

    def bounded_exp(j, kvh, slot):
        e_ref, m_ref = e_refs[2 * slot + kvh], m_refs[2 * slot + kvh]
        q0 = pl.multiple_of(j * BLOCK, BLOCK)
        kw, kc = key_tiles(j, kvh)
        kn2 = jnp.maximum(jnp.max(kn_ref[kvh, pl.ds(j, 3)], axis=0), ctx_norm[kvh])[0:1, :]
        tiles = head_queries(j, kvh)
        for g in range(GQA_GROUP):
            qf = tiles[g].astype(F32)
            qn2 = jnp.max(jnp.sum(qf * qf, axis=1, keepdims=True), axis=0, keepdims=True)
            bound = jnp.maximum(jnp.sqrt(qn2 * kn2), sink_ref[kvh * GQA_GROUP + g] * LOG2E)
            m_ref[g * BLOCK:(g + 1) * BLOCK, :] = jnp.broadcast_to(bound, (BLOCK, LANES))
        qs = jnp.concatenate(tiles, axis=0)
        s_w = _dot_nt(qs, kw)
        s_c = _dot_nt(qs, kc)
        mask_l, mask_r = window_masks(j)
        for g, t0 in chunks:
            rows = slice(g * BLOCK + t0, g * BLOCK + t0 + SOFTMAX_ROWS)
            trows = slice(t0, t0 + SOFTMAX_ROWS)
            m = m_ref[rows, :]
            pieces = ([s_w[rows, p * LANES:(p + 1) * LANES] for p in range(nwin // LANES)]
                      + [s_c[rows, p * LANES:(p + 1) * LANES] for p in range(ctx_len // LANES)])
            pieces[0] = jnp.where(mask_l[trows], pieces[0], NEG)
            pieces[2] = jnp.where(mask_r[trows], pieces[2], NEG)
            for p, piece in enumerate(pieces):
                e_ref[rows, p * LANES:(p + 1) * LANES] = jnp.exp2(piece - m).astype(BF16)

    bounded_exp(0, 0, 0)
    bounded_exp(0, 1, 0)
    store_key_norms(3, nblk + 2)

    def fast_blocks(j0, last):
        for k in range(FAST_UNROLL):
            for kvh in range(N_KV_HEADS):
                weighted_values(j0 + k, kvh, k % 2, True)
                if not (last and k + 1 == FAST_UNROLL):
                    bounded_exp(j0 + k + 1, kvh, (k + 1) % 2)

    def fast_body(jj, carry):
        fast_blocks(FAST_UNROLL * jj, False)
        return carry

    lax.fori_loop(0, nblk // FAST_UNROLL - 1, fast_body, 0)
    fast_blocks(nblk - FAST_UNROLL, True)
    smallest = jnp.min(minden_ref[...])

    @pl.when(jnp.logical_not(smallest >= 2.0 ** MIN_SAFE_DENOMINATOR_LOG2))
    def _():
        lax.fori_loop(0, nblk, exact_body, 0)


def _attention(q, kv, kvc, sink, tq):
    nb, seq_len, _ = q.shape
    ctx_len = kvc.shape[1]
    nblk = seq_len // BLOCK
    qpb = tq // BLOCK
    rows = GQA_GROUP * BLOCK
    nkey = 3 * BLOCK + ctx_len
    return pl.pallas_call(
        functools.partial(_attn_kernel, seq_len, tq, ctx_len),
        grid=(nb, seq_len // tq),
        in_specs=[
            pl.BlockSpec(memory_space=pltpu.SMEM),
            pl.BlockSpec((None, tq, Q_DIM), lambda b, i: (b, i, 0)),
            pl.BlockSpec((None, BLOCK, KV_W), lambda b, i: (b, jnp.maximum(i * qpb - 1, 0), 0)),
            pl.BlockSpec((None, tq, KV_W), lambda b, i: (b, i, 0)),
            pl.BlockSpec((None, BLOCK, KV_W),
                         lambda b, i: (b, jnp.minimum((i + 1) * qpb, nblk - 1), 0)),
            pl.BlockSpec((None, ctx_len, KV_W), lambda b, i: (b, 0, 0)),
        ],
        out_specs=pl.BlockSpec((None, tq, Q_DIM), lambda b, i: (b, i, 0)),
        out_shape=jax.ShapeDtypeStruct((nb, seq_len, Q_DIM), BF16),
        scratch_shapes=([pltpu.VMEM((tq + 2 * BLOCK, KV_W), BF16)]
                        + [pltpu.VMEM((rows, nkey), F32)] * N_KV_HEADS
                        + [pltpu.VMEM((rows, nkey), BF16)] * (2 * N_KV_HEADS)
                        + [pltpu.VMEM((rows, LANES), F32)] * (2 * N_KV_HEADS)
                        + [pltpu.VMEM((N_KV_HEADS, qpb + 2, 8, LANES), F32),
                           pltpu.VMEM((BLOCK, LANES), F32)]),
        compiler_params=pltpu.CompilerParams(
            dimension_semantics=("arbitrary", "arbitrary"), vmem_limit_bytes=VMEM_LIMIT),
        name="attention",
    )(sink, q, kv, kv, kv, kvc)


def _post_kernel(layer, x_ref, o_ref_in, mod_ref, gmlp_ref, gfin_ref, wo_ref, w1_ref, w2_ref, out_ref):
    g1 = mod_ref[:, 2 * D_MODEL:3 * D_MODEL]
    sh2 = mod_ref[:, 3 * D_MODEL:4 * D_MODEL]
    sc2 = mod_ref[:, 4 * D_MODEL:5 * D_MODEL]
    g2 = mod_ref[:, 5 * D_MODEL:6 * D_MODEL]
    gain = gmlp_ref[layer:layer + 1, :]
    tm = x_ref.shape[0]
    halves = [slice(0, tm // 2), slice(tm // 2, tm)]
    x1 = [x_ref[r, :] + g1 * _dot(o_ref_in[r, :], wo_ref[...]) for r in halves]
    h = [_rms_mod(v, gain, sh2, sc2).astype(BF16) for v in x1]
    a = []
    for hv in h:
        up = jnp.maximum(_dot(hv, w1_ref[...]), 0.0)
        a.append((up * up).astype(BF16))
    for r, v, av in zip(halves, x1, a):
        x2 = v + g2 * _dot(av, w2_ref[...])
        y = x2 * lax.rsqrt(jnp.mean(x2 * x2, axis=-1, keepdims=True) + EPS)
        out_ref[r, :] = y * gfin_ref[...]


def _post(x, o, mod, layer, tm, gmlp, gfin, w_o, w1, w2):
    nb, seq_len, _ = x.shape
    return pl.pallas_call(
        functools.partial(_post_kernel, layer),
        grid=(nb, seq_len // tm),
        in_specs=[
            pl.BlockSpec((None, tm, D_MODEL), lambda b, i: (b, i, 0)),
            pl.BlockSpec((None, tm, Q_DIM), lambda b, i: (b, i, 0)),
            pl.BlockSpec((None, None, 1, N_MOD * D_MODEL), lambda b, i: (layer, b, 0, 0)),
            _layer_rows_spec(),
            _const_spec((1, D_MODEL)),
            _const_spec((Q_DIM, D_MODEL)),
            _const_spec((D_MODEL, D_FF)),
            _const_spec((D_FF, D_MODEL)),
        ],
        out_specs=pl.BlockSpec((None, tm, D_MODEL), lambda b, i: (b, i, 0)),
        out_shape=jax.ShapeDtypeStruct(x.shape, F32),
        compiler_params=pltpu.CompilerParams(
            dimension_semantics=("arbitrary", "arbitrary"), vmem_limit_bytes=VMEM_LIMIT),
        name="post_attn",
    )(x, o, mod, gmlp, gfin, w_o, w1, w2)


def _rope_tables(seq_len):
    pos = np.arange(seq_len)
    inv = ROPE_BASE ** (-np.arange(0, AXIS_ROT, 2, dtype=np.float32) / AXIS_ROT)
    ang = np.concatenate([(pos // GRID_W).astype(np.float32)[:, None] * inv[None],
                          (pos % GRID_W).astype(np.float32)[:, None] * inv[None]], axis=-1)
    cos = np.tile(np.cos(ang.astype(np.float64)), (1, LANES // AXIS_ROT))
    sin = np.sin(ang.astype(np.float64))
    return (jnp.asarray(cos, F32), jnp.asarray(np.concatenate([-sin, -sin, sin, sin], axis=-1), F32))


def kernel(x, c, ctx, c_ctx, ada_w, ada_b, norm_mix_g, norm_mlp_g, pool_w_in, pool_w_grp,
           pool_scale, pool_w_out, attn_w_qkv, attn_sink, attn_w_o, mlp_w1, mlp_w2, final_g):
    nb, seq_len, _ = x.shape
    ctx_len = ctx.shape[1]
    ctx_row = nb
    c8 = jnp.concatenate(
        [c, c_ctx[None], jnp.zeros((MOD_ROWS - nb - 1, D_MODEL), F32)], axis=0)
    n_grp = len(POOL_WINDOWS)
    layer0_weights = ((pool_w_in, 0), (pool_w_grp.reshape(-1, n_grp * POOL_GROUP_DIM, POOL_GROUP_DIM), 0),
                      (pool_w_out, 0), (mlp_w1, 0), (mlp_w2, 0))
    mod, (w_in, w_grp, w_out, w1_0, w2_0) = _modulation(c8, ada_w, ada_b, layer0_weights)

    gmix, gmlp = norm_mix_g, norm_mlp_g

    l0 = (gmix, gmlp, pool_scale, w_in,
          w_grp.reshape(n_grp, POOL_GROUP_DIM, POOL_GROUP_DIM), w_out, w1_0, w2_0)
    layer1_weights = ((attn_w_qkv, 0), (attn_w_o, 0), (mlp_w1, 1), (mlp_w2, 1))
    x1, (w_qkv, w_o, w1, w2) = _layer0(x, mod, None, 0, LAYER_TILE, True, layer1_weights, *l0)
    ctx1, _ = _layer0(ctx, mod, ctx_row, 0, ctx_len, False, (), *l0)

    cos, sin = _rope_tables(seq_len)
    q, kv = _qkv(x1, mod, 1, QKV_TILE, gmix, w_qkv, cos, sin)
    kvc = _ctx_kv(ctx1, mod, ctx_row, 1, gmix, w_qkv)
    o = _attention(q, kv, kvc, attn_sink[0], ATTN_TILE)
    return _post(x1, o, mod, 1, LAYER_TILE, gmlp, final_g.reshape(1, D_MODEL), w_o, w1, w2)
```

```python
import functools

import numpy as np
import jax
import jax.numpy as jnp
from jax import lax
from jax.experimental import pallas as pl
from jax.experimental.pallas import tpu as pltpu

D_MODEL = 1024
DEPTH = 2
GRID_W = 64
POOL_WINDOWS = (2, 4, 8, 16)
POOL_GROUP_DIM = D_MODEL // len(POOL_WINDOWS)
POOL_HALO = 8
HEAD_DIM = 64
N_HEADS = D_MODEL // HEAD_DIM
N_KV_HEADS = 2
GQA_GROUP = N_HEADS // N_KV_HEADS
Q_DIM = N_HEADS * HEAD_DIM
KV_DIM = N_KV_HEADS * HEAD_DIM
WINDOW = 128
BLOCK = 128
ROPE_BASE = 10000.0
AXIS_ROT = HEAD_DIM // 2
ROT_HALF = AXIS_ROT // 2
D_FF = 4 * D_MODEL
N_MOD = 6
EPS = 1e-6
NEG = -1e30
MOD_ROWS = 8
LANES = 128
LOG2E = 1.4426950408889634
KV_W = 3 * LANES
SOFTMAX_ROWS = 32
MIN_SAFE_DENOMINATOR_LOG2 = -64.0
QKV_COL_GROUP = 256
FAST_UNROLL = 2
VMEM_LIMIT = 56 * 1024 * 1024

LAYER_TILE = 512
QKV_TILE = 1024
ATTN_TILE = 2048
MOD_COLS = 1536

BF16 = jnp.bfloat16
F32 = jnp.float32


def _const_spec(shape):
    nd = len(shape)
    return pl.BlockSpec(shape, lambda *_: (0,) * nd, pipeline_mode=pl.Buffered(1))


def _layer_rows_spec():
    return _const_spec((DEPTH, D_MODEL))


def _dot(a, b):
    return jnp.dot(a, b, preferred_element_type=F32)


def _rms_mod(x, g, shift, scale):
    y = x * lax.rsqrt(jnp.mean(x * x, axis=-1, keepdims=True) + EPS)
    return y * (g * (1.0 + scale)) + shift


def _mlp_residual(x, mod_ref, gain, w1_ref, w2_ref):
    sh2 = mod_ref[:, 3 * D_MODEL:4 * D_MODEL]
    sc2 = mod_ref[:, 4 * D_MODEL:5 * D_MODEL]
    g2 = mod_ref[:, 5 * D_MODEL:6 * D_MODEL]
    h = _rms_mod(x, gain, sh2, sc2).astype(BF16)
    a = jnp.maximum(_dot(h, w1_ref[...]), 0.0)
    a = (a * a).astype(BF16)
    return x + g2 * _dot(a, w2_ref[...])


def _staged_steps(step, n_tiles, n_stages, body):
    assert n_tiles >= n_stages
    for t in range(n_stages - 1):
        pl.when(step == t)(functools.partial(body, *[s <= t for s in range(n_stages)]))
    pl.when((step >= n_stages - 1) & (step < n_tiles))(functools.partial(body, *[True] * n_stages))
    for k in range(n_stages - 1):
        pl.when(step == n_tiles + k)(functools.partial(body, *[s > k for s in range(n_stages)]))


def _skewed_steps(step, n_tiles, body):
    _staged_steps(step, n_tiles, 2, body)


def _stage_tile(n_tiles, tiles_per_row, stage):
    def tile(t):
        k = jnp.clip(t - stage, 0, n_tiles - 1)
        return k // tiles_per_row, k % tiles_per_row
    return tile


def _skew_maps(n_tiles, tiles_per_row):
    return _stage_tile(n_tiles, tiles_per_row, 0), _stage_tile(n_tiles, tiles_per_row, 1)


def _mod_kernel(n_prep, c_ref, w_ref, b_ref, *refs):
    prep_in, o_ref, prep_out = refs[:n_prep], refs[n_prep], refs[n_prep + 1:]
    c = c_ref[...]
    s = (c * jax.nn.sigmoid(c)).astype(BF16)
    r = _dot(s, w_ref[...].astype(BF16)) + b_ref[...]
    for row in range(MOD_ROWS):
        o_ref[row] = r[row:row + 1, :]
    for src, dst in zip(prep_in, prep_out):
        dst[...] = src[...].astype(BF16)


def _modulation(c8, ada_w, ada_b, first_layer_weights):
    tn = MOD_COLS
    n = N_MOD * D_MODEL
    n_col = n // tn
    steps = DEPTH * n_col
    prep_in_specs, prep_out_specs = [], []
    for w, idx in first_layer_weights:
        rows = w.shape[1] // steps
        prep_in_specs.append(pl.BlockSpec(
            (None, rows, w.shape[2]), lambda i, j, idx=idx: (idx, i * n_col + j, 0)))
        prep_out_specs.append(pl.BlockSpec((rows, w.shape[2]), lambda i, j: (i * n_col + j, 0)))
    outs = pl.pallas_call(
        functools.partial(_mod_kernel, len(first_layer_weights)),
        grid=(DEPTH, n_col),
        in_specs=[
            pl.BlockSpec((MOD_ROWS, D_MODEL), lambda i, j: (0, 0)),
            pl.BlockSpec((None, D_MODEL, tn), lambda i, j: (i, 0, j)),
            pl.BlockSpec((None, 1, tn), lambda i, j: (i, 0, j)),
        ] + prep_in_specs,
        out_specs=[pl.BlockSpec((None, MOD_ROWS, 1, tn), lambda i, j: (i, 0, 0, j))] + prep_out_specs,
        out_shape=[jax.ShapeDtypeStruct((DEPTH, MOD_ROWS, 1, n), F32)]
        + [jax.ShapeDtypeStruct(w.shape[1:], BF16) for w, _ in first_layer_weights],
        compiler_params=pltpu.CompilerParams(
            dimension_semantics=("arbitrary", "arbitrary"), vmem_limit_bytes=VMEM_LIMIT),
        name="modulation",
    )(c8, ada_w, ada_b.reshape(DEPTH, 1, n), *[w for w, _ in first_layer_weights])
    return outs[0], tuple(outs[1:])


def _to_pair_layout(v):
    lane = lax.broadcasted_iota(jnp.int32, v.shape, 1)
    for width in (ROT_HALF, AXIS_ROT):
        hi = (lane // (2 * width)) % 2
        lo = (lane // width) % 2
        v = jnp.where(hi == lo, v, jnp.where(hi == 0, pltpu.roll(v, LANES - width, 1),
                                             pltpu.roll(v, width, 1)))
    return v


def _prepare_next_weights(prep_in, prep_out):
    wqkv_in, wqkv_out = prep_in[0], prep_out[0]
    low = lax.broadcasted_iota(jnp.int32, (wqkv_in.shape[0], LANES), 1) < HEAD_DIM
    for p in range(GQA_GROUP):
        first = wqkv_in[:, (p // 2) * LANES:(p // 2 + 1) * LANES]
        second = wqkv_in[:, (p // 2 + GQA_GROUP // 2) * LANES:(p // 2 + GQA_GROUP // 2 + 1) * LANES]
        if p % 2 == 0:
            both = jnp.where(low, first, pltpu.roll(second, HEAD_DIM, 1))
        else:
            both = jnp.where(low, pltpu.roll(first, HEAD_DIM, 1), second)
        wqkv_out[:, p * LANES:(p + 1) * LANES] = _to_pair_layout(both).astype(BF16)
    wqkv_out[:, Q_DIM:Q_DIM + KV_DIM] = _to_pair_layout(wqkv_in[:, Q_DIM:Q_DIM + KV_DIM]).astype(BF16)
    wqkv_out[:, Q_DIM + KV_DIM:] = wqkv_in[:, Q_DIM + KV_DIM:].astype(BF16)
    for src, dst in zip(prep_in[1:], prep_out[1:]):
        dst[...] = src[...].astype(BF16)


def _layer0_kernel(seq_len, tm, n_tiles, skew, n_prep, layer, *refs):
    (x_ref, xp_ref, xn_ref, moda_ref, modb_ref, gmix_ref, gmlp_ref, pscale_ref, win_ref, wgrp_ref,
     wout_ref, w1_ref, w2_ref) = refs[:13]
    prep_in = refs[13:13 + n_prep]
    o_ref = refs[13 + n_prep]
    prep_out = refs[14 + n_prep:14 + 2 * n_prep]
    u_ref, d_ref, x1_ref, h2_ref = refs[14 + 2 * n_prep:]
    step = pl.program_id(0)
    i = jnp.minimum(step, n_tiles - 1) % (seq_len // tm)
    half = D_FF // 2

    def mlp_up(h2, c0, c1):
        a = jnp.maximum(_dot(h2, w1_ref[:, c0:c1]), 0.0)
        return (a * a).astype(BF16)

    def body(do_mixer, do_prev_mlp):
        if do_prev_mlp:
            g2 = modb_ref[:, 5 * D_MODEL:6 * D_MODEL]
            h2_prev = h2_ref[...]
            a_lo = mlp_up(h2_prev, 0, half)

        if do_mixer:
            sh1 = moda_ref[:, 0:D_MODEL]
            sc1 = moda_ref[:, D_MODEL:2 * D_MODEL]
            x = x_ref[...]
            xe = jnp.concatenate([xp_ref[...], x, xn_ref[...]], axis=0)
            u = _dot(_rms_mod(xe, gmix_ref[layer:layer + 1, :], sh1, sc1).astype(BF16), win_ref[...])
            u_ref[0:POOL_HALO, :] = jnp.where(i > 0, u[0:POOL_HALO], 0.0)
            u_ref[POOL_HALO:POOL_HALO + tm, :] = u[POOL_HALO:POOL_HALO + tm]
            u_ref[POOL_HALO + tm:POOL_HALO + tm + POOL_HALO, :] = jnp.where(
                (i + 1) * tm < seq_len, u[POOL_HALO + tm:], 0.0)

        if do_prev_mlp:
            a_hi = mlp_up(h2_prev, half, D_FF)

        if do_mixer:
            t = i * tm + lax.broadcasted_iota(jnp.int32, (tm, 1), 0)
            for g, w in enumerate(POOL_WINDOWS):
                c0, c1 = g * POOL_GROUP_DIM, (g + 1) * POOL_GROUP_DIM
                acc = u_ref[POOL_HALO - w // 2:POOL_HALO - w // 2 + tm, c0:c1]
                for s in range(-w // 2 + 1, w // 2):
                    acc = acc + u_ref[POOL_HALO + s:POOL_HALO + s + tm, c0:c1]
                lo = jnp.clip(t - w // 2, 0, seq_len)
                hi = jnp.clip(t + w // 2, 0, seq_len)
                inv_cnt = 1.0 / (hi - lo).astype(F32)
                d = acc * inv_cnt - u_ref[POOL_HALO:POOL_HALO + tm, c0:c1]
                d_ref[:, c0:c1] = _dot(d.astype(BF16), wgrp_ref[g])
            y = _dot((d_ref[...] * pscale_ref[...]).astype(BF16), wout_ref[...])

        if do_prev_mlp:
            down = _dot(jnp.concatenate([a_lo, a_hi], axis=1), w2_ref[...])

        if do_mixer:
            g1 = moda_ref[:, 2 * D_MODEL:3 * D_MODEL]
            sh2 = moda_ref[:, 3 * D_MODEL:4 * D_MODEL]
            sc2 = moda_ref[:, 4 * D_MODEL:5 * D_MODEL]
            x1 = x + g1 * y
            h2 = _rms_mod(x1, gmlp_ref[layer:layer + 1, :], sh2, sc2).astype(BF16)

        if do_prev_mlp:
            o_ref[...] = x1_ref[...] + g2 * down
        if do_mixer and n_prep:
            _prepare_next_weights(prep_in, prep_out)
        if do_mixer and skew:
            x1_ref[...] = x1
            h2_ref[...] = h2
        if do_mixer and not skew:
            g2_now = moda_ref[:, 5 * D_MODEL:6 * D_MODEL]
            o_ref[...] = x1 + g2_now * _dot(mlp_up(h2, 0, D_FF), w2_ref[...])

    if skew:
        _skewed_steps(step, n_tiles, body)
    else:
        body(True, False)


def _layer0(x, mod, mod_row, layer, tm, skew, next_weights, gmix, gmlp, pscale, w_in, w_grp, w_out, w1, w2):
    nb, seq_len, _ = x.shape
    rows8 = seq_len // POOL_HALO
    tpb = tm // POOL_HALO
    nt = seq_len // tm
    n_tiles = nb * nt
    tile_a, tile_b = _skew_maps(n_tiles, nt)
    if not skew:
        tile_b = tile_a

    def x_map(t):
        b, i = tile_a(t)
        return (b, i, 0)

    def prev_map(t):
        b, i = tile_a(t)
        return (b, jnp.maximum(i * tpb - 1, 0), 0)

    def next_map(t):
        b, i = tile_a(t)
        return (b, jnp.minimum((i + 1) * tpb, rows8 - 1), 0)

    def moda_map(t):
        return (layer, tile_a(t)[0] if mod_row is None else mod_row, 0, 0)

    def modb_map(t):
        return (layer, tile_b(t)[0] if mod_row is None else mod_row, 0, 0)

    def out_map(t):
        b, i = tile_b(t)
        return (b, i, 0)

    prep_in_specs, prep_out_specs = [], []
    for w, idx in next_weights:
        rows = w.shape[1] // n_tiles
        prep_in_specs.append(pl.BlockSpec(
            (None, rows, w.shape[2]), lambda t, idx=idx: (idx, jnp.minimum(t, n_tiles - 1), 0)))
        prep_out_specs.append(pl.BlockSpec(
            (rows, w.shape[2]), lambda t: (jnp.minimum(t, n_tiles - 1), 0)))

    outs = pl.pallas_call(
        functools.partial(_layer0_kernel, seq_len, tm, n_tiles, skew, len(next_weights), layer),
        grid=(n_tiles + (1 if skew else 0),),
        in_specs=[
            pl.BlockSpec((None, tm, D_MODEL), x_map),
            pl.BlockSpec((None, POOL_HALO, D_MODEL), prev_map),
            pl.BlockSpec((None, POOL_HALO, D_MODEL), next_map),
            pl.BlockSpec((None, None, 1, N_MOD * D_MODEL), moda_map),
            pl.BlockSpec((None, None, 1, N_MOD * D_MODEL), modb_map),
            _layer_rows_spec(),
            _layer_rows_spec(),
            _const_spec((1, D_MODEL)),
            _const_spec((D_MODEL, D_MODEL)),
            _const_spec((len(POOL_WINDOWS), POOL_GROUP_DIM, POOL_GROUP_DIM)),
            _const_spec((D_MODEL, D_MODEL)),
            _const_spec((D_MODEL, D_FF)),
            _const_spec((D_FF, D_MODEL)),
        ] + prep_in_specs,
        out_specs=[pl.BlockSpec((None, tm, D_MODEL), out_map)] + prep_out_specs,
        out_shape=[jax.ShapeDtypeStruct(x.shape, F32)]
        + [jax.ShapeDtypeStruct(w.shape[1:], BF16) for w, _ in next_weights],
        scratch_shapes=[pltpu.VMEM((tm + 2 * POOL_HALO, D_MODEL), F32),
                        pltpu.VMEM((tm, D_MODEL), F32),
                        pltpu.VMEM((tm, D_MODEL), F32),
                        pltpu.VMEM((tm, D_MODEL), BF16)],
        compiler_params=pltpu.CompilerParams(
            dimension_semantics=("arbitrary",), vmem_limit_bytes=VMEM_LIMIT),
        name="layer0",
    )(x, x, x, mod, mod, gmix, gmlp, pscale, w_in, w_grp, w_out, w1, w2, *[w for w, _ in next_weights])
    return outs[0], tuple(outs[1:])


def _store_kv(kv_ref, k, v):
    low = lax.broadcasted_iota(jnp.int32, v.shape, 1) < HEAD_DIM
    kv_ref[:, 0:LANES] = k.astype(BF16)
    kv_ref[:, LANES:2 * LANES] = jnp.where(low, v, 1.0).astype(BF16)
    kv_ref[:, 2 * LANES:3 * LANES] = jnp.where(low, 1.0, v).astype(BF16)


def _qkv_kernel(layer, x_ref, mod_ref, gmix_ref, w_ref, cos_ref, sin_ref, q_ref, kv_ref):
    sh1 = mod_ref[:, 0:D_MODEL]
    sc1 = mod_ref[:, D_MODEL:2 * D_MODEL]
    gmix = gmix_ref[layer:layer + 1, :]
    tm = x_ref.shape[0]
    halves = [slice(0, tm // 2), slice(tm // 2, tm)]
    h = [_rms_mod(x_ref[r, :], gmix, sh1, sc1).astype(BF16) for r in halves]
    q_groups = [[_dot(hv, w_ref[:, c:c + QKV_COL_GROUP]) for c in range(0, Q_DIM, QKV_COL_GROUP)] for hv in h]
    kv = [_dot(hv, w_ref[:, Q_DIM:]) for hv in h]
    qscale = HEAD_DIM ** -0.5 * LOG2E

    def rope(t, c, s):
        return t * c + pltpu.roll(t, HEAD_DIM, 1) * s

    for r, groups, kv_half in zip(halves, q_groups, kv):
        cos = cos_ref[r, :]
        sin = sin_ref[r, :]
        cos_q = cos * qscale
        sin_q = sin * qscale
        for gi, group in enumerate(groups):
            for cb in range(QKV_COL_GROUP // LANES):
                c0 = gi * QKV_COL_GROUP + cb * LANES
                q_ref[r, c0:c0 + LANES] = rope(group[:, cb * LANES:(cb + 1) * LANES], cos_q, sin_q).astype(BF16)
        _store_kv(kv_ref.at[r, :], rope(kv_half[:, 0:KV_DIM], cos, sin), kv_half[:, KV_DIM:])


def _qkv(x, mod, layer, tm, gmix, w_qkv, cos, sin):
    nb, seq_len, _ = x.shape
    return pl.pallas_call(
        functools.partial(_qkv_kernel, layer),
        grid=(nb, seq_len // tm),
        in_specs=[
            pl.BlockSpec((None, tm, D_MODEL), lambda b, i: (b, i, 0)),
            pl.BlockSpec((None, None, 1, N_MOD * D_MODEL), lambda b, i: (layer, b, 0, 0)),
            _layer_rows_spec(),
            _const_spec((D_MODEL, Q_DIM + 2 * KV_DIM)),
            pl.BlockSpec((tm, LANES), lambda b, i: (i, 0)),
            pl.BlockSpec((tm, LANES), lambda b, i: (i, 0)),
        ],
        out_specs=[pl.BlockSpec((None, tm, Q_DIM), lambda b, i: (b, i, 0)),
                   pl.BlockSpec((None, tm, KV_W), lambda b, i: (b, i, 0))],
        out_shape=[jax.ShapeDtypeStruct((nb, seq_len, Q_DIM), BF16),
                   jax.ShapeDtypeStruct((nb, seq_len, KV_W), BF16)],
        compiler_params=pltpu.CompilerParams(
            dimension_semantics=("arbitrary", "arbitrary"), vmem_limit_bytes=VMEM_LIMIT),
        name="qkv_rope",
    )(x, mod, gmix, w_qkv, cos, sin)


def _ctx_kv_kernel(layer, x_ref, mod_ref, gmix_ref, w_ref, kv_ref):
    sh1 = mod_ref[:, 0:D_MODEL]
    sc1 = mod_ref[:, D_MODEL:2 * D_MODEL]
    h = _rms_mod(x_ref[...], gmix_ref[layer:layer + 1, :], sh1, sc1).astype(BF16)
    kv = _dot(h, w_ref[...])
    _store_kv(kv_ref, kv[:, 0:KV_DIM], kv[:, KV_DIM:])


def _ctx_kv(ctx, mod, mod_row, layer, gmix, w_kv):
    nb, ctx_len, _ = ctx.shape
    return pl.pallas_call(
        functools.partial(_ctx_kv_kernel, layer),
        grid=(nb,),
        in_specs=[
            pl.BlockSpec((None, ctx_len, D_MODEL), lambda b: (b, 0, 0)),
            pl.BlockSpec((None, None, 1, N_MOD * D_MODEL), lambda b: (layer, mod_row, 0, 0)),
            _layer_rows_spec(),
            pl.BlockSpec((D_MODEL, 2 * KV_DIM), lambda b: (0, Q_DIM // (2 * KV_DIM)),
                         pipeline_mode=pl.Buffered(1)),
        ],
        out_specs=pl.BlockSpec((None, ctx_len, KV_W), lambda b: (b, 0, 0)),
        out_shape=jax.ShapeDtypeStruct((nb, ctx_len, KV_W), BF16),
        compiler_params=pltpu.CompilerParams(
            dimension_semantics=("arbitrary",), vmem_limit_bytes=VMEM_LIMIT),
        name="ctx_kv",
    )(ctx, mod, gmix, w_kv)


def _dot_nt(a, b):
    return lax.dot_general(a, b, (((1,), (1,)), ((), ())), preferred_element_type=F32)


def _attn_kernel(seq_len, tq, ctx_len, sink_ref, q_ref, kvp_ref, kvm_ref, kvn_ref, kvc_ref, o_ref,
                 kv_ext, *bufs):
    s_refs, e_refs, m_refs = bufs[0:2], bufs[2:6], bufs[6:10]
    kn_ref, minden_ref = bufs[10], bufs[11]
    i = pl.program_id(1)
    kv_ext[0:BLOCK, :] = kvp_ref[...]
    kv_ext[BLOCK:BLOCK + tq, :] = kvm_ref[...]
    kv_ext[BLOCK + tq:2 * BLOCK + tq, :] = kvn_ref[...]
    nblk = tq // BLOCK
    nwin = 3 * BLOCK
    nkey = nwin + ctx_len
    row = lax.broadcasted_iota(jnp.int32, (BLOCK, BLOCK), 0)
    col = lax.broadcasted_iota(jnp.int32, (BLOCK, BLOCK), 1)
    lane = lax.broadcasted_iota(jnp.int32, (BLOCK, LANES), 1)
    low = lane < HEAD_DIM
    head_a = (lane % HEAD_DIM) < AXIS_ROT
    my_lanes = (head_a, jnp.logical_not(head_a))
    den_lanes = (jnp.logical_not(low), low)

    chunks = [(g, t0) for g in range(GQA_GROUP) for t0 in range(0, BLOCK, SOFTMAX_ROWS)]

    def head_queries(j, kvh):
        q0 = pl.multiple_of(j * BLOCK, BLOCK)
        tiles = []
        for g in range(GQA_GROUP):
            pair = q_ref[pl.ds(q0, BLOCK), g * LANES:(g + 1) * LANES]
            tiles.append(jnp.where(my_lanes[kvh], pair, jnp.zeros_like(pair)))
        return tiles

    def window_masks(j):
        mask_l = (col >= row) & (i * tq + (j - 1) * BLOCK >= 0)
        mask_r = (col <= row) & (i * tq + (j + 1) * BLOCK < seq_len)
        return mask_l, mask_r

    def key_tiles(j, kvh):
        q0 = pl.multiple_of(j * BLOCK, BLOCK)
        return kv_ext[pl.ds(q0, nwin), 0:LANES], kvc_ref[:, 0:LANES]

    def scores(j, kvh):
        qs = jnp.concatenate(head_queries(j, kvh), axis=0)
        kw, kc = key_tiles(j, kvh)
        s_refs[kvh][:, 0:nwin] = _dot_nt(qs, kw)
        s_refs[kvh][:, nwin:nkey] = _dot_nt(qs, kc)

    def softmax(j, kvh):
        mask_l, mask_r = window_masks(j)

        def masked_pieces(rows, trows):
            pieces = [s_refs[kvh][rows, p * LANES:(p + 1) * LANES] for p in range(nkey // LANES)]
            pieces[0] = jnp.where(mask_l[trows], pieces[0], NEG)
            pieces[2] = jnp.where(mask_r[trows], pieces[2], NEG)
            return pieces

        for g, t0 in chunks:
            rows = slice(g * BLOCK + t0, g * BLOCK + t0 + SOFTMAX_ROWS)
            pieces = masked_pieces(rows, slice(t0, t0 + SOFTMAX_ROWS))
            m = pieces[0]
            for p in pieces[1:]:
                m = jnp.maximum(m, p)
            m = jnp.maximum(jnp.max(m, axis=1, keepdims=True), sink_ref[kvh * GQA_GROUP + g] * LOG2E)
            m_refs[kvh][rows, :] = jnp.broadcast_to(m, (SOFTMAX_ROWS, LANES))
        for g, t0 in chunks:
            rows = slice(g * BLOCK + t0, g * BLOCK + t0 + SOFTMAX_ROWS)
            pieces = masked_pieces(rows, slice(t0, t0 + SOFTMAX_ROWS))
            m = m_refs[kvh][rows, :]
            for p, piece in enumerate(pieces):
                e_refs[kvh][rows, p * LANES:(p + 1) * LANES] = jnp.exp2((piece - m).astype(BF16))

    def weighted_values(j, kvh, slot, fast):
        e_ref, m_ref = e_refs[2 * slot + kvh], m_refs[2 * slot + kvh]
        q0 = pl.multiple_of(j * BLOCK, BLOCK)
        vcols = slice((1 + kvh) * LANES, (2 + kvh) * LANES)
        values = jnp.concatenate([kv_ext[pl.ds(q0, nwin), vcols], kvc_ref[:, vcols]], axis=0)
        o = _dot(e_ref[...], values)
        on, smallest = [], None
        for g in range(GQA_GROUP):
            rows = slice(g * BLOCK, (g + 1) * BLOCK)
            sink = sink_ref[kvh * GQA_GROUP + g] * LOG2E
            if fast:
                sink_term = jnp.exp2(sink - m_ref[g * BLOCK:g * BLOCK + 1, :])
            else:
                sink_term = jnp.exp2(sink - m_ref[rows, :])
            total = o[rows] + sink_term
            smallest = total if smallest is None else jnp.minimum(smallest, total)
            on.append(o[rows] * pltpu.roll(1.0 / total, HEAD_DIM, 1))
        if fast:
            minden_ref[...] = jnp.minimum(minden_ref[...], jnp.where(den_lanes[kvh], smallest, 3e38))
        for pair in range(GQA_GROUP // 2):
            if kvh == 0:
                tile = jnp.where(low, on[2 * pair], pltpu.roll(on[2 * pair + 1], HEAD_DIM, 1))
            else:
                tile = jnp.where(low, pltpu.roll(on[2 * pair], HEAD_DIM, 1), on[2 * pair + 1])
            c0 = (kvh * GQA_GROUP + 2 * pair) * HEAD_DIM
            o_ref[pl.ds(q0, BLOCK), c0:c0 + LANES] = tile.astype(BF16)

    def exact_body(j, carry):
        scores(j, 0)
        scores(j, 1)
        softmax(j, 0)
        weighted_values(j, 0, 0, False)
        softmax(j, 1)
        weighted_values(j, 1, 0, False)
        return carry

    def max_sq_norms(k_tile):
        sq = k_tile.astype(F32)
        sq = sq * sq
        return [jnp.broadcast_to(jnp.max(jnp.sum(jnp.where(my_lanes[kvh], sq, 0.0), axis=1, keepdims=True),
                                         axis=0, keepdims=True), (8, LANES)) for kvh in range(N_KV_HEADS)]

    def store_key_norms(first, last):
        for kb in range(first, last):
            for kvh, n2 in enumerate(max_sq_norms(kv_ext[kb * BLOCK:(kb + 1) * BLOCK, 0:LANES])):
                kn_ref[kvh, kb] = n2

    store_key_norms(0, 3)
    ctx_norm = None
    for cb in range(ctx_len // BLOCK):
        n2 = max_sq_norms(kvc_ref[cb * BLOCK:(cb + 1) * BLOCK, 0:LANES])
        ctx_norm = n2 if ctx_norm is None else [jnp.maximum(a, b) for a, b in zip(ctx_norm, n2)]
    minden_ref[...] = jnp.full(minden_ref.shape, 3e38, F32)

    def bounded_exp(j, kvh, slot):
        e_ref, m_ref = e_refs[2 * slot + kvh], m_refs[2 * slot + kvh]
        q0 = pl.multiple_of(j * BLOCK, BLOCK)
        kw, kc = key_tiles(j, kvh)
        kn2 = jnp.maximum(jnp.max(kn_ref[kvh, pl.ds(j, 3)], axis=0), ctx_norm[kvh])[0:1, :]
        tiles = head_queries(j, kvh)
        for g in range(GQA_GROUP):
            qf = tiles[g].astype(F32)
            qn2 = jnp.max(jnp.sum(qf * qf, axis=1, keepdims=True), axis=0, keepdims=True)
            bound = jnp.maximum(jnp.sqrt(qn2 * kn2), sink_ref[kvh * GQA_GROUP + g] * LOG2E)
            m_ref[g * BLOCK:(g + 1) * BLOCK, :] = jnp.broadcast_to(bound, (BLOCK, LANES))
        qs = jnp.concatenate(tiles, axis=0)
        s_w = _dot_nt(qs, kw)
        s_c = _dot_nt(qs, kc)
        mask_l, mask_r = window_masks(j)
        for g, t0 in chunks:
            rows = slice(g * BLOCK + t0, g * BLOCK + t0 + SOFTMAX_ROWS)
            trows = slice(t0, t0 + SOFTMAX_ROWS)
            m = m_ref[rows, :]
            pieces = ([s_w[rows, p * LANES:(p + 1) * LANES] for p in range(nwin // LANES)]
                      + [s_c[rows, p * LANES:(p + 1) * LANES] for p in range(ctx_len // LANES)])
            pieces[0] = jnp.where(mask_l[trows], pieces[0], NEG)
            pieces[2] = jnp.where(mask_r[trows], pieces[2], NEG)
            for p, piece in enumerate(pieces):
                e_ref[rows, p * LANES:(p + 1) * LANES] = jnp.exp2(piece - m).astype(BF16)

    bounded_exp(0, 0, 0)
    bounded_exp(0, 1, 0)
    store_key_norms(3, nblk + 2)

    def fast_blocks(j0, last):
        for k in range(FAST_UNROLL):
            for kvh in range(N_KV_HEADS):
                weighted_values(j0 + k, kvh, k % 2, True)
                if not (last and k + 1 == FAST_UNROLL):
                    bounded_exp(j0 + k + 1, kvh, (k + 1) % 2)

    def fast_body(jj, carry):
        fast_blocks(FAST_UNROLL * jj, False)
        return carry

    lax.fori_loop(0, nblk // FAST_UNROLL - 1, fast_body, 0)
    fast_blocks(nblk - FAST_UNROLL, True)
    smallest = jnp.min(minden_ref[...])

    @pl.when(jnp.logical_not(smallest >= 2.0 ** MIN_SAFE_DENOMINATOR_LOG2))
    def _():
        lax.fori_loop(0, nblk, exact_body, 0)


def _attention(q, kv, kvc, sink, tq):
    nb, seq_len, _ = q.shape
    ctx_len = kvc.shape[1]
    nblk = seq_len // BLOCK
    qpb = tq // BLOCK
    rows = GQA_GROUP * BLOCK
    nkey = 3 * BLOCK + ctx_len
    return pl.pallas_call(
        functools.partial(_attn_kernel, seq_len, tq, ctx_len),
        grid=(nb, seq_len // tq),
        in_specs=[
            pl.BlockSpec(memory_space=pltpu.SMEM),
            pl.BlockSpec((None, tq, Q_DIM), lambda b, i: (b, i, 0)),
            pl.BlockSpec((None, BLOCK, KV_W), lambda b, i: (b, jnp.maximum(i * qpb - 1, 0), 0)),
            pl.BlockSpec((None, tq, KV_W), lambda b, i: (b, i, 0)),
            pl.BlockSpec((None, BLOCK, KV_W),
                         lambda b, i: (b, jnp.minimum((i + 1) * qpb, nblk - 1), 0)),
            pl.BlockSpec((None, ctx_len, KV_W), lambda b, i: (b, 0, 0)),
        ],
        out_specs=pl.BlockSpec((None, tq, Q_DIM), lambda b, i: (b, i, 0)),
        out_shape=jax.ShapeDtypeStruct((nb, seq_len, Q_DIM), BF16),
        scratch_shapes=([pltpu.VMEM((tq + 2 * BLOCK, KV_W), BF16)]
                        + [pltpu.VMEM((rows, nkey), F32)] * N_KV_HEADS
                        + [pltpu.VMEM((rows, nkey), BF16)] * (2 * N_KV_HEADS)
                        + [pltpu.VMEM((rows, LANES), F32)] * (2 * N_KV_HEADS)
                        + [pltpu.VMEM((N_KV_HEADS, qpb + 2, 8, LANES), F32),
                           pltpu.VMEM((BLOCK, LANES), F32)]),
        compiler_params=pltpu.CompilerParams(
            dimension_semantics=("arbitrary", "arbitrary"), vmem_limit_bytes=VMEM_LIMIT),
        name="attention",
    )(sink, q, kv, kv, kv, kvc)


def _post_kernel(layer, x_ref, o_ref_in, mod_ref, gmlp_ref, gfin_ref, wo_ref, w1_ref, w2_ref, out_ref):
    g1 = mod_ref[:, 2 * D_MODEL:3 * D_MODEL]
    sh2 = mod_ref[:, 3 * D_MODEL:4 * D_MODEL]
    sc2 = mod_ref[:, 4 * D_MODEL:5 * D_MODEL]
    g2 = mod_ref[:, 5 * D_MODEL:6 * D_MODEL]
    gain = gmlp_ref[layer:layer + 1, :]
    tm = x_ref.shape[0]
    halves = [slice(0, tm // 2), slice(tm // 2, tm)]
    x1 = [x_ref[r, :] + g1 * _dot(o_ref_in[r, :], wo_ref[...]) for r in halves]
    h = [_rms_mod(v, gain, sh2, sc2).astype(BF16) for v in x1]
    a = []
    for hv in h:
        up = jnp.maximum(_dot(hv, w1_ref[...]), 0.0)
        a.append((up * up).astype(BF16))
    for r, v, av in zip(halves, x1, a):
        x2 = v + g2 * _dot(av, w2_ref[...])
        y = x2 * lax.rsqrt(jnp.mean(x2 * x2, axis=-1, keepdims=True) + EPS)
        out_ref[r, :] = y * gfin_ref[...]


def _post_stream_kernel(layer, grid, tm, x_hbm, o_hbm, mod_hbm, gmlp_ref, gfin_ref, wo_ref, w1_ref, w2_ref,
                        out_hbm):
    def step(x_ref, o_ref_in, mod_ref, out_ref):
        _post_kernel(layer, x_ref, o_ref_in, mod_ref, gmlp_ref, gfin_ref, wo_ref, w1_ref, w2_ref, out_ref)

    pltpu.emit_pipeline(
        step,
        grid=grid,
        in_specs=[
            pl.BlockSpec((None, tm, D_MODEL), lambda b, i: (b, i, 0)),
            pl.BlockSpec((None, tm, Q_DIM), lambda b, i: (b, i, 0)),
            pl.BlockSpec((None, None, 1, N_MOD * D_MODEL), lambda b, i: (layer, b, 0, 0)),
        ],
        out_specs=[pl.BlockSpec((None, tm, D_MODEL), lambda b, i: (b, i, 0))],
    )(x_hbm, o_hbm, mod_hbm, out_hbm)


def _post(x, o, mod, layer, tm, gmlp, gfin, w_o, w1, w2):
    nb, seq_len, _ = x.shape
    hbm = pl.BlockSpec(memory_space=pl.ANY)
    vmem = pl.BlockSpec(memory_space=pltpu.VMEM)
    return pl.pallas_call(
        functools.partial(_post_stream_kernel, layer, (nb, seq_len // tm), tm),
        in_specs=[hbm, hbm, hbm, vmem, vmem, vmem, vmem, vmem],
        out_specs=hbm,
        out_shape=jax.ShapeDtypeStruct(x.shape, F32),
        compiler_params=pltpu.CompilerParams(vmem_limit_bytes=VMEM_LIMIT),
        name="post_attn",
    )(x, o, mod, gmlp, gfin, w_o, w1, w2)


def _rope_tables(seq_len):
    pos = np.arange(seq_len)
    inv = ROPE_BASE ** (-np.arange(0, AXIS_ROT, 2, dtype=np.float32) / AXIS_ROT)
    ang = np.concatenate([(pos // GRID_W).astype(np.float32)[:, None] * inv[None],
                          (pos % GRID_W).astype(np.float32)[:, None] * inv[None]], axis=-1)
    cos = np.tile(np.cos(ang.astype(np.float64)), (1, LANES // AXIS_ROT))
    sin = np.sin(ang.astype(np.float64))
    return (jnp.asarray(cos, F32), jnp.asarray(np.concatenate([-sin, -sin, sin, sin], axis=-1), F32))


def kernel(x, c, ctx, c_ctx, ada_w, ada_b, norm_mix_g, norm_mlp_g, pool_w_in, pool_w_grp,
           pool_scale, pool_w_out, attn_w_qkv, attn_sink, attn_w_o, mlp_w1, mlp_w2, final_g):
    nb, seq_len, _ = x.shape
    ctx_len = ctx.shape[1]
    ctx_row = nb
    c8 = jnp.concatenate(
        [c, c_ctx[None], jnp.zeros((MOD_ROWS - nb - 1, D_MODEL), F32)], axis=0)
    n_grp = len(POOL_WINDOWS)
    layer0_weights = ((pool_w_in, 0), (pool_w_grp.reshape(-1, n_grp * POOL_GROUP_DIM, POOL_GROUP_DIM), 0),
                      (pool_w_out, 0), (mlp_w1, 0), (mlp_w2, 0))
    mod, (w_in, w_grp, w_out, w1_0, w2_0) = _modulation(c8, ada_w, ada_b, layer0_weights)

    gmix, gmlp = norm_mix_g, norm_mlp_g

    l0 = (gmix, gmlp, pool_scale, w_in,
          w_grp.reshape(n_grp, POOL_GROUP_DIM, POOL_GROUP_DIM), w_out, w1_0, w2_0)
    layer1_weights = ((attn_w_qkv, 0), (attn_w_o, 0), (mlp_w1, 1), (mlp_w2, 1))
    x1, (w_qkv, w_o, w1, w2) = _layer0(x, mod, None, 0, LAYER_TILE, True, layer1_weights, *l0)
    ctx1, _ = _layer0(ctx, mod, ctx_row, 0, ctx_len, False, (), *l0)

    cos, sin = _rope_tables(seq_len)
    q, kv = _qkv(x1, mod, 1, QKV_TILE, gmix, w_qkv, cos, sin)
    kvc = _ctx_kv(ctx1, mod, ctx_row, 1, gmix, w_qkv)
    o = _attention(q, kv, kvc, attn_sink[0], ATTN_TILE)
    return _post(x1, o, mod, 1, LAYER_TILE, gmlp, final_g.reshape(1, D_MODEL), w_o, w1, w2)
```

```python
import functools

import numpy as np
import jax
import jax.numpy as jnp
from jax import lax
from jax.experimental import pallas as pl
from jax.experimental.pallas import tpu as pltpu

D_MODEL = 1024
DEPTH = 2
GRID_W = 64
POOL_WINDOWS = (2, 4, 8, 16)
POOL_GROUP_DIM = D_MODEL // len(POOL_WINDOWS)
POOL_HALO = 8
HEAD_DIM = 64
N_HEADS = D_MODEL // HEAD_DIM
N_KV_HEADS = 2
GQA_GROUP = N_HEADS // N_KV_HEADS
Q_DIM = N_HEADS * HEAD_DIM
KV_DIM = N_KV_HEADS * HEAD_DIM
WINDOW = 128
BLOCK = 128
ROPE_BASE = 10000.0
AXIS_ROT = HEAD_DIM // 2
ROT_HALF = AXIS_ROT // 2
D_FF = 4 * D_MODEL
N_MOD = 6
EPS = 1e-6
NEG = -1e30
MOD_ROWS = 8
LANES = 128
LOG2E = 1.4426950408889634
KV_W = 3 * LANES
SOFTMAX_ROWS = 32
MIN_SAFE_DENOMINATOR_LOG2 = -64.0
QKV_COL_GROUP = 256
FAST_UNROLL = 2
VMEM_LIMIT = 56 * 1024 * 1024

LAYER_TILE = 512
QKV_TILE = 1024
ATTN_TILE = 2048
MOD_COLS = 1536

BF16 = jnp.bfloat16
F32 = jnp.float32


def _const_spec(shape):
    nd = len(shape)
    return pl.BlockSpec(shape, lambda *_: (0,) * nd, pipeline_mode=pl.Buffered(1))


def _layer_rows_spec():
    return _const_spec((DEPTH, D_MODEL))


def _dot(a, b):
    return jnp.dot(a, b, preferred_element_type=F32)


def _rms_mod(x, g, shift, scale):
    y = x * lax.rsqrt(jnp.mean(x * x, axis=-1, keepdims=True) + EPS)
    return y * (g * (1.0 + scale)) + shift


def _mlp_residual(x, mod_ref, gain, w1_ref, w2_ref):
    sh2 = mod_ref[:, 3 * D_MODEL:4 * D_MODEL]
    sc2 = mod_ref[:, 4 * D_MODEL:5 * D_MODEL]
    g2 = mod_ref[:, 5 * D_MODEL:6 * D_MODEL]
    h = _rms_mod(x, gain, sh2, sc2).astype(BF16)
    a = jnp.maximum(_dot(h, w1_ref[...]), 0.0)
    a = (a * a).astype(BF16)
    return x + g2 * _dot(a, w2_ref[...])


def _staged_steps(step, n_tiles, n_stages, body):
    assert n_tiles >= n_stages
    for t in range(n_stages - 1):
        pl.when(step == t)(functools.partial(body, *[s <= t for s in range(n_stages)]))
    pl.when((step >= n_stages - 1) & (step < n_tiles))(functools.partial(body, *[True] * n_stages))
    for k in range(n_stages - 1):
        pl.when(step == n_tiles + k)(functools.partial(body, *[s > k for s in range(n_stages)]))


def _skewed_steps(step, n_tiles, body):
    _staged_steps(step, n_tiles, 2, body)


def _stage_tile(n_tiles, tiles_per_row, stage):
    def tile(t):
        k = jnp.clip(t - stage, 0, n_tiles - 1)
        return k // tiles_per_row, k % tiles_per_row
    return tile


def _skew_maps(n_tiles, tiles_per_row):
    return _stage_tile(n_tiles, tiles_per_row, 0), _stage_tile(n_tiles, tiles_per_row, 1)


def _mod_kernel(n_prep, c_ref, w_ref, b_ref, *refs):
    prep_in, o_ref, prep_out = refs[:n_prep], refs[n_prep], refs[n_prep + 1:]
    c = c_ref[...]
    s = (c * jax.nn.sigmoid(c)).astype(BF16)
    r = _dot(s, w_ref[...].astype(BF16)) + b_ref[...]
    for row in range(MOD_ROWS):
        o_ref[row] = r[row:row + 1, :]
    for src, dst in zip(prep_in, prep_out):
        dst[...] = src[...].astype(BF16)


def _modulation(c8, ada_w, ada_b, first_layer_weights):
    tn = MOD_COLS
    n = N_MOD * D_MODEL
    n_col = n // tn
    steps = DEPTH * n_col
    prep_in_specs, prep_out_specs = [], []
    for w, idx in first_layer_weights:
        rows = w.shape[1] // steps
        prep_in_specs.append(pl.BlockSpec(
            (None, rows, w.shape[2]), lambda i, j, idx=idx: (idx, i * n_col + j, 0)))
        prep_out_specs.append(pl.BlockSpec((rows, w.shape[2]), lambda i, j: (i * n_col + j, 0)))
    outs = pl.pallas_call(
        functools.partial(_mod_kernel, len(first_layer_weights)),
        grid=(DEPTH, n_col),
        in_specs=[
            pl.BlockSpec((MOD_ROWS, D_MODEL), lambda i, j: (0, 0)),
            pl.BlockSpec((None, D_MODEL, tn), lambda i, j: (i, 0, j)),
            pl.BlockSpec((None, 1, tn), lambda i, j: (i, 0, j)),
        ] + prep_in_specs,
        out_specs=[pl.BlockSpec((None, MOD_ROWS, 1, tn), lambda i, j: (i, 0, 0, j))] + prep_out_specs,
        out_shape=[jax.ShapeDtypeStruct((DEPTH, MOD_ROWS, 1, n), F32)]
        + [jax.ShapeDtypeStruct(w.shape[1:], BF16) for w, _ in first_layer_weights],
        compiler_params=pltpu.CompilerParams(
            dimension_semantics=("arbitrary", "arbitrary"), vmem_limit_bytes=VMEM_LIMIT),
        name="modulation",
    )(c8, ada_w, ada_b.reshape(DEPTH, 1, n), *[w for w, _ in first_layer_weights])
    return outs[0], tuple(outs[1:])


def _to_pair_layout(v):
    lane = lax.broadcasted_iota(jnp.int32, v.shape, 1)
    for width in (ROT_HALF, AXIS_ROT):
        hi = (lane // (2 * width)) % 2
        lo = (lane // width) % 2
        v = jnp.where(hi == lo, v, jnp.where(hi == 0, pltpu.roll(v, LANES - width, 1),
                                             pltpu.roll(v, width, 1)))
    return v


def _prepare_next_weights(prep_in, prep_out):
    wqkv_in, wqkv_out = prep_in[0], prep_out[0]
    low = lax.broadcasted_iota(jnp.int32, (wqkv_in.shape[0], LANES), 1) < HEAD_DIM
    for p in range(GQA_GROUP):
        first = wqkv_in[:, (p // 2) * LANES:(p // 2 + 1) * LANES]
        second = wqkv_in[:, (p // 2 + GQA_GROUP // 2) * LANES:(p // 2 + GQA_GROUP // 2 + 1) * LANES]
        if p % 2 == 0:
            both = jnp.where(low, first, pltpu.roll(second, HEAD_DIM, 1))
        else:
            both = jnp.where(low, pltpu.roll(first, HEAD_DIM, 1), second)
        wqkv_out[:, p * LANES:(p + 1) * LANES] = _to_pair_layout(both).astype(BF16)
    wqkv_out[:, Q_DIM:Q_DIM + KV_DIM] = _to_pair_layout(wqkv_in[:, Q_DIM:Q_DIM + KV_DIM]).astype(BF16)
    wqkv_out[:, Q_DIM + KV_DIM:] = wqkv_in[:, Q_DIM + KV_DIM:].astype(BF16)
    for src, dst in zip(prep_in[1:], prep_out[1:]):
        dst[...] = src[...].astype(BF16)


def _layer0_kernel(seq_len, tm, n_tiles, skew, n_prep, layer, *refs):
    (x_ref, xp_ref, xn_ref, moda_ref, modb_ref, gmix_ref, gmlp_ref, pscale_ref, win_ref, wgrp_ref,
     wout_ref, w1_ref, w2_ref) = refs[:13]
    prep_in = refs[13:13 + n_prep]
    o_ref = refs[13 + n_prep]
    prep_out = refs[14 + n_prep:14 + 2 * n_prep]
    u_ref, d_ref, x1_ref, h2_ref = refs[14 + 2 * n_prep:]
    step = pl.program_id(0)
    i = jnp.minimum(step, n_tiles - 1) % (seq_len // tm)
    half = D_FF // 2

    def mlp_up(h2, c0, c1):
        a = jnp.maximum(_dot(h2, w1_ref[:, c0:c1]), 0.0)
        return (a * a).astype(BF16)

    def body(do_mixer, do_prev_mlp):
        if do_prev_mlp:
            g2 = modb_ref[:, 5 * D_MODEL:6 * D_MODEL]
            h2_prev = h2_ref[...]
            a_lo = mlp_up(h2_prev, 0, half)

        if do_mixer:
            sh1 = moda_ref[:, 0:D_MODEL]
            sc1 = moda_ref[:, D_MODEL:2 * D_MODEL]
            x = x_ref[...]
            xe = jnp.concatenate([xp_ref[...], x, xn_ref[...]], axis=0)
            u = _dot(_rms_mod(xe, gmix_ref[layer:layer + 1, :], sh1, sc1).astype(BF16), win_ref[...])
            u_ref[0:POOL_HALO, :] = jnp.where(i > 0, u[0:POOL_HALO], 0.0)
            u_ref[POOL_HALO:POOL_HALO + tm, :] = u[POOL_HALO:POOL_HALO + tm]
            u_ref[POOL_HALO + tm:POOL_HALO + tm + POOL_HALO, :] = jnp.where(
                (i + 1) * tm < seq_len, u[POOL_HALO + tm:], 0.0)

        if do_prev_mlp:
            a_hi = mlp_up(h2_prev, half, D_FF)

        if do_mixer:
            t = i * tm + lax.broadcasted_iota(jnp.int32, (tm, 1), 0)
            for g, w in enumerate(POOL_WINDOWS):
                c0, c1 = g * POOL_GROUP_DIM, (g + 1) * POOL_GROUP_DIM
                acc = u_ref[POOL_HALO - w // 2:POOL_HALO - w // 2 + tm, c0:c1]
                for s in range(-w // 2 + 1, w // 2):
                    acc = acc + u_ref[POOL_HALO + s:POOL_HALO + s + tm, c0:c1]
                lo = jnp.clip(t - w // 2, 0, seq_len)
                hi = jnp.clip(t + w // 2, 0, seq_len)
                inv_cnt = 1.0 / (hi - lo).astype(F32)
                d = acc * inv_cnt - u_ref[POOL_HALO:POOL_HALO + tm, c0:c1]
                d_ref[:, c0:c1] = _dot(d.astype(BF16), wgrp_ref[g])
            y = _dot((d_ref[...] * pscale_ref[...]).astype(BF16), wout_ref[...])

        if do_prev_mlp:
            down = _dot(jnp.concatenate([a_lo, a_hi], axis=1), w2_ref[...])

        if do_mixer:
            g1 = moda_ref[:, 2 * D_MODEL:3 * D_MODEL]
            sh2 = moda_ref[:, 3 * D_MODEL:4 * D_MODEL]
            sc2 = moda_ref[:, 4 * D_MODEL:5 * D_MODEL]
            x1 = x + g1 * y
            h2 = _rms_mod(x1, gmlp_ref[layer:layer + 1, :], sh2, sc2).astype(BF16)

        if do_prev_mlp:
            o_ref[...] = x1_ref[...] + g2 * down
        if do_mixer and n_prep:
            _prepare_next_weights(prep_in, prep_out)
        if do_mixer and skew:
            x1_ref[...] = x1
            h2_ref[...] = h2
        if do_mixer and not skew:
            g2_now = moda_ref[:, 5 * D_MODEL:6 * D_MODEL]
            o_ref[...] = x1 + g2_now * _dot(mlp_up(h2, 0, D_FF), w2_ref[...])

    if skew:
        _skewed_steps(step, n_tiles, body)
    else:
        body(True, False)


def _layer0(x, mod, mod_row, layer, tm, skew, next_weights, gmix, gmlp, pscale, w_in, w_grp, w_out, w1, w2):
    nb, seq_len, _ = x.shape
    rows8 = seq_len // POOL_HALO
    tpb = tm // POOL_HALO
    nt = seq_len // tm
    n_tiles = nb * nt
    tile_a, tile_b = _skew_maps(n_tiles, nt)
    if not skew:
        tile_b = tile_a

    def x_map(t):
        b, i = tile_a(t)
        return (b, i, 0)

    def prev_map(t):
        b, i = tile_a(t)
        return (b, jnp.maximum(i * tpb - 1, 0), 0)

    def next_map(t):
        b, i = tile_a(t)
        return (b, jnp.minimum((i + 1) * tpb, rows8 - 1), 0)

    def moda_map(t):
        return (layer, tile_a(t)[0] if mod_row is None else mod_row, 0, 0)

    def modb_map(t):
        return (layer, tile_b(t)[0] if mod_row is None else mod_row, 0, 0)

    def out_map(t):
        b, i = tile_b(t)
        return (b, i, 0)

    prep_in_specs, prep_out_specs = [], []
    for w, idx in next_weights:
        rows = w.shape[1] // n_tiles
        prep_in_specs.append(pl.BlockSpec(
            (None, rows, w.shape[2]), lambda t, idx=idx: (idx, jnp.minimum(t, n_tiles - 1), 0)))
        prep_out_specs.append(pl.BlockSpec(
            (rows, w.shape[2]), lambda t: (jnp.minimum(t, n_tiles - 1), 0)))

    outs = pl.pallas_call(
        functools.partial(_layer0_kernel, seq_len, tm, n_tiles, skew, len(next_weights), layer),
        grid=(n_tiles + (1 if skew else 0),),
        in_specs=[
            pl.BlockSpec((None, tm, D_MODEL), x_map),
            pl.BlockSpec((None, POOL_HALO, D_MODEL), prev_map),
            pl.BlockSpec((None, POOL_HALO, D_MODEL), next_map),
            pl.BlockSpec((None, None, 1, N_MOD * D_MODEL), moda_map),
            pl.BlockSpec((None, None, 1, N_MOD * D_MODEL), modb_map),
            _layer_rows_spec(),
            _layer_rows_spec(),
            _const_spec((1, D_MODEL)),
            _const_spec((D_MODEL, D_MODEL)),
            _const_spec((len(POOL_WINDOWS), POOL_GROUP_DIM, POOL_GROUP_DIM)),
            _const_spec((D_MODEL, D_MODEL)),
            _const_spec((D_MODEL, D_FF)),
            _const_spec((D_FF, D_MODEL)),
        ] + prep_in_specs,
        out_specs=[pl.BlockSpec((None, tm, D_MODEL), out_map)] + prep_out_specs,
        out_shape=[jax.ShapeDtypeStruct(x.shape, F32)]
        + [jax.ShapeDtypeStruct(w.shape[1:], BF16) for w, _ in next_weights],
        scratch_shapes=[pltpu.VMEM((tm + 2 * POOL_HALO, D_MODEL), F32),
                        pltpu.VMEM((tm, D_MODEL), F32),
                        pltpu.VMEM((tm, D_MODEL), F32),
                        pltpu.VMEM((tm, D_MODEL), BF16)],
        compiler_params=pltpu.CompilerParams(
            dimension_semantics=("arbitrary",), vmem_limit_bytes=VMEM_LIMIT),
        name="layer0",
    )(x, x, x, mod, mod, gmix, gmlp, pscale, w_in, w_grp, w_out, w1, w2, *[w for w, _ in next_weights])
    return outs[0], tuple(outs[1:])


def _store_kv(kv_ref, k, v):
    low = lax.broadcasted_iota(jnp.int32, v.shape, 1) < HEAD_DIM
    kv_ref[:, 0:LANES] = k.astype(BF16)
    kv_ref[:, LANES:2 * LANES] = jnp.where(low, v, 1.0).astype(BF16)
    kv_ref[:, 2 * LANES:3 * LANES] = jnp.where(low, 1.0, v).astype(BF16)


def _qkv_kernel(layer, x_ref, mod_ref, gmix_ref, w_ref, cos_ref, sin_ref, q_ref, kv_ref):
    sh1 = mod_ref[:, 0:D_MODEL]
    sc1 = mod_ref[:, D_MODEL:2 * D_MODEL]
    gmix = gmix_ref[layer:layer + 1, :]
    tm = x_ref.shape[0]
    halves = [slice(0, tm // 2), slice(tm // 2, tm)]
    h = [_rms_mod(x_ref[r, :], gmix, sh1, sc1).astype(BF16) for r in halves]
    q_groups = [[_dot(hv, w_ref[:, c:c + QKV_COL_GROUP]) for c in range(0, Q_DIM, QKV_COL_GROUP)] for hv in h]
    kv = [_dot(hv, w_ref[:, Q_DIM:]) for hv in h]
    qscale = HEAD_DIM ** -0.5 * LOG2E

    def rope(t, c, s):
        return t * c + pltpu.roll(t, HEAD_DIM, 1) * s

    for r, groups, kv_half in zip(halves, q_groups, kv):
        cos = cos_ref[r, :]
        sin = sin_ref[r, :]
        cos_q = cos * qscale
        sin_q = sin * qscale
        for gi, group in enumerate(groups):
            for cb in range(QKV_COL_GROUP // LANES):
                c0 = gi * QKV_COL_GROUP + cb * LANES
                q_ref[r, c0:c0 + LANES] = rope(group[:, cb * LANES:(cb + 1) * LANES], cos_q, sin_q).astype(BF16)
        _store_kv(kv_ref.at[r, :], rope(kv_half[:, 0:KV_DIM], cos, sin), kv_half[:, KV_DIM:])


def _qkv(x, mod, layer, tm, gmix, w_qkv, cos, sin):
    nb, seq_len, _ = x.shape
    def stream_kernel(x_hbm, mod_hbm, gmix_ref, w_ref, cos_hbm, sin_hbm, q_hbm, kv_hbm):
        def step(x_ref, mod_ref, cos_ref, sin_ref, q_ref, kv_ref):
            _qkv_kernel(layer, x_ref, mod_ref, gmix_ref, w_ref, cos_ref, sin_ref, q_ref, kv_ref)

        pltpu.emit_pipeline(
            step,
            grid=(nb, seq_len // tm),
            in_specs=[
                pl.BlockSpec((None, tm, D_MODEL), lambda b, i: (b, i, 0)),
                pl.BlockSpec((None, None, 1, N_MOD * D_MODEL), lambda b, i: (layer, b, 0, 0)),
                pl.BlockSpec((tm, LANES), lambda b, i: (i, 0)),
                pl.BlockSpec((tm, LANES), lambda b, i: (i, 0)),
            ],
            out_specs=[pl.BlockSpec((None, tm, Q_DIM), lambda b, i: (b, i, 0)),
                       pl.BlockSpec((None, tm, KV_W), lambda b, i: (b, i, 0))],
        )(x_hbm, mod_hbm, cos_hbm, sin_hbm, q_hbm, kv_hbm)

    hbm = pl.BlockSpec(memory_space=pl.ANY)
    vmem = pl.BlockSpec(memory_space=pltpu.VMEM)
    return pl.pallas_call(
        stream_kernel,
        in_specs=[hbm, hbm, vmem, vmem, hbm, hbm],
        out_specs=[hbm, hbm],
        out_shape=[jax.ShapeDtypeStruct((nb, seq_len, Q_DIM), BF16),
                   jax.ShapeDtypeStruct((nb, seq_len, KV_W), BF16)],
        compiler_params=pltpu.CompilerParams(vmem_limit_bytes=VMEM_LIMIT),
        name="qkv_rope",
    )(x, mod, gmix, w_qkv, cos, sin)


def _ctx_kv_kernel(layer, x_ref, mod_ref, gmix_ref, w_ref, kv_ref):
    sh1 = mod_ref[:, 0:D_MODEL]
    sc1 = mod_ref[:, D_MODEL:2 * D_MODEL]
    h = _rms_mod(x_ref[...], gmix_ref[layer:layer + 1, :], sh1, sc1).astype(BF16)
    kv = _dot(h, w_ref[...])
    _store_kv(kv_ref, kv[:, 0:KV_DIM], kv[:, KV_DIM:])


def _ctx_kv(ctx, mod, mod_row, layer, gmix, w_kv):
    nb, ctx_len, _ = ctx.shape
    return pl.pallas_call(
        functools.partial(_ctx_kv_kernel, layer),
        grid=(nb,),
        in_specs=[
            pl.BlockSpec((None, ctx_len, D_MODEL), lambda b: (b, 0, 0)),
            pl.BlockSpec((None, None, 1, N_MOD * D_MODEL), lambda b: (layer, mod_row, 0, 0)),
            _layer_rows_spec(),
            pl.BlockSpec((D_MODEL, 2 * KV_DIM), lambda b: (0, Q_DIM // (2 * KV_DIM)),
                         pipeline_mode=pl.Buffered(1)),
        ],
        out_specs=pl.BlockSpec((None, ctx_len, KV_W), lambda b: (b, 0, 0)),
        out_shape=jax.ShapeDtypeStruct((nb, ctx_len, KV_W), BF16),
        compiler_params=pltpu.CompilerParams(
            dimension_semantics=("arbitrary",), vmem_limit_bytes=VMEM_LIMIT),
        name="ctx_kv",
    )(ctx, mod, gmix, w_kv)


def _dot_nt(a, b):
    return lax.dot_general(a, b, (((1,), (1,)), ((), ())), preferred_element_type=F32)


def _attn_kernel(seq_len, tq, ctx_len, sink_ref, q_ref, kvp_ref, kvm_ref, kvn_ref, kvc_ref, o_ref,
                 kv_ext, *bufs):
    s_refs, e_refs, m_refs = bufs[0:2], bufs[2:6], bufs[6:10]
    kn_ref, minden_ref = bufs[10], bufs[11]
    i = pl.program_id(1)
    kv_ext[0:BLOCK, :] = kvp_ref[...]
    kv_ext[BLOCK:BLOCK + tq, :] = kvm_ref[...]
    kv_ext[BLOCK + tq:2 * BLOCK + tq, :] = kvn_ref[...]
    nblk = tq // BLOCK
    nwin = 3 * BLOCK
    nkey = nwin + ctx_len
    row = lax.broadcasted_iota(jnp.int32, (BLOCK, BLOCK), 0)
    col = lax.broadcasted_iota(jnp.int32, (BLOCK, BLOCK), 1)
    lane = lax.broadcasted_iota(jnp.int32, (BLOCK, LANES), 1)
    low = lane < HEAD_DIM
    head_a = (lane % HEAD_DIM) < AXIS_ROT
    my_lanes = (head_a, jnp.logical_not(head_a))
    den_lanes = (jnp.logical_not(low), low)

    chunks = [(g, t0) for g in range(GQA_GROUP) for t0 in range(0, BLOCK, SOFTMAX_ROWS)]

    def head_queries(j, kvh):
        q0 = pl.multiple_of(j * BLOCK, BLOCK)
        tiles = []
        for g in range(GQA_GROUP):
            pair = q_ref[pl.ds(q0, BLOCK), g * LANES:(g + 1) * LANES]
            tiles.append(jnp.where(my_lanes[kvh], pair, jnp.zeros_like(pair)))
        return tiles

    def window_masks(j):
        mask_l = (col >= row) & (i * tq + (j - 1) * BLOCK >= 0)
        mask_r = (col <= row) & (i * tq + (j + 1) * BLOCK < seq_len)
        return mask_l, mask_r

    def key_tiles(j, kvh):
        q0 = pl.multiple_of(j * BLOCK, BLOCK)
        return kv_ext[pl.ds(q0, nwin), 0:LANES], kvc_ref[:, 0:LANES]

    def scores(j, kvh):
        qs = jnp.concatenate(head_queries(j, kvh), axis=0)
        kw, kc = key_tiles(j, kvh)
        s_refs[kvh][:, 0:nwin] = _dot_nt(qs, kw)
        s_refs[kvh][:, nwin:nkey] = _dot_nt(qs, kc)

    def softmax(j, kvh):
        mask_l, mask_r = window_masks(j)

        def masked_pieces(rows, trows):
            pieces = [s_refs[kvh][rows, p * LANES:(p + 1) * LANES] for p in range(nkey // LANES)]
            pieces[0] = jnp.where(mask_l[trows], pieces[0], NEG)
            pieces[2] = jnp.where(mask_r[trows], pieces[2], NEG)
            return pieces

        for g, t0 in chunks:
            rows = slice(g * BLOCK + t0, g * BLOCK + t0 + SOFTMAX_ROWS)
            pieces = masked_pieces(rows, slice(t0, t0 + SOFTMAX_ROWS))
            m = pieces[0]
            for p in pieces[1:]:
                m = jnp.maximum(m, p)
            m = jnp.maximum(jnp.max(m, axis=1, keepdims=True), sink_ref[kvh * GQA_GROUP + g] * LOG2E)
            m_refs[kvh][rows, :] = jnp.broadcast_to(m, (SOFTMAX_ROWS, LANES))
        for g, t0 in chunks:
            rows = slice(g * BLOCK + t0, g * BLOCK + t0 + SOFTMAX_ROWS)
            pieces = masked_pieces(rows, slice(t0, t0 + SOFTMAX_ROWS))
            m = m_refs[kvh][rows, :]
            for p, piece in enumerate(pieces):
                e_refs[kvh][rows, p * LANES:(p + 1) * LANES] = jnp.exp2((piece - m).astype(BF16))

    def weighted_values(j, kvh, slot, fast):
        e_ref, m_ref = e_refs[2 * slot + kvh], m_refs[2 * slot + kvh]
        q0 = pl.multiple_of(j * BLOCK, BLOCK)
        vcols = slice((1 + kvh) * LANES, (2 + kvh) * LANES)
        values = jnp.concatenate([kv_ext[pl.ds(q0, nwin), vcols], kvc_ref[:, vcols]], axis=0)
        o = _dot(e_ref[...], values)
        on, smallest = [], None
        for g in range(GQA_GROUP):
            rows = slice(g * BLOCK, (g + 1) * BLOCK)
            sink = sink_ref[kvh * GQA_GROUP + g] * LOG2E
            if fast:
                sink_term = jnp.exp2(sink - m_ref[g * BLOCK:g * BLOCK + 1, :])
            else:
                sink_term = jnp.exp2(sink - m_ref[rows, :])
            total = o[rows] + sink_term
            smallest = total if smallest is None else jnp.minimum(smallest, total)
            on.append(o[rows] * pltpu.roll(1.0 / total, HEAD_DIM, 1))
        if fast:
            minden_ref[...] = jnp.minimum(minden_ref[...], jnp.where(den_lanes[kvh], smallest, 3e38))
        for pair in range(GQA_GROUP // 2):
            if kvh == 0:
                tile = jnp.where(low, on[2 * pair], pltpu.roll(on[2 * pair + 1], HEAD_DIM, 1))
            else:
                tile = jnp.where(low, pltpu.roll(on[2 * pair], HEAD_DIM, 1), on[2 * pair + 1])
            c0 = (kvh * GQA_GROUP + 2 * pair) * HEAD_DIM
            o_ref[pl.ds(q0, BLOCK), c0:c0 + LANES] = tile.astype(BF16)

    def exact_body(j, carry):
        scores(j, 0)
        scores(j, 1)
        softmax(j, 0)
        weighted_values(j, 0, 0, False)
        softmax(j, 1)
        weighted_values(j, 1, 0, False)
        return carry

    def max_sq_norms(k_tile):
        sq = k_tile.astype(F32)
        sq = sq * sq
        return [jnp.broadcast_to(jnp.max(jnp.sum(jnp.where(my_lanes[kvh], sq, 0.0), axis=1, keepdims=True),
                                         axis=0, keepdims=True), (8, LANES)) for kvh in range(N_KV_HEADS)]

    def store_key_norms(first, last):
        for kb in range(first, last):
            for kvh, n2 in enumerate(max_sq_norms(kv_ext[kb * BLOCK:(kb + 1) * BLOCK, 0:LANES])):
                kn_ref[kvh, kb] = n2

    store_key_norms(0, 3)
    ctx_norm = None
    for cb in range(ctx_len // BLOCK):
        n2 = max_sq_norms(kvc_ref[cb * BLOCK:(cb + 1) * BLOCK, 0:LANES])
        ctx_norm = n2 if ctx_norm is None else [jnp.maximum(a, b) for a, b in zip(ctx_norm, n2)]
    minden_ref[...] = jnp.full(minden_ref.shape, 3e38, F32)

    def bounded_exp(j, kvh, slot):
        e_ref, m_ref = e_refs[2 * slot + kvh], m_refs[2 * slot + kvh]
        q0 = pl.multiple_of(j * BLOCK, BLOCK)
        kw, kc = key_tiles(j, kvh)
        kn2 = jnp.maximum(jnp.max(kn_ref[kvh, pl.ds(j, 3)], axis=0), ctx_norm[kvh])[0:1, :]
        tiles = head_queries(j, kvh)
        for g in range(GQA_GROUP):
            qf = tiles[g].astype(F32)
            qn2 = jnp.max(jnp.sum(qf * qf, axis=1, keepdims=True), axis=0, keepdims=True)
            bound = jnp.maximum(jnp.sqrt(qn2 * kn2), sink_ref[kvh * GQA_GROUP + g] * LOG2E)
            m_ref[g * BLOCK:(g + 1) * BLOCK, :] = jnp.broadcast_to(bound, (BLOCK, LANES))
        qs = jnp.concatenate(tiles, axis=0)
        s_w = _dot_nt(qs, kw)
        s_c = _dot_nt(qs, kc)
        mask_l, mask_r = window_masks(j)
        for g, t0 in chunks:
            rows = slice(g * BLOCK + t0, g * BLOCK + t0 + SOFTMAX_ROWS)
            trows = slice(t0, t0 + SOFTMAX_ROWS)
            m = m_ref[rows, :]
            pieces = ([s_w[rows, p * LANES:(p + 1) * LANES] for p in range(nwin // LANES)]
                      + [s_c[rows, p * LANES:(p + 1) * LANES] for p in range(ctx_len // LANES)])
            pieces[0] = jnp.where(mask_l[trows], pieces[0], NEG)
            pieces[2] = jnp.where(mask_r[trows], pieces[2], NEG)
            for p, piece in enumerate(pieces):
                e_ref[rows, p * LANES:(p + 1) * LANES] = jnp.exp2(piece - m).astype(BF16)

    bounded_exp(0, 0, 0)
    bounded_exp(0, 1, 0)
    store_key_norms(3, nblk + 2)

    def fast_blocks(j0, last):
        for k in range(FAST_UNROLL):
            for kvh in range(N_KV_HEADS):
                weighted_values(j0 + k, kvh, k % 2, True)
                if not (last and k + 1 == FAST_UNROLL):
                    bounded_exp(j0 + k + 1, kvh, (k + 1) % 2)

    def fast_body(jj, carry):
        fast_blocks(FAST_UNROLL * jj, False)
        return carry

    lax.fori_loop(0, nblk // FAST_UNROLL - 1, fast_body, 0)
    fast_blocks(nblk - FAST_UNROLL, True)
    smallest = jnp.min(minden_ref[...])

    @pl.when(jnp.logical_not(smallest >= 2.0 ** MIN_SAFE_DENOMINATOR_LOG2))
    def _():
        lax.fori_loop(0, nblk, exact_body, 0)


def _attention(q, kv, kvc, sink, tq):
    nb, seq_len, _ = q.shape
    ctx_len = kvc.shape[1]
    nblk = seq_len // BLOCK
    qpb = tq // BLOCK
    rows = GQA_GROUP * BLOCK
    nkey = 3 * BLOCK + ctx_len
    return pl.pallas_call(
        functools.partial(_attn_kernel, seq_len, tq, ctx_len),
        grid=(nb, seq_len // tq),
        in_specs=[
            pl.BlockSpec(memory_space=pltpu.SMEM),
            pl.BlockSpec((None, tq, Q_DIM), lambda b, i: (b, i, 0)),
            pl.BlockSpec((None, BLOCK, KV_W), lambda b, i: (b, jnp.maximum(i * qpb - 1, 0), 0)),
            pl.BlockSpec((None, tq, KV_W), lambda b, i: (b, i, 0)),
            pl.BlockSpec((None, BLOCK, KV_W),
                         lambda b, i: (b, jnp.minimum((i + 1) * qpb, nblk - 1), 0)),
            pl.BlockSpec((None, ctx_len, KV_W), lambda b, i: (b, 0, 0)),
        ],
        out_specs=pl.BlockSpec((None, tq, Q_DIM), lambda b, i: (b, i, 0)),
        out_shape=jax.ShapeDtypeStruct((nb, seq_len, Q_DIM), BF16),
        scratch_shapes=([pltpu.VMEM((tq + 2 * BLOCK, KV_W), BF16)]
                        + [pltpu.VMEM((rows, nkey), F32)] * N_KV_HEADS
                        + [pltpu.VMEM((rows, nkey), BF16)] * (2 * N_KV_HEADS)
                        + [pltpu.VMEM((rows, LANES), F32)] * (2 * N_KV_HEADS)
                        + [pltpu.VMEM((N_KV_HEADS, qpb + 2, 8, LANES), F32),
                           pltpu.VMEM((BLOCK, LANES), F32)]),
        compiler_params=pltpu.CompilerParams(
            dimension_semantics=("arbitrary", "arbitrary"), vmem_limit_bytes=VMEM_LIMIT),
        name="attention",
    )(sink, q, kv, kv, kv, kvc)


def _post_kernel(layer, x_ref, o_ref_in, mod_ref, gmlp_ref, gfin_ref, wo_ref, w1_ref, w2_ref, out_ref):
    g1 = mod_ref[:, 2 * D_MODEL:3 * D_MODEL]
    sh2 = mod_ref[:, 3 * D_MODEL:4 * D_MODEL]
    sc2 = mod_ref[:, 4 * D_MODEL:5 * D_MODEL]
    g2 = mod_ref[:, 5 * D_MODEL:6 * D_MODEL]
    gain = gmlp_ref[layer:layer + 1, :]
    tm = x_ref.shape[0]
    halves = [slice(0, tm // 2), slice(tm // 2, tm)]
    x1 = [x_ref[r, :] + g1 * _dot(o_ref_in[r, :], wo_ref[...]) for r in halves]
    h = [_rms_mod(v, gain, sh2, sc2).astype(BF16) for v in x1]
    a = []
    for hv in h:
        up = jnp.maximum(_dot(hv, w1_ref[...]), 0.0)
        a.append((up * up).astype(BF16))
    for r, v, av in zip(halves, x1, a):
        x2 = v + g2 * _dot(av, w2_ref[...])
        y = x2 * lax.rsqrt(jnp.mean(x2 * x2, axis=-1, keepdims=True) + EPS)
        out_ref[r, :] = y * gfin_ref[...]


def _post_stream_kernel(layer, grid, tm, x_hbm, o_hbm, mod_hbm, gmlp_ref, gfin_ref, wo_ref, w1_ref, w2_ref,
                        out_hbm):
    def step(x_ref, o_ref_in, mod_ref, out_ref):
        _post_kernel(layer, x_ref, o_ref_in, mod_ref, gmlp_ref, gfin_ref, wo_ref, w1_ref, w2_ref, out_ref)

    pltpu.emit_pipeline(
        step,
        grid=grid,
        in_specs=[
            pl.BlockSpec((None, tm, D_MODEL), lambda b, i: (b, i, 0)),
            pl.BlockSpec((None, tm, Q_DIM), lambda b, i: (b, i, 0)),
            pl.BlockSpec((None, None, 1, N_MOD * D_MODEL), lambda b, i: (layer, b, 0, 0)),
        ],
        out_specs=[pl.BlockSpec((None, tm, D_MODEL), lambda b, i: (b, i, 0))],
    )(x_hbm, o_hbm, mod_hbm, out_hbm)


def _post(x, o, mod, layer, tm, gmlp, gfin, w_o, w1, w2):
    nb, seq_len, _ = x.shape
    hbm = pl.BlockSpec(memory_space=pl.ANY)
    vmem = pl.BlockSpec(memory_space=pltpu.VMEM)
    return pl.pallas_call(
        functools.partial(_post_stream_kernel, layer, (nb, seq_len // tm), tm),
        in_specs=[hbm, hbm, hbm, vmem, vmem, vmem, vmem, vmem],
        out_specs=hbm,
        out_shape=jax.ShapeDtypeStruct(x.shape, F32),
        compiler_params=pltpu.CompilerParams(vmem_limit_bytes=VMEM_LIMIT),
        name="post_attn",
    )(x, o, mod, gmlp, gfin, w_o, w1, w2)


def _rope_tables(seq_len):
    pos = np.arange(seq_len)
    inv = ROPE_BASE ** (-np.arange(0, AXIS_ROT, 2, dtype=np.float32) / AXIS_ROT)
    ang = np.concatenate([(pos // GRID_W).astype(np.float32)[:, None] * inv[None],
                          (pos % GRID_W).astype(np.float32)[:, None] * inv[None]], axis=-1)
    cos = np.tile(np.cos(ang.astype(np.float64)), (1, LANES // AXIS_ROT))
    sin = np.sin(ang.astype(np.float64))
    return (jnp.asarray(cos, F32), jnp.asarray(np.concatenate([-sin, -sin, sin, sin], axis=-1), F32))


def kernel(x, c, ctx, c_ctx, ada_w, ada_b, norm_mix_g, norm_mlp_g, pool_w_in, pool_w_grp,
           pool_scale, pool_w_out, attn_w_qkv, attn_sink, attn_w_o, mlp_w1, mlp_w2, final_g):
    nb, seq_len, _ = x.shape
    ctx_len = ctx.shape[1]
    ctx_row = nb
    c8 = jnp.concatenate(
        [c, c_ctx[None], jnp.zeros((MOD_ROWS - nb - 1, D_MODEL), F32)], axis=0)
    n_grp = len(POOL_WINDOWS)
    layer0_weights = ((pool_w_in, 0), (pool_w_grp.reshape(-1, n_grp * POOL_GROUP_DIM, POOL_GROUP_DIM), 0),
                      (pool_w_out, 0), (mlp_w1, 0), (mlp_w2, 0))
    mod, (w_in, w_grp, w_out, w1_0, w2_0) = _modulation(c8, ada_w, ada_b, layer0_weights)

    gmix, gmlp = norm_mix_g, norm_mlp_g

    l0 = (gmix, gmlp, pool_scale, w_in,
          w_grp.reshape(n_grp, POOL_GROUP_DIM, POOL_GROUP_DIM), w_out, w1_0, w2_0)
    layer1_weights = ((attn_w_qkv, 0), (attn_w_o, 0), (mlp_w1, 1), (mlp_w2, 1))
    x1, (w_qkv, w_o, w1, w2) = _layer0(x, mod, None, 0, LAYER_TILE, True, layer1_weights, *l0)
    ctx1, _ = _layer0(ctx, mod, ctx_row, 0, ctx_len, False, (), *l0)

    cos, sin = _rope_tables(seq_len)
    q, kv = _qkv(x1, mod, 1, QKV_TILE, gmix, w_qkv, cos, sin)
    kvc = _ctx_kv(ctx1, mod, ctx_row, 1, gmix, w_qkv)
    o = _attention(q, kv, kvc, attn_sink[0], ATTN_TILE)
    return _post(x1, o, mod, 1, LAYER_TILE, gmlp, final_g.reshape(1, D_MODEL), w_o, w1, w2)
```

```python
import functools

import numpy as np
import jax
import jax.numpy as jnp
from jax import lax
from jax.experimental import pallas as pl
from jax.experimental.pallas import tpu as pltpu

D_MODEL = 1024
DEPTH = 2
GRID_W = 64
POOL_WINDOWS = (2, 4, 8, 16)
POOL_GROUP_DIM = D_MODEL // len(POOL_WINDOWS)
POOL_HALO = 8
HEAD_DIM = 64
N_HEADS = D_MODEL // HEAD_DIM
N_KV_HEADS = 2
GQA_GROUP = N_HEADS // N_KV_HEADS
Q_DIM = N_HEADS * HEAD_DIM
KV_DIM = N_KV_HEADS * HEAD_DIM
WINDOW = 128
BLOCK = 128
ROPE_BASE = 10000.0
AXIS_ROT = HEAD_DIM // 2
ROT_HALF = AXIS_ROT // 2
D_FF = 4 * D_MODEL
N_MOD = 6
EPS = 1e-6
NEG = -1e30
MOD_ROWS = 8
LANES = 128
LOG2E = 1.4426950408889634
KV_W = 3 * LANES
SOFTMAX_ROWS = 32
MIN_SAFE_DENOMINATOR_LOG2 = -64.0
QKV_COL_GROUP = 256
FAST_UNROLL = 2
VMEM_LIMIT = 56 * 1024 * 1024

LAYER_TILE = 512
QKV_TILE = 1024
ATTN_TILE = 2048
MOD_COLS = 1536

BF16 = jnp.bfloat16
F32 = jnp.float32


def _const_spec(shape):
    nd = len(shape)
    return pl.BlockSpec(shape, lambda *_: (0,) * nd, pipeline_mode=pl.Buffered(1))


def _layer_rows_spec():
    return _const_spec((DEPTH, D_MODEL))


def _dot(a, b):
    return jnp.dot(a, b, preferred_element_type=F32)


def _rms_mod(x, g, shift, scale):
    y = x * lax.rsqrt(jnp.mean(x * x, axis=-1, keepdims=True) + EPS)
    return y * (g * (1.0 + scale)) + shift


def _mlp_residual(x, mod_ref, gain, w1_ref, w2_ref):
    sh2 = mod_ref[:, 3 * D_MODEL:4 * D_MODEL]
    sc2 = mod_ref[:, 4 * D_MODEL:5 * D_MODEL]
    g2 = mod_ref[:, 5 * D_MODEL:6 * D_MODEL]
    h = _rms_mod(x, gain, sh2, sc2).astype(BF16)
    a = jnp.maximum(_dot(h, w1_ref[...]), 0.0)
    a = (a * a).astype(BF16)
    return x + g2 * _dot(a, w2_ref[...])


def _staged_steps(step, n_tiles, n_stages, body):
    assert n_tiles >= n_stages
    for t in range(n_stages - 1):
        pl.when(step == t)(functools.partial(body, *[s <= t for s in range(n_stages)]))
    pl.when((step >= n_stages - 1) & (step < n_tiles))(functools.partial(body, *[True] * n_stages))
    for k in range(n_stages - 1):
        pl.when(step == n_tiles + k)(functools.partial(body, *[s > k for s in range(n_stages)]))


def _skewed_steps(step, n_tiles, body):
    _staged_steps(step, n_tiles, 2, body)


def _stage_tile(n_tiles, tiles_per_row, stage):
    def tile(t):
        k = jnp.clip(t - stage, 0, n_tiles - 1)
        return k // tiles_per_row, k % tiles_per_row
    return tile


def _skew_maps(n_tiles, tiles_per_row):
    return _stage_tile(n_tiles, tiles_per_row, 0), _stage_tile(n_tiles, tiles_per_row, 1)


def _mod_kernel(n_prep, c_ref, w_ref, b_ref, *refs):
    prep_in, o_ref, prep_out = refs[:n_prep], refs[n_prep], refs[n_prep + 1:]
    c = c_ref[...]
    s = (c * jax.nn.sigmoid(c)).astype(BF16)
    r = _dot(s, w_ref[...].astype(BF16)) + b_ref[...]
    for row in range(MOD_ROWS):
        o_ref[row] = r[row:row + 1, :]
    for src, dst in zip(prep_in, prep_out):
        dst[...] = src[...].astype(BF16)


def _modulation(c8, ada_w, ada_b, first_layer_weights):
    tn = MOD_COLS
    n = N_MOD * D_MODEL
    n_col = n // tn
    steps = DEPTH * n_col
    prep_in_specs, prep_out_specs = [], []
    for w, idx in first_layer_weights:
        rows = w.shape[1] // steps
        prep_in_specs.append(pl.BlockSpec(
            (None, rows, w.shape[2]), lambda i, j, idx=idx: (idx, i * n_col + j, 0)))
        prep_out_specs.append(pl.BlockSpec((rows, w.shape[2]), lambda i, j: (i * n_col + j, 0)))
    outs = pl.pallas_call(
        functools.partial(_mod_kernel, len(first_layer_weights)),
        grid=(DEPTH, n_col),
        in_specs=[
            pl.BlockSpec((MOD_ROWS, D_MODEL), lambda i, j: (0, 0)),
            pl.BlockSpec((None, D_MODEL, tn), lambda i, j: (i, 0, j)),
            pl.BlockSpec((None, 1, tn), lambda i, j: (i, 0, j)),
        ] + prep_in_specs,
        out_specs=[pl.BlockSpec((None, MOD_ROWS, 1, tn), lambda i, j: (i, 0, 0, j))] + prep_out_specs,
        out_shape=[jax.ShapeDtypeStruct((DEPTH, MOD_ROWS, 1, n), F32)]
        + [jax.ShapeDtypeStruct(w.shape[1:], BF16) for w, _ in first_layer_weights],
        compiler_params=pltpu.CompilerParams(
            dimension_semantics=("arbitrary", "arbitrary"), vmem_limit_bytes=VMEM_LIMIT),
        name="modulation",
    )(c8, ada_w, ada_b.reshape(DEPTH, 1, n), *[w for w, _ in first_layer_weights])
    return outs[0], tuple(outs[1:])


def _to_pair_layout(v):
    lane = lax.broadcasted_iota(jnp.int32, v.shape, 1)
    for width in (ROT_HALF, AXIS_ROT):
        hi = (lane // (2 * width)) % 2
        lo = (lane // width) % 2
        v = jnp.where(hi == lo, v, jnp.where(hi == 0, pltpu.roll(v, LANES - width, 1),
                                             pltpu.roll(v, width, 1)))
    return v


def _prepare_next_weights(prep_in, prep_out):
    wqkv_in, wqkv_out = prep_in[0], prep_out[0]
    low = lax.broadcasted_iota(jnp.int32, (wqkv_in.shape[0], LANES), 1) < HEAD_DIM
    for p in range(GQA_GROUP):
        first = wqkv_in[:, (p // 2) * LANES:(p // 2 + 1) * LANES]
        second = wqkv_in[:, (p // 2 + GQA_GROUP // 2) * LANES:(p // 2 + GQA_GROUP // 2 + 1) * LANES]
        if p % 2 == 0:
            both = jnp.where(low, first, pltpu.roll(second, HEAD_DIM, 1))
        else:
            both = jnp.where(low, pltpu.roll(first, HEAD_DIM, 1), second)
        wqkv_out[:, p * LANES:(p + 1) * LANES] = _to_pair_layout(both).astype(BF16)
    wqkv_out[:, Q_DIM:Q_DIM + KV_DIM] = _to_pair_layout(wqkv_in[:, Q_DIM:Q_DIM + KV_DIM]).astype(BF16)
    wqkv_out[:, Q_DIM + KV_DIM:] = wqkv_in[:, Q_DIM + KV_DIM:].astype(BF16)
    for src, dst in zip(prep_in[1:], prep_out[1:]):
        dst[...] = src[...].astype(BF16)


def _layer0_kernel(seq_len, tm, n_tiles, skew, n_prep, layer, *refs):
    (x_ref, xp_ref, xn_ref, moda_ref, modb_ref, gmix_ref, gmlp_ref, pscale_ref, win_ref, wgrp_ref,
     wout_ref, w1_ref, w2_ref) = refs[:13]
    prep_in = refs[13:13 + n_prep]
    o_ref = refs[13 + n_prep]
    prep_out = refs[14 + n_prep:14 + 2 * n_prep]
    u_ref, d_ref, x1_ref, h2_ref = refs[14 + 2 * n_prep:]
    step = pl.program_id(0)
    i = jnp.minimum(step, n_tiles - 1) % (seq_len // tm)
    half = D_FF // 2

    def mlp_up(h2, c0, c1):
        a = jnp.maximum(_dot(h2, w1_ref[:, c0:c1]), 0.0)
        return (a * a).astype(BF16)

    def body(do_mixer, do_prev_mlp):
        if do_prev_mlp:
            g2 = modb_ref[:, 5 * D_MODEL:6 * D_MODEL]
            h2_prev = h2_ref[...]
            a_lo = mlp_up(h2_prev, 0, half)

        if do_mixer:
            sh1 = moda_ref[:, 0:D_MODEL]
            sc1 = moda_ref[:, D_MODEL:2 * D_MODEL]
            x = x_ref[...]
            xe = jnp.concatenate([xp_ref[...], x, xn_ref[...]], axis=0)
            u = _dot(_rms_mod(xe, gmix_ref[layer:layer + 1, :], sh1, sc1).astype(BF16), win_ref[...])
            u_ref[0:POOL_HALO, :] = jnp.where(i > 0, u[0:POOL_HALO], 0.0)
            u_ref[POOL_HALO:POOL_HALO + tm, :] = u[POOL_HALO:POOL_HALO + tm]
            u_ref[POOL_HALO + tm:POOL_HALO + tm + POOL_HALO, :] = jnp.where(
                (i + 1) * tm < seq_len, u[POOL_HALO + tm:], 0.0)

        if do_prev_mlp:
            a_hi = mlp_up(h2_prev, half, D_FF)

        if do_mixer:
            t = i * tm + lax.broadcasted_iota(jnp.int32, (tm, 1), 0)
            for g, w in enumerate(POOL_WINDOWS):
                c0, c1 = g * POOL_GROUP_DIM, (g + 1) * POOL_GROUP_DIM
                acc = u_ref[POOL_HALO - w // 2:POOL_HALO - w // 2 + tm, c0:c1]
                for s in range(-w // 2 + 1, w // 2):
                    acc = acc + u_ref[POOL_HALO + s:POOL_HALO + s + tm, c0:c1]
                lo = jnp.clip(t - w // 2, 0, seq_len)
                hi = jnp.clip(t + w // 2, 0, seq_len)
                inv_cnt = 1.0 / (hi - lo).astype(F32)
                d = acc * inv_cnt - u_ref[POOL_HALO:POOL_HALO + tm, c0:c1]
                d_ref[:, c0:c1] = _dot(d.astype(BF16), wgrp_ref[g])
            y = _dot((d_ref[...] * pscale_ref[...]).astype(BF16), wout_ref[...])

        if do_prev_mlp:
            down = _dot(jnp.concatenate([a_lo, a_hi], axis=1), w2_ref[...])

        if do_mixer:
            g1 = moda_ref[:, 2 * D_MODEL:3 * D_MODEL]
            sh2 = moda_ref[:, 3 * D_MODEL:4 * D_MODEL]
            sc2 = moda_ref[:, 4 * D_MODEL:5 * D_MODEL]
            x1 = x + g1 * y
            h2 = _rms_mod(x1, gmlp_ref[layer:layer + 1, :], sh2, sc2).astype(BF16)

        if do_prev_mlp:
            o_ref[...] = x1_ref[...] + g2 * down
        if do_mixer and n_prep:
            _prepare_next_weights(prep_in, prep_out)
        if do_mixer and skew:
            x1_ref[...] = x1
            h2_ref[...] = h2
        if do_mixer and not skew:
            g2_now = moda_ref[:, 5 * D_MODEL:6 * D_MODEL]
            o_ref[...] = x1 + g2_now * _dot(mlp_up(h2, 0, D_FF), w2_ref[...])

    if skew:
        _skewed_steps(step, n_tiles, body)
    else:
        body(True, False)


def _layer0(x, mod, mod_row, layer, tm, skew, next_weights, gmix, gmlp, pscale, w_in, w_grp, w_out, w1, w2):
    nb, seq_len, _ = x.shape
    rows8 = seq_len // POOL_HALO
    tpb = tm // POOL_HALO
    nt = seq_len // tm
    n_tiles = nb * nt
    tile_a, tile_b = _skew_maps(n_tiles, nt)
    if not skew:
        tile_b = tile_a

    def x_map(t):
        b, i = tile_a(t)
        return (b, i, 0)

    def prev_map(t):
        b, i = tile_a(t)
        return (b, jnp.maximum(i * tpb - 1, 0), 0)

    def next_map(t):
        b, i = tile_a(t)
        return (b, jnp.minimum((i + 1) * tpb, rows8 - 1), 0)

    def moda_map(t):
        return (layer, tile_a(t)[0] if mod_row is None else mod_row, 0, 0)

    def modb_map(t):
        return (layer, tile_b(t)[0] if mod_row is None else mod_row, 0, 0)

    def out_map(t):
        b, i = tile_b(t)
        return (b, i, 0)

    prep_in_specs, prep_out_specs = [], []
    for w, idx in next_weights:
        rows = w.shape[1] // n_tiles
        prep_in_specs.append(pl.BlockSpec(
            (None, rows, w.shape[2]), lambda t, idx=idx: (idx, jnp.minimum(t, n_tiles - 1), 0)))
        prep_out_specs.append(pl.BlockSpec(
            (rows, w.shape[2]), lambda t: (jnp.minimum(t, n_tiles - 1), 0)))

    outs = pl.pallas_call(
        functools.partial(_layer0_kernel, seq_len, tm, n_tiles, skew, len(next_weights), layer),
        grid=(n_tiles + (1 if skew else 0),),
        in_specs=[
            pl.BlockSpec((None, tm, D_MODEL), x_map),
            pl.BlockSpec((None, POOL_HALO, D_MODEL), prev_map),
            pl.BlockSpec((None, POOL_HALO, D_MODEL), next_map),
            pl.BlockSpec((None, None, 1, N_MOD * D_MODEL), moda_map),
            pl.BlockSpec((None, None, 1, N_MOD * D_MODEL), modb_map),
            _layer_rows_spec(),
            _layer_rows_spec(),
            _const_spec((1, D_MODEL)),
            _const_spec((D_MODEL, D_MODEL)),
            _const_spec((len(POOL_WINDOWS), POOL_GROUP_DIM, POOL_GROUP_DIM)),
            _const_spec((D_MODEL, D_MODEL)),
            _const_spec((D_MODEL, D_FF)),
            _const_spec((D_FF, D_MODEL)),
        ] + prep_in_specs,
        out_specs=[pl.BlockSpec((None, tm, D_MODEL), out_map)] + prep_out_specs,
        out_shape=[jax.ShapeDtypeStruct(x.shape, F32)]
        + [jax.ShapeDtypeStruct(w.shape[1:], BF16) for w, _ in next_weights],
        scratch_shapes=[pltpu.VMEM((tm + 2 * POOL_HALO, D_MODEL), F32),
                        pltpu.VMEM((tm, D_MODEL), F32),
                        pltpu.VMEM((tm, D_MODEL), F32),
                        pltpu.VMEM((tm, D_MODEL), BF16)],
        compiler_params=pltpu.CompilerParams(
            dimension_semantics=("arbitrary",), vmem_limit_bytes=VMEM_LIMIT),
        name="layer0",
    )(x, x, x, mod, mod, gmix, gmlp, pscale, w_in, w_grp, w_out, w1, w2, *[w for w, _ in next_weights])
    return outs[0], tuple(outs[1:])


def _store_kv(kv_ref, k, v):
    low = lax.broadcasted_iota(jnp.int32, v.shape, 1) < HEAD_DIM
    kv_ref[:, 0:LANES] = k.astype(BF16)
    kv_ref[:, LANES:2 * LANES] = jnp.where(low, v, 1.0).astype(BF16)
    kv_ref[:, 2 * LANES:3 * LANES] = jnp.where(low, 1.0, v).astype(BF16)


def _qkv_kernel(layer, x_ref, mod_ref, gmix_ref, w_ref, cos_ref, sin_ref, q_ref, kv_ref):
    sh1 = mod_ref[:, 0:D_MODEL]
    sc1 = mod_ref[:, D_MODEL:2 * D_MODEL]
    gmix = gmix_ref[layer:layer + 1, :]
    tm = x_ref.shape[0]
    halves = [slice(0, tm // 2), slice(tm // 2, tm)]
    h = [_rms_mod(x_ref[r, :], gmix, sh1, sc1).astype(BF16) for r in halves]
    q_groups = [[_dot(hv, w_ref[:, c:c + QKV_COL_GROUP]) for c in range(0, Q_DIM, QKV_COL_GROUP)] for hv in h]
    kv = [_dot(hv, w_ref[:, Q_DIM:]) for hv in h]
    qscale = HEAD_DIM ** -0.5 * LOG2E

    def rope(t, c, s):
        return t * c + pltpu.roll(t, HEAD_DIM, 1) * s

    for r, groups, kv_half in zip(halves, q_groups, kv):
        cos = cos_ref[r, :]
        sin = sin_ref[r, :]
        cos_q = cos * qscale
        sin_q = sin * qscale
        for gi, group in enumerate(groups):
            for cb in range(QKV_COL_GROUP // LANES):
                c0 = gi * QKV_COL_GROUP + cb * LANES
                q_ref[r, c0:c0 + LANES] = rope(group[:, cb * LANES:(cb + 1) * LANES], cos_q, sin_q).astype(BF16)
        _store_kv(kv_ref.at[r, :], rope(kv_half[:, 0:KV_DIM], cos, sin), kv_half[:, KV_DIM:])


def _qkv(x, mod, layer, tm, gmix, w_qkv, cos, sin):
    nb, seq_len, _ = x.shape
    return pl.pallas_call(
        functools.partial(_qkv_kernel, layer),
        grid=(nb, seq_len // tm),
        in_specs=[
            pl.BlockSpec((None, tm, D_MODEL), lambda b, i: (b, i, 0)),
            pl.BlockSpec((None, None, 1, N_MOD * D_MODEL), lambda b, i: (layer, b, 0, 0)),
            _layer_rows_spec(),
            _const_spec((D_MODEL, Q_DIM + 2 * KV_DIM)),
            pl.BlockSpec((tm, LANES), lambda b, i: (i, 0)),
            pl.BlockSpec((tm, LANES), lambda b, i: (i, 0)),
        ],
        out_specs=[pl.BlockSpec((None, tm, Q_DIM), lambda b, i: (b, i, 0)),
                   pl.BlockSpec((None, tm, KV_W), lambda b, i: (b, i, 0))],
        out_shape=[jax.ShapeDtypeStruct((nb, seq_len, Q_DIM), BF16),
                   jax.ShapeDtypeStruct((nb, seq_len, KV_W), BF16)],
        compiler_params=pltpu.CompilerParams(
            dimension_semantics=("arbitrary", "arbitrary"), vmem_limit_bytes=VMEM_LIMIT),
        name="qkv_rope",
    )(x, mod, gmix, w_qkv, cos, sin)


def _ctx_kv_kernel(layer, x_ref, mod_ref, gmix_ref, w_ref, kv_ref):
    sh1 = mod_ref[:, 0:D_MODEL]
    sc1 = mod_ref[:, D_MODEL:2 * D_MODEL]
    h = _rms_mod(x_ref[...], gmix_ref[layer:layer + 1, :], sh1, sc1).astype(BF16)
    kv = _dot(h, w_ref[...])
    _store_kv(kv_ref, kv[:, 0:KV_DIM], kv[:, KV_DIM:])


def _ctx_kv(ctx, mod, mod_row, layer, gmix, w_kv):
    nb, ctx_len, _ = ctx.shape
    return pl.pallas_call(
        functools.partial(_ctx_kv_kernel, layer),
        grid=(nb,),
        in_specs=[
            pl.BlockSpec((None, ctx_len, D_MODEL), lambda b: (b, 0, 0)),
            pl.BlockSpec((None, None, 1, N_MOD * D_MODEL), lambda b: (layer, mod_row, 0, 0)),
            _layer_rows_spec(),
            pl.BlockSpec((D_MODEL, 2 * KV_DIM), lambda b: (0, Q_DIM // (2 * KV_DIM)),
                         pipeline_mode=pl.Buffered(1)),
        ],
        out_specs=pl.BlockSpec((None, ctx_len, KV_W), lambda b: (b, 0, 0)),
        out_shape=jax.ShapeDtypeStruct((nb, ctx_len, KV_W), BF16),
        compiler_params=pltpu.CompilerParams(
            dimension_semantics=("arbitrary",), vmem_limit_bytes=VMEM_LIMIT),
        name="ctx_kv",
    )(ctx, mod, gmix, w_kv)


def _dot_nt(a, b):
    return lax.dot_general(a, b, (((1,), (1,)), ((), ())), preferred_element_type=F32)


def _attn_kernel(seq_len, tq, ctx_len, sink_ref, q_ref, kvp_ref, kvm_ref, kvn_ref, kvc_ref, o_ref,
                 kv_ext, *bufs):
    s_refs, e_refs, m_refs = bufs[0:2], bufs[2:6], bufs[6:10]
    kn_ref, minden_ref = bufs[10], bufs[11]
    i = pl.program_id(1)
    kv_ext[0:BLOCK, :] = kvp_ref[...]
    kv_ext[BLOCK:BLOCK + tq, :] = kvm_ref[...]
    kv_ext[BLOCK + tq:2 * BLOCK + tq, :] = kvn_ref[...]
    nblk = tq // BLOCK
    nwin = 3 * BLOCK
    nkey = nwin + ctx_len
    row = lax.broadcasted_iota(jnp.int32, (BLOCK, BLOCK), 0)
    col = lax.broadcasted_iota(jnp.int32, (BLOCK, BLOCK), 1)
    lane = lax.broadcasted_iota(jnp.int32, (BLOCK, LANES), 1)
    low = lane < HEAD_DIM
    head_a = (lane % HEAD_DIM) < AXIS_ROT
    my_lanes = (head_a, jnp.logical_not(head_a))
    den_lanes = (jnp.logical_not(low), low)

    chunks = [(g, t0) for g in range(GQA_GROUP) for t0 in range(0, BLOCK, SOFTMAX_ROWS)]

    def head_queries(j, kvh):
        q0 = pl.multiple_of(j * BLOCK, BLOCK)
        tiles = []
        for g in range(GQA_GROUP):
            pair = q_ref[pl.ds(q0, BLOCK), g * LANES:(g + 1) * LANES]
            tiles.append(jnp.where(my_lanes[kvh], pair, jnp.zeros_like(pair)))
        return tiles

    def window_masks(j):
        mask_l = (col >= row) & (i * tq + (j - 1) * BLOCK >= 0)
        mask_r = (col <= row) & (i * tq + (j + 1) * BLOCK < seq_len)
        return mask_l, mask_r

    def key_tiles(j, kvh):
        q0 = pl.multiple_of(j * BLOCK, BLOCK)
        return kv_ext[pl.ds(q0, nwin), 0:LANES], kvc_ref[:, 0:LANES]

    def scores(j, kvh):
        qs = jnp.concatenate(head_queries(j, kvh), axis=0)
        kw, kc = key_tiles(j, kvh)
        s_refs[kvh][:, 0:nwin] = _dot_nt(qs, kw)
        s_refs[kvh][:, nwin:nkey] = _dot_nt(qs, kc)

    def softmax(j, kvh):
        mask_l, mask_r = window_masks(j)

        def masked_pieces(rows, trows):
            pieces = [s_refs[kvh][rows, p * LANES:(p + 1) * LANES] for p in range(nkey // LANES)]
            pieces[0] = jnp.where(mask_l[trows], pieces[0], NEG)
            pieces[2] = jnp.where(mask_r[trows], pieces[2], NEG)
            return pieces

        for g, t0 in chunks:
            rows = slice(g * BLOCK + t0, g * BLOCK + t0 + SOFTMAX_ROWS)
            pieces = masked_pieces(rows, slice(t0, t0 + SOFTMAX_ROWS))
            m = pieces[0]
            for p in pieces[1:]:
                m = jnp.maximum(m, p)
            m = jnp.maximum(jnp.max(m, axis=1, keepdims=True), sink_ref[kvh * GQA_GROUP + g] * LOG2E)
            m_refs[kvh][rows, :] = jnp.broadcast_to(m, (SOFTMAX_ROWS, LANES))
        for g, t0 in chunks:
            rows = slice(g * BLOCK + t0, g * BLOCK + t0 + SOFTMAX_ROWS)
            pieces = masked_pieces(rows, slice(t0, t0 + SOFTMAX_ROWS))
            m = m_refs[kvh][rows, :]
            for p, piece in enumerate(pieces):
                e_refs[kvh][rows, p * LANES:(p + 1) * LANES] = jnp.exp2((piece - m).astype(BF16))

    def weighted_values(j, kvh, slot, fast):
        e_ref, m_ref = e_refs[2 * slot + kvh], m_refs[2 * slot + kvh]
        q0 = pl.multiple_of(j * BLOCK, BLOCK)
        vcols = slice((1 + kvh) * LANES, (2 + kvh) * LANES)
        values = jnp.concatenate([kv_ext[pl.ds(q0, nwin), vcols], kvc_ref[:, vcols]], axis=0)
        o = _dot(e_ref[...], values)
        on, smallest = [], None
        for g in range(GQA_GROUP):
            rows = slice(g * BLOCK, (g + 1) * BLOCK)
            sink = sink_ref[kvh * GQA_GROUP + g] * LOG2E
            if fast:
                sink_term = jnp.exp2(sink - m_ref[g * BLOCK:g * BLOCK + 1, :])
            else:
                sink_term = jnp.exp2(sink - m_ref[rows, :])
            total = o[rows] + sink_term
            smallest = total if smallest is None else jnp.minimum(smallest, total)
            on.append(o[rows] * pltpu.roll(1.0 / total, HEAD_DIM, 1))
        if fast:
            minden_ref[...] = jnp.minimum(minden_ref[...], jnp.where(den_lanes[kvh], smallest, 3e38))
        for pair in range(GQA_GROUP // 2):
            if kvh == 0:
                tile = jnp.where(low, on[2 * pair], pltpu.roll(on[2 * pair + 1], HEAD_DIM, 1))
            else:
                tile = jnp.where(low, pltpu.roll(on[2 * pair], HEAD_DIM, 1), on[2 * pair + 1])
            c0 = (kvh * GQA_GROUP + 2 * pair) * HEAD_DIM
            o_ref[pl.ds(q0, BLOCK), c0:c0 + LANES] = tile.astype(BF16)

    def exact_body(j, carry):
        scores(j, 0)
        scores(j, 1)
        softmax(j, 0)
        weighted_values(j, 0, 0, False)
        softmax(j, 1)
        weighted_values(j, 1, 0, False)
        return carry

    def max_sq_norms(k_tile):
        sq = k_tile.astype(F32)
        sq = sq * sq
        return [jnp.broadcast_to(jnp.max(jnp.sum(jnp.where(my_lanes[kvh], sq, 0.0), axis=1, keepdims=True),
                                         axis=0, keepdims=True), (8, LANES)) for kvh in range(N_KV_HEADS)]

    def store_key_norms(first, last):
        for kb in range(first, last):
            for kvh, n2 in enumerate(max_sq_norms(kv_ext[kb * BLOCK:(kb + 1) * BLOCK, 0:LANES])):
                kn_ref[kvh, kb] = n2

    store_key_norms(0, 3)
    ctx_norm = None
    for cb in range(ctx_len // BLOCK):
        n2 = max_sq_norms(kvc_ref[cb * BLOCK:(cb + 1) * BLOCK, 0:LANES])
        ctx_norm = n2 if ctx_norm is None else [jnp.maximum(a, b) for a, b in zip(ctx_norm, n2)]
    minden_ref[...] = jnp.full(minden_ref.shape, 3e38, F32)

    def bounded_exp(j, kvh, slot):
        e_ref, m_ref = e_refs[2 * slot + kvh], m_refs[2 * slot + kvh]
        q0 = pl.multiple_of(j * BLOCK, BLOCK)
        kw, kc = key_tiles(j, kvh)
        kn2 = jnp.maximum(jnp.max(kn_ref[kvh, pl.ds(j, 3)], axis=0), ctx_norm[kvh])[0:1, :]
        tiles = head_queries(j, kvh)
        for g in range(GQA_GROUP):
            qf = tiles[g].astype(F32)
            qn2 = jnp.max(jnp.sum(qf * qf, axis=1, keepdims=True), axis=0, keepdims=True)
            bound = jnp.maximum(jnp.sqrt(qn2 * kn2), sink_ref[kvh * GQA_GROUP + g] * LOG2E)
            m_ref[g * BLOCK:(g + 1) * BLOCK, :] = jnp.broadcast_to(bound, (BLOCK, LANES))
        qs = jnp.concatenate(tiles, axis=0)
        s_w = _dot_nt(qs, kw)
        s_c = _dot_nt(qs, kc)
        mask_l, mask_r = window_masks(j)
        for g, t0 in chunks:
            rows = slice(g * BLOCK + t0, g * BLOCK + t0 + SOFTMAX_ROWS)
            trows = slice(t0, t0 + SOFTMAX_ROWS)
            m = m_ref[rows, :]
            pieces = ([s_w[rows, p * LANES:(p + 1) * LANES] for p in range(nwin // LANES)]
                      + [s_c[rows, p * LANES:(p + 1) * LANES] for p in range(ctx_len // LANES)])
            pieces[0] = jnp.where(mask_l[trows], pieces[0], NEG)
            pieces[2] = jnp.where(mask_r[trows], pieces[2], NEG)
            for p, piece in enumerate(pieces):
                e_ref[rows, p * LANES:(p + 1) * LANES] = jnp.exp2(piece - m).astype(BF16)

    bounded_exp(0, 0, 0)
    bounded_exp(0, 1, 0)
    store_key_norms(3, nblk + 2)

    def fast_blocks(j0, last):
        for k in range(FAST_UNROLL):
            for kvh in range(N_KV_HEADS):
                weighted_values(j0 + k, kvh, k % 2, True)
                if not (last and k + 1 == FAST_UNROLL):
                    bounded_exp(j0 + k + 1, kvh, (k + 1) % 2)

    def fast_body(jj, carry):
        fast_blocks(FAST_UNROLL * jj, False)
        return carry

    lax.fori_loop(0, nblk // FAST_UNROLL - 1, fast_body, 0)
    fast_blocks(nblk - FAST_UNROLL, True)
    smallest = jnp.min(minden_ref[...])

    @pl.when(jnp.logical_not(smallest >= 2.0 ** MIN_SAFE_DENOMINATOR_LOG2))
    def _():
        lax.fori_loop(0, nblk, exact_body, 0)


def _attention(q, kv, kvc, sink, tq):
    nb, seq_len, _ = q.shape
    ctx_len = kvc.shape[1]
    nblk = seq_len // BLOCK
    qpb = tq // BLOCK
    rows = GQA_GROUP * BLOCK
    nkey = 3 * BLOCK + ctx_len
    return pl.pallas_call(
        functools.partial(_attn_kernel, seq_len, tq, ctx_len),
        grid=(nb, seq_len // tq),
        in_specs=[
            pl.BlockSpec(memory_space=pltpu.SMEM),
            pl.BlockSpec((None, tq, Q_DIM), lambda b, i: (b, i, 0)),
            pl.BlockSpec((None, BLOCK, KV_W), lambda b, i: (b, jnp.maximum(i * qpb - 1, 0), 0)),
            pl.BlockSpec((None, tq, KV_W), lambda b, i: (b, i, 0)),
            pl.BlockSpec((None, BLOCK, KV_W),
                         lambda b, i: (b, jnp.minimum((i + 1) * qpb, nblk - 1), 0)),
            pl.BlockSpec((None, ctx_len, KV_W), lambda b, i: (b, 0, 0)),
        ],
        out_specs=pl.BlockSpec((None, tq, Q_DIM), lambda b, i: (b, i, 0)),
        out_shape=jax.ShapeDtypeStruct((nb, seq_len, Q_DIM), BF16),
        scratch_shapes=([pltpu.VMEM((tq + 2 * BLOCK, KV_W), BF16)]
                        + [pltpu.VMEM((rows, nkey), F32)] * N_KV_HEADS
                        + [pltpu.VMEM((rows, nkey), BF16)] * (2 * N_KV_HEADS)
                        + [pltpu.VMEM((rows, LANES), F32)] * (2 * N_KV_HEADS)
                        + [pltpu.VMEM((N_KV_HEADS, qpb + 2, 8, LANES), F32),
                           pltpu.VMEM((BLOCK, LANES), F32)]),
        compiler_params=pltpu.CompilerParams(
            dimension_semantics=("arbitrary", "arbitrary"), vmem_limit_bytes=VMEM_LIMIT),
        name="attention",
    )(sink, q, kv, kv, kv, kvc)


def _post_kernel(layer, x_ref, o_ref_in, mod_ref, gmlp_ref, gfin_ref, wo_ref, w1_ref, w2_ref, out_ref):
    g1 = mod_ref[:, 2 * D_MODEL:3 * D_MODEL]
    sh2 = mod_ref[:, 3 * D_MODEL:4 * D_MODEL]
    sc2 = mod_ref[:, 4 * D_MODEL:5 * D_MODEL]
    g2 = mod_ref[:, 5 * D_MODEL:6 * D_MODEL]
    gain = gmlp_ref[layer:layer + 1, :]
    tm = x_ref.shape[0]
    halves = [slice(0, tm // 2), slice(tm // 2, tm)]
    x1 = [x_ref[r, :] + g1 * _dot(o_ref_in[r, :], wo_ref[...]) for r in halves]
    h = [_rms_mod(v, gain, sh2, sc2).astype(BF16) for v in x1]
    a = []
    for hv in h:
        up = jnp.maximum(_dot(hv, w1_ref[...]), 0.0)
        a.append((up * up).astype(BF16))
    for r, v, av in zip(halves, x1, a):
        x2 = v + g2 * _dot(av, w2_ref[...])
        y = x2 * lax.rsqrt(jnp.mean(x2 * x2, axis=-1, keepdims=True) + EPS)
        out_ref[r, :] = y * gfin_ref[...]


def _post_stream_kernel(layer, grid, tm, x_hbm, o_hbm, mod_hbm, gmlp_ref, gfin_ref, wo_ref, w1_ref, w2_ref,
                        out_hbm):
    def step(x_ref, o_ref_in, mod_ref, out_ref):
        _post_kernel(layer, x_ref, o_ref_in, mod_ref, gmlp_ref, gfin_ref, wo_ref, w1_ref, w2_ref, out_ref)

    pltpu.emit_pipeline(
        step,
        grid=grid,
        in_specs=[
            pl.BlockSpec((None, tm, D_MODEL), lambda b, i: (b, i, 0), pipeline_mode=pl.Buffered(3)),
            pl.BlockSpec((None, tm, Q_DIM), lambda b, i: (b, i, 0), pipeline_mode=pl.Buffered(3)),
            pl.BlockSpec((None, None, 1, N_MOD * D_MODEL), lambda b, i: (layer, b, 0, 0)),
        ],
        out_specs=[pl.BlockSpec((None, tm, D_MODEL), lambda b, i: (b, i, 0))],
    )(x_hbm, o_hbm, mod_hbm, out_hbm)


def _post(x, o, mod, layer, tm, gmlp, gfin, w_o, w1, w2):
    nb, seq_len, _ = x.shape
    hbm = pl.BlockSpec(memory_space=pl.ANY)
    vmem = pl.BlockSpec(memory_space=pltpu.VMEM)
    return pl.pallas_call(
        functools.partial(_post_stream_kernel, layer, (nb, seq_len // tm), tm),
        in_specs=[hbm, hbm, hbm, vmem, vmem, vmem, vmem, vmem],
        out_specs=hbm,
        out_shape=jax.ShapeDtypeStruct(x.shape, F32),
        compiler_params=pltpu.CompilerParams(vmem_limit_bytes=VMEM_LIMIT),
        name="post_attn",
    )(x, o, mod, gmlp, gfin, w_o, w1, w2)


def _rope_tables(seq_len):
    pos = np.arange(seq_len)
    inv = ROPE_BASE ** (-np.arange(0, AXIS_ROT, 2, dtype=np.float32) / AXIS_ROT)
    ang = np.concatenate([(pos // GRID_W).astype(np.float32)[:, None] * inv[None],
                          (pos % GRID_W).astype(np.float32)[:, None] * inv[None]], axis=-1)
    cos = np.tile(np.cos(ang.astype(np.float64)), (1, LANES // AXIS_ROT))
    sin = np.sin(ang.astype(np.float64))
    return (jnp.asarray(cos, F32), jnp.asarray(np.concatenate([-sin, -sin, sin, sin], axis=-1), F32))


def kernel(x, c, ctx, c_ctx, ada_w, ada_b, norm_mix_g, norm_mlp_g, pool_w_in, pool_w_grp,
           pool_scale, pool_w_out, attn_w_qkv, attn_sink, attn_w_o, mlp_w1, mlp_w2, final_g):
    nb, seq_len, _ = x.shape
    ctx_len = ctx.shape[1]
    ctx_row = nb
    c8 = jnp.concatenate(
        [c, c_ctx[None], jnp.zeros((MOD_ROWS - nb - 1, D_MODEL), F32)], axis=0)
    n_grp = len(POOL_WINDOWS)
    layer0_weights = ((pool_w_in, 0), (pool_w_grp.reshape(-1, n_grp * POOL_GROUP_DIM, POOL_GROUP_DIM), 0),
                      (pool_w_out, 0), (mlp_w1, 0), (mlp_w2, 0))
    mod, (w_in, w_grp, w_out, w1_0, w2_0) = _modulation(c8, ada_w, ada_b, layer0_weights)

    gmix, gmlp = norm_mix_g, norm_mlp_g

    l0 = (gmix, gmlp, pool_scale, w_in,
          w_grp.reshape(n_grp, POOL_GROUP_DIM, POOL_GROUP_DIM), w_out, w1_0, w2_0)
    layer1_weights = ((attn_w_qkv, 0), (attn_w_o, 0), (mlp_w1, 1), (mlp_w2, 1))
    x1, (w_qkv, w_o, w1, w2) = _layer0(x, mod, None, 0, LAYER_TILE, True, layer1_weights, *l0)
    ctx1, _ = _layer0(ctx, mod, ctx_row, 0, ctx_len, False, (), *l0)

    cos, sin = _rope_tables(seq_len)
    q, kv = _qkv(x1, mod, 1, QKV_TILE, gmix, w_qkv, cos, sin)
    kvc = _ctx_kv(ctx1, mod, ctx_row, 1, gmix, w_qkv)
    o = _attention(q, kv, kvc, attn_sink[0], ATTN_TILE)
    return _post(x1, o, mod, 1, LAYER_TILE, gmlp, final_g.reshape(1, D_MODEL), w_o, w1, w2)
```
